```python
import math
import jax, jax.numpy as jnp
from jax import lax
import numpy as np


D_MODEL = 1024
BATCH = 8
SEQ = 2048
DEPTH = 1
DEC_BATCH = 128
DEC_SEQ = 1
PAST_LEN = 2048
PAGE_SIZE = 128

NSA_HEADS = 8
NSA_KV_HEADS = 2
NSA_GROUP = NSA_HEADS // NSA_KV_HEADS
HEAD_DIM = 64
CMP_BLOCK = 32
SEL_BLOCK = 64
TOP_N = 16
WINDOW = 512
N_KV_SLOTS = 4
QBLK = 64
SCALE = HEAD_DIM ** -0.5
NSA_WIDTH = NSA_HEADS * HEAD_DIM
KV_WIDTH = NSA_KV_HEADS * HEAD_DIM
HG_HEADS = 4
HG_DIM = 128
HG_WIDTH = HG_HEADS * HG_DIM
HG_CHUNK = 64
HG_SCALE = HG_DIM ** -0.5
REL_BUCKETS = 32
REL_MAX_DIST = 128
D_FF = 2816
CONV_W = 3
EPS = 1e-6
SPLITS = (NSA_WIDTH, 6 * KV_WIDTH, 3 * NSA_HEADS, HG_WIDTH, HG_WIDTH, HG_WIDTH, HG_WIDTH, 2 * D_MODEL)
SPLIT_IDX = tuple(int(v) for v in np.cumsum(SPLITS)[:-1])
C_IN = sum(SPLITS)

kernel_name = 'nsa_hgrn2_convffn_hybrid_step'


def rms_norm(x, g):
    xf = x.astype(jnp.float32)
    y = xf * lax.rsqrt(jnp.mean(xf * xf, axis=-1, keepdims=True) + EPS)
    return (y * g.astype(jnp.float32)).astype(x.dtype)


def rel_bucket(dist):
    n = jnp.maximum(dist, 0)
    exact = REL_BUCKETS // 2
    big = exact + (jnp.log(jnp.maximum(n, 1).astype(jnp.float32) / exact)
                   / math.log(REL_MAX_DIST / exact) * (REL_BUCKETS - exact)).astype(jnp.int32)
    return jnp.where(n < exact, n, jnp.minimum(big, REL_BUCKETS - 1))


def masked_softmax(s, mask):
    s = jnp.where(mask, s.astype(jnp.float32), -jnp.inf)
    m = jnp.max(s, axis=-1, keepdims=True)
    m = jnp.where(jnp.isfinite(m), m, 0.0)
    e = jnp.where(mask, jnp.exp(s - m), 0.0)
    den = jnp.sum(e, axis=-1, keepdims=True)
    return e / jnp.where(den > 0, den, 1.0)


def compress_block(rows, w1, w2):
    h = jax.nn.gelu(jnp.einsum('bnjgd,jde->bnge', rows.astype(jnp.float32), w1.astype(jnp.float32)))
    return jnp.einsum('bnge,ef->bngf', h, w2.astype(jnp.float32))


def nsa_attention(q, gates, kv_all, win_all, pos0, rel_table, k_cmp_g,
                  phi_k_w1, phi_k_w2, phi_v_w1, phi_v_w2):
    B, L = q.shape[0], q.shape[1]
    G, R, HD = NSA_KV_HEADS, NSA_GROUP, HEAD_DIM
    T = kv_all.shape[1]
    Tp = -(-T // SEL_BLOCK) * SEL_BLOCK
    kv_all = jnp.pad(kv_all, ((0, 0), (0, Tp - T), (0, 0), (0, 0), (0, 0)))
    n_cmp, n_sel = Tp // CMP_BLOCK, Tp // SEL_BLOCK
    n_top = min(TOP_N, n_sel)
    cmp_rows = kv_all[:, :, :2].reshape(B, n_cmp, CMP_BLOCK, 2, G, HD)
    ck = rms_norm(compress_block(cmp_rows[:, :, :, 0], phi_k_w1, phi_k_w2), k_cmp_g)
    cv = compress_block(cmp_rows[:, :, :, 1], phi_v_w1, phi_v_w2)
    c_end = (jnp.arange(n_cmp) + 1) * CMP_BLOCK - 1
    sel_k = kv_all[:, :, 2].reshape(B, n_sel, SEL_BLOCK, G, HD).transpose(0, 3, 1, 2, 4)
    sel_v = kv_all[:, :, 3].reshape(B, n_sel, SEL_BLOCK, G, HD).transpose(0, 3, 1, 2, 4)
    n_past_win = win_all.shape[1] - L
    win = jnp.pad(win_all, ((0, 0), (WINDOW - n_past_win, 0), (0, 0), (0, 0), (0, 0)))
    tab = rel_table.astype(jnp.float32).reshape(REL_BUCKETS, G, R)
    bi = jnp.arange(B)[:, None, None, None]
    gi = jnp.arange(G)[None, :, None, None]
    js = jnp.arange(n_sel)
    qb = QBLK if L % QBLK == 0 else L
    nqb = L // qb
    q_blocks = q.reshape(B, nqb, qb, G, R, HD).swapaxes(0, 1)
    g_blocks = gates.reshape(B, nqb, qb, G, R, 3).swapaxes(0, 1)

    def one_block(args):
        qc, gc, s0 = args
        qc = qc.astype(jnp.float32)
        qpos = pos0 + s0 + jnp.arange(qb)
        s_c = jnp.einsum('bqgrd,bngd->bgrqn', qc, ck) * SCALE
        s_c = s_c + tab[rel_bucket(qpos[:, None] - c_end[None])].transpose(2, 3, 0, 1)
        p_c = masked_softmax(s_c, c_end[None] <= qpos[:, None])
        o_c = jnp.einsum('bgrqn,bngd->bqgrd', p_c, cv)
        imp = p_c.sum(axis=2).reshape(B, G, qb, n_sel, SEL_BLOCK // CMP_BLOCK).sum(-1)
        cur = (qpos // SEL_BLOCK)[:, None]
        valid = js[None] <= cur
        forced = valid & ((js[None] == 0) | (js[None] >= cur - 1))
        score = jnp.where(forced, jnp.inf, jnp.where(valid, imp, -jnp.inf))
        top_v, top_i = lax.top_k(score, n_top)
        k_g = sel_k[bi, gi, top_i]
        v_g = sel_v[bi, gi, top_i].reshape(B, G, qb, n_top * SEL_BLOCK, HD)
        kpos = top_i[..., None] * SEL_BLOCK + jnp.arange(SEL_BLOCK)
        dist_s = qpos[:, None, None] - kpos
        m_s = (top_v > -jnp.inf)[..., None] & (dist_s >= 0)
        b_s = jnp.moveaxis(tab[rel_bucket(dist_s), gi[..., None]], -1, 2)
        s_s = jnp.einsum('bqgrd,bgqnjd->bgrqnj', qc, k_g) * SCALE + b_s
        n_keys = n_top * SEL_BLOCK
        p_s = masked_softmax(s_s.reshape(B, G, R, qb, n_keys), m_s.reshape(B, G, 1, qb, n_keys))
        o_s = jnp.einsum('bgrql,bgqld->bqgrd', p_s, v_g)
        w_rows = lax.dynamic_slice_in_dim(win, s0, qb + WINDOW, axis=1)
        kpos_w = pos0 + s0 - WINDOW + jnp.arange(qb + WINDOW)
        dist_w = qpos[:, None] - kpos_w[None]
        m_w = (dist_w >= 0) & (dist_w <= WINDOW) & (kpos_w[None] >= 0)
        s_w = jnp.einsum('bqgrd,bkgd->bgrqk', qc, w_rows[:, :, 0]) * SCALE
        s_w = s_w + tab[rel_bucket(dist_w)].transpose(2, 3, 0, 1)
        p_w = masked_softmax(s_w, m_w)
        o_w = jnp.einsum('bgrqk,bkgd->bqgrd', p_w, w_rows[:, :, 1])
        return gc[..., 0:1] * o_c + gc[..., 1:2] * o_s + gc[..., 2:3] * o_w

    o = lax.map(one_block, (q_blocks, g_blocks, jnp.arange(nqb) * qb))
    return o.swapaxes(0, 1).reshape(B, L, NSA_WIDTH)


def hgrn2_chunked(q, k, v, g_log, S0):
    B, L, H, _ = q.shape
    C = min(HG_CHUNK, L)
    Lp = -(-L // C) * C
    pw = ((0, 0), (0, Lp - L), (0, 0), (0, 0))
    q, k, v, g_log = (jnp.pad(a.astype(jnp.float32), pw) for a in (q, k, v, g_log))
    nc = Lp // C
    to_chunks = lambda a: a.reshape(B, nc, C, H, a.shape[-1]).swapaxes(0, 1)
    tri = jnp.tril(jnp.ones((C, C), dtype=bool))[None, :, :, None, None]

    def step(S, inp):
        qc, kc, vc, gc = inp
        b = jnp.cumsum(gc, axis=1)
        diff = b[:, :, None] - b[:, None, :]
        decay = jnp.where(tri, jnp.exp(jnp.where(tri, diff, 0.0)), 0.0)
        A = jnp.einsum('bthk,bshk,btshk->bhts', qc, kc, decay)
        o = jnp.einsum('bhts,bshv->bthv', A, vc) + jnp.einsum('bthk,bhkv->bthv', qc * jnp.exp(b), S)
        b_last = b[:, -1]
        S = jnp.exp(b_last)[..., None] * S + jnp.einsum('bshk,bshv->bhkv', kc * jnp.exp(b_last[:, None] - b), vc)
        return S, o

    S, o = lax.scan(step, S0.astype(jnp.float32), (to_chunks(q), to_chunks(k), to_chunks(v), to_chunks(g_log)))
    return o.swapaxes(0, 1).reshape(B, Lp, H, -1)[:, :L], S


def trunk_layer(x, past_kv, past_win, hg_state, conv_state, lb, rel_table,
                norm1_g, w_in, q_norm_g, k_norm_g, phi_k_w1, phi_k_w2, phi_v_w1, phi_v_w2,
                hg_norm_g, w_nsa_out, w_hg_out, w_o, norm2_g, w_gate, w_up, conv_w, conv_b, w_down):
    B, L, _ = x.shape
    pos0 = past_kv.shape[1]
    xn = rms_norm(x, norm1_g)
    z = xn @ w_in
    q, kv, nsa_gl, hq, hf, hi, hg, mg = jnp.split(z, SPLIT_IDX, axis=-1)
    q = rms_norm(q.reshape(B, L, NSA_HEADS, HEAD_DIM), q_norm_g)
    kv = kv.reshape(B, L, 6, NSA_KV_HEADS, HEAD_DIM)
    new_kv = jnp.stack([kv[:, :, 0], kv[:, :, 1], rms_norm(kv[:, :, 2], k_norm_g[1]), kv[:, :, 3]], axis=2)
    new_win = jnp.stack([rms_norm(kv[:, :, 4], k_norm_g[2]), kv[:, :, 5]], axis=2)
    win_all = jnp.concatenate([past_win, new_win], axis=1)
    gates = jax.nn.sigmoid(nsa_gl.astype(jnp.float32)).reshape(B, L, NSA_HEADS, 3)
    o_nsa = nsa_attention(q, gates, jnp.concatenate([past_kv, new_kv], axis=1), win_all, pos0, rel_table,
                          k_norm_g[0], phi_k_w1, phi_k_w2, phi_v_w1, phi_v_w2)
    f = lb + (1.0 - lb) * jax.nn.sigmoid(hf.astype(jnp.float32))
    hs = (B, L, HG_HEADS, HG_DIM)
    o_hg, S_new = hgrn2_chunked((hq.astype(jnp.float32) * HG_SCALE).reshape(hs), (1.0 - f).reshape(hs),
                                hi.reshape(hs), jnp.log(f).reshape(hs), hg_state)
    o_hg = rms_norm(o_hg, hg_norm_g) * jax.nn.silu(hg.astype(jnp.float32)).reshape(hs)
    ga, gb = jnp.split(mg, 2, axis=-1)
    y_a = o_nsa.astype(x.dtype) @ w_nsa_out
    y_b = o_hg.reshape(B, L, HG_WIDTH).astype(x.dtype) @ w_hg_out
    x = x + (jax.nn.sigmoid(ga) * y_a + jax.nn.sigmoid(gb) * y_b) @ w_o
    xn2 = rms_norm(x, norm2_g)
    ext = jnp.concatenate([conv_state.astype(xn2.dtype), xn2 @ w_gate], axis=1)
    c = conv_b + sum(conv_w[j] * ext[:, j:j + L] for j in range(CONV_W))
    x = x + (jax.nn.silu(c) * (xn2 @ w_up)) @ w_down
    new_win_state = win_all[:, -min(WINDOW, win_all.shape[1]):]
    return x, new_kv, new_win_state, S_new, ext[:, -(CONV_W - 1):]


def setup_inputs(seed: int = 0) -> dict:
    key = jax.random.key(seed)
    ks = iter(jax.random.split(key, 40))
    nrm = lambda shape, scale: jax.random.normal(next(ks), shape, jnp.float32) * scale
    gain = lambda shape: 1.0 + nrm(shape, 0.02)
    n_pages = PAST_LEN // PAGE_SIZE
    n_used = DEC_BATCH * n_pages
    n_phys = (n_used * 5 + 3) // 4
    kvh = (NSA_KV_HEADS, HEAD_DIM)
    x_prompt = nrm((BATCH, SEQ, D_MODEL), 1.0)
    x_sample = nrm((DEC_BATCH, DEC_SEQ, D_MODEL), 1.0)
    cache_kv = nrm((DEPTH, n_phys, PAGE_SIZE, N_KV_SLOTS) + kvh, 1.0)
    page_table = jax.random.permutation(next(ks), n_phys)[:n_used].reshape(DEC_BATCH, n_pages).astype(jnp.int32)
    state_kv_win = nrm((DEPTH, DEC_BATCH, min(WINDOW, PAST_LEN), 2) + kvh, 1.0)
    state_hgrn = nrm((DEPTH, DEC_BATCH, HG_HEADS, HG_DIM, HG_DIM), 0.5)
    state_conv = nrm((DEPTH, DEC_BATCH, CONV_W - 1, D_FF), 1.0)
    return {
        'x_prompt': x_prompt, 'x_sample': x_sample, 'cache_kv': cache_kv, 'page_table': page_table,
        'state_kv_win': state_kv_win, 'state_hgrn': state_hgrn, 'state_conv': state_conv,
        'rel_table': nrm((REL_BUCKETS, NSA_HEADS), 0.5),
        'hg_lb_logits': nrm((DEPTH + 1, HG_WIDTH), 0.5),
        'norm1_g': gain((DEPTH, D_MODEL)),
        'w_in': nrm((DEPTH, D_MODEL, C_IN), D_MODEL ** -0.5),
        'q_norm_g': gain((DEPTH, HEAD_DIM)),
        'k_norm_g': gain((DEPTH, 3, HEAD_DIM)),
        'phi_k_w1': nrm((DEPTH, CMP_BLOCK, HEAD_DIM, HEAD_DIM), (CMP_BLOCK * HEAD_DIM) ** -0.5),
        'phi_k_w2': nrm((DEPTH, HEAD_DIM, HEAD_DIM), HEAD_DIM ** -0.5),
        'phi_v_w1': nrm((DEPTH, CMP_BLOCK, HEAD_DIM, HEAD_DIM), (CMP_BLOCK * HEAD_DIM) ** -0.5),
        'phi_v_w2': nrm((DEPTH, HEAD_DIM, HEAD_DIM), HEAD_DIM ** -0.5),
        'hg_norm_g': gain((DEPTH, HG_DIM)),
        'w_nsa_out': nrm((DEPTH, NSA_WIDTH, D_MODEL), NSA_WIDTH ** -0.5),
        'w_hg_out': nrm((DEPTH, HG_WIDTH, D_MODEL), HG_WIDTH ** -0.5),
        'w_o': nrm((DEPTH, D_MODEL, D_MODEL), D_MODEL ** -0.5),
        'norm2_g': gain((DEPTH, D_MODEL)),
        'w_gate': nrm((DEPTH, D_MODEL, D_FF), D_MODEL ** -0.5),
        'w_up': nrm((DEPTH, D_MODEL, D_FF), D_MODEL ** -0.5),
        'conv_w': nrm((DEPTH, CONV_W, D_FF), CONV_W ** -0.5),
        'conv_b': nrm((DEPTH, D_FF), 0.02),
        'w_down': nrm((DEPTH, D_FF, D_MODEL), D_FF ** -0.5),
    }


def reference(x_prompt, x_sample, cache_kv, page_table, state_kv_win, state_hgrn, state_conv,
              rel_table, hg_lb_logits, norm1_g, w_in, q_norm_g, k_norm_g, phi_k_w1, phi_k_w2,
              phi_v_w1, phi_v_w2, hg_norm_g, w_nsa_out, w_hg_out, w_o, norm2_g, w_gate, w_up,
              conv_w, conv_b, w_down):
    lbs = jnp.cumsum(jax.nn.softmax(hg_lb_logits.astype(jnp.float32), axis=0), axis=0)
    Bp, Bs = x_prompt.shape[0], x_sample.shape[0]
    n_pages = page_table.shape[1]
    slot_shape = (N_KV_SLOTS, NSA_KV_HEADS, HEAD_DIM)
    yp, ys = x_prompt, x_sample
    kvp, kvs, wnp, wns, hgp, hgs, cvp, cvs = [], [], [], [], [], [], [], []
    for l in range(DEPTH):
        lw = (norm1_g[l], w_in[l], q_norm_g[l], k_norm_g[l], phi_k_w1[l], phi_k_w2[l], phi_v_w1[l],
              phi_v_w2[l], hg_norm_g[l], w_nsa_out[l], w_hg_out[l], w_o[l], norm2_g[l], w_gate[l],
              w_up[l], conv_w[l], conv_b[l], w_down[l])
        yp, a, b, c, d = trunk_layer(
            yp, jnp.zeros((Bp, 0) + slot_shape, yp.dtype),
            jnp.zeros((Bp, 0, 2, NSA_KV_HEADS, HEAD_DIM), yp.dtype),
            jnp.zeros((Bp, HG_HEADS, HG_DIM, HG_DIM), jnp.float32),
            jnp.zeros((Bp, CONV_W - 1, D_FF), yp.dtype), lbs[l], rel_table, *lw)
        kvp.append(a); wnp.append(b); hgp.append(c); cvp.append(d)
        past = cache_kv[l][page_table].reshape((Bs, n_pages * cache_kv.shape[2]) + slot_shape)
        ys, a, b, c, d = trunk_layer(ys, past, state_kv_win[l], state_hgrn[l], state_conv[l],
                                     lbs[l], rel_table, *lw)
        kvs.append(a); wns.append(b); hgs.append(c); cvs.append(d)
    return (yp, ys, jnp.stack(kvp), jnp.stack(kvs), jnp.stack(wnp), jnp.stack(wns),
            jnp.stack(hgp), jnp.stack(hgs), jnp.stack(cvp), jnp.stack(cvs))
```

```python
import functools
import math

import numpy as np
import jax
import jax.numpy as jnp
from jax import lax
from jax.experimental import pallas as pl
from jax.experimental.pallas import tpu as pltpu

f32, bf16, i32 = jnp.float32, jnp.bfloat16, jnp.int32

D_MODEL = 1024
PAGE_SIZE = 128
NSA_HEADS, NSA_KV_HEADS, NSA_GROUP, HEAD_DIM = 8, 2, 4, 64
CMP_BLOCK, SEL_BLOCK, TOP_N, WINDOW = 32, 64, 16, 512
SCALE = HEAD_DIM ** -0.5
NSA_WIDTH = NSA_HEADS * HEAD_DIM
KV_WIDTH = NSA_KV_HEADS * HEAD_DIM
HG_HEADS, HG_DIM = 4, 128
HG_WIDTH = HG_HEADS * HG_DIM
HG_SCALE = HG_DIM ** -0.5
HG_BLOCK = 16
REL_BUCKETS, REL_MAX_DIST = 32, 128
D_FF = 2816
EPS = 1e-6
SPLITS = (NSA_WIDTH, 6 * KV_WIDTH, 3 * NSA_HEADS, HG_WIDTH, HG_WIDTH, HG_WIDTH, HG_WIDTH, 2 * D_MODEL)
LANES = 128
NEG = -1e30
FORCED_SCORE, INVALID_SCORE = 8.0, -1.0
VMEM_LIMIT = 56 * 1024 * 1024

NT = (((1,), (1,)), ((), ()))
TN = (((0,), (0,)), ((), ()))


def _params(sem):
    return pltpu.CompilerParams(dimension_semantics=sem, vmem_limit_bytes=VMEM_LIMIT)


def _bucket_np(dist):
    n = np.maximum(dist, 0)
    exact = REL_BUCKETS // 2
    ratio = np.log(np.maximum(n, 1).astype(np.float32) / np.float32(exact)) / np.float32(math.log(REL_MAX_DIST / exact))
    big = exact + (ratio.astype(np.float32) * np.float32(REL_BUCKETS - exact)).astype(np.int32)
    return np.where(n < exact, n, np.minimum(big, REL_BUCKETS - 1)).astype(np.int32)


def _cmp_perm(n_cmp):
    half = n_cmp // 2
    c = np.arange(n_cmp)
    return np.where(c < half, 2 * c, 2 * (c - half) + 1)


def _prompt_bucket_tables(L):
    nqb = L // SEL_BLOCK
    n_cmp = L // CMP_BLOCK
    qi = np.arange(SEL_BLOCK)
    c_end = (_cmp_perm(n_cmp) + 1) * CMP_BLOCK - 1
    idx_c = np.stack([_bucket_np((i * SEL_BLOCK + qi)[:, None] - c_end[None, :]) for i in range(nqb)])
    kj = np.arange(SEL_BLOCK)
    def tile(delta):
        return _bucket_np(delta * SEL_BLOCK + qi[:, None] - kj[None, :])
    idx_t = np.stack([np.concatenate([tile(e), tile(e - 1)], axis=1) for e in range(5)])
    return idx_c.astype(np.int32), idx_t.astype(np.int32)


def _decode_bucket_table(past_len, n_win):
    n_cmp = past_len // CMP_BLOCK
    c_end = (_cmp_perm(n_cmp) + 1) * CMP_BLOCK - 1
    d_c = np.zeros((LANES,), np.int64)
    d_c[:n_cmp] = past_len - c_end
    d_s = past_len - np.arange(past_len)
    d_w = n_win - np.arange(n_win)
    d_new = np.zeros((LANES,), np.int64)
    return _bucket_np(np.concatenate([d_s, d_w, d_c, d_new]))[None, :].astype(np.int32)


def _expand_np(n_blocks, n_keys):
    e = (np.arange(n_keys)[None, :] // SEL_BLOCK == np.arange(n_blocks)[:, None])
    return e.astype(np.float32)


def _seg_rmsnorm(x, gain, seg):
    outs = []
    for t in range(x.shape[1] // LANES):
        xt = x[:, t * LANES:(t + 1) * LANES]
        sq = xt * xt
        if seg == LANES:
            r = lax.rsqrt(jnp.sum(sq, axis=-1, keepdims=True) * (1.0 / HEAD_DIM) + EPS)
        else:
            lo = lax.broadcasted_iota(i32, xt.shape, 1) < HEAD_DIM
            s_lo = jnp.sum(jnp.where(lo, sq, 0.0), axis=-1, keepdims=True)
            s_hi = jnp.sum(jnp.where(lo, 0.0, sq), axis=-1, keepdims=True)
            r = jnp.where(lo, lax.rsqrt(s_lo * (1.0 / HEAD_DIM) + EPS), lax.rsqrt(s_hi * (1.0 / HEAD_DIM) + EPS))
        outs.append(xt * r)
    y = outs[0] if len(outs) == 1 else jnp.concatenate(outs, axis=1)
    return y * gain


def _rmsnorm_rows(x, gain):
    return x * lax.rsqrt(jnp.mean(x * x, axis=-1, keepdims=True) + EPS) * gain


def _softmax_parts(s, mask):
    s = jnp.where(mask, s, NEG)
    m = jnp.max(s, axis=-1, keepdims=True)
    e = jnp.where(mask, jnp.exp(s - m), 0.0)
    return e, jnp.sum(e, axis=-1, keepdims=True)


def _bias_table_kernel(tab_ref, idx_ref, out_ref):
    idx = idx_ref[...]
    for h in range(NSA_HEADS):
        acc = jnp.zeros(idx.shape, f32)
        for b in range(REL_BUCKETS):
            acc = jnp.where(idx == b, tab_ref[b, h], acc)
        out_ref[h] = acc


def _bias_tables(rel_table, idx):
    n, rows, w = idx.shape
    return pl.pallas_call(
        _bias_table_kernel,
        out_shape=jax.ShapeDtypeStruct((n, NSA_HEADS, rows, w), f32),
        grid=(n,),
        in_specs=[pl.BlockSpec(memory_space=pltpu.SMEM),
                  pl.BlockSpec((None, rows, w), lambda i: (i, 0, 0))],
        out_specs=pl.BlockSpec((None, NSA_HEADS, rows, w), lambda i: (i, 0, 0, 0)),
        compiler_params=_params(("arbitrary",)),
        name="bias_tables",
    )(rel_table, idx)


def _in_proj_kernel(x_ref, g1_ref, w_ref, gq_ref, gk1_ref, gk2_ref,
                    q_ref, kv_ref, win_ref, katt_ref, h4_ref, mg_ref, gl_ref, *, qw, q_seg):
    x = x_ref[...]
    xn = _rmsnorm_rows(x, g1_ref[...]).astype(bf16)

    def proj(c0, width):
        return jnp.dot(xn, w_ref[:, c0:c0 + width], preferred_element_type=f32)

    q_ref[...] = _seg_rmsnorm(proj(0, qw), gq_ref[...], q_seg).astype(bf16)
    c = qw
    kv = proj(c, 6 * KV_WIDTH)
    c += 6 * KV_WIDTH
    k_sel = _seg_rmsnorm(kv[:, 2 * KV_WIDTH:3 * KV_WIDTH], gk1_ref[...], HEAD_DIM)
    k_win = _seg_rmsnorm(kv[:, 4 * KV_WIDTH:5 * KV_WIDTH], gk2_ref[...], HEAD_DIM)
    v_sel = kv[:, 3 * KV_WIDTH:4 * KV_WIDTH]
    v_win = kv[:, 5 * KV_WIDTH:6 * KV_WIDTH]
    kv_ref[:, 0:2 * KV_WIDTH] = kv[:, 0:2 * KV_WIDTH]
    kv_ref[:, 2 * KV_WIDTH:3 * KV_WIDTH] = k_sel
    kv_ref[:, 3 * KV_WIDTH:4 * KV_WIDTH] = v_sel
    win_ref[:, 0:KV_WIDTH] = k_win
    win_ref[:, KV_WIDTH:2 * KV_WIDTH] = v_win
    katt_ref[:, 0:KV_WIDTH] = k_sel.astype(bf16)
    katt_ref[:, KV_WIDTH:2 * KV_WIDTH] = v_sel.astype(bf16)
    katt_ref[:, 2 * KV_WIDTH:3 * KV_WIDTH] = k_win.astype(bf16)
    katt_ref[:, 3 * KV_WIDTH:4 * KV_WIDTH] = v_win.astype(bf16)
    h4_ref[...] = proj(c, 4 * HG_WIDTH)
    c += 4 * HG_WIDTH
    mg_ref[...] = jax.nn.sigmoid(proj(c, 2 * D_MODEL))
    c += 2 * D_MODEL
    gl_ref[...] = jax.nn.sigmoid(proj(c, LANES))


def _in_proj(x2d, g1, w, gq, gk1, gk2, *, qw, q_seg, tm):
    n = x2d.shape[0]
    cw = w.shape[1]
    row = lambda width: pl.BlockSpec((tm, width), lambda i: (i, 0))
    full = lambda a: pl.BlockSpec(a.shape, lambda i: (0,) * a.ndim)
    widths = (qw, 4 * KV_WIDTH, 2 * KV_WIDTH, 4 * KV_WIDTH, 4 * HG_WIDTH, 2 * D_MODEL, LANES)
    dtypes = (bf16, f32, f32, bf16, f32, f32, f32)
    return pl.pallas_call(
        functools.partial(_in_proj_kernel, qw=qw, q_seg=q_seg),
        out_shape=tuple(jax.ShapeDtypeStruct((n, wd), dt) for wd, dt in zip(widths, dtypes)),
        grid=(n // tm,),
        in_specs=[row(D_MODEL), full(g1), pl.BlockSpec((D_MODEL, cw), lambda i: (0, 0)), full(gq), full(gk1), full(gk2)],
        out_specs=tuple(row(wd) for wd in widths),
        compiler_params=_params(("arbitrary",)),
        name="in_proj",
    )(x2d, g1, w, gq, gk1, gk2)


def _compress(src_ref, w1_ref, w2_ref, gk0, n_cmp):
    half = n_cmp // 2
    acc = jnp.zeros((n_cmp, 2 * LANES), f32)
    for j in range(CMP_BLOCK):
        def rows(s, start):
            return src_ref[s, pl.ds(start, half, stride=2 * CMP_BLOCK), :]
        xe = jnp.concatenate([rows(0, j), rows(1, j)], axis=1)
        xo = jnp.concatenate([rows(0, CMP_BLOCK + j), rows(1, CMP_BLOCK + j)], axis=1)
        xj = jnp.concatenate([xe, xo], axis=0).astype(bf16)
        acc = acc + jnp.dot(xj, w1_ref[j], preferred_element_type=f32)
    hmid = jax.nn.gelu(acc).astype(bf16)
    y = jnp.dot(hmid, w2_ref[...], preferred_element_type=f32)
    ck = _seg_rmsnorm(y[:, 0:LANES], gk0, HEAD_DIM)
    return jnp.concatenate([ck, y[:, LANES:2 * LANES]], axis=1)


def _compress_prompt_kernel(kv_ref, w1_ref, w2_ref, gk0_ref, out_ref, src_ref, *, n_cmp):
    src_ref[0] = kv_ref[:, 0:LANES]
    src_ref[1] = kv_ref[:, LANES:2 * LANES]
    out_ref[...] = _compress(src_ref, w1_ref, w2_ref, gk0_ref[...], n_cmp).astype(bf16)


def _compress_prompt(kv32, w1bd, w2bd, gk0, B, L):
    n_cmp = L // CMP_BLOCK
    return pl.pallas_call(
        functools.partial(_compress_prompt_kernel, n_cmp=n_cmp),
        out_shape=jax.ShapeDtypeStruct((B, n_cmp, 2 * LANES), bf16),
        grid=(B,),
        in_specs=[pl.BlockSpec((L, 4 * KV_WIDTH), lambda b: (b, 0)),
                  pl.BlockSpec(w1bd.shape, lambda b: (0, 0, 0)),
                  pl.BlockSpec(w2bd.shape, lambda b: (0, 0)),
                  pl.BlockSpec(gk0.shape, lambda b: (0, 0))],
        out_specs=pl.BlockSpec((None, n_cmp, 2 * LANES), lambda b: (b, 0, 0)),
        scratch_shapes=[pltpu.VMEM((2, L, LANES), f32)],
        compiler_params=_params(("arbitrary",)),
        name="compress_prompt",
    )(kv32, w1bd, w2bd, gk0)


def _select_blocks(score, valid, n_keep):
    nb = score.shape[1]
    jj = lax.broadcasted_iota(i32, score.shape, 1)
    cnt = jnp.zeros(score.shape, f32)
    for k in range(nb):
        sk = score[:, k:k + 1]
        better = (sk > score) | ((sk == score) & (jj > k))
        cnt = cnt + jnp.where(better, 1.0, 0.0)
    return (cnt < n_keep) & valid


def _nsa_prompt_kernel(q_ref, gate_ref, katt_ref, cmp_ref, tc_ref, tt_ref, ex_ref, o_ref, selx_ref, *, L):
    i = pl.program_id(1)
    n_cmp = L // CMP_BLOCK
    n_sel = L // SEL_BLOCK
    qb = SEL_BLOCK
    rows = NSA_GROUP * qb
    gates = gate_ref[...]
    row64 = lax.broadcasted_iota(i32, (qb, LANES), 0)
    lane = lax.broadcasted_iota(i32, (qb, LANES), 1)
    qpos = i * qb + row64

    for g in range(NSA_KV_HEADS):
        qg = jnp.concatenate(
            [q_ref[:, (NSA_GROUP * g + r) * HEAD_DIM:(NSA_GROUP * g + r + 1) * HEAD_DIM] for r in range(NSA_GROUP)],
            axis=0)

        ck = cmp_ref[:, g * HEAD_DIM:(g + 1) * HEAD_DIM]
        cv = cmp_ref[:, LANES + g * HEAD_DIM:LANES + (g + 1) * HEAD_DIM]
        s = lax.dot_general(qg, ck, NT, preferred_element_type=f32) * SCALE + tc_ref[g]
        col = lax.broadcasted_iota(i32, (rows, n_cmp), 1)
        rq = lax.broadcasted_iota(i32, (rows, n_cmp), 0) & (qb - 1)
        blk = jnp.where(col < n_cmp // 2, 2 * col, 2 * col - (n_cmp - 1))
        vis = (blk + 1) * CMP_BLOCK - 1 <= i * qb + rq
        e, den = _softmax_parts(s, vis)
        p = e / jnp.where(den > 0, den, 1.0)
        o_c = jnp.dot(p.astype(bf16), cv, preferred_element_type=f32)

        ps = p[0:qb] + p[qb:2 * qb] + p[2 * qb:3 * qb] + p[3 * qb:4 * qb]
        imp = ps[:, 0:n_sel] + ps[:, n_sel:2 * n_sel]
        jj = lax.broadcasted_iota(i32, (qb, n_sel), 1)
        valid = jj <= i
        forced = valid & ((jj == 0) | (jj >= i - 1))
        score = jnp.where(forced, FORCED_SCORE, jnp.where(valid, imp, INVALID_SCORE))
        sel = _select_blocks(score, valid, min(TOP_N, n_sel))
        selx = jnp.dot(jnp.where(sel, 1.0, 0.0).astype(bf16), ex_ref[...], preferred_element_type=f32)
        for t in range(L // LANES):
            selx_ref[t] = selx[:, t * LANES:(t + 1) * LANES]

        def flash(k_off, v_off, t_lo, t_hi, mask_fn):
            def step(t, carry):
                m, l, acc = carry
                r0 = pl.multiple_of(t * LANES, LANES)
                kt = katt_ref[pl.ds(r0, LANES), k_off:k_off + HEAD_DIM]
                vt = katt_ref[pl.ds(r0, LANES), v_off:v_off + HEAD_DIM]
                s = lax.dot_general(qg, kt, NT, preferred_element_type=f32) * SCALE
                s = s + tt_ref[jnp.minimum(i - 2 * t, 4), g]
                msk = mask_fn(t, t * LANES + lane)
                s3 = jnp.where(msk[None], s.reshape(NSA_GROUP, qb, LANES), NEG)
                m_new = jnp.maximum(m, jnp.max(s3, axis=-1, keepdims=True))
                pt = jnp.where(msk[None], jnp.exp(s3 - m_new), 0.0)
                alpha = jnp.exp(m - m_new)
                l = alpha * l + jnp.sum(pt, axis=-1, keepdims=True)
                pv = jnp.dot(pt.reshape(rows, LANES).astype(bf16), vt, preferred_element_type=f32)
                acc = alpha.reshape(rows, 1) * acc + pv
                return m_new, l, acc
            init = (jnp.full((NSA_GROUP, qb, 1), NEG, f32), jnp.zeros((NSA_GROUP, qb, 1), f32),
                    jnp.zeros((rows, HEAD_DIM), f32))
            m, l, acc = lax.fori_loop(t_lo, t_hi, step, init)
            return acc / jnp.where(l > 0, l, 1.0).reshape(rows, 1)

        o_s = flash(g * HEAD_DIM, KV_WIDTH + g * HEAD_DIM, 0, i // 2 + 1,
                    lambda t, kpos: (selx_ref[t] > 0.5) & (kpos <= qpos))
        o_w = flash(2 * KV_WIDTH + g * HEAD_DIM, 3 * KV_WIDTH + g * HEAD_DIM,
                    jnp.maximum(i - WINDOW // qb, 0) // 2, i // 2 + 1,
                    lambda t, kpos: (kpos <= qpos) & (qpos - kpos <= WINDOW))

        for r in range(NSA_GROUP):
            h = NSA_GROUP * g + r
            sl = slice(r * qb, (r + 1) * qb)
            o = (gates[:, 3 * h:3 * h + 1] * o_c[sl] + gates[:, 3 * h + 1:3 * h + 2] * o_s[sl]
                 + gates[:, 3 * h + 2:3 * h + 3] * o_w[sl])
            o_ref[:, h * HEAD_DIM:(h + 1) * HEAD_DIM] = o.astype(bf16)


def _nsa_prompt(qn, gates, katt, cmp, tc, tt, expand, B, L):
    nqb = L // SEL_BLOCK
    qb = SEL_BLOCK
    return pl.pallas_call(
        functools.partial(_nsa_prompt_kernel, L=L),
        out_shape=jax.ShapeDtypeStruct((B * L, NSA_WIDTH), bf16),
        grid=(B, nqb),
        in_specs=[pl.BlockSpec((qb, NSA_WIDTH), lambda b, i: (b * nqb + i, 0)),
                  pl.BlockSpec((qb, LANES), lambda b, i: (b * nqb + i, 0)),
                  pl.BlockSpec((L, 4 * KV_WIDTH), lambda b, i: (b, 0)),
                  pl.BlockSpec((None, L // CMP_BLOCK, 2 * LANES), lambda b, i: (b, 0, 0)),
                  pl.BlockSpec((None,) + tc.shape[1:], lambda b, i: (i, 0, 0, 0)),
                  pl.BlockSpec(tt.shape, lambda b, i: (0, 0, 0, 0)),
                  pl.BlockSpec(expand.shape, lambda b, i: (0, 0))],
        out_specs=pl.BlockSpec((qb, NSA_WIDTH), lambda b, i: (b * nqb + i, 0)),
        scratch_shapes=[pltpu.VMEM((L // LANES, qb, LANES), f32)],
        compiler_params=_params(("arbitrary", "arbitrary")),
        name="nsa_prompt",
    )(qn, gates, katt, cmp, tc, tt, expand)


def _hgrn_gates(hf, lb):
    f = lb + (1.0 - lb) * jax.nn.sigmoid(hf)
    return f, 1.0 - f, jnp.log(f)


def _hgrn_out(o, gn, hgate):
    return _rmsnorm_rows(o, gn) * jax.nn.silu(hgate)


def _hgrn_prompt_kernel(h4_ref, lb_ref, gn_ref, o_ref, s_out_ref, st_ref, *, tc):
    c = pl.program_id(1)
    nb = HG_BLOCK

    @pl.when(c == 0)
    def _():
        st_ref[...] = jnp.zeros_like(st_ref)

    tri = (lax.broadcasted_iota(i32, (nb, nb), 0) >= lax.broadcasted_iota(i32, (nb, nb), 1)).astype(f32)
    t_col = lax.broadcasted_iota(i32, (nb, 1), 0)
    gn = gn_ref[...]

    def block(bi, _):
        r0 = pl.multiple_of(bi * nb, nb)
        for h in range(HG_HEADS):
            sl = lambda part: h4_ref[pl.ds(r0, nb), part * HG_WIDTH + h * HG_DIM:part * HG_WIDTH + (h + 1) * HG_DIM]
            q = sl(0) * HG_SCALE
            f, k, glog = _hgrn_gates(sl(1), lb_ref[:, h * HG_DIM:(h + 1) * HG_DIM])
            v = sl(2)
            b = jnp.dot(tri, glog, precision=lax.Precision.HIGHEST, preferred_element_type=f32)
            st = st_ref[h]
            o = lax.dot_general((q * jnp.exp(b)).astype(bf16), st.astype(bf16), NT, preferred_element_type=f32)
            for s in range(nb):
                w = q * k[s:s + 1] * jnp.exp(jnp.minimum(b - b[s:s + 1], 0.0))
                a = jnp.where(t_col >= s, jnp.sum(w, axis=-1, keepdims=True), 0.0)
                o = o + a * v[s:s + 1]
            b_last = b[nb - 1:nb]
            kt = k * jnp.exp(b_last - b)
            upd = lax.dot_general(v.astype(bf16), kt.astype(bf16), TN, preferred_element_type=f32)
            st_ref[h] = jnp.exp(b_last) * st + upd
            o_ref[pl.ds(r0, nb), h * HG_DIM:(h + 1) * HG_DIM] = _hgrn_out(o, gn, sl(3)).astype(bf16)
        return 0

    lax.fori_loop(0, tc // nb, block, 0)

    @pl.when(c == pl.num_programs(1) - 1)
    def _():
        for h in range(HG_HEADS):
            s_out_ref[h] = st_ref[h].T


def _hgrn_prompt(h4, lb, gn, B, L, tc=256):
    nc = L // tc
    return pl.pallas_call(
        functools.partial(_hgrn_prompt_kernel, tc=tc),
        out_shape=(jax.ShapeDtypeStruct((B * L, HG_WIDTH), bf16),
                   jax.ShapeDtypeStruct((B, HG_HEADS, HG_DIM, HG_DIM), f32)),
        grid=(B, nc),
        in_specs=[pl.BlockSpec((tc, 4 * HG_WIDTH), lambda b, c: (b * nc + c, 0)),
                  pl.BlockSpec(lb.shape, lambda b, c: (0, 0)),
                  pl.BlockSpec(gn.shape, lambda b, c: (0, 0))],
        out_specs=(pl.BlockSpec((tc, HG_WIDTH), lambda b, c: (b * nc + c, 0)),
                   pl.BlockSpec((None, HG_HEADS, HG_DIM, HG_DIM), lambda b, c: (b, 0, 0, 0))),
        scratch_shapes=[pltpu.VMEM((HG_HEADS, HG_DIM, HG_DIM), f32)],
        compiler_params=_params(("arbitrary", "arbitrary")),
        name="hgrn_prompt",
    )(h4, lb, gn)


def _ffn_kernel(*refs, decode, tiles_per_seq):
    if decode:
        (x_ref, on_ref, oh_ref, mg_ref, wn_ref, wh_ref, wo_ref, g2_ref, wg_ref, wu_ref, cw_ref, cb_ref, wd_ref,
         cs0_ref, cs1_ref, y_ref, gate_out_ref, x1_ref, xn2_ref, acc_ref) = refs
    else:
        (x_ref, on_ref, oh_ref, mg_ref, wn_ref, wh_ref, wo_ref, g2_ref, wg_ref, wu_ref, cw_ref, cb_ref, wd_ref,
         y_ref, gate_out_ref, x1_ref, xn2_ref, acc_ref, carry_ref) = refs
    i = pl.program_id(0)
    fi = pl.program_id(1)
    tm = x_ref.shape[0]

    @pl.when(fi == 0)
    def _():
        y_a = jnp.dot(on_ref[...], wn_ref[...], preferred_element_type=f32)
        y_b = jnp.dot(oh_ref[...], wh_ref[...], preferred_element_type=f32)
        mg = mg_ref[...]
        merged = mg[:, 0:D_MODEL] * y_a + mg[:, D_MODEL:2 * D_MODEL] * y_b
        x1 = x_ref[...] + jnp.dot(merged.astype(bf16), wo_ref[...], preferred_element_type=f32)
        x1_ref[...] = x1
        xn2_ref[...] = _rmsnorm_rows(x1, g2_ref[...]).astype(bf16)
        acc_ref[...] = jnp.zeros_like(acc_ref)

    xn2 = xn2_ref[...]
    gate = jnp.dot(xn2, wg_ref[...], preferred_element_type=f32)
    up = jnp.dot(xn2, wu_ref[...], preferred_element_type=f32)
    cw = cw_ref[...]
    if decode:
        prev2, prev1 = cs0_ref[...], cs1_ref[...]
        gate_out_ref[...] = gate
    else:
        @pl.when(i % tiles_per_seq == 0)
        def _():
            carry_ref[fi] = jnp.zeros(carry_ref.shape[1:], f32)
        carry = carry_ref[fi]
        rid = lax.broadcasted_iota(i32, gate.shape, 0)
        prev1 = jnp.where(rid == 0, carry[7:8], pltpu.roll(gate, 1, axis=0))
        prev2 = jnp.where(rid == 0, carry[6:7], jnp.where(rid == 1, carry[7:8], pltpu.roll(gate, 2, axis=0)))
        carry_ref[fi, 6:8, :] = gate[tm - 2:tm]
        gate_out_ref[...] = gate[tm - 2:tm]
    conv = cb_ref[...] + cw[0:1] * prev2 + cw[1:2] * prev1 + cw[2:3] * gate
    hmid = (jax.nn.silu(conv) * up).astype(bf16)
    acc_ref[...] += jnp.dot(hmid, wd_ref[...], preferred_element_type=f32)

    @pl.when(fi == pl.num_programs(1) - 1)
    def _():
        y_ref[...] = x1_ref[...] + acc_ref[...]


def _ffn(x2d, o_nsa, o_hg, mgs, wn, wh, wo, g2, wg, wu, cw, cb, wd, *, tm, tf, seq_len=None, conv_state=None):
    n = x2d.shape[0]
    decode = conv_state is not None
    nf = D_FF // tf
    row = lambda width: pl.BlockSpec((tm, width), lambda i, f: (i, 0))
    full = lambda a: pl.BlockSpec(a.shape, lambda i, f: (0,) * a.ndim)
    in_specs = [row(D_MODEL), row(NSA_WIDTH), row(HG_WIDTH), row(2 * D_MODEL), full(wn), full(wh), full(wo), full(g2),
                pl.BlockSpec((D_MODEL, tf), lambda i, f: (0, f)), pl.BlockSpec((D_MODEL, tf), lambda i, f: (0, f)),
                pl.BlockSpec((3, tf), lambda i, f: (0, f)), pl.BlockSpec((1, tf), lambda i, f: (0, f)),
                pl.BlockSpec((tf, D_MODEL), lambda i, f: (f, 0))]
    args = [x2d, o_nsa, o_hg, mgs, wn, wh, wo, g2, wg, wu, cw, cb, wd]
    scratch = [pltpu.VMEM((tm, D_MODEL), f32), pltpu.VMEM((tm, D_MODEL), bf16), pltpu.VMEM((tm, D_MODEL), f32)]
    if decode:
        in_specs += [pl.BlockSpec((tm, tf), lambda i, f: (i, f))] * 2
        args += [conv_state[:, 0], conv_state[:, 1]]
        gate_shape = jax.ShapeDtypeStruct((n, D_FF), f32)
        gate_spec = pl.BlockSpec((tm, tf), lambda i, f: (i, f))
        tiles_per_seq = 1
    else:
        tiles_per_seq = seq_len // tm
        gate_shape = jax.ShapeDtypeStruct((n // tm, 2, D_FF), f32)
        gate_spec = pl.BlockSpec((None, 2, tf), lambda i, f: (i, 0, f))
        scratch.append(pltpu.VMEM((nf, 8, tf), f32))
    return pl.pallas_call(
        functools.partial(_ffn_kernel, decode=decode, tiles_per_seq=tiles_per_seq),
        out_shape=(jax.ShapeDtypeStruct((n, D_MODEL), f32), gate_shape),
        grid=(n // tm, nf),
        in_specs=in_specs,
        out_specs=(row(D_MODEL), gate_spec),
        scratch_shapes=scratch,
        compiler_params=_params(("arbitrary", "arbitrary")),
        name="ffn_decode" if decode else "ffn_prompt",
    )(*args)


def _nsa_decode_kernel(pt_ref, page_ref, q_ref, kvn_ref, winn_ref, gate_ref, win_ref, w1_ref, w2_ref, gk0_ref,
                       bias_ref, ex_ref, o_ref, win_out_ref, big_ref, src_ref, *, n_pages, n_win):
    del pt_ref
    p = pl.program_id(1)
    past = n_pages * PAGE_SIZE
    n_cmp = past // CMP_BLOCK
    n_sel = past // SEL_BLOCK
    r0 = pl.multiple_of(p * PAGE_SIZE, PAGE_SIZE)
    page = page_ref[...]
    src_ref[0, pl.ds(r0, PAGE_SIZE), :] = page[:, 0:LANES]
    src_ref[1, pl.ds(r0, PAGE_SIZE), :] = page[:, LANES:2 * LANES]
    big_ref[pl.ds(r0, PAGE_SIZE), :] = page[:, 2 * LANES:4 * LANES].astype(bf16)

    @pl.when(p == n_pages - 1)
    def _():
        heads = NSA_HEADS
        q = q_ref[...]
        qf = q.astype(f32)
        hrow = lax.broadcasted_iota(i32, (heads, 1), 0)
        g0 = hrow < NSA_GROUP
        bias = bias_ref[...]
        b_s = bias[:, 0:past]
        b_w = bias[:, past:past + n_win]
        b_c = bias[:, past + n_win:past + n_win + n_cmp]
        b_new = bias[:, past + n_win + LANES:past + n_win + LANES + 1]

        def pick(o2):
            return jnp.where(g0, o2[:, 0:HEAD_DIM], o2[:, HEAD_DIM:2 * HEAD_DIM])

        cmp = _compress(src_ref, w1_ref, w2_ref, gk0_ref[...], n_cmp)
        ck = cmp[:, 0:LANES].astype(bf16)
        cv = cmp[:, LANES:2 * LANES].astype(bf16)
        s_c = lax.dot_general(q, ck, NT, preferred_element_type=f32) * SCALE + b_c
        e_c, den_c = _softmax_parts(s_c, jnp.full(s_c.shape, True))
        p_c = e_c / jnp.where(den_c > 0, den_c, 1.0)
        o_c = pick(jnp.dot(p_c.astype(bf16), cv, preferred_element_type=f32))

        ps = jnp.concatenate([jnp.sum(p_c[0:NSA_GROUP], axis=0, keepdims=True),
                              jnp.sum(p_c[NSA_GROUP:heads], axis=0, keepdims=True)], axis=0)
        imp = ps[:, 0:n_sel] + ps[:, n_sel:2 * n_sel]
        cur = past // SEL_BLOCK
        jrow = lax.broadcasted_iota(i32, (NSA_KV_HEADS, n_sel), 1)
        forced = (jrow == 0) | (jrow >= cur - 1)
        score_row = jnp.where(forced, FORCED_SCORE, imp)
        n_keep = min(TOP_N, n_sel + 1) - 1
        ri = lax.broadcasted_iota(i32, (n_sel, n_sel), 0)
        ci = lax.broadcasted_iota(i32, (n_sel, n_sel), 1)
        sels = []
        for g in range(NSA_KV_HEADS):
            rowb = jnp.broadcast_to(score_row[g:g + 1], (n_sel, n_sel))
            colb = jnp.sum(jnp.where(ri == ci, rowb, 0.0), axis=-1, keepdims=True)
            better = (colb > rowb) | ((colb == rowb) & (ri < ci))
            cnt = jnp.sum(jnp.where(better, 1.0, 0.0), axis=0, keepdims=True)
            sels.append(jnp.where(cnt < n_keep, 1.0, 0.0))
        sel = jnp.concatenate(sels, axis=0).astype(bf16)
        selx = jnp.dot(sel, ex_ref[...], preferred_element_type=f32)
        mask_s = jnp.where(g0, selx[0:1], selx[1:2]) > 0.5

        def with_new(s_past, mask, k_new, v_past, v_new):
            s_new = jnp.sum(qf * k_new, axis=-1, keepdims=True) * SCALE + b_new
            sm = jnp.where(mask, s_past, NEG)
            m = jnp.maximum(jnp.max(sm, axis=-1, keepdims=True), s_new)
            e = jnp.where(mask, jnp.exp(sm - m), 0.0)
            e_new = jnp.exp(s_new - m)
            den = jnp.sum(e, axis=-1, keepdims=True) + e_new
            o2 = jnp.dot(e.astype(bf16), v_past, preferred_element_type=f32) + e_new * v_new
            return pick(o2 / den)

        kvn = kvn_ref[...]
        s_s = lax.dot_general(q, big_ref[:, 0:LANES], NT, preferred_element_type=f32) * SCALE + b_s
        o_s = with_new(s_s, mask_s, kvn[:, 2 * LANES:3 * LANES], big_ref[:, LANES:2 * LANES], kvn[:, 3 * LANES:4 * LANES])

        winn = winn_ref[...]
        k_w = win_ref[:, 0:LANES].astype(bf16)
        v_w = win_ref[:, LANES:2 * LANES].astype(bf16)
        s_w = lax.dot_general(q, k_w, NT, preferred_element_type=f32) * SCALE + b_w
        o_w = with_new(s_w, jnp.full(s_w.shape, True), winn[:, 0:LANES], v_w, winn[:, LANES:2 * LANES])

        gates = gate_ref[...]
        o_ref[...] = (gates[:, 0:1] * o_c + gates[:, 1:2] * o_s + gates[:, 2:3] * o_w).astype(bf16)
        win_out_ref[0:n_win - 1, :] = win_ref[1:n_win, :]
        win_out_ref[n_win - 1:n_win, :] = winn


def _nsa_decode(cache2d, page_table, q_bd, kv32, win32, gates, win_state, w1bd, w2bd, gk0, bias, expand):
    Bs, n_pages = page_table.shape
    n_win = win_state.shape[1]
    past = n_pages * PAGE_SIZE
    per_b = lambda shape: pl.BlockSpec((None,) + shape, lambda b, p, pt: (b,) + (0,) * len(shape))
    const = lambda a: pl.BlockSpec(a.shape, lambda b, p, pt: (0,) * a.ndim)
    grid_spec = pltpu.PrefetchScalarGridSpec(
        num_scalar_prefetch=1,
        grid=(Bs, n_pages),
        in_specs=[pl.BlockSpec((None, PAGE_SIZE, 4 * LANES), lambda b, p, pt: (pt[b, p], 0, 0)),
                  per_b((NSA_HEADS, LANES)), per_b((1, 4 * LANES)), per_b((1, 2 * LANES)), per_b((NSA_HEADS, 3)),
                  per_b((n_win, 2 * LANES)), const(w1bd), const(w2bd), const(gk0), const(bias), const(expand)],
        out_specs=(per_b((NSA_HEADS, HEAD_DIM)), per_b((n_win, 2 * LANES))),
        scratch_shapes=[pltpu.VMEM((past, 2 * LANES), bf16), pltpu.VMEM((2, past, LANES), f32)],
    )
    return pl.pallas_call(
        functools.partial(_nsa_decode_kernel, n_pages=n_pages, n_win=n_win),
        out_shape=(jax.ShapeDtypeStruct((Bs, NSA_HEADS, HEAD_DIM), bf16),
                   jax.ShapeDtypeStruct((Bs, n_win, 2 * LANES), f32)),
        grid_spec=grid_spec,
        compiler_params=_params(("arbitrary", "arbitrary")),
        name="nsa_decode",
    )(page_table, cache2d, q_bd.reshape(Bs, NSA_HEADS, LANES), kv32.reshape(Bs, 1, 4 * LANES),
      win32.reshape(Bs, 1, 2 * LANES), gates[:, 0:3 * NSA_HEADS].reshape(Bs, NSA_HEADS, 3), win_state,
      w1bd, w2bd, gk0, bias, expand)


def _hgrn_decode_kernel(h4_ref, lb_ref, gn_ref, s_ref, o_ref, s_out_ref, ft_ref):
    b = pl.program_id(0)
    nbatch = h4_ref.shape[0]

    @pl.when(b == 0)
    def _():
        for h in range(HG_HEADS):
            f, _, _ = _hgrn_gates(h4_ref[:, HG_WIDTH + h * HG_DIM:HG_WIDTH + (h + 1) * HG_DIM],
                                  lb_ref[:, h * HG_DIM:(h + 1) * HG_DIM])
            ft_ref[h] = f.T

    lane = lax.broadcasted_iota(i32, (HG_DIM, nbatch), 1)
    row = h4_ref[pl.ds(b, 1), :]
    gn = gn_ref[...]
    for h in range(HG_HEADS):
        part = lambda k: row[:, k * HG_WIDTH + h * HG_DIM:k * HG_WIDTH + (h + 1) * HG_DIM]
        f_col = jnp.sum(jnp.where(lane == b, ft_ref[h], 0.0), axis=-1, keepdims=True)
        s_new = f_col * s_ref[h] + (1.0 - f_col) * part(2)
        s_out_ref[h] = s_new
        q = jnp.broadcast_to(part(0) * HG_SCALE, (8, HG_DIM)).astype(bf16)
        o = jnp.dot(q, s_new.astype(bf16), preferred_element_type=f32)[0:1]
        o_ref[:, h * HG_DIM:(h + 1) * HG_DIM] = _hgrn_out(o, gn, part(3)).astype(bf16)


def _hgrn_decode(h4, lb, gn, state):
    Bs = h4.shape[0]
    sspec = pl.BlockSpec((None, HG_HEADS, HG_DIM, HG_DIM), lambda b: (b, 0, 0, 0))
    return pl.pallas_call(
        _hgrn_decode_kernel,
        out_shape=(jax.ShapeDtypeStruct((Bs, 1, HG_WIDTH), bf16),
                   jax.ShapeDtypeStruct((Bs, HG_HEADS, HG_DIM, HG_DIM), f32)),
        grid=(Bs,),
        in_specs=[pl.BlockSpec(h4.shape, lambda b: (0, 0)), pl.BlockSpec(lb.shape, lambda b: (0, 0)),
                  pl.BlockSpec(gn.shape, lambda b: (0, 0)), sspec],
        out_specs=(pl.BlockSpec((None, 1, HG_WIDTH), lambda b: (b, 0, 0)), sspec),
        scratch_shapes=[pltpu.VMEM((HG_HEADS, HG_DIM, Bs), f32)],
        compiler_params=_params(("arbitrary",)),
        name="hgrn_decode",
    )(h4, lb, gn, state)


def _split_w_in(w):
    idx = np.cumsum(SPLITS)[:-1]
    return jnp.split(w, [int(v) for v in idx], axis=1)


def kernel(x_prompt, x_sample, cache_kv, page_table, state_kv_win, state_hgrn, state_conv, rel_table, hg_lb_logits,
           norm1_g, w_in, q_norm_g, k_norm_g, phi_k_w1, phi_k_w2, phi_v_w1, phi_v_w2, hg_norm_g, w_nsa_out, w_hg_out,
           w_o, norm2_g, w_gate, w_up, conv_w, conv_b, w_down):
    assert w_in.shape[0] == 1, "one layer"
    Bp, L, _ = x_prompt.shape
    Bs = x_sample.shape[0]
    n_pages = page_table.shape[1]
    past = n_pages * PAGE_SIZE
    n_win = state_kv_win.shape[2]
    assert x_sample.shape[1] == 1 and n_win == WINDOW and past % (2 * CMP_BLOCK) == 0 and L % LANES == 0

    wq, wkv, wgl, whq, whf, whi, whg, wmg = _split_w_in(w_in[0])
    wgl = jnp.pad(wgl, ((0, 0), (0, LANES - wgl.shape[1])))
    w_prompt = jnp.concatenate([wq, wkv, whq, whf, whi, whg, wmg, wgl], axis=1).astype(bf16)
    wq_h = wq.reshape(D_MODEL, NSA_HEADS, HEAD_DIM)
    zq = jnp.zeros_like(wq_h)
    in_g0 = (np.arange(NSA_HEADS) < NSA_GROUP)[None, :, None]
    wq_bd = jnp.where(in_g0, jnp.concatenate([wq_h, zq], axis=-1), jnp.concatenate([zq, wq_h], axis=-1))
    w_sample = jnp.concatenate([wq_bd.reshape(D_MODEL, NSA_HEADS * LANES), wkv, whq, whf, whi, whg, wmg, wgl],
                               axis=1).astype(bf16)
    g1 = norm1_g[0][None, :]
    gq = q_norm_g[0]
    gq_prompt = jnp.tile(gq, NSA_HEADS)[None, :]
    gq_sample = jnp.tile(gq, 2 * NSA_HEADS)[None, :]
    gk = [jnp.tile(k_norm_g[0, s], NSA_KV_HEADS)[None, :] for s in range(3)]

    def block_diag(mats):
        n = len(mats)
        out = jnp.zeros(mats[0].shape[:-2] + (n * HEAD_DIM, n * HEAD_DIM), f32)
        for t, m in enumerate(mats):
            out = out.at[..., t * HEAD_DIM:(t + 1) * HEAD_DIM, t * HEAD_DIM:(t + 1) * HEAD_DIM].set(m)
        return out.astype(bf16)

    w1bd = block_diag([phi_k_w1[0], phi_k_w1[0], phi_v_w1[0], phi_v_w1[0]])
    w2bd = block_diag([phi_k_w2[0], phi_k_w2[0], phi_v_w2[0], phi_v_w2[0]])
    lb = jnp.cumsum(jax.nn.softmax(hg_lb_logits.astype(f32), axis=0), axis=0)[0][None, :]
    gn = hg_norm_g[0][None, :]
    ffn_w = (w_nsa_out[0].astype(bf16), w_hg_out[0].astype(bf16), w_o[0].astype(bf16), norm2_g[0][None, :],
             w_gate[0].astype(bf16), w_up[0].astype(bf16), conv_w[0], conv_b[0][None, :], w_down[0].astype(bf16))

    idx_c, idx_t = _prompt_bucket_tables(L)
    tc = _bias_tables(rel_table, jnp.asarray(idx_c)).reshape(L // SEL_BLOCK, NSA_KV_HEADS, NSA_GROUP * SEL_BLOCK, -1)
    tt = _bias_tables(rel_table, jnp.asarray(idx_t)).reshape(5, NSA_KV_HEADS, NSA_GROUP * SEL_BLOCK, LANES)
    bias_dec = _bias_tables(rel_table, jnp.asarray(_decode_bucket_table(past, n_win))[None]).reshape(NSA_HEADS, -1)
    ex_prompt = jnp.asarray(_expand_np(L // SEL_BLOCK, L), dtype=bf16)
    ex_sample = jnp.asarray(_expand_np(past // SEL_BLOCK, past), dtype=bf16)

    xp = x_prompt.reshape(Bp * L, D_MODEL)
    qn, kv32, win32, katt, h4, mgs, gates = _in_proj(xp, g1, w_prompt, gq_prompt, gk[1], gk[2],
                                                     qw=NSA_WIDTH, q_seg=HEAD_DIM, tm=256)
    cmp = _compress_prompt(kv32, w1bd, w2bd, gk[0], Bp, L)
    o_nsa = _nsa_prompt(qn, gates, katt, cmp, tc, tt, ex_prompt, Bp, L)
    o_hg, s_prompt = _hgrn_prompt(h4, lb, gn, Bp, L)
    tm_ffn = 512
    yp, gate_tails = _ffn(xp, o_nsa, o_hg, mgs, *ffn_w, tm=tm_ffn, tf=256, seq_len=L)
    conv_p = gate_tails.reshape(Bp, L // tm_ffn, 2, D_FF)[:, -1]
    n_keep_win = min(WINDOW, L)
    win_p = win32.reshape(Bp, L, 2, NSA_KV_HEADS, HEAD_DIM)[:, L - n_keep_win:]

    xs = x_sample.reshape(Bs, D_MODEL)
    q_bd, kv32_s, win32_s, _, h4_s, mgs_s, gates_s = _in_proj(xs, g1, w_sample, gq_sample, gk[1], gk[2],
                                                              qw=NSA_HEADS * LANES, q_seg=LANES, tm=Bs)
    cache2d = cache_kv[0].reshape(cache_kv.shape[1], PAGE_SIZE, 4 * LANES)
    win_state = state_kv_win[0].reshape(Bs, n_win, 2 * LANES)
    o_nsa_s, win_s = _nsa_decode(cache2d, page_table, q_bd, kv32_s, win32_s, gates_s, win_state, w1bd, w2bd, gk[0],
                                 bias_dec, ex_sample)
    o_hg_s, s_sample = _hgrn_decode(h4_s, lb, gn, state_hgrn[0])
    ys, gate_s = _ffn(xs, o_nsa_s.reshape(Bs, NSA_WIDTH), o_hg_s.reshape(Bs, HG_WIDTH), mgs_s, *ffn_w,
                      tm=Bs, tf=256, conv_state=state_conv[0])
    conv_s = jnp.stack([state_conv[0][:, 1], gate_s], axis=1)

    kvh = (NSA_KV_HEADS, HEAD_DIM)
    return (yp.reshape(Bp, L, D_MODEL), ys.reshape(Bs, 1, D_MODEL),
            kv32.reshape((1, Bp, L, 4) + kvh), kv32_s.reshape((1, Bs, 1, 4) + kvh),
            win_p[None], win_s.reshape((1, Bs, n_win, 2) + kvh),
            s_prompt[None], s_sample[None], conv_p[None], conv_s[None])
```

```python
import functools
import math

import numpy as np
import jax
import jax.numpy as jnp
from jax import lax
from jax.experimental import pallas as pl
from jax.experimental.pallas import tpu as pltpu

f32, bf16, i32 = jnp.float32, jnp.bfloat16, jnp.int32

D_MODEL = 1024
PAGE_SIZE = 128
NSA_HEADS, NSA_KV_HEADS, NSA_GROUP, HEAD_DIM = 8, 2, 4, 64
CMP_BLOCK, SEL_BLOCK, TOP_N, WINDOW = 32, 64, 16, 512
SCALE = HEAD_DIM ** -0.5
NSA_WIDTH = NSA_HEADS * HEAD_DIM
KV_WIDTH = NSA_KV_HEADS * HEAD_DIM
HG_HEADS, HG_DIM = 4, 128
HG_WIDTH = HG_HEADS * HG_DIM
HG_SCALE = HG_DIM ** -0.5
HG_BLOCK = 16
REL_BUCKETS, REL_MAX_DIST = 32, 128
D_FF = 2816
EPS = 1e-6
SPLITS = (NSA_WIDTH, 6 * KV_WIDTH, 3 * NSA_HEADS, HG_WIDTH, HG_WIDTH, HG_WIDTH, HG_WIDTH, 2 * D_MODEL)
LANES = 128
NEG = -1e30
FORCED_SCORE, INVALID_SCORE = 8.0, -1.0
VMEM_LIMIT = 56 * 1024 * 1024

NT = (((1,), (1,)), ((), ()))
TN = (((0,), (0,)), ((), ()))


def _params(sem):
    return pltpu.CompilerParams(dimension_semantics=sem, vmem_limit_bytes=VMEM_LIMIT)


def _bucket_np(dist):
    n = np.maximum(dist, 0)
    exact = REL_BUCKETS // 2
    ratio = np.log(np.maximum(n, 1).astype(np.float32) / np.float32(exact)) / np.float32(math.log(REL_MAX_DIST / exact))
    big = exact + (ratio.astype(np.float32) * np.float32(REL_BUCKETS - exact)).astype(np.int32)
    return np.where(n < exact, n, np.minimum(big, REL_BUCKETS - 1)).astype(np.int32)


def _cmp_perm(n_cmp):
    half = n_cmp // 2
    c = np.arange(n_cmp)
    return np.where(c < half, 2 * c, 2 * (c - half) + 1)


def _prompt_bucket_tables(L):
    nqb = L // SEL_BLOCK
    n_cmp = L // CMP_BLOCK
    qi = np.arange(SEL_BLOCK)
    c_end = (_cmp_perm(n_cmp) + 1) * CMP_BLOCK - 1
    idx_c = np.stack([_bucket_np((i * SEL_BLOCK + qi)[:, None] - c_end[None, :]) for i in range(nqb)])
    kj = np.arange(SEL_BLOCK)
    def tile(delta):
        return _bucket_np(delta * SEL_BLOCK + qi[:, None] - kj[None, :])
    idx_t = np.stack([np.concatenate([tile(e), tile(e - 1)], axis=1) for e in range(5)])
    return idx_c.astype(np.int32), idx_t.astype(np.int32)


def _decode_bucket_table(past_len, n_win):
    n_cmp = past_len // CMP_BLOCK
    c_end = (_cmp_perm(n_cmp) + 1) * CMP_BLOCK - 1
    d_c = np.zeros((LANES,), np.int64)
    d_c[:n_cmp] = past_len - c_end
    d_s = past_len - np.arange(past_len)
    d_w = n_win - np.arange(n_win)
    d_new = np.zeros((LANES,), np.int64)
    return _bucket_np(np.concatenate([d_s, d_w, d_c, d_new]))[None, :].astype(np.int32)


def _expand_np(n_blocks, n_keys):
    e = (np.arange(n_keys)[None, :] // SEL_BLOCK == np.arange(n_blocks)[:, None])
    return e.astype(np.float32)


def _seg_rmsnorm(x, gain, seg):
    outs = []
    for t in range(x.shape[1] // LANES):
        xt = x[:, t * LANES:(t + 1) * LANES]
        sq = xt * xt
        if seg == LANES:
            r = lax.rsqrt(jnp.sum(sq, axis=-1, keepdims=True) * (1.0 / HEAD_DIM) + EPS)
        else:
            lo = lax.broadcasted_iota(i32, xt.shape, 1) < HEAD_DIM
            s_lo = jnp.sum(jnp.where(lo, sq, 0.0), axis=-1, keepdims=True)
            s_hi = jnp.sum(jnp.where(lo, 0.0, sq), axis=-1, keepdims=True)
            r = jnp.where(lo, lax.rsqrt(s_lo * (1.0 / HEAD_DIM) + EPS), lax.rsqrt(s_hi * (1.0 / HEAD_DIM) + EPS))
        outs.append(xt * r)
    y = outs[0] if len(outs) == 1 else jnp.concatenate(outs, axis=1)
    return y * gain


def _rmsnorm_rows(x, gain):
    return x * lax.rsqrt(jnp.mean(x * x, axis=-1, keepdims=True) + EPS) * gain


def _softmax_parts(s, mask):
    s = jnp.where(mask, s, NEG)
    m = jnp.max(s, axis=-1, keepdims=True)
    e = jnp.where(mask, jnp.exp(s - m), 0.0)
    return e, jnp.sum(e, axis=-1, keepdims=True)


def _bias_table_kernel(tab_ref, idx_ref, out_ref):
    idx = idx_ref[...]
    for h in range(NSA_HEADS):
        acc = jnp.zeros(idx.shape, f32)
        for b in range(REL_BUCKETS):
            acc = jnp.where(idx == b, tab_ref[b, h], acc)
        out_ref[h] = acc


def _bias_tables(rel_table, idx):
    n, rows, w = idx.shape
    return pl.pallas_call(
        _bias_table_kernel,
        out_shape=jax.ShapeDtypeStruct((n, NSA_HEADS, rows, w), f32),
        grid=(n,),
        in_specs=[pl.BlockSpec(memory_space=pltpu.SMEM),
                  pl.BlockSpec((None, rows, w), lambda i: (i, 0, 0))],
        out_specs=pl.BlockSpec((None, NSA_HEADS, rows, w), lambda i: (i, 0, 0, 0)),
        compiler_params=_params(("arbitrary",)),
        name="bias_tables",
    )(rel_table, idx)


def _in_proj_kernel(x_ref, g1_ref, w_ref, gq_ref, gk1_ref, gk2_ref,
                    q_ref, kv_ref, win_ref, katt_ref, h4_ref, mg_ref, gl_ref, *, qw, q_seg):
    x = x_ref[...]
    xn = _rmsnorm_rows(x, g1_ref[...]).astype(bf16)

    def proj(c0, width):
        return jnp.dot(xn, w_ref[:, c0:c0 + width], preferred_element_type=f32)

    q_ref[...] = _seg_rmsnorm(proj(0, qw), gq_ref[...], q_seg).astype(bf16)
    c = qw
    kv = proj(c, 6 * KV_WIDTH)
    c += 6 * KV_WIDTH
    k_sel = _seg_rmsnorm(kv[:, 2 * KV_WIDTH:3 * KV_WIDTH], gk1_ref[...], HEAD_DIM)
    k_win = _seg_rmsnorm(kv[:, 4 * KV_WIDTH:5 * KV_WIDTH], gk2_ref[...], HEAD_DIM)
    v_sel = kv[:, 3 * KV_WIDTH:4 * KV_WIDTH]
    v_win = kv[:, 5 * KV_WIDTH:6 * KV_WIDTH]
    kv_ref[:, 0:2 * KV_WIDTH] = kv[:, 0:2 * KV_WIDTH]
    kv_ref[:, 2 * KV_WIDTH:3 * KV_WIDTH] = k_sel
    kv_ref[:, 3 * KV_WIDTH:4 * KV_WIDTH] = v_sel
    win_ref[:, 0:KV_WIDTH] = k_win
    win_ref[:, KV_WIDTH:2 * KV_WIDTH] = v_win
    katt_ref[:, 0:KV_WIDTH] = k_sel.astype(bf16)
    katt_ref[:, KV_WIDTH:2 * KV_WIDTH] = v_sel.astype(bf16)
    katt_ref[:, 2 * KV_WIDTH:3 * KV_WIDTH] = k_win.astype(bf16)
    katt_ref[:, 3 * KV_WIDTH:4 * KV_WIDTH] = v_win.astype(bf16)
    h4_ref[...] = proj(c, 4 * HG_WIDTH)
    c += 4 * HG_WIDTH
    mg_ref[...] = jax.nn.sigmoid(proj(c, 2 * D_MODEL))
    c += 2 * D_MODEL
    gl_ref[...] = jax.nn.sigmoid(proj(c, LANES))


def _in_proj(x2d, g1, w, gq, gk1, gk2, *, qw, q_seg, tm):
    n = x2d.shape[0]
    cw = w.shape[1]
    row = lambda width: pl.BlockSpec((tm, width), lambda i: (i, 0))
    full = lambda a: pl.BlockSpec(a.shape, lambda i: (0,) * a.ndim)
    widths = (qw, 4 * KV_WIDTH, 2 * KV_WIDTH, 4 * KV_WIDTH, 4 * HG_WIDTH, 2 * D_MODEL, LANES)
    dtypes = (bf16, f32, f32, bf16, f32, f32, f32)
    return pl.pallas_call(
        functools.partial(_in_proj_kernel, qw=qw, q_seg=q_seg),
        out_shape=tuple(jax.ShapeDtypeStruct((n, wd), dt) for wd, dt in zip(widths, dtypes)),
        grid=(n // tm,),
        in_specs=[row(D_MODEL), full(g1), pl.BlockSpec((D_MODEL, cw), lambda i: (0, 0)), full(gq), full(gk1), full(gk2)],
        out_specs=tuple(row(wd) for wd in widths),
        compiler_params=_params(("arbitrary",)),
        name="in_proj",
    )(x2d, g1, w, gq, gk1, gk2)


def _compress(src_ref, w1_ref, w2_ref, gk0, n_cmp):
    half = n_cmp // 2
    acc = jnp.zeros((n_cmp, 2 * LANES), f32)
    for j in range(CMP_BLOCK):
        def rows(s, start):
            return src_ref[s, pl.ds(start, half, stride=2 * CMP_BLOCK), :]
        xe = jnp.concatenate([rows(0, j), rows(1, j)], axis=1)
        xo = jnp.concatenate([rows(0, CMP_BLOCK + j), rows(1, CMP_BLOCK + j)], axis=1)
        xj = jnp.concatenate([xe, xo], axis=0).astype(bf16)
        acc = acc + jnp.dot(xj, w1_ref[j], preferred_element_type=f32)
    hmid = jax.nn.gelu(acc).astype(bf16)
    y = jnp.dot(hmid, w2_ref[...], preferred_element_type=f32)
    ck = _seg_rmsnorm(y[:, 0:LANES], gk0, HEAD_DIM)
    return jnp.concatenate([ck, y[:, LANES:2 * LANES]], axis=1)


def _compress_prompt_kernel(kv_ref, w1_ref, w2_ref, gk0_ref, out_ref, src_ref, *, n_cmp):
    src_ref[0] = kv_ref[:, 0:LANES]
    src_ref[1] = kv_ref[:, LANES:2 * LANES]
    out_ref[...] = _compress(src_ref, w1_ref, w2_ref, gk0_ref[...], n_cmp).astype(bf16)


def _compress_prompt(kv32, w1bd, w2bd, gk0, B, L):
    n_cmp = L // CMP_BLOCK
    return pl.pallas_call(
        functools.partial(_compress_prompt_kernel, n_cmp=n_cmp),
        out_shape=jax.ShapeDtypeStruct((B, n_cmp, 2 * LANES), bf16),
        grid=(B,),
        in_specs=[pl.BlockSpec((L, 4 * KV_WIDTH), lambda b: (b, 0)),
                  pl.BlockSpec(w1bd.shape, lambda b: (0, 0, 0)),
                  pl.BlockSpec(w2bd.shape, lambda b: (0, 0)),
                  pl.BlockSpec(gk0.shape, lambda b: (0, 0))],
        out_specs=pl.BlockSpec((None, n_cmp, 2 * LANES), lambda b: (b, 0, 0)),
        scratch_shapes=[pltpu.VMEM((2, L, LANES), f32)],
        compiler_params=_params(("arbitrary",)),
        name="compress_prompt",
    )(kv32, w1bd, w2bd, gk0)


def _select_blocks_t(score, valid, n_keep):
    nb = score.shape[0]
    jj = lax.broadcasted_iota(i32, score.shape, 0)
    cnt = jnp.zeros(score.shape, f32)
    for k in range(nb):
        sk = score[k:k + 1, :]
        better = (sk > score) | ((sk == score) & (jj > k))
        cnt = cnt + jnp.where(better, 1.0, 0.0)
    return (cnt < n_keep) & valid


def _sel_tile_classes(n_tiles):
    return sorted({-(-n_tiles * c // 4) for c in range(1, 5)})


def _nsa_prompt_kernel(q_ref, gate_ref, katt_ref, cmp_ref, tc_ref, tt_ref, ex_ref, o_ref,
                       selx_ref, s_ref, os_ref, *, L):
    i = pl.program_id(1)
    n_cmp = L // CMP_BLOCK
    n_sel = L // SEL_BLOCK
    n_tiles = L // LANES
    qb = SEL_BLOCK
    rows = NSA_GROUP * qb
    groups = range(NSA_KV_HEADS)
    gates = gate_ref[...]
    row64 = lax.broadcasted_iota(i32, (qb, LANES), 0)
    lane = lax.broadcasted_iota(i32, (qb, LANES), 1)
    qpos = i * qb + row64

    qs = [jnp.concatenate(
        [q_ref[:, (NSA_GROUP * g + r) * HEAD_DIM:(NSA_GROUP * g + r + 1) * HEAD_DIM] for r in range(NSA_GROUP)],
        axis=0) for g in groups]

    o_c, imp_t = [], []
    col = lax.broadcasted_iota(i32, (rows, n_cmp), 1)
    rq = lax.broadcasted_iota(i32, (rows, n_cmp), 0) & (qb - 1)
    blk = jnp.where(col < n_cmp // 2, 2 * col, 2 * col - (n_cmp - 1))
    vis = (blk + 1) * CMP_BLOCK - 1 <= i * qb + rq
    for g in groups:
        ck = cmp_ref[:, g * HEAD_DIM:(g + 1) * HEAD_DIM]
        cv = cmp_ref[:, LANES + g * HEAD_DIM:LANES + (g + 1) * HEAD_DIM]
        s = lax.dot_general(qs[g], ck, NT, preferred_element_type=f32) * SCALE + tc_ref[g]
        e, den = _softmax_parts(s, vis)
        p = e / jnp.where(den > 0, den, 1.0)
        o_c.append(jnp.dot(p.astype(bf16), cv, preferred_element_type=f32))
        ps_t = (p[0:qb] + p[qb:2 * qb] + p[2 * qb:3 * qb] + p[3 * qb:4 * qb]).T
        imp_t.append(ps_t[0:n_sel] + ps_t[n_sel:2 * n_sel])

    imp = jnp.concatenate(imp_t, axis=1)
    jj = lax.broadcasted_iota(i32, imp.shape, 0)
    valid = jj <= i
    forced = valid & ((jj == 0) | (jj >= i - 1))
    score = jnp.where(forced, FORCED_SCORE, jnp.where(valid, imp, INVALID_SCORE))
    sel = _select_blocks_t(score, valid, min(TOP_N, n_sel))
    selx = lax.dot_general(jnp.where(sel, 1.0, 0.0).astype(bf16), ex_ref[...], TN, preferred_element_type=f32)
    for g in groups:
        for t in range(n_tiles):
            selx_ref[g, t] = selx[g * qb:(g + 1) * qb, t * LANES:(t + 1) * LANES]

    def attend(k_off, v_off, tile0, n_t, mask_fn):
        def scores(t, mruns):
            ta = tile0 + t
            r0 = pl.multiple_of(ta * LANES, LANES)
            bias_idx = jnp.clip(i - 2 * ta, 0, 4)
            kpos = ta * LANES + lane
            out = []
            for g in groups:
                kt = katt_ref[pl.ds(r0, LANES), k_off + g * HEAD_DIM:k_off + (g + 1) * HEAD_DIM]
                s = lax.dot_general(qs[g], kt, NT, preferred_element_type=f32) * SCALE + tt_ref[bias_idx, g]
                msk = mask_fn(g, ta, kpos)
                s = jnp.where(msk[None], s.reshape(NSA_GROUP, qb, LANES), NEG).reshape(rows, LANES)
                s_ref[g, t] = s
                out.append(jnp.maximum(mruns[g], s))
            return tuple(out)

        mruns = lax.fori_loop(0, n_t, scores, tuple(jnp.full((rows, LANES), NEG, f32) for _ in groups), unroll=2)
        ms = [jnp.max(m, axis=-1, keepdims=True) for m in mruns]

        def values(t, carry):
            r0 = pl.multiple_of((tile0 + t) * LANES, LANES)
            ls, accs = carry
            new_l, new_acc = [], []
            for g in groups:
                vt = katt_ref[pl.ds(r0, LANES), v_off + g * HEAD_DIM:v_off + (g + 1) * HEAD_DIM]
                p = jnp.exp(s_ref[g, t] - ms[g])
                new_l.append(ls[g] + p)
                new_acc.append(accs[g] + jnp.dot(p.astype(bf16), vt, preferred_element_type=f32))
            return tuple(new_l), tuple(new_acc)

        init = (tuple(jnp.zeros((rows, LANES), f32) for _ in groups),
                tuple(jnp.zeros((rows, HEAD_DIM), f32) for _ in groups))
        ls, accs = lax.fori_loop(0, n_t, values, init, unroll=2)
        return [accs[g] / jnp.sum(ls[g], axis=-1, keepdims=True) for g in groups]

    need = i // 2 + 1
    prev = 0
    for n_t in _sel_tile_classes(n_tiles):
        @pl.when((need > prev) & (need <= n_t))
        def _(n_t=n_t):
            o_s = attend(0, KV_WIDTH, 0, n_t, lambda g, ta, kpos: (selx_ref[g, ta] > 0.5) & (kpos <= qpos))
            for g in groups:
                os_ref[g] = o_s[g]
        prev = n_t

    n_wt = min(WINDOW // LANES + 1, n_tiles)
    w0 = jnp.clip((i - WINDOW // qb) // 2, 0, n_tiles - n_wt)
    o_w = attend(2 * KV_WIDTH, 3 * KV_WIDTH, w0, n_wt,
                 lambda g, ta, kpos: (kpos <= qpos) & (qpos - kpos <= WINDOW))

    for g in groups:
        o_s = os_ref[g]
        for r in range(NSA_GROUP):
            h = NSA_GROUP * g + r
            sl = slice(r * qb, (r + 1) * qb)
            o = (gates[:, 3 * h:3 * h + 1] * o_c[g][sl] + gates[:, 3 * h + 1:3 * h + 2] * o_s[sl]
                 + gates[:, 3 * h + 2:3 * h + 3] * o_w[g][sl])
            o_ref[:, h * HEAD_DIM:(h + 1) * HEAD_DIM] = o.astype(bf16)


def _nsa_prompt(qn, gates, katt, cmp, tc, tt, expand, B, L):
    nqb = L // SEL_BLOCK
    qb = SEL_BLOCK
    n_tiles = L // LANES
    rows = NSA_GROUP * qb
    return pl.pallas_call(
        functools.partial(_nsa_prompt_kernel, L=L),
        out_shape=jax.ShapeDtypeStruct((B * L, NSA_WIDTH), bf16),
        grid=(B, nqb),
        in_specs=[pl.BlockSpec((qb, NSA_WIDTH), lambda b, i: (b * nqb + i, 0)),
                  pl.BlockSpec((qb, LANES), lambda b, i: (b * nqb + i, 0)),
                  pl.BlockSpec((L, 4 * KV_WIDTH), lambda b, i: (b, 0)),
                  pl.BlockSpec((None, L // CMP_BLOCK, 2 * LANES), lambda b, i: (b, 0, 0)),
                  pl.BlockSpec((None,) + tc.shape[1:], lambda b, i: (i, 0, 0, 0)),
                  pl.BlockSpec(tt.shape, lambda b, i: (0, 0, 0, 0)),
                  pl.BlockSpec(expand.shape, lambda b, i: (0, 0))],
        out_specs=pl.BlockSpec((qb, NSA_WIDTH), lambda b, i: (b * nqb + i, 0)),
        scratch_shapes=[pltpu.VMEM((NSA_KV_HEADS, n_tiles, qb, LANES), f32),
                        pltpu.VMEM((NSA_KV_HEADS, n_tiles, rows, LANES), f32),
                        pltpu.VMEM((NSA_KV_HEADS, rows, HEAD_DIM), f32)],
        compiler_params=_params(("arbitrary", "arbitrary")),
        name="nsa_prompt",
    )(qn, gates, katt, cmp, tc, tt, expand)


def _hgrn_gates(hf, lb):
    f = lb + (1.0 - lb) * jax.nn.sigmoid(hf)
    return f, 1.0 - f, jnp.log(f)


def _hgrn_out(o, gn, hgate):
    return _rmsnorm_rows(o, gn) * jax.nn.silu(hgate)


def _hgrn_prompt_kernel(h4_ref, lb_ref, gn_ref, o_ref, s_out_ref, st_ref, *, tc):
    c = pl.program_id(1)
    nb = HG_BLOCK

    @pl.when(c == 0)
    def _():
        st_ref[...] = jnp.zeros_like(st_ref)

    tri = (lax.broadcasted_iota(i32, (nb, nb), 0) >= lax.broadcasted_iota(i32, (nb, nb), 1)).astype(f32)
    t_col = lax.broadcasted_iota(i32, (nb, 1), 0)
    gn = gn_ref[...]

    def block(bi, _):
        r0 = pl.multiple_of(bi * nb, nb)
        for h in range(HG_HEADS):
            sl = lambda part: h4_ref[pl.ds(r0, nb), part * HG_WIDTH + h * HG_DIM:part * HG_WIDTH + (h + 1) * HG_DIM]
            q = sl(0) * HG_SCALE
            f, k, glog = _hgrn_gates(sl(1), lb_ref[:, h * HG_DIM:(h + 1) * HG_DIM])
            v = sl(2)
            b = jnp.dot(tri, glog, precision=lax.Precision.HIGHEST, preferred_element_type=f32)
            st = st_ref[h]
            o = lax.dot_general((q * jnp.exp(b)).astype(bf16), st.astype(bf16), NT, preferred_element_type=f32)
            for s in range(nb):
                w = q * k[s:s + 1] * jnp.exp(jnp.minimum(b - b[s:s + 1], 0.0))
                a = jnp.where(t_col >= s, jnp.sum(w, axis=-1, keepdims=True), 0.0)
                o = o + a * v[s:s + 1]
            b_last = b[nb - 1:nb]
            kt = k * jnp.exp(b_last - b)
            upd = lax.dot_general(v.astype(bf16), kt.astype(bf16), TN, preferred_element_type=f32)
            st_ref[h] = jnp.exp(b_last) * st + upd
            o_ref[pl.ds(r0, nb), h * HG_DIM:(h + 1) * HG_DIM] = _hgrn_out(o, gn, sl(3)).astype(bf16)
        return 0

    lax.fori_loop(0, tc // nb, block, 0)

    @pl.when(c == pl.num_programs(1) - 1)
    def _():
        for h in range(HG_HEADS):
            s_out_ref[h] = st_ref[h].T


def _hgrn_prompt(h4, lb, gn, B, L, tc=256):
    nc = L // tc
    return pl.pallas_call(
        functools.partial(_hgrn_prompt_kernel, tc=tc),
        out_shape=(jax.ShapeDtypeStruct((B * L, HG_WIDTH), bf16),
                   jax.ShapeDtypeStruct((B, HG_HEADS, HG_DIM, HG_DIM), f32)),
        grid=(B, nc),
        in_specs=[pl.BlockSpec((tc, 4 * HG_WIDTH), lambda b, c: (b * nc + c, 0)),
                  pl.BlockSpec(lb.shape, lambda b, c: (0, 0)),
                  pl.BlockSpec(gn.shape, lambda b, c: (0, 0))],
        out_specs=(pl.BlockSpec((tc, HG_WIDTH), lambda b, c: (b * nc + c, 0)),
                   pl.BlockSpec((None, HG_HEADS, HG_DIM, HG_DIM), lambda b, c: (b, 0, 0, 0))),
        scratch_shapes=[pltpu.VMEM((HG_HEADS, HG_DIM, HG_DIM), f32)],
        compiler_params=_params(("arbitrary", "arbitrary")),
        name="hgrn_prompt",
    )(h4, lb, gn)


def _ffn_kernel(*refs, decode, tiles_per_seq):
    if decode:
        (x_ref, on_ref, oh_ref, mg_ref, wn_ref, wh_ref, wo_ref, g2_ref, wg_ref, wu_ref, cw_ref, cb_ref, wd_ref,
         cs0_ref, cs1_ref, y_ref, gate_out_ref, x1_ref, xn2_ref, acc_ref) = refs
    else:
        (x_ref, on_ref, oh_ref, mg_ref, wn_ref, wh_ref, wo_ref, g2_ref, wg_ref, wu_ref, cw_ref, cb_ref, wd_ref,
         y_ref, gate_out_ref, x1_ref, xn2_ref, acc_ref, carry_ref) = refs
    i = pl.program_id(0)
    fi = pl.program_id(1)
    tm = x_ref.shape[0]

    @pl.when(fi == 0)
    def _():
        y_a = jnp.dot(on_ref[...], wn_ref[...], preferred_element_type=f32)
        y_b = jnp.dot(oh_ref[...], wh_ref[...], preferred_element_type=f32)
        mg = mg_ref[...]
        merged = mg[:, 0:D_MODEL] * y_a + mg[:, D_MODEL:2 * D_MODEL] * y_b
        x1 = x_ref[...] + jnp.dot(merged.astype(bf16), wo_ref[...], preferred_element_type=f32)
        x1_ref[...] = x1
        xn2_ref[...] = _rmsnorm_rows(x1, g2_ref[...]).astype(bf16)
        acc_ref[...] = jnp.zeros_like(acc_ref)

    xn2 = xn2_ref[...]
    gate = jnp.dot(xn2, wg_ref[...], preferred_element_type=f32)
    up = jnp.dot(xn2, wu_ref[...], preferred_element_type=f32)
    cw = cw_ref[...]
    if decode:
        prev2, prev1 = cs0_ref[...], cs1_ref[...]
        gate_out_ref[...] = gate
    else:
        @pl.when(i % tiles_per_seq == 0)
        def _():
            carry_ref[fi] = jnp.zeros(carry_ref.shape[1:], f32)
        carry = carry_ref[fi]
        rid = lax.broadcasted_iota(i32, gate.shape, 0)
        prev1 = jnp.where(rid == 0, carry[7:8], pltpu.roll(gate, 1, axis=0))
        prev2 = jnp.where(rid == 0, carry[6:7], jnp.where(rid == 1, carry[7:8], pltpu.roll(gate, 2, axis=0)))
        carry_ref[fi, 6:8, :] = gate[tm - 2:tm]
        gate_out_ref[...] = gate[tm - 2:tm]
    conv = cb_ref[...] + cw[0:1] * prev2 + cw[1:2] * prev1 + cw[2:3] * gate
    hmid = (jax.nn.silu(conv) * up).astype(bf16)
    acc_ref[...] += jnp.dot(hmid, wd_ref[...], preferred_element_type=f32)

    @pl.when(fi == pl.num_programs(1) - 1)
    def _():
        y_ref[...] = x1_ref[...] + acc_ref[...]


def _ffn(x2d, o_nsa, o_hg, mgs, wn, wh, wo, g2, wg, wu, cw, cb, wd, *, tm, tf, seq_len=None, conv_state=None):
    n = x2d.shape[0]
    decode = conv_state is not None
    nf = D_FF // tf
    row = lambda width: pl.BlockSpec((tm, width), lambda i, f: (i, 0))
    full = lambda a: pl.BlockSpec(a.shape, lambda i, f: (0,) * a.ndim)
    in_specs = [row(D_MODEL), row(NSA_WIDTH), row(HG_WIDTH), row(2 * D_MODEL), full(wn), full(wh), full(wo), full(g2),
                pl.BlockSpec((D_MODEL, tf), lambda i, f: (0, f)), pl.BlockSpec((D_MODEL, tf), lambda i, f: (0, f)),
                pl.BlockSpec((3, tf), lambda i, f: (0, f)), pl.BlockSpec((1, tf), lambda i, f: (0, f)),
                pl.BlockSpec((tf, D_MODEL), lambda i, f: (f, 0))]
    args = [x2d, o_nsa, o_hg, mgs, wn, wh, wo, g2, wg, wu, cw, cb, wd]
    scratch = [pltpu.VMEM((tm, D_MODEL), f32), pltpu.VMEM((tm, D_MODEL), bf16), pltpu.VMEM((tm, D_MODEL), f32)]
    if decode:
        in_specs += [pl.BlockSpec((tm, tf), lambda i, f: (i, f))] * 2
        args += [conv_state[:, 0], conv_state[:, 1]]
        gate_shape = jax.ShapeDtypeStruct((n, D_FF), f32)
        gate_spec = pl.BlockSpec((tm, tf), lambda i, f: (i, f))
        tiles_per_seq = 1
    else:
        tiles_per_seq = seq_len // tm
        gate_shape = jax.ShapeDtypeStruct((n // tm, 2, D_FF), f32)
        gate_spec = pl.BlockSpec((None, 2, tf), lambda i, f: (i, 0, f))
        scratch.append(pltpu.VMEM((nf, 8, tf), f32))
    return pl.pallas_call(
        functools.partial(_ffn_kernel, decode=decode, tiles_per_seq=tiles_per_seq),
        out_shape=(jax.ShapeDtypeStruct((n, D_MODEL), f32), gate_shape),
        grid=(n // tm, nf),
        in_specs=in_specs,
        out_specs=(row(D_MODEL), gate_spec),
        scratch_shapes=scratch,
        compiler_params=_params(("arbitrary", "arbitrary")),
        name="ffn_decode" if decode else "ffn_prompt",
    )(*args)


def _nsa_decode_kernel(pt_ref, page_ref, q_ref, kvn_ref, winn_ref, gate_ref, win_ref, w1_ref, w2_ref, gk0_ref,
                       bias_ref, ex_ref, o_ref, win_out_ref, big_ref, src_ref, *, n_pages, n_win):
    del pt_ref
    p = pl.program_id(1)
    past = n_pages * PAGE_SIZE
    n_cmp = past // CMP_BLOCK
    n_sel = past // SEL_BLOCK
    r0 = pl.multiple_of(p * PAGE_SIZE, PAGE_SIZE)
    page = page_ref[...]
    src_ref[0, pl.ds(r0, PAGE_SIZE), :] = page[:, 0:LANES]
    src_ref[1, pl.ds(r0, PAGE_SIZE), :] = page[:, LANES:2 * LANES]
    big_ref[pl.ds(r0, PAGE_SIZE), :] = page[:, 2 * LANES:4 * LANES].astype(bf16)

    @pl.when(p == n_pages - 1)
    def _():
        heads = NSA_HEADS
        q = q_ref[...]
        qf = q.astype(f32)
        hrow = lax.broadcasted_iota(i32, (heads, 1), 0)
        g0 = hrow < NSA_GROUP
        bias = bias_ref[...]
        b_s = bias[:, 0:past]
        b_w = bias[:, past:past + n_win]
        b_c = bias[:, past + n_win:past + n_win + n_cmp]
        b_new = bias[:, past + n_win + LANES:past + n_win + LANES + 1]

        def pick(o2):
            return jnp.where(g0, o2[:, 0:HEAD_DIM], o2[:, HEAD_DIM:2 * HEAD_DIM])

        cmp = _compress(src_ref, w1_ref, w2_ref, gk0_ref[...], n_cmp)
        ck = cmp[:, 0:LANES].astype(bf16)
        cv = cmp[:, LANES:2 * LANES].astype(bf16)
        s_c = lax.dot_general(q, ck, NT, preferred_element_type=f32) * SCALE + b_c
        e_c, den_c = _softmax_parts(s_c, jnp.full(s_c.shape, True))
        p_c = e_c / jnp.where(den_c > 0, den_c, 1.0)
        o_c = pick(jnp.dot(p_c.astype(bf16), cv, preferred_element_type=f32))

        ps = jnp.concatenate([jnp.sum(p_c[0:NSA_GROUP], axis=0, keepdims=True),
                              jnp.sum(p_c[NSA_GROUP:heads], axis=0, keepdims=True)], axis=0)
        imp = ps[:, 0:n_sel] + ps[:, n_sel:2 * n_sel]
        cur = past // SEL_BLOCK
        jrow = lax.broadcasted_iota(i32, (NSA_KV_HEADS, n_sel), 1)
        forced = (jrow == 0) | (jrow >= cur - 1)
        score_row = jnp.where(forced, FORCED_SCORE, imp)
        n_keep = min(TOP_N, n_sel + 1) - 1
        ri = lax.broadcasted_iota(i32, (n_sel, n_sel), 0)
        ci = lax.broadcasted_iota(i32, (n_sel, n_sel), 1)
        sels = []
        for g in range(NSA_KV_HEADS):
            rowb = jnp.broadcast_to(score_row[g:g + 1], (n_sel, n_sel))
            colb = jnp.sum(jnp.where(ri == ci, rowb, 0.0), axis=-1, keepdims=True)
            better = (colb > rowb) | ((colb == rowb) & (ri < ci))
            cnt = jnp.sum(jnp.where(better, 1.0, 0.0), axis=0, keepdims=True)
            sels.append(jnp.where(cnt < n_keep, 1.0, 0.0))
        sel = jnp.concatenate(sels, axis=0).astype(bf16)
        selx = jnp.dot(sel, ex_ref[...], preferred_element_type=f32)
        mask_s = jnp.where(g0, selx[0:1], selx[1:2]) > 0.5

        def with_new(s_past, mask, k_new, v_past, v_new):
            s_new = jnp.sum(qf * k_new, axis=-1, keepdims=True) * SCALE + b_new
            sm = jnp.where(mask, s_past, NEG)
            m = jnp.maximum(jnp.max(sm, axis=-1, keepdims=True), s_new)
            e = jnp.where(mask, jnp.exp(sm - m), 0.0)
            e_new = jnp.exp(s_new - m)
            den = jnp.sum(e, axis=-1, keepdims=True) + e_new
            o2 = jnp.dot(e.astype(bf16), v_past, preferred_element_type=f32) + e_new * v_new
            return pick(o2 / den)

        kvn = kvn_ref[...]
        s_s = lax.dot_general(q, big_ref[:, 0:LANES], NT, preferred_element_type=f32) * SCALE + b_s
        o_s = with_new(s_s, mask_s, kvn[:, 2 * LANES:3 * LANES], big_ref[:, LANES:2 * LANES], kvn[:, 3 * LANES:4 * LANES])

        winn = winn_ref[...]
        k_w = win_ref[:, 0:LANES].astype(bf16)
        v_w = win_ref[:, LANES:2 * LANES].astype(bf16)
        s_w = lax.dot_general(q, k_w, NT, preferred_element_type=f32) * SCALE + b_w
        o_w = with_new(s_w, jnp.full(s_w.shape, True), winn[:, 0:LANES], v_w, winn[:, LANES:2 * LANES])

        gates = gate_ref[...]
        o_ref[...] = (gates[:, 0:1] * o_c + gates[:, 1:2] * o_s + gates[:, 2:3] * o_w).astype(bf16)
        win_out_ref[0:n_win - 1, :] = win_ref[1:n_win, :]
        win_out_ref[n_win - 1:n_win, :] = winn


def _nsa_decode(cache2d, page_table, q_bd, kv32, win32, gates, win_state, w1bd, w2bd, gk0, bias, expand):
    Bs, n_pages = page_table.shape
    n_win = win_state.shape[1]
    past = n_pages * PAGE_SIZE
    per_b = lambda shape: pl.BlockSpec((None,) + shape, lambda b, p, pt: (b,) + (0,) * len(shape))
    const = lambda a: pl.BlockSpec(a.shape, lambda b, p, pt: (0,) * a.ndim)
    grid_spec = pltpu.PrefetchScalarGridSpec(
        num_scalar_prefetch=1,
        grid=(Bs, n_pages),
        in_specs=[pl.BlockSpec((None, PAGE_SIZE, 4 * LANES), lambda b, p, pt: (pt[b, p], 0, 0)),
                  per_b((NSA_HEADS, LANES)), per_b((1, 4 * LANES)), per_b((1, 2 * LANES)), per_b((NSA_HEADS, 3)),
                  per_b((n_win, 2 * LANES)), const(w1bd), const(w2bd), const(gk0), const(bias), const(expand)],
        out_specs=(per_b((NSA_HEADS, HEAD_DIM)), per_b((n_win, 2 * LANES))),
        scratch_shapes=[pltpu.VMEM((past, 2 * LANES), bf16), pltpu.VMEM((2, past, LANES), f32)],
    )
    return pl.pallas_call(
        functools.partial(_nsa_decode_kernel, n_pages=n_pages, n_win=n_win),
        out_shape=(jax.ShapeDtypeStruct((Bs, NSA_HEADS, HEAD_DIM), bf16),
                   jax.ShapeDtypeStruct((Bs, n_win, 2 * LANES), f32)),
        grid_spec=grid_spec,
        compiler_params=_params(("arbitrary", "arbitrary")),
        name="nsa_decode",
    )(page_table, cache2d, q_bd.reshape(Bs, NSA_HEADS, LANES), kv32.reshape(Bs, 1, 4 * LANES),
      win32.reshape(Bs, 1, 2 * LANES), gates[:, 0:3 * NSA_HEADS].reshape(Bs, NSA_HEADS, 3), win_state,
      w1bd, w2bd, gk0, bias, expand)


def _hgrn_decode_kernel(h4_ref, lb_ref, gn_ref, s_ref, o_ref, s_out_ref, ft_ref):
    b = pl.program_id(0)
    nbatch = h4_ref.shape[0]

    @pl.when(b == 0)
    def _():
        for h in range(HG_HEADS):
            f, _, _ = _hgrn_gates(h4_ref[:, HG_WIDTH + h * HG_DIM:HG_WIDTH + (h + 1) * HG_DIM],
                                  lb_ref[:, h * HG_DIM:(h + 1) * HG_DIM])
            ft_ref[h] = f.T

    lane = lax.broadcasted_iota(i32, (HG_DIM, nbatch), 1)
    row = h4_ref[pl.ds(b, 1), :]
    gn = gn_ref[...]
    for h in range(HG_HEADS):
        part = lambda k: row[:, k * HG_WIDTH + h * HG_DIM:k * HG_WIDTH + (h + 1) * HG_DIM]
        f_col = jnp.sum(jnp.where(lane == b, ft_ref[h], 0.0), axis=-1, keepdims=True)
        s_new = f_col * s_ref[h] + (1.0 - f_col) * part(2)
        s_out_ref[h] = s_new
        q = jnp.broadcast_to(part(0) * HG_SCALE, (8, HG_DIM)).astype(bf16)
        o = jnp.dot(q, s_new.astype(bf16), preferred_element_type=f32)[0:1]
        o_ref[:, h * HG_DIM:(h + 1) * HG_DIM] = _hgrn_out(o, gn, part(3)).astype(bf16)


def _hgrn_decode(h4, lb, gn, state):
    Bs = h4.shape[0]
    sspec = pl.BlockSpec((None, HG_HEADS, HG_DIM, HG_DIM), lambda b: (b, 0, 0, 0))
    return pl.pallas_call(
        _hgrn_decode_kernel,
        out_shape=(jax.ShapeDtypeStruct((Bs, 1, HG_WIDTH), bf16),
                   jax.ShapeDtypeStruct((Bs, HG_HEADS, HG_DIM, HG_DIM), f32)),
        grid=(Bs,),
        in_specs=[pl.BlockSpec(h4.shape, lambda b: (0, 0)), pl.BlockSpec(lb.shape, lambda b: (0, 0)),
                  pl.BlockSpec(gn.shape, lambda b: (0, 0)), sspec],
        out_specs=(pl.BlockSpec((None, 1, HG_WIDTH), lambda b: (b, 0, 0)), sspec),
        scratch_shapes=[pltpu.VMEM((HG_HEADS, HG_DIM, Bs), f32)],
        compiler_params=_params(("arbitrary",)),
        name="hgrn_decode",
    )(h4, lb, gn, state)


def _split_w_in(w):
    idx = np.cumsum(SPLITS)[:-1]
    return jnp.split(w, [int(v) for v in idx], axis=1)


def kernel(x_prompt, x_sample, cache_kv, page_table, state_kv_win, state_hgrn, state_conv, rel_table, hg_lb_logits,
           norm1_g, w_in, q_norm_g, k_norm_g, phi_k_w1, phi_k_w2, phi_v_w1, phi_v_w2, hg_norm_g, w_nsa_out, w_hg_out,
           w_o, norm2_g, w_gate, w_up, conv_w, conv_b, w_down):
    assert w_in.shape[0] == 1, "one layer"
    Bp, L, _ = x_prompt.shape
    Bs = x_sample.shape[0]
    n_pages = page_table.shape[1]
    past = n_pages * PAGE_SIZE
    n_win = state_kv_win.shape[2]
    assert x_sample.shape[1] == 1 and n_win == WINDOW and past % (2 * CMP_BLOCK) == 0 and L % LANES == 0

    wq, wkv, wgl, whq, whf, whi, whg, wmg = _split_w_in(w_in[0])
    wgl = jnp.pad(wgl, ((0, 0), (0, LANES - wgl.shape[1])))
    w_prompt = jnp.concatenate([wq, wkv, whq, whf, whi, whg, wmg, wgl], axis=1).astype(bf16)
    wq_h = wq.reshape(D_MODEL, NSA_HEADS, HEAD_DIM)
    zq = jnp.zeros_like(wq_h)
    in_g0 = (np.arange(NSA_HEADS) < NSA_GROUP)[None, :, None]
    wq_bd = jnp.where(in_g0, jnp.concatenate([wq_h, zq], axis=-1), jnp.concatenate([zq, wq_h], axis=-1))
    w_sample = jnp.concatenate([wq_bd.reshape(D_MODEL, NSA_HEADS * LANES), wkv, whq, whf, whi, whg, wmg, wgl],
                               axis=1).astype(bf16)
    g1 = norm1_g[0][None, :]
    gq = q_norm_g[0]
    gq_prompt = jnp.tile(gq, NSA_HEADS)[None, :]
    gq_sample = jnp.tile(gq, 2 * NSA_HEADS)[None, :]
    gk = [jnp.tile(k_norm_g[0, s], NSA_KV_HEADS)[None, :] for s in range(3)]

    def block_diag(mats):
        n = len(mats)
        out = jnp.zeros(mats[0].shape[:-2] + (n * HEAD_DIM, n * HEAD_DIM), f32)
        for t, m in enumerate(mats):
            out = out.at[..., t * HEAD_DIM:(t + 1) * HEAD_DIM, t * HEAD_DIM:(t + 1) * HEAD_DIM].set(m)
        return out.astype(bf16)

    w1bd = block_diag([phi_k_w1[0], phi_k_w1[0], phi_v_w1[0], phi_v_w1[0]])
    w2bd = block_diag([phi_k_w2[0], phi_k_w2[0], phi_v_w2[0], phi_v_w2[0]])
    lb = jnp.cumsum(jax.nn.softmax(hg_lb_logits.astype(f32), axis=0), axis=0)[0][None, :]
    gn = hg_norm_g[0][None, :]
    ffn_w = (w_nsa_out[0].astype(bf16), w_hg_out[0].astype(bf16), w_o[0].astype(bf16), norm2_g[0][None, :],
             w_gate[0].astype(bf16), w_up[0].astype(bf16), conv_w[0], conv_b[0][None, :], w_down[0].astype(bf16))

    idx_c, idx_t = _prompt_bucket_tables(L)
    tc = _bias_tables(rel_table, jnp.asarray(idx_c)).reshape(L // SEL_BLOCK, NSA_KV_HEADS, NSA_GROUP * SEL_BLOCK, -1)
    tt = _bias_tables(rel_table, jnp.asarray(idx_t)).reshape(5, NSA_KV_HEADS, NSA_GROUP * SEL_BLOCK, LANES)
    bias_dec = _bias_tables(rel_table, jnp.asarray(_decode_bucket_table(past, n_win))[None]).reshape(NSA_HEADS, -1)
    ex_prompt = jnp.asarray(_expand_np(L // SEL_BLOCK, L), dtype=bf16)
    ex_sample = jnp.asarray(_expand_np(past // SEL_BLOCK, past), dtype=bf16)

    xp = x_prompt.reshape(Bp * L, D_MODEL)
    qn, kv32, win32, katt, h4, mgs, gates = _in_proj(xp, g1, w_prompt, gq_prompt, gk[1], gk[2],
                                                     qw=NSA_WIDTH, q_seg=HEAD_DIM, tm=256)
    cmp = _compress_prompt(kv32, w1bd, w2bd, gk[0], Bp, L)
    o_nsa = _nsa_prompt(qn, gates, katt, cmp, tc, tt, ex_prompt, Bp, L)
    o_hg, s_prompt = _hgrn_prompt(h4, lb, gn, Bp, L)
    tm_ffn = 512
    yp, gate_tails = _ffn(xp, o_nsa, o_hg, mgs, *ffn_w, tm=tm_ffn, tf=256, seq_len=L)
    conv_p = gate_tails.reshape(Bp, L // tm_ffn, 2, D_FF)[:, -1]
    n_keep_win = min(WINDOW, L)
    win_p = win32.reshape(Bp, L, 2, NSA_KV_HEADS, HEAD_DIM)[:, L - n_keep_win:]

    xs = x_sample.reshape(Bs, D_MODEL)
    q_bd, kv32_s, win32_s, _, h4_s, mgs_s, gates_s = _in_proj(xs, g1, w_sample, gq_sample, gk[1], gk[2],
                                                              qw=NSA_HEADS * LANES, q_seg=LANES, tm=Bs)
    cache2d = cache_kv[0].reshape(cache_kv.shape[1], PAGE_SIZE, 4 * LANES)
    win_state = state_kv_win[0].reshape(Bs, n_win, 2 * LANES)
    o_nsa_s, win_s = _nsa_decode(cache2d, page_table, q_bd, kv32_s, win32_s, gates_s, win_state, w1bd, w2bd, gk[0],
                                 bias_dec, ex_sample)
    o_hg_s, s_sample = _hgrn_decode(h4_s, lb, gn, state_hgrn[0])
    ys, gate_s = _ffn(xs, o_nsa_s.reshape(Bs, NSA_WIDTH), o_hg_s.reshape(Bs, HG_WIDTH), mgs_s, *ffn_w,
                      tm=Bs, tf=256, conv_state=state_conv[0])
    conv_s = jnp.stack([state_conv[0][:, 1], gate_s], axis=1)

    kvh = (NSA_KV_HEADS, HEAD_DIM)
    return (yp.reshape(Bp, L, D_MODEL), ys.reshape(Bs, 1, D_MODEL),
            kv32.reshape((1, Bp, L, 4) + kvh), kv32_s.reshape((1, Bs, 1, 4) + kvh),
            win_p[None], win_s.reshape((1, Bs, n_win, 2) + kvh),
            s_prompt[None], s_sample[None], conv_p[None], conv_s[None])
```

```python
import functools
import math

import numpy as np
import jax
import jax.numpy as jnp
from jax import lax
from jax.experimental import pallas as pl
from jax.experimental.pallas import tpu as pltpu

f32, bf16, i32 = jnp.float32, jnp.bfloat16, jnp.int32

D_MODEL = 1024
PAGE_SIZE = 128
NSA_HEADS, NSA_KV_HEADS, NSA_GROUP, HEAD_DIM = 8, 2, 4, 64
CMP_BLOCK, SEL_BLOCK, TOP_N, WINDOW = 32, 64, 16, 512
SCALE = HEAD_DIM ** -0.5
NSA_WIDTH = NSA_HEADS * HEAD_DIM
KV_WIDTH = NSA_KV_HEADS * HEAD_DIM
HG_HEADS, HG_DIM = 4, 128
HG_WIDTH = HG_HEADS * HG_DIM
HG_SCALE = HG_DIM ** -0.5
HG_BLOCK = 16
REL_BUCKETS, REL_MAX_DIST = 32, 128
D_FF = 2816
EPS = 1e-6
SPLITS = (NSA_WIDTH, 6 * KV_WIDTH, 3 * NSA_HEADS, HG_WIDTH, HG_WIDTH, HG_WIDTH, HG_WIDTH, 2 * D_MODEL)
LANES = 128
NEG = -1e30
FORCED_SCORE, INVALID_SCORE = 8.0, -1.0
VMEM_LIMIT = 56 * 1024 * 1024

NT = (((1,), (1,)), ((), ()))
TN = (((0,), (0,)), ((), ()))


def _params(sem):
    return pltpu.CompilerParams(dimension_semantics=sem, vmem_limit_bytes=VMEM_LIMIT)


def _bucket_np(dist):
    n = np.maximum(dist, 0)
    exact = REL_BUCKETS // 2
    ratio = np.log(np.maximum(n, 1).astype(np.float32) / np.float32(exact)) / np.float32(math.log(REL_MAX_DIST / exact))
    big = exact + (ratio.astype(np.float32) * np.float32(REL_BUCKETS - exact)).astype(np.int32)
    return np.where(n < exact, n, np.minimum(big, REL_BUCKETS - 1)).astype(np.int32)


def _cmp_perm(n_cmp):
    half = n_cmp // 2
    c = np.arange(n_cmp)
    return np.where(c < half, 2 * c, 2 * (c - half) + 1)


def _prompt_bucket_tables(L):
    nqb = L // SEL_BLOCK
    n_cmp = L // CMP_BLOCK
    qi = np.arange(SEL_BLOCK)
    c_end = (_cmp_perm(n_cmp) + 1) * CMP_BLOCK - 1
    idx_c = np.stack([_bucket_np((i * SEL_BLOCK + qi)[:, None] - c_end[None, :]) for i in range(nqb)])
    kj = np.arange(SEL_BLOCK)
    def tile(delta):
        return _bucket_np(delta * SEL_BLOCK + qi[:, None] - kj[None, :])
    idx_t = np.stack([np.concatenate([tile(e), tile(e - 1)], axis=1) for e in range(5)])
    return idx_c.astype(np.int32), idx_t.astype(np.int32)


def _decode_bucket_table(past_len, n_win):
    n_cmp = past_len // CMP_BLOCK
    c_end = (_cmp_perm(n_cmp) + 1) * CMP_BLOCK - 1
    d_c = np.zeros((LANES,), np.int64)
    d_c[:n_cmp] = past_len - c_end
    d_s = past_len - np.arange(past_len)
    d_w = n_win - np.arange(n_win)
    d_new = np.zeros((LANES,), np.int64)
    return _bucket_np(np.concatenate([d_s, d_w, d_c, d_new]))[None, :].astype(np.int32)


def _expand_np(n_blocks, n_keys):
    e = (np.arange(n_keys)[None, :] // SEL_BLOCK == np.arange(n_blocks)[:, None])
    return e.astype(np.float32)


def _seg_rmsnorm(x, gain, seg):
    outs = []
    for t in range(x.shape[1] // LANES):
        xt = x[:, t * LANES:(t + 1) * LANES]
        sq = xt * xt
        if seg == LANES:
            r = lax.rsqrt(jnp.sum(sq, axis=-1, keepdims=True) * (1.0 / HEAD_DIM) + EPS)
        else:
            lo = lax.broadcasted_iota(i32, xt.shape, 1) < HEAD_DIM
            s_lo = jnp.sum(jnp.where(lo, sq, 0.0), axis=-1, keepdims=True)
            s_hi = jnp.sum(jnp.where(lo, 0.0, sq), axis=-1, keepdims=True)
            r = jnp.where(lo, lax.rsqrt(s_lo * (1.0 / HEAD_DIM) + EPS), lax.rsqrt(s_hi * (1.0 / HEAD_DIM) + EPS))
        outs.append(xt * r)
    y = outs[0] if len(outs) == 1 else jnp.concatenate(outs, axis=1)
    return y * gain


def _rmsnorm_rows(x, gain):
    return x * lax.rsqrt(jnp.mean(x * x, axis=-1, keepdims=True) + EPS) * gain


def _softmax_parts(s, mask):
    s = jnp.where(mask, s, NEG)
    m = jnp.max(s, axis=-1, keepdims=True)
    e = jnp.where(mask, jnp.exp(s - m), 0.0)
    return e, jnp.sum(e, axis=-1, keepdims=True)


def _bias_table_kernel(tab_ref, idx_ref, out_ref):
    idx = idx_ref[...]
    for h in range(NSA_HEADS):
        acc = jnp.zeros(idx.shape, f32)
        for b in range(REL_BUCKETS):
            acc = jnp.where(idx == b, tab_ref[b, h], acc)
        out_ref[h] = acc


def _bias_tables(rel_table, idx):
    n, rows, w = idx.shape
    return pl.pallas_call(
        _bias_table_kernel,
        out_shape=jax.ShapeDtypeStruct((n, NSA_HEADS, rows, w), f32),
        grid=(n,),
        in_specs=[pl.BlockSpec(memory_space=pltpu.SMEM),
                  pl.BlockSpec((None, rows, w), lambda i: (i, 0, 0))],
        out_specs=pl.BlockSpec((None, NSA_HEADS, rows, w), lambda i: (i, 0, 0, 0)),
        compiler_params=_params(("arbitrary",)),
        name="bias_tables",
    )(rel_table, idx)


def _in_proj_kernel(x_ref, g1_ref, w_ref, gq_ref, gk1_ref, gk2_ref,
                    q_ref, kv_ref, win_ref, katt_ref, h4_ref, mg_ref, gl_ref, *, qw, q_seg, kv_feature_major):
    x = x_ref[...]
    xn = _rmsnorm_rows(x, g1_ref[...]).astype(bf16)

    def proj(c0, width):
        return jnp.dot(xn, w_ref[:, c0:c0 + width], preferred_element_type=f32)

    q_ref[...] = _seg_rmsnorm(proj(0, qw), gq_ref[...], q_seg).astype(bf16)
    c = qw
    kv = proj(c, 6 * KV_WIDTH)
    c += 6 * KV_WIDTH
    k_sel = _seg_rmsnorm(kv[:, 2 * KV_WIDTH:3 * KV_WIDTH], gk1_ref[...], HEAD_DIM)
    k_win = _seg_rmsnorm(kv[:, 4 * KV_WIDTH:5 * KV_WIDTH], gk2_ref[...], HEAD_DIM)
    v_sel = kv[:, 3 * KV_WIDTH:4 * KV_WIDTH]
    v_win = kv[:, 5 * KV_WIDTH:6 * KV_WIDTH]
    if kv_feature_major:
        kv_ref[0:2 * KV_WIDTH, :] = kv[:, 0:2 * KV_WIDTH].T
        kv_ref[2 * KV_WIDTH:3 * KV_WIDTH, :] = k_sel.T
        kv_ref[3 * KV_WIDTH:4 * KV_WIDTH, :] = v_sel.T
    else:
        kv_ref[:, 0:2 * KV_WIDTH] = kv[:, 0:2 * KV_WIDTH]
        kv_ref[:, 2 * KV_WIDTH:3 * KV_WIDTH] = k_sel
        kv_ref[:, 3 * KV_WIDTH:4 * KV_WIDTH] = v_sel
    win_ref[:, 0:KV_WIDTH] = k_win
    win_ref[:, KV_WIDTH:2 * KV_WIDTH] = v_win
    katt_ref[:, 0:KV_WIDTH] = k_sel.astype(bf16)
    katt_ref[:, KV_WIDTH:2 * KV_WIDTH] = v_sel.astype(bf16)
    katt_ref[:, 2 * KV_WIDTH:3 * KV_WIDTH] = k_win.astype(bf16)
    katt_ref[:, 3 * KV_WIDTH:4 * KV_WIDTH] = v_win.astype(bf16)
    h4_ref[...] = proj(c, 4 * HG_WIDTH)
    c += 4 * HG_WIDTH
    mg_ref[...] = jax.nn.sigmoid(proj(c, 2 * D_MODEL))
    c += 2 * D_MODEL
    gl_ref[...] = jax.nn.sigmoid(proj(c, LANES))


def _in_proj(x2d, g1, w, gq, gk1, gk2, *, qw, q_seg, tm, seq_len=None):
    n = x2d.shape[0]
    cw = w.shape[1]
    row = lambda width: pl.BlockSpec((tm, width), lambda i: (i, 0))
    full = lambda a: pl.BlockSpec(a.shape, lambda i: (0,) * a.ndim)
    widths = (qw, 4 * KV_WIDTH, 2 * KV_WIDTH, 4 * KV_WIDTH, 4 * HG_WIDTH, 2 * D_MODEL, LANES)
    dtypes = (bf16, f32, f32, bf16, f32, f32, f32)
    out_shape = [jax.ShapeDtypeStruct((n, wd), dt) for wd, dt in zip(widths, dtypes)]
    out_specs = [row(wd) for wd in widths]
    if seq_len is not None:
        tiles = seq_len // tm
        out_shape[1] = jax.ShapeDtypeStruct((n // seq_len, 4 * KV_WIDTH, seq_len), f32)
        out_specs[1] = pl.BlockSpec((None, 4 * KV_WIDTH, tm), lambda i: (i // tiles, 0, i % tiles))
    return pl.pallas_call(
        functools.partial(_in_proj_kernel, qw=qw, q_seg=q_seg, kv_feature_major=seq_len is not None),
        out_shape=tuple(out_shape),
        grid=(n // tm,),
        in_specs=[row(D_MODEL), full(g1), pl.BlockSpec((D_MODEL, cw), lambda i: (0, 0)), full(gq), full(gk1), full(gk2)],
        out_specs=tuple(out_specs),
        compiler_params=_params(("arbitrary",)),
        name="in_proj",
    )(x2d, g1, w, gq, gk1, gk2)


CMP_SLOT = CMP_BLOCK + 1


def _cmp_staging_rows(n_cmp):
    return -(-n_cmp * CMP_SLOT // 8) * 8


def _stage_cmp_chunk(src_ref, chunk, kc_t, vc_t, n_cmp):
    per_chunk = LANES // CMP_BLOCK
    for s, t in enumerate((kc_t.T, vc_t.T)):
        for nl in range(per_chunk):
            n = chunk * per_chunk + nl
            slot = n // 2 + (n % 2) * (n_cmp // 2)
            src_ref[s, slot * CMP_SLOT:slot * CMP_SLOT + CMP_BLOCK, :] = t[nl * CMP_BLOCK:(nl + 1) * CMP_BLOCK]


def _compress(src_ref, w1_ref, w2_ref, gk0, n_cmp):
    acc = jnp.zeros((n_cmp, 2 * LANES), f32)
    for j in range(CMP_BLOCK):
        xj = jnp.concatenate([src_ref[s, pl.ds(j, n_cmp, stride=CMP_SLOT), :] for s in range(2)], axis=1)
        acc = acc + jnp.dot(xj.astype(bf16), w1_ref[j], preferred_element_type=f32)
    hmid = jax.nn.gelu(acc).astype(bf16)
    y = jnp.dot(hmid, w2_ref[...], preferred_element_type=f32)
    ck = _seg_rmsnorm(y[:, 0:LANES], gk0, HEAD_DIM)
    return jnp.concatenate([ck, y[:, LANES:2 * LANES]], axis=1)


def _compress_prompt_kernel(kv_ref, w1_ref, w2_ref, gk0_ref, out_ref, src_ref, *, n_cmp):
    for c in range(kv_ref.shape[1] // LANES):
        cols = slice(c * LANES, (c + 1) * LANES)
        _stage_cmp_chunk(src_ref, c, kv_ref[0:LANES, cols], kv_ref[LANES:2 * LANES, cols], n_cmp)
    out_ref[...] = _compress(src_ref, w1_ref, w2_ref, gk0_ref[...], n_cmp).astype(bf16)


def _compress_prompt(kv_t, w1bd, w2bd, gk0):
    B, _, L = kv_t.shape
    n_cmp = L // CMP_BLOCK
    return pl.pallas_call(
        functools.partial(_compress_prompt_kernel, n_cmp=n_cmp),
        out_shape=jax.ShapeDtypeStruct((B, n_cmp, 2 * LANES), bf16),
        grid=(B,),
        in_specs=[pl.BlockSpec((None, 2 * LANES, L), lambda b: (b, 0, 0)),
                  pl.BlockSpec(w1bd.shape, lambda b: (0, 0, 0)),
                  pl.BlockSpec(w2bd.shape, lambda b: (0, 0)),
                  pl.BlockSpec(gk0.shape, lambda b: (0, 0))],
        out_specs=pl.BlockSpec((None, n_cmp, 2 * LANES), lambda b: (b, 0, 0)),
        scratch_shapes=[pltpu.VMEM((2, _cmp_staging_rows(n_cmp), LANES), f32)],
        compiler_params=_params(("arbitrary",)),
        name="compress_prompt",
    )(kv_t, w1bd, w2bd, gk0)


def _select_blocks_t(score, valid, n_keep):
    nb = score.shape[0]
    jj = lax.broadcasted_iota(i32, score.shape, 0)
    cnt = jnp.zeros(score.shape, f32)
    for k in range(nb):
        sk = score[k:k + 1, :]
        better = (sk > score) | ((sk == score) & (jj > k))
        cnt = cnt + jnp.where(better, 1.0, 0.0)
    return (cnt < n_keep) & valid


def _sel_tile_classes(n_tiles):
    return sorted({-(-n_tiles * c // 4) for c in range(1, 5)})


def _nsa_prompt_kernel(q_ref, gate_ref, katt_ref, cmp_ref, tc_ref, tt_ref, ex_ref, o_ref,
                       selx_ref, s_ref, os_ref, *, L):
    i = pl.program_id(1)
    n_cmp = L // CMP_BLOCK
    n_sel = L // SEL_BLOCK
    n_tiles = L // LANES
    qb = SEL_BLOCK
    rows = NSA_GROUP * qb
    groups = range(NSA_KV_HEADS)
    gates = gate_ref[...]
    row64 = lax.broadcasted_iota(i32, (qb, LANES), 0)
    lane = lax.broadcasted_iota(i32, (qb, LANES), 1)
    qpos = i * qb + row64

    qs = [jnp.concatenate(
        [q_ref[:, (NSA_GROUP * g + r) * HEAD_DIM:(NSA_GROUP * g + r + 1) * HEAD_DIM] for r in range(NSA_GROUP)],
        axis=0) for g in groups]

    o_c, imp_t = [], []
    col = lax.broadcasted_iota(i32, (rows, n_cmp), 1)
    rq = lax.broadcasted_iota(i32, (rows, n_cmp), 0) & (qb - 1)
    blk = jnp.where(col < n_cmp // 2, 2 * col, 2 * col - (n_cmp - 1))
    vis = (blk + 1) * CMP_BLOCK - 1 <= i * qb + rq
    for g in groups:
        ck = cmp_ref[:, g * HEAD_DIM:(g + 1) * HEAD_DIM]
        cv = cmp_ref[:, LANES + g * HEAD_DIM:LANES + (g + 1) * HEAD_DIM]
        s = lax.dot_general(qs[g], ck, NT, preferred_element_type=f32) * SCALE + tc_ref[g]
        e, den = _softmax_parts(s, vis)
        p = e / jnp.where(den > 0, den, 1.0)
        o_c.append(jnp.dot(p.astype(bf16), cv, preferred_element_type=f32))
        ps_t = (p[0:qb] + p[qb:2 * qb] + p[2 * qb:3 * qb] + p[3 * qb:4 * qb]).T
        imp_t.append(ps_t[0:n_sel] + ps_t[n_sel:2 * n_sel])

    imp = jnp.concatenate(imp_t, axis=1)
    jj = lax.broadcasted_iota(i32, imp.shape, 0)
    valid = jj <= i
    forced = valid & ((jj == 0) | (jj >= i - 1))
    score = jnp.where(forced, FORCED_SCORE, jnp.where(valid, imp, INVALID_SCORE))
    sel = _select_blocks_t(score, valid, min(TOP_N, n_sel))
    selx = lax.dot_general(jnp.where(sel, 1.0, 0.0).astype(bf16), ex_ref[...], TN, preferred_element_type=f32)
    for g in groups:
        for t in range(n_tiles):
            selx_ref[g, t] = selx[g * qb:(g + 1) * qb, t * LANES:(t + 1) * LANES]

    def attend(k_off, v_off, tile0, n_t, mask_fn):
        def scores(t, mruns):
            ta = tile0 + t
            r0 = pl.multiple_of(ta * LANES, LANES)
            bias_idx = jnp.clip(i - 2 * ta, 0, 4)
            kpos = ta * LANES + lane
            out = []
            for g in groups:
                kt = katt_ref[pl.ds(r0, LANES), k_off + g * HEAD_DIM:k_off + (g + 1) * HEAD_DIM]
                s = lax.dot_general(qs[g], kt, NT, preferred_element_type=f32) * SCALE + tt_ref[bias_idx, g]
                msk = mask_fn(g, ta, kpos)
                s = jnp.where(msk[None], s.reshape(NSA_GROUP, qb, LANES), NEG).reshape(rows, LANES)
                s_ref[g, t] = s
                out.append(jnp.maximum(mruns[g], s))
            return tuple(out)

        mruns = lax.fori_loop(0, n_t, scores, tuple(jnp.full((rows, LANES), NEG, f32) for _ in groups), unroll=2)
        ms = [jnp.max(m, axis=-1, keepdims=True) for m in mruns]

        def values(t, carry):
            r0 = pl.multiple_of((tile0 + t) * LANES, LANES)
            ls, accs = carry
            new_l, new_acc = [], []
            for g in groups:
                vt = katt_ref[pl.ds(r0, LANES), v_off + g * HEAD_DIM:v_off + (g + 1) * HEAD_DIM]
                p = jnp.exp(s_ref[g, t] - ms[g])
                new_l.append(ls[g] + p)
                new_acc.append(accs[g] + jnp.dot(p.astype(bf16), vt, preferred_element_type=f32))
            return tuple(new_l), tuple(new_acc)

        init = (tuple(jnp.zeros((rows, LANES), f32) for _ in groups),
                tuple(jnp.zeros((rows, HEAD_DIM), f32) for _ in groups))
        ls, accs = lax.fori_loop(0, n_t, values, init, unroll=2)
        return [accs[g] / jnp.sum(ls[g], axis=-1, keepdims=True) for g in groups]

    need = i // 2 + 1
    prev = 0
    for n_t in _sel_tile_classes(n_tiles):
        @pl.when((need > prev) & (need <= n_t))
        def _(n_t=n_t):
            o_s = attend(0, KV_WIDTH, 0, n_t, lambda g, ta, kpos: (selx_ref[g, ta] > 0.5) & (kpos <= qpos))
            for g in groups:
                os_ref[g] = o_s[g]
        prev = n_t

    n_wt = min(WINDOW // LANES + 1, n_tiles)
    w0 = jnp.clip((i - WINDOW // qb) // 2, 0, n_tiles - n_wt)
    o_w = attend(2 * KV_WIDTH, 3 * KV_WIDTH, w0, n_wt,
                 lambda g, ta, kpos: (kpos <= qpos) & (qpos - kpos <= WINDOW))

    for g in groups:
        o_s = os_ref[g]
        for r in range(NSA_GROUP):
            h = NSA_GROUP * g + r
            sl = slice(r * qb, (r + 1) * qb)
            o = (gates[:, 3 * h:3 * h + 1] * o_c[g][sl] + gates[:, 3 * h + 1:3 * h + 2] * o_s[sl]
                 + gates[:, 3 * h + 2:3 * h + 3] * o_w[g][sl])
            o_ref[:, h * HEAD_DIM:(h + 1) * HEAD_DIM] = o.astype(bf16)


def _nsa_prompt(qn, gates, katt, cmp, tc, tt, expand, B, L):
    nqb = L // SEL_BLOCK
    qb = SEL_BLOCK
    n_tiles = L // LANES
    rows = NSA_GROUP * qb
    return pl.pallas_call(
        functools.partial(_nsa_prompt_kernel, L=L),
        out_shape=jax.ShapeDtypeStruct((B * L, NSA_WIDTH), bf16),
        grid=(B, nqb),
        in_specs=[pl.BlockSpec((qb, NSA_WIDTH), lambda b, i: (b * nqb + i, 0)),
                  pl.BlockSpec((qb, LANES), lambda b, i: (b * nqb + i, 0)),
                  pl.BlockSpec((L, 4 * KV_WIDTH), lambda b, i: (b, 0)),
                  pl.BlockSpec((None, L // CMP_BLOCK, 2 * LANES), lambda b, i: (b, 0, 0)),
                  pl.BlockSpec((None,) + tc.shape[1:], lambda b, i: (i, 0, 0, 0)),
                  pl.BlockSpec(tt.shape, lambda b, i: (0, 0, 0, 0)),
                  pl.BlockSpec(expand.shape, lambda b, i: (0, 0))],
        out_specs=pl.BlockSpec((qb, NSA_WIDTH), lambda b, i: (b * nqb + i, 0)),
        scratch_shapes=[pltpu.VMEM((NSA_KV_HEADS, n_tiles, qb, LANES), f32),
                        pltpu.VMEM((NSA_KV_HEADS, n_tiles, rows, LANES), f32),
                        pltpu.VMEM((NSA_KV_HEADS, rows, HEAD_DIM), f32)],
        compiler_params=_params(("arbitrary", "arbitrary")),
        name="nsa_prompt",
    )(qn, gates, katt, cmp, tc, tt, expand)


def _hgrn_gates(hf, lb):
    f = lb + (1.0 - lb) * jax.nn.sigmoid(hf)
    return f, 1.0 - f, jnp.log(f)


def _hgrn_out(o, gn, hgate):
    return _rmsnorm_rows(o, gn) * jax.nn.silu(hgate)


def _hgrn_prompt_kernel(h4_ref, lb_ref, gn_ref, o_ref, s_out_ref, st_ref, *, tc):
    c = pl.program_id(1)
    nb = HG_BLOCK

    @pl.when(c == 0)
    def _():
        st_ref[...] = jnp.zeros_like(st_ref)

    tri = (lax.broadcasted_iota(i32, (nb, nb), 0) >= lax.broadcasted_iota(i32, (nb, nb), 1)).astype(f32)
    t_col = lax.broadcasted_iota(i32, (nb, 1), 0)
    gn = gn_ref[...]

    def block(bi, _):
        r0 = pl.multiple_of(bi * nb, nb)
        for h in range(HG_HEADS):
            sl = lambda part: h4_ref[pl.ds(r0, nb), part * HG_WIDTH + h * HG_DIM:part * HG_WIDTH + (h + 1) * HG_DIM]
            q = sl(0) * HG_SCALE
            f, k, glog = _hgrn_gates(sl(1), lb_ref[:, h * HG_DIM:(h + 1) * HG_DIM])
            v = sl(2)
            b = jnp.dot(tri, glog, precision=lax.Precision.HIGHEST, preferred_element_type=f32)
            st = st_ref[h]
            o = lax.dot_general((q * jnp.exp(b)).astype(bf16), st.astype(bf16), NT, preferred_element_type=f32)
            for s in range(nb):
                w = q * k[s:s + 1] * jnp.exp(jnp.minimum(b - b[s:s + 1], 0.0))
                a = jnp.where(t_col >= s, jnp.sum(w, axis=-1, keepdims=True), 0.0)
                o = o + a * v[s:s + 1]
            b_last = b[nb - 1:nb]
            kt = k * jnp.exp(b_last - b)
            upd = lax.dot_general(v.astype(bf16), kt.astype(bf16), TN, preferred_element_type=f32)
            st_ref[h] = jnp.exp(b_last) * st + upd
            o_ref[pl.ds(r0, nb), h * HG_DIM:(h + 1) * HG_DIM] = _hgrn_out(o, gn, sl(3)).astype(bf16)
        return 0

    lax.fori_loop(0, tc // nb, block, 0)

    @pl.when(c == pl.num_programs(1) - 1)
    def _():
        for h in range(HG_HEADS):
            s_out_ref[h] = st_ref[h].T


def _hgrn_prompt(h4, lb, gn, B, L, tc=256):
    nc = L // tc
    return pl.pallas_call(
        functools.partial(_hgrn_prompt_kernel, tc=tc),
        out_shape=(jax.ShapeDtypeStruct((B * L, HG_WIDTH), bf16),
                   jax.ShapeDtypeStruct((B, HG_HEADS, HG_DIM, HG_DIM), f32)),
        grid=(B, nc),
        in_specs=[pl.BlockSpec((tc, 4 * HG_WIDTH), lambda b, c: (b * nc + c, 0)),
                  pl.BlockSpec(lb.shape, lambda b, c: (0, 0)),
                  pl.BlockSpec(gn.shape, lambda b, c: (0, 0))],
        out_specs=(pl.BlockSpec((tc, HG_WIDTH), lambda b, c: (b * nc + c, 0)),
                   pl.BlockSpec((None, HG_HEADS, HG_DIM, HG_DIM), lambda b, c: (b, 0, 0, 0))),
        scratch_shapes=[pltpu.VMEM((HG_HEADS, HG_DIM, HG_DIM), f32)],
        compiler_params=_params(("arbitrary", "arbitrary")),
        name="hgrn_prompt",
    )(h4, lb, gn)


def _ffn_kernel(*refs, decode, tiles_per_seq):
    if decode:
        (x_ref, on_ref, oh_ref, mg_ref, wn_ref, wh_ref, wo_ref, g2_ref, wg_ref, wu_ref, cw_ref, cb_ref, wd_ref,
         cs0_ref, cs1_ref, y_ref, gate_out_ref, x1_ref, xn2_ref, acc_ref) = refs
    else:
        (x_ref, on_ref, oh_ref, mg_ref, wn_ref, wh_ref, wo_ref, g2_ref, wg_ref, wu_ref, cw_ref, cb_ref, wd_ref,
         y_ref, gate_out_ref, x1_ref, xn2_ref, acc_ref, carry_ref) = refs
    i = pl.program_id(0)
    fi = pl.program_id(1)
    tm = x_ref.shape[0]

    @pl.when(fi == 0)
    def _():
        y_a = jnp.dot(on_ref[...], wn_ref[...], preferred_element_type=f32)
        y_b = jnp.dot(oh_ref[...], wh_ref[...], preferred_element_type=f32)
        mg = mg_ref[...]
        merged = mg[:, 0:D_MODEL] * y_a + mg[:, D_MODEL:2 * D_MODEL] * y_b
        x1 = x_ref[...] + jnp.dot(merged.astype(bf16), wo_ref[...], preferred_element_type=f32)
        x1_ref[...] = x1
        xn2_ref[...] = _rmsnorm_rows(x1, g2_ref[...]).astype(bf16)
        acc_ref[...] = jnp.zeros_like(acc_ref)

    xn2 = xn2_ref[...]
    gate = jnp.dot(xn2, wg_ref[...], preferred_element_type=f32)
    up = jnp.dot(xn2, wu_ref[...], preferred_element_type=f32)
    cw = cw_ref[...]
    if decode:
        prev2, prev1 = cs0_ref[...], cs1_ref[...]
        gate_out_ref[...] = gate
    else:
        @pl.when(i % tiles_per_seq == 0)
        def _():
            carry_ref[fi] = jnp.zeros(carry_ref.shape[1:], f32)
        carry = carry_ref[fi]
        rid = lax.broadcasted_iota(i32, gate.shape, 0)
        prev1 = jnp.where(rid == 0, carry[7:8], pltpu.roll(gate, 1, axis=0))
        prev2 = jnp.where(rid == 0, carry[6:7], jnp.where(rid == 1, carry[7:8], pltpu.roll(gate, 2, axis=0)))
        carry_ref[fi, 6:8, :] = gate[tm - 2:tm]
        gate_out_ref[...] = gate[tm - 2:tm]
    conv = cb_ref[...] + cw[0:1] * prev2 + cw[1:2] * prev1 + cw[2:3] * gate
    hmid = (jax.nn.silu(conv) * up).astype(bf16)
    acc_ref[...] += jnp.dot(hmid, wd_ref[...], preferred_element_type=f32)

    @pl.when(fi == pl.num_programs(1) - 1)
    def _():
        y_ref[...] = x1_ref[...] + acc_ref[...]


def _ffn(x2d, o_nsa, o_hg, mgs, wn, wh, wo, g2, wg, wu, cw, cb, wd, *, tm, tf, seq_len=None, conv_state=None):
    n = x2d.shape[0]
    decode = conv_state is not None
    nf = D_FF // tf
    row = lambda width: pl.BlockSpec((tm, width), lambda i, f: (i, 0))
    full = lambda a: pl.BlockSpec(a.shape, lambda i, f: (0,) * a.ndim)
    in_specs = [row(D_MODEL), row(NSA_WIDTH), row(HG_WIDTH), row(2 * D_MODEL), full(wn), full(wh), full(wo), full(g2),
                pl.BlockSpec((D_MODEL, tf), lambda i, f: (0, f)), pl.BlockSpec((D_MODEL, tf), lambda i, f: (0, f)),
                pl.BlockSpec((3, tf), lambda i, f: (0, f)), pl.BlockSpec((1, tf), lambda i, f: (0, f)),
                pl.BlockSpec((tf, D_MODEL), lambda i, f: (f, 0))]
    args = [x2d, o_nsa, o_hg, mgs, wn, wh, wo, g2, wg, wu, cw, cb, wd]
    scratch = [pltpu.VMEM((tm, D_MODEL), f32), pltpu.VMEM((tm, D_MODEL), bf16), pltpu.VMEM((tm, D_MODEL), f32)]
    if decode:
        in_specs += [pl.BlockSpec((tm, tf), lambda i, f: (i, f))] * 2
        args += [conv_state[:, 0], conv_state[:, 1]]
        gate_shape = jax.ShapeDtypeStruct((n, D_FF), f32)
        gate_spec = pl.BlockSpec((tm, tf), lambda i, f: (i, f))
        tiles_per_seq = 1
    else:
        tiles_per_seq = seq_len // tm
        gate_shape = jax.ShapeDtypeStruct((n // tm, 2, D_FF), f32)
        gate_spec = pl.BlockSpec((None, 2, tf), lambda i, f: (i, 0, f))
        scratch.append(pltpu.VMEM((nf, 8, tf), f32))
    return pl.pallas_call(
        functools.partial(_ffn_kernel, decode=decode, tiles_per_seq=tiles_per_seq),
        out_shape=(jax.ShapeDtypeStruct((n, D_MODEL), f32), gate_shape),
        grid=(n // tm, nf),
        in_specs=in_specs,
        out_specs=(row(D_MODEL), gate_spec),
        scratch_shapes=scratch,
        compiler_params=_params(("arbitrary", "arbitrary")),
        name="ffn_decode" if decode else "ffn_prompt",
    )(*args)


def _nsa_decode_kernel(*refs, n_pages, n_win):
    page_refs = refs[1:1 + n_pages]
    (q_ref, kvn_ref, winn_ref, win_all_ref, gate_ref, win_ref, w1_ref, w2_ref, gk0_ref, bias_ref, ex_ref,
     o_ref, win_out_ref, src_ref, ksel_ref, vsel_ref, wcol_ref) = refs[1 + n_pages:]
    b = pl.program_id(0)
    nbatch = win_all_ref.shape[0]
    past = n_pages * PAGE_SIZE
    n_cmp = past // CMP_BLOCK
    n_sel = past // SEL_BLOCK
    heads = NSA_HEADS

    @pl.when(b == 0)
    def _():
        for t in range(2):
            wcol_ref[t * LANES:(t + 1) * LANES, :] = win_all_ref[:, t * LANES:(t + 1) * LANES].T

    for k, page in enumerate(page_refs):
        cols = slice(k * PAGE_SIZE, (k + 1) * PAGE_SIZE)
        _stage_cmp_chunk(src_ref, k, page[0:LANES, :], page[LANES:2 * LANES, :], n_cmp)
        ksel_ref[:, cols] = page[2 * LANES:3 * LANES, :].astype(bf16)
        vsel_ref[:, cols] = page[3 * LANES:4 * LANES, :].astype(bf16)

    q = q_ref[...]
    qf = q.astype(f32)
    hrow = lax.broadcasted_iota(i32, (heads, 1), 0)
    g0 = hrow < NSA_GROUP
    bias = bias_ref[...]
    b_s = bias[:, 0:past]
    b_w = bias[:, past:past + n_win]
    b_c = bias[:, past + n_win:past + n_win + n_cmp]
    b_new = bias[:, past + n_win + LANES:past + n_win + LANES + 1]

    def pick(o2):
        return jnp.where(g0, o2[:, 0:HEAD_DIM], o2[:, HEAD_DIM:2 * HEAD_DIM])

    cmp = _compress(src_ref, w1_ref, w2_ref, gk0_ref[...], n_cmp)
    ck = cmp[:, 0:LANES].astype(bf16)
    cv = cmp[:, LANES:2 * LANES].astype(bf16)
    s_c = lax.dot_general(q, ck, NT, preferred_element_type=f32) * SCALE + b_c
    e_c, den_c = _softmax_parts(s_c, jnp.full(s_c.shape, True))
    p_c = e_c / jnp.where(den_c > 0, den_c, 1.0)
    o_c = pick(jnp.dot(p_c.astype(bf16), cv, preferred_element_type=f32))

    ps = jnp.concatenate([jnp.sum(p_c[0:NSA_GROUP], axis=0, keepdims=True),
                          jnp.sum(p_c[NSA_GROUP:heads], axis=0, keepdims=True)], axis=0)
    imp = ps[:, 0:n_sel] + ps[:, n_sel:2 * n_sel]
    cur = past // SEL_BLOCK
    jrow = lax.broadcasted_iota(i32, (NSA_KV_HEADS, n_sel), 1)
    forced = (jrow == 0) | (jrow >= cur - 1)
    score_row = jnp.where(forced, FORCED_SCORE, imp)
    n_keep = min(TOP_N, n_sel + 1) - 1
    ri = lax.broadcasted_iota(i32, (n_sel, n_sel), 0)
    ci = lax.broadcasted_iota(i32, (n_sel, n_sel), 1)
    sels = []
    for g in range(NSA_KV_HEADS):
        rowb = jnp.broadcast_to(score_row[g:g + 1], (n_sel, n_sel))
        colb = jnp.sum(jnp.where(ri == ci, rowb, 0.0), axis=-1, keepdims=True)
        better = (colb > rowb) | ((colb == rowb) & (ri < ci))
        cnt = jnp.sum(jnp.where(better, 1.0, 0.0), axis=0, keepdims=True)
        sels.append(jnp.where(cnt < n_keep, 1.0, 0.0))
    sel = jnp.concatenate(sels, axis=0).astype(bf16)
    selx = jnp.dot(sel, ex_ref[...], preferred_element_type=f32)
    mask_s = jnp.where(g0, selx[0:1], selx[1:2]) > 0.5

    def with_new(k_t, v_t, b_past, mask, k_new, v_new):
        s_past = jnp.dot(q, k_t, preferred_element_type=f32) * SCALE + b_past
        s_new = jnp.sum(qf * k_new, axis=-1, keepdims=True) * SCALE + b_new
        sm = s_past if mask is None else jnp.where(mask, s_past, NEG)
        m = jnp.maximum(jnp.max(sm, axis=-1, keepdims=True), s_new)
        e = jnp.exp(sm - m)
        e_new = jnp.exp(s_new - m)
        den = jnp.sum(e, axis=-1, keepdims=True) + e_new
        o2 = lax.dot_general(e.astype(bf16), v_t, NT, preferred_element_type=f32) + e_new * v_new
        return pick(o2 / den)

    kvn = kvn_ref[...]
    o_s = with_new(ksel_ref[...], vsel_ref[...], b_s, mask_s, kvn[:, 2 * LANES:3 * LANES], kvn[:, 3 * LANES:4 * LANES])
    winn = winn_ref[...]
    o_w = with_new(win_ref[0:LANES, :].astype(bf16), win_ref[LANES:2 * LANES, :].astype(bf16), b_w, None,
                   winn[:, 0:LANES], winn[:, LANES:2 * LANES])

    gates = gate_ref[...]
    o_ref[...] = (gates[:, 0:1] * o_c + gates[:, 1:2] * o_s + gates[:, 2:3] * o_w).astype(bf16)

    lane_b = lax.broadcasted_iota(i32, wcol_ref.shape, 1)
    new_col = jnp.sum(jnp.where(lane_b == b, wcol_ref[...], 0.0), axis=-1, keepdims=True)
    lane_w = lax.broadcasted_iota(i32, win_ref.shape, 1)
    win_out_ref[...] = jnp.where(lane_w == n_win - 1, new_col, pltpu.roll(win_ref[...], n_win - 1, axis=1))


def _nsa_decode(cache_t, page_table, q_bd, kv32, win32, gates, win_t, w1bd, w2bd, gk0, bias, expand):
    Bs, n_pages = page_table.shape
    n_win = win_t.shape[2]
    past = n_pages * PAGE_SIZE
    n_cmp = past // CMP_BLOCK
    per_b = lambda shape: pl.BlockSpec((None,) + shape, lambda b, pt: (b,) + (0,) * len(shape))
    const = lambda a: pl.BlockSpec(a.shape, lambda b, pt: (0,) * a.ndim)
    page_spec = lambda k: pl.BlockSpec((None, 4 * LANES, PAGE_SIZE), lambda b, pt: (pt[b, k], 0, 0))
    grid_spec = pltpu.PrefetchScalarGridSpec(
        num_scalar_prefetch=1,
        grid=(Bs,),
        in_specs=[page_spec(k) for k in range(n_pages)] + [
            per_b((NSA_HEADS, LANES)), per_b((1, 4 * LANES)), per_b((1, 2 * LANES)), const(win32),
            per_b((NSA_HEADS, 3)), per_b((2 * LANES, n_win)), const(w1bd), const(w2bd), const(gk0), const(bias),
            const(expand)],
        out_specs=(per_b((NSA_HEADS, HEAD_DIM)), per_b((2 * LANES, n_win))),
        scratch_shapes=[pltpu.VMEM((2, _cmp_staging_rows(n_cmp), LANES), f32),
                        pltpu.VMEM((LANES, past), bf16), pltpu.VMEM((LANES, past), bf16),
                        pltpu.VMEM((2 * LANES, Bs), f32)],
    )
    return pl.pallas_call(
        functools.partial(_nsa_decode_kernel, n_pages=n_pages, n_win=n_win),
        out_shape=(jax.ShapeDtypeStruct((Bs, NSA_HEADS, HEAD_DIM), bf16),
                   jax.ShapeDtypeStruct((Bs, 2 * LANES, n_win), f32)),
        grid_spec=grid_spec,
        compiler_params=_params(("arbitrary",)),
        name="nsa_decode",
    )(page_table, *([cache_t] * n_pages), q_bd.reshape(Bs, NSA_HEADS, LANES), kv32.reshape(Bs, 1, 4 * LANES),
      win32.reshape(Bs, 1, 2 * LANES), win32, gates[:, 0:3 * NSA_HEADS].reshape(Bs, NSA_HEADS, 3), win_t,
      w1bd, w2bd, gk0, bias, expand)


def _hgrn_decode_kernel(h4_ref, lb_ref, gn_ref, s_ref, o_ref, s_out_ref, ft_ref):
    b = pl.program_id(0)
    nbatch = h4_ref.shape[0]

    @pl.when(b == 0)
    def _():
        for h in range(HG_HEADS):
            f, _, _ = _hgrn_gates(h4_ref[:, HG_WIDTH + h * HG_DIM:HG_WIDTH + (h + 1) * HG_DIM],
                                  lb_ref[:, h * HG_DIM:(h + 1) * HG_DIM])
            ft_ref[h] = f.T

    lane = lax.broadcasted_iota(i32, (HG_DIM, nbatch), 1)
    row = h4_ref[pl.ds(b, 1), :]
    gn = gn_ref[...]
    for h in range(HG_HEADS):
        part = lambda k: row[:, k * HG_WIDTH + h * HG_DIM:k * HG_WIDTH + (h + 1) * HG_DIM]
        f_col = jnp.sum(jnp.where(lane == b, ft_ref[h], 0.0), axis=-1, keepdims=True)
        s_new = f_col * s_ref[h] + (1.0 - f_col) * part(2)
        s_out_ref[h] = s_new
        q = jnp.broadcast_to(part(0) * HG_SCALE, (8, HG_DIM)).astype(bf16)
        o = jnp.dot(q, s_new.astype(bf16), preferred_element_type=f32)[0:1]
        o_ref[:, h * HG_DIM:(h + 1) * HG_DIM] = _hgrn_out(o, gn, part(3)).astype(bf16)


def _hgrn_decode(h4, lb, gn, state):
    Bs = h4.shape[0]
    sspec = pl.BlockSpec((None, HG_HEADS, HG_DIM, HG_DIM), lambda b: (b, 0, 0, 0))
    return pl.pallas_call(
        _hgrn_decode_kernel,
        out_shape=(jax.ShapeDtypeStruct((Bs, 1, HG_WIDTH), bf16),
                   jax.ShapeDtypeStruct((Bs, HG_HEADS, HG_DIM, HG_DIM), f32)),
        grid=(Bs,),
        in_specs=[pl.BlockSpec(h4.shape, lambda b: (0, 0)), pl.BlockSpec(lb.shape, lambda b: (0, 0)),
                  pl.BlockSpec(gn.shape, lambda b: (0, 0)), sspec],
        out_specs=(pl.BlockSpec((None, 1, HG_WIDTH), lambda b: (b, 0, 0)), sspec),
        scratch_shapes=[pltpu.VMEM((HG_HEADS, HG_DIM, Bs), f32)],
        compiler_params=_params(("arbitrary",)),
        name="hgrn_decode",
    )(h4, lb, gn, state)


def _split_w_in(w):
    idx = np.cumsum(SPLITS)[:-1]
    return jnp.split(w, [int(v) for v in idx], axis=1)


def kernel(x_prompt, x_sample, cache_kv, page_table, state_kv_win, state_hgrn, state_conv, rel_table, hg_lb_logits,
           norm1_g, w_in, q_norm_g, k_norm_g, phi_k_w1, phi_k_w2, phi_v_w1, phi_v_w2, hg_norm_g, w_nsa_out, w_hg_out,
           w_o, norm2_g, w_gate, w_up, conv_w, conv_b, w_down):
    assert w_in.shape[0] == 1, "one layer"
    Bp, L, _ = x_prompt.shape
    Bs = x_sample.shape[0]
    n_pages = page_table.shape[1]
    past = n_pages * PAGE_SIZE
    n_win = state_kv_win.shape[2]
    assert x_sample.shape[1] == 1 and n_win == WINDOW and past % (2 * CMP_BLOCK) == 0 and L % LANES == 0

    wq, wkv, wgl, whq, whf, whi, whg, wmg = _split_w_in(w_in[0])
    wgl = jnp.pad(wgl, ((0, 0), (0, LANES - wgl.shape[1])))
    w_prompt = jnp.concatenate([wq, wkv, whq, whf, whi, whg, wmg, wgl], axis=1).astype(bf16)
    wq_h = wq.reshape(D_MODEL, NSA_HEADS, HEAD_DIM)
    zq = jnp.zeros_like(wq_h)
    in_g0 = (np.arange(NSA_HEADS) < NSA_GROUP)[None, :, None]
    wq_bd = jnp.where(in_g0, jnp.concatenate([wq_h, zq], axis=-1), jnp.concatenate([zq, wq_h], axis=-1))
    w_sample = jnp.concatenate([wq_bd.reshape(D_MODEL, NSA_HEADS * LANES), wkv, whq, whf, whi, whg, wmg, wgl],
                               axis=1).astype(bf16)
    g1 = norm1_g[0][None, :]
    gq = q_norm_g[0]
    gq_prompt = jnp.tile(gq, NSA_HEADS)[None, :]
    gq_sample = jnp.tile(gq, 2 * NSA_HEADS)[None, :]
    gk = [jnp.tile(k_norm_g[0, s], NSA_KV_HEADS)[None, :] for s in range(3)]

    def block_diag(mats):
        n = len(mats)
        out = jnp.zeros(mats[0].shape[:-2] + (n * HEAD_DIM, n * HEAD_DIM), f32)
        for t, m in enumerate(mats):
            out = out.at[..., t * HEAD_DIM:(t + 1) * HEAD_DIM, t * HEAD_DIM:(t + 1) * HEAD_DIM].set(m)
        return out.astype(bf16)

    w1bd = block_diag([phi_k_w1[0], phi_k_w1[0], phi_v_w1[0], phi_v_w1[0]])
    w2bd = block_diag([phi_k_w2[0], phi_k_w2[0], phi_v_w2[0], phi_v_w2[0]])
    lb = jnp.cumsum(jax.nn.softmax(hg_lb_logits.astype(f32), axis=0), axis=0)[0][None, :]
    gn = hg_norm_g[0][None, :]
    ffn_w = (w_nsa_out[0].astype(bf16), w_hg_out[0].astype(bf16), w_o[0].astype(bf16), norm2_g[0][None, :],
             w_gate[0].astype(bf16), w_up[0].astype(bf16), conv_w[0], conv_b[0][None, :], w_down[0].astype(bf16))

    idx_c, idx_t = _prompt_bucket_tables(L)
    tc = _bias_tables(rel_table, jnp.asarray(idx_c)).reshape(L // SEL_BLOCK, NSA_KV_HEADS, NSA_GROUP * SEL_BLOCK, -1)
    tt = _bias_tables(rel_table, jnp.asarray(idx_t)).reshape(5, NSA_KV_HEADS, NSA_GROUP * SEL_BLOCK, LANES)
    bias_dec = _bias_tables(rel_table, jnp.asarray(_decode_bucket_table(past, n_win))[None]).reshape(NSA_HEADS, -1)
    ex_prompt = jnp.asarray(_expand_np(L // SEL_BLOCK, L), dtype=bf16)
    ex_sample = jnp.asarray(_expand_np(past // SEL_BLOCK, past), dtype=bf16)

    xp = x_prompt.reshape(Bp * L, D_MODEL)
    qn, kv_t, win32, katt, h4, mgs, gates = _in_proj(xp, g1, w_prompt, gq_prompt, gk[1], gk[2],
                                                     qw=NSA_WIDTH, q_seg=HEAD_DIM, tm=256, seq_len=L)
    cmp = _compress_prompt(kv_t, w1bd, w2bd, gk[0])
    o_nsa = _nsa_prompt(qn, gates, katt, cmp, tc, tt, ex_prompt, Bp, L)
    o_hg, s_prompt = _hgrn_prompt(h4, lb, gn, Bp, L)
    tm_ffn = 512
    yp, gate_tails = _ffn(xp, o_nsa, o_hg, mgs, *ffn_w, tm=tm_ffn, tf=256, seq_len=L)
    conv_p = gate_tails.reshape(Bp, L // tm_ffn, 2, D_FF)[:, -1]
    n_keep_win = min(WINDOW, L)
    win_p = win32.reshape(Bp, L, 2, NSA_KV_HEADS, HEAD_DIM)[:, L - n_keep_win:]

    xs = x_sample.reshape(Bs, D_MODEL)
    q_bd, kv32_s, win32_s, _, h4_s, mgs_s, gates_s = _in_proj(xs, g1, w_sample, gq_sample, gk[1], gk[2],
                                                              qw=NSA_HEADS * LANES, q_seg=LANES, tm=Bs)
    cache_t = jnp.swapaxes(cache_kv[0].reshape(cache_kv.shape[1], PAGE_SIZE, 4 * LANES), 1, 2)
    win_t = jnp.swapaxes(state_kv_win[0].reshape(Bs, n_win, 2 * LANES), 1, 2)
    o_nsa_s, win_out_t = _nsa_decode(cache_t, page_table, q_bd, kv32_s, win32_s, gates_s, win_t, w1bd, w2bd, gk[0],
                                     bias_dec, ex_sample)
    win_s = jnp.swapaxes(win_out_t, 1, 2)
    kv32 = jnp.swapaxes(kv_t, 1, 2)
    o_hg_s, s_sample = _hgrn_decode(h4_s, lb, gn, state_hgrn[0])
    ys, gate_s = _ffn(xs, o_nsa_s.reshape(Bs, NSA_WIDTH), o_hg_s.reshape(Bs, HG_WIDTH), mgs_s, *ffn_w,
                      tm=Bs, tf=256, conv_state=state_conv[0])
    conv_s = jnp.stack([state_conv[0][:, 1], gate_s], axis=1)

    kvh = (NSA_KV_HEADS, HEAD_DIM)
    return (yp.reshape(Bp, L, D_MODEL), ys.reshape(Bs, 1, D_MODEL),
            kv32.reshape((1, Bp, L, 4) + kvh), kv32_s.reshape((1, Bs, 1, 4) + kvh),
            win_p[None], win_s.reshape((1, Bs, n_win, 2) + kvh),
            s_prompt[None], s_sample[None], conv_p[None], conv_s[None])
```

```python
import functools
import math

import numpy as np
import jax
import jax.numpy as jnp
from jax import lax
from jax.experimental import pallas as pl
from jax.experimental.pallas import tpu as pltpu

f32, bf16, i32 = jnp.float32, jnp.bfloat16, jnp.int32

D_MODEL = 1024
PAGE_SIZE = 128
NSA_HEADS, NSA_KV_HEADS, NSA_GROUP, HEAD_DIM = 8, 2, 4, 64
CMP_BLOCK, SEL_BLOCK, TOP_N, WINDOW = 32, 64, 16, 512
SCALE = HEAD_DIM ** -0.5
NSA_WIDTH = NSA_HEADS * HEAD_DIM
KV_WIDTH = NSA_KV_HEADS * HEAD_DIM
HG_HEADS, HG_DIM = 4, 128
HG_WIDTH = HG_HEADS * HG_DIM
HG_SCALE = HG_DIM ** -0.5
HG_BLOCK = 16
REL_BUCKETS, REL_MAX_DIST = 32, 128
D_FF = 2816
EPS = 1e-6
SPLITS = (NSA_WIDTH, 6 * KV_WIDTH, 3 * NSA_HEADS, HG_WIDTH, HG_WIDTH, HG_WIDTH, HG_WIDTH, 2 * D_MODEL)
LANES = 128
NEG = -1e30
FORCED_SCORE, INVALID_SCORE = 8.0, -1.0
VMEM_LIMIT = 56 * 1024 * 1024

NT = (((1,), (1,)), ((), ()))
TN = (((0,), (0,)), ((), ()))


def _params(sem):
    return pltpu.CompilerParams(dimension_semantics=sem, vmem_limit_bytes=VMEM_LIMIT)


def _bucket_np(dist):
    n = np.maximum(dist, 0)
    exact = REL_BUCKETS // 2
    ratio = np.log(np.maximum(n, 1).astype(np.float32) / np.float32(exact)) / np.float32(math.log(REL_MAX_DIST / exact))
    big = exact + (ratio.astype(np.float32) * np.float32(REL_BUCKETS - exact)).astype(np.int32)
    return np.where(n < exact, n, np.minimum(big, REL_BUCKETS - 1)).astype(np.int32)


def _cmp_perm(n_cmp):
    half = n_cmp // 2
    c = np.arange(n_cmp)
    return np.where(c < half, 2 * c, 2 * (c - half) + 1)


def _prompt_bucket_tables(L):
    nqb = L // SEL_BLOCK
    n_cmp = L // CMP_BLOCK
    qi = np.arange(SEL_BLOCK)
    c_end = (_cmp_perm(n_cmp) + 1) * CMP_BLOCK - 1
    idx_c = np.stack([_bucket_np((i * SEL_BLOCK + qi)[:, None] - c_end[None, :]) for i in range(nqb)])
    kj = np.arange(SEL_BLOCK)
    def tile(delta):
        return _bucket_np(delta * SEL_BLOCK + qi[:, None] - kj[None, :])
    idx_t = np.stack([np.concatenate([tile(e), tile(e - 1)], axis=1) for e in range(5)])
    return idx_c.astype(np.int32), idx_t.astype(np.int32)


def _decode_bucket_table(past_len, n_win):
    n_cmp = past_len // CMP_BLOCK
    c_end = (_cmp_perm(n_cmp) + 1) * CMP_BLOCK - 1
    d_c = np.zeros((LANES,), np.int64)
    d_c[:n_cmp] = past_len - c_end
    d_s = past_len - np.arange(past_len)
    d_w = n_win - np.arange(n_win)
    d_new = np.zeros((LANES,), np.int64)
    return _bucket_np(np.concatenate([d_s, d_w, d_c, d_new]))[None, :].astype(np.int32)


def _expand_np(n_blocks, n_keys):
    e = (np.arange(n_keys)[None, :] // SEL_BLOCK == np.arange(n_blocks)[:, None])
    return e.astype(np.float32)


def _seg_rmsnorm(x, gain, seg):
    outs = []
    for t in range(x.shape[1] // LANES):
        xt = x[:, t * LANES:(t + 1) * LANES]
        sq = xt * xt
        if seg == LANES:
            r = lax.rsqrt(jnp.sum(sq, axis=-1, keepdims=True) * (1.0 / HEAD_DIM) + EPS)
        else:
            lo = lax.broadcasted_iota(i32, xt.shape, 1) < HEAD_DIM
            s_lo = jnp.sum(jnp.where(lo, sq, 0.0), axis=-1, keepdims=True)
            s_hi = jnp.sum(jnp.where(lo, 0.0, sq), axis=-1, keepdims=True)
            r = jnp.where(lo, lax.rsqrt(s_lo * (1.0 / HEAD_DIM) + EPS), lax.rsqrt(s_hi * (1.0 / HEAD_DIM) + EPS))
        outs.append(xt * r)
    y = outs[0] if len(outs) == 1 else jnp.concatenate(outs, axis=1)
    return y * gain


def _rmsnorm_rows(x, gain):
    return x * lax.rsqrt(jnp.mean(x * x, axis=-1, keepdims=True) + EPS) * gain


def _softmax_parts(s, mask):
    s = jnp.where(mask, s, NEG)
    m = jnp.max(s, axis=-1, keepdims=True)
    e = jnp.where(mask, jnp.exp(s - m), 0.0)
    return e, jnp.sum(e, axis=-1, keepdims=True)


def _bias_table_kernel(tab_ref, idx_ref, out_ref):
    idx = idx_ref[...]
    for h in range(NSA_HEADS):
        acc = jnp.zeros(idx.shape, f32)
        for b in range(REL_BUCKETS):
            acc = jnp.where(idx == b, tab_ref[b, h], acc)
        out_ref[h] = acc


def _bias_tables(rel_table, idx):
    n, rows, w = idx.shape
    return pl.pallas_call(
        _bias_table_kernel,
        out_shape=jax.ShapeDtypeStruct((n, NSA_HEADS, rows, w), f32),
        grid=(n,),
        in_specs=[pl.BlockSpec(memory_space=pltpu.SMEM),
                  pl.BlockSpec((None, rows, w), lambda i: (i, 0, 0))],
        out_specs=pl.BlockSpec((None, NSA_HEADS, rows, w), lambda i: (i, 0, 0, 0)),
        compiler_params=_params(("arbitrary",)),
        name="bias_tables",
    )(rel_table, idx)


def _in_proj_kernel(x_ref, g1_ref, w_ref, gq_ref, gk1_ref, gk2_ref,
                    q_ref, kv_ref, win_ref, katt_ref, h4_ref, mg_ref, gl_ref, *, qw, q_seg, kv_feature_major):
    x = x_ref[...]
    xn = _rmsnorm_rows(x, g1_ref[...]).astype(bf16)

    def proj(c0, width):
        return jnp.dot(xn, w_ref[:, c0:c0 + width], preferred_element_type=f32)

    q_ref[...] = (_seg_rmsnorm(proj(0, qw), gq_ref[...], q_seg) * SCALE).astype(bf16)
    c = qw
    kv = proj(c, 6 * KV_WIDTH)
    c += 6 * KV_WIDTH
    k_sel = _seg_rmsnorm(kv[:, 2 * KV_WIDTH:3 * KV_WIDTH], gk1_ref[...], HEAD_DIM)
    k_win = _seg_rmsnorm(kv[:, 4 * KV_WIDTH:5 * KV_WIDTH], gk2_ref[...], HEAD_DIM)
    v_sel = kv[:, 3 * KV_WIDTH:4 * KV_WIDTH]
    v_win = kv[:, 5 * KV_WIDTH:6 * KV_WIDTH]
    if kv_feature_major:
        kv_ref[0:2 * KV_WIDTH, :] = kv[:, 0:2 * KV_WIDTH].T
        kv_ref[2 * KV_WIDTH:3 * KV_WIDTH, :] = k_sel.T
        kv_ref[3 * KV_WIDTH:4 * KV_WIDTH, :] = v_sel.T
    else:
        kv_ref[:, 0:2 * KV_WIDTH] = kv[:, 0:2 * KV_WIDTH]
        kv_ref[:, 2 * KV_WIDTH:3 * KV_WIDTH] = k_sel
        kv_ref[:, 3 * KV_WIDTH:4 * KV_WIDTH] = v_sel
    win_ref[:, 0:KV_WIDTH] = k_win
    win_ref[:, KV_WIDTH:2 * KV_WIDTH] = v_win
    katt_ref[:, 0:KV_WIDTH] = k_sel.astype(bf16)
    katt_ref[:, KV_WIDTH:2 * KV_WIDTH] = v_sel.astype(bf16)
    katt_ref[:, 2 * KV_WIDTH:3 * KV_WIDTH] = k_win.astype(bf16)
    katt_ref[:, 3 * KV_WIDTH:4 * KV_WIDTH] = v_win.astype(bf16)
    h4_ref[...] = proj(c, 4 * HG_WIDTH)
    c += 4 * HG_WIDTH
    mg_ref[...] = jax.nn.sigmoid(proj(c, 2 * D_MODEL))
    c += 2 * D_MODEL
    gl_ref[...] = jax.nn.sigmoid(proj(c, LANES))


def _in_proj(x2d, g1, w, gq, gk1, gk2, *, qw, q_seg, tm, seq_len=None):
    n = x2d.shape[0]
    cw = w.shape[1]
    row = lambda width: pl.BlockSpec((tm, width), lambda i: (i, 0))
    full = lambda a: pl.BlockSpec(a.shape, lambda i: (0,) * a.ndim)
    widths = (qw, 4 * KV_WIDTH, 2 * KV_WIDTH, 4 * KV_WIDTH, 4 * HG_WIDTH, 2 * D_MODEL, LANES)
    dtypes = (bf16, f32, f32, bf16, f32, f32, f32)
    out_shape = [jax.ShapeDtypeStruct((n, wd), dt) for wd, dt in zip(widths, dtypes)]
    out_specs = [row(wd) for wd in widths]
    if seq_len is not None:
        tiles = seq_len // tm
        out_shape[1] = jax.ShapeDtypeStruct((n // seq_len, 4 * KV_WIDTH, seq_len), f32)
        out_specs[1] = pl.BlockSpec((None, 4 * KV_WIDTH, tm), lambda i: (i // tiles, 0, i % tiles))
    return pl.pallas_call(
        functools.partial(_in_proj_kernel, qw=qw, q_seg=q_seg, kv_feature_major=seq_len is not None),
        out_shape=tuple(out_shape),
        grid=(n // tm,),
        in_specs=[row(D_MODEL), full(g1), pl.BlockSpec((D_MODEL, cw), lambda i: (0, 0)), full(gq), full(gk1), full(gk2)],
        out_specs=tuple(out_specs),
        compiler_params=_params(("arbitrary",)),
        name="in_proj",
    )(x2d, g1, w, gq, gk1, gk2)


CMP_SLOT = CMP_BLOCK + 1


def _cmp_staging_rows(n_cmp):
    return -(-n_cmp * CMP_SLOT // 8) * 8


def _stage_cmp_chunk(src_ref, chunk, kc_t, vc_t, n_cmp):
    per_chunk = LANES // CMP_BLOCK
    for s, t in enumerate((kc_t.T, vc_t.T)):
        for nl in range(per_chunk):
            n = chunk * per_chunk + nl
            slot = n // 2 + (n % 2) * (n_cmp // 2)
            src_ref[s, slot * CMP_SLOT:slot * CMP_SLOT + CMP_BLOCK, :] = t[nl * CMP_BLOCK:(nl + 1) * CMP_BLOCK]


def _compress(src_ref, w1_ref, w2_ref, gk0, n_cmp):
    acc = jnp.zeros((n_cmp, 2 * LANES), f32)
    for j in range(CMP_BLOCK):
        xj = jnp.concatenate([src_ref[s, pl.ds(j, n_cmp, stride=CMP_SLOT), :] for s in range(2)], axis=1)
        acc = acc + jnp.dot(xj.astype(bf16), w1_ref[j], preferred_element_type=f32)
    hmid = jax.nn.gelu(acc).astype(bf16)
    y = jnp.dot(hmid, w2_ref[...], preferred_element_type=f32)
    ck = _seg_rmsnorm(y[:, 0:LANES], gk0, HEAD_DIM)
    return jnp.concatenate([ck, y[:, LANES:2 * LANES]], axis=1)


def _compress_prompt_kernel(kv_ref, w1_ref, w2_ref, gk0_ref, out_ref, src_ref, *, n_cmp):
    for c in range(kv_ref.shape[1] // LANES):
        cols = slice(c * LANES, (c + 1) * LANES)
        _stage_cmp_chunk(src_ref, c, kv_ref[0:LANES, cols], kv_ref[LANES:2 * LANES, cols], n_cmp)
    out_ref[...] = _compress(src_ref, w1_ref, w2_ref, gk0_ref[...], n_cmp).astype(bf16)


def _compress_prompt(kv_t, w1bd, w2bd, gk0):
    B, _, L = kv_t.shape
    n_cmp = L // CMP_BLOCK
    return pl.pallas_call(
        functools.partial(_compress_prompt_kernel, n_cmp=n_cmp),
        out_shape=jax.ShapeDtypeStruct((B, n_cmp, 2 * LANES), bf16),
        grid=(B,),
        in_specs=[pl.BlockSpec((None, 2 * LANES, L), lambda b: (b, 0, 0)),
                  pl.BlockSpec(w1bd.shape, lambda b: (0, 0, 0)),
                  pl.BlockSpec(w2bd.shape, lambda b: (0, 0)),
                  pl.BlockSpec(gk0.shape, lambda b: (0, 0))],
        out_specs=pl.BlockSpec((None, n_cmp, 2 * LANES), lambda b: (b, 0, 0)),
        scratch_shapes=[pltpu.VMEM((2, _cmp_staging_rows(n_cmp), LANES), f32)],
        compiler_params=_params(("arbitrary",)),
        name="compress_prompt",
    )(kv_t, w1bd, w2bd, gk0)


def _select_blocks_t(score, valid, n_keep):
    nb = score.shape[0]
    jj = lax.broadcasted_iota(i32, score.shape, 0)
    cnt = jnp.zeros(score.shape, f32)
    for k in range(nb):
        sk = score[k:k + 1, :]
        better = (sk > score) | ((sk == score) & (jj > k))
        cnt = cnt + jnp.where(better, 1.0, 0.0)
    return (cnt < n_keep) & valid


def _sel_tile_classes(n_tiles):
    return sorted({-(-n_tiles * c // 4) for c in range(1, 5)})


def _nsa_prompt_kernel(q_ref, gate_ref, katt_ref, cmp_ref, tc_ref, tt_ref, ex_ref, o_ref,
                       selx_ref, s_ref, os_ref, *, L):
    i = pl.program_id(1)
    n_cmp = L // CMP_BLOCK
    n_sel = L // SEL_BLOCK
    n_tiles = L // LANES
    qb = SEL_BLOCK
    rows = NSA_GROUP * qb
    groups = range(NSA_KV_HEADS)
    gates = gate_ref[...]
    row64 = lax.broadcasted_iota(i32, (qb, LANES), 0)
    lane = lax.broadcasted_iota(i32, (qb, LANES), 1)
    qpos = i * qb + row64

    qs = [jnp.concatenate(
        [q_ref[:, (NSA_GROUP * g + r) * HEAD_DIM:(NSA_GROUP * g + r + 1) * HEAD_DIM] for r in range(NSA_GROUP)],
        axis=0) for g in groups]

    o_c, imp_t = [], []
    col = lax.broadcasted_iota(i32, (rows, n_cmp), 1)
    rq = lax.broadcasted_iota(i32, (rows, n_cmp), 0) & (qb - 1)
    blk = jnp.where(col < n_cmp // 2, 2 * col, 2 * col - (n_cmp - 1))
    vis = (blk + 1) * CMP_BLOCK - 1 <= i * qb + rq
    for g in groups:
        ck = cmp_ref[:, g * HEAD_DIM:(g + 1) * HEAD_DIM]
        cv = cmp_ref[:, LANES + g * HEAD_DIM:LANES + (g + 1) * HEAD_DIM]
        s = lax.dot_general(qs[g], ck, NT, preferred_element_type=f32) + tc_ref[g]
        e, den = _softmax_parts(s, vis)
        p = e / jnp.where(den > 0, den, 1.0)
        o_c.append(jnp.dot(p.astype(bf16), cv, preferred_element_type=f32))
        ps_t = (p[0:qb] + p[qb:2 * qb] + p[2 * qb:3 * qb] + p[3 * qb:4 * qb]).T
        imp_t.append(ps_t[0:n_sel] + ps_t[n_sel:2 * n_sel])

    imp = jnp.concatenate(imp_t, axis=1)
    jj = lax.broadcasted_iota(i32, imp.shape, 0)
    valid = jj <= i
    forced = valid & ((jj == 0) | (jj >= i - 1))
    score = jnp.where(forced, FORCED_SCORE, jnp.where(valid, imp, INVALID_SCORE))
    sel = _select_blocks_t(score, valid, min(TOP_N, n_sel))
    selx = lax.dot_general(jnp.where(sel, 1.0, 0.0).astype(bf16), ex_ref[...], TN, preferred_element_type=f32)
    for g in groups:
        for t in range(n_tiles):
            selx_ref[g, t] = selx[g * qb:(g + 1) * qb, t * LANES:(t + 1) * LANES]

    def attend(k_off, v_off, tile0, n_t, mask_fn):
        chunk = 4

        def key_rows(t0, count, off, g):
            r0 = pl.multiple_of((tile0 + t0) * LANES, LANES)
            return katt_ref[pl.ds(r0, count * LANES), off + g * HEAD_DIM:off + (g + 1) * HEAD_DIM]

        def scores(t0, count, mruns):
            dots = [lax.dot_general(qs[g], key_rows(t0, count, k_off, g), NT, preferred_element_type=f32)
                    for g in groups]
            out = list(mruns)
            for j in range(count):
                ta = tile0 + t0 + j
                bias_idx = jnp.clip(i - 2 * ta, 0, 4)
                kpos = ta * LANES + lane
                for g in groups:
                    s = dots[g][:, j * LANES:(j + 1) * LANES] + tt_ref[bias_idx, g]
                    msk = mask_fn(g, ta, kpos)
                    s = jnp.where(msk[None], s.reshape(NSA_GROUP, qb, LANES), NEG).reshape(rows, LANES)
                    s_ref[g, t0 + j] = s
                    out[g] = jnp.maximum(out[g], s)
            return tuple(out)

        def values(t0, count, carry, ms):
            ls, accs = list(carry[0]), list(carry[1])
            ps = []
            for g in groups:
                pj = [jnp.exp(s_ref[g, t0 + j] - ms[g]) for j in range(count)]
                for p in pj:
                    ls[g] = ls[g] + p
                ps.append(jnp.concatenate([p.astype(bf16) for p in pj], axis=1))
            for g in groups:
                accs[g] = accs[g] + jnp.dot(ps[g], key_rows(t0, count, v_off, g), preferred_element_type=f32)
            return tuple(ls), tuple(accs)

        def over_chunks(fn, carry):
            n_full = n_t // chunk
            if n_full:
                carry = lax.fori_loop(0, n_full, lambda c, cr: fn(c * chunk, chunk, cr), carry)
            if n_t % chunk:
                carry = fn(n_full * chunk, n_t % chunk, carry)
            return carry

        mruns = over_chunks(scores, tuple(jnp.full((rows, LANES), NEG, f32) for _ in groups))
        ms = [jnp.max(m, axis=-1, keepdims=True) for m in mruns]
        init = (tuple(jnp.zeros((rows, LANES), f32) for _ in groups),
                tuple(jnp.zeros((rows, HEAD_DIM), f32) for _ in groups))
        ls, accs = over_chunks(lambda t0, count, cr: values(t0, count, cr, ms), init)
        return [accs[g] / jnp.sum(ls[g], axis=-1, keepdims=True) for g in groups]

    need = i // 2 + 1
    prev = 0
    for n_t in _sel_tile_classes(n_tiles):
        @pl.when((need > prev) & (need <= n_t))
        def _(n_t=n_t):
            o_s = attend(0, KV_WIDTH, 0, n_t, lambda g, ta, kpos: (selx_ref[g, ta] > 0.5) & (kpos <= qpos))
            for g in groups:
                os_ref[g] = o_s[g]
        prev = n_t

    n_wt = min(WINDOW // LANES + 1, n_tiles)
    w0 = jnp.clip((i - WINDOW // qb) // 2, 0, n_tiles - n_wt)
    o_w = attend(2 * KV_WIDTH, 3 * KV_WIDTH, w0, n_wt,
                 lambda g, ta, kpos: (kpos <= qpos) & (qpos - kpos <= WINDOW))

    for g in groups:
        o_s = os_ref[g]
        for r in range(NSA_GROUP):
            h = NSA_GROUP * g + r
            sl = slice(r * qb, (r + 1) * qb)
            o = (gates[:, 3 * h:3 * h + 1] * o_c[g][sl] + gates[:, 3 * h + 1:3 * h + 2] * o_s[sl]
                 + gates[:, 3 * h + 2:3 * h + 3] * o_w[g][sl])
            o_ref[:, h * HEAD_DIM:(h + 1) * HEAD_DIM] = o.astype(bf16)


def _nsa_prompt(qn, gates, katt, cmp, tc, tt, expand, B, L):
    nqb = L // SEL_BLOCK
    qb = SEL_BLOCK
    n_tiles = L // LANES
    rows = NSA_GROUP * qb
    return pl.pallas_call(
        functools.partial(_nsa_prompt_kernel, L=L),
        out_shape=jax.ShapeDtypeStruct((B * L, NSA_WIDTH), bf16),
        grid=(B, nqb),
        in_specs=[pl.BlockSpec((qb, NSA_WIDTH), lambda b, i: (b * nqb + i, 0)),
                  pl.BlockSpec((qb, LANES), lambda b, i: (b * nqb + i, 0)),
                  pl.BlockSpec((L, 4 * KV_WIDTH), lambda b, i: (b, 0)),
                  pl.BlockSpec((None, L // CMP_BLOCK, 2 * LANES), lambda b, i: (b, 0, 0)),
                  pl.BlockSpec((None,) + tc.shape[1:], lambda b, i: (i, 0, 0, 0)),
                  pl.BlockSpec(tt.shape, lambda b, i: (0, 0, 0, 0)),
                  pl.BlockSpec(expand.shape, lambda b, i: (0, 0))],
        out_specs=pl.BlockSpec((qb, NSA_WIDTH), lambda b, i: (b * nqb + i, 0)),
        scratch_shapes=[pltpu.VMEM((NSA_KV_HEADS, n_tiles, qb, LANES), f32),
                        pltpu.VMEM((NSA_KV_HEADS, n_tiles, rows, LANES), f32),
                        pltpu.VMEM((NSA_KV_HEADS, rows, HEAD_DIM), f32)],
        compiler_params=_params(("arbitrary", "arbitrary")),
        name="nsa_prompt",
    )(qn, gates, katt, cmp, tc, tt, expand)


def _hgrn_gates(hf, lb):
    f = lb + (1.0 - lb) * jax.nn.sigmoid(hf)
    return f, 1.0 - f, jnp.log(f)


def _hgrn_out(o, gn, hgate):
    return _rmsnorm_rows(o, gn) * jax.nn.silu(hgate)


def _hgrn_prompt_kernel(h4_ref, lb_ref, gn_ref, o_ref, s_out_ref, st_ref, *, tc):
    c = pl.program_id(1)
    nb = HG_BLOCK
    nseq = h4_ref.shape[0]

    @pl.when(c == 0)
    def _():
        st_ref[...] = jnp.zeros_like(st_ref)

    t_row = lax.broadcasted_iota(i32, (nb, HG_DIM), 0)
    t_col = lax.broadcasted_iota(i32, (nb, 1), 0)
    gn = gn_ref[...]
    chains = [(sq, h) for sq in range(nseq) for h in range(HG_HEADS)]

    def cumsum_rows(x):
        shift = 1
        while shift < nb:
            x = x + jnp.where(t_row >= shift, pltpu.roll(x, shift, axis=0), 0.0)
            shift *= 2
        return x

    def block(bi, _):
        r0 = pl.multiple_of(bi * nb, nb)

        def sl(sq, h, part):
            return h4_ref[sq, pl.ds(r0, nb), part * HG_WIDTH + h * HG_DIM:part * HG_WIDTH + (h + 1) * HG_DIM]

        qs, ks, vs, bs, sts, os_ = [], [], [], [], [], []
        for sq, h in chains:
            _, k, glog = _hgrn_gates(sl(sq, h, 1), lb_ref[:, h * HG_DIM:(h + 1) * HG_DIM])
            qs.append(sl(sq, h, 0) * HG_SCALE)
            ks.append(k)
            vs.append(sl(sq, h, 2))
            bs.append(cumsum_rows(glog))
        for c_, (sq, h) in enumerate(chains):
            sts.append(st_ref[sq, h])
            os_.append(lax.dot_general((qs[c_] * jnp.exp(bs[c_])).astype(bf16), sts[c_].astype(bf16), NT,
                                       preferred_element_type=f32))
        for c_ in range(len(chains)):
            q, k, v, b, o = qs[c_], ks[c_], vs[c_], bs[c_], os_[c_]
            for s in range(nb):
                w = q * k[s:s + 1] * jnp.exp(jnp.minimum(b - b[s:s + 1], 0.0))
                a = jnp.where(t_col >= s, jnp.sum(w, axis=-1, keepdims=True), 0.0)
                o = o + a * v[s:s + 1]
            os_[c_] = o
        for c_, (sq, h) in enumerate(chains):
            b_last = bs[c_][nb - 1:nb]
            kt = ks[c_] * jnp.exp(b_last - bs[c_])
            upd = lax.dot_general(vs[c_].astype(bf16), kt.astype(bf16), TN, preferred_element_type=f32)
            st_ref[sq, h] = jnp.exp(b_last) * sts[c_] + upd
            o_ref[sq, pl.ds(r0, nb), h * HG_DIM:(h + 1) * HG_DIM] = _hgrn_out(os_[c_], gn, sl(sq, h, 3)).astype(bf16)
        return 0

    lax.fori_loop(0, tc // nb, block, 0)

    @pl.when(c == pl.num_programs(1) - 1)
    def _():
        for sq in range(nseq):
            for h in range(HG_HEADS):
                s_out_ref[sq, h] = st_ref[sq, h].T


def _hgrn_prompt(h4, lb, gn, B, L, tc=256, nseq=4):
    nc = L // tc
    nseq = math.gcd(nseq, B)
    return pl.pallas_call(
        functools.partial(_hgrn_prompt_kernel, tc=tc),
        out_shape=(jax.ShapeDtypeStruct((B, L, HG_WIDTH), bf16),
                   jax.ShapeDtypeStruct((B, HG_HEADS, HG_DIM, HG_DIM), f32)),
        grid=(B // nseq, nc),
        in_specs=[pl.BlockSpec((nseq, tc, 4 * HG_WIDTH), lambda b, c: (b, c, 0)),
                  pl.BlockSpec(lb.shape, lambda b, c: (0, 0)),
                  pl.BlockSpec(gn.shape, lambda b, c: (0, 0))],
        out_specs=(pl.BlockSpec((nseq, tc, HG_WIDTH), lambda b, c: (b, c, 0)),
                   pl.BlockSpec((nseq, HG_HEADS, HG_DIM, HG_DIM), lambda b, c: (b, 0, 0, 0))),
        scratch_shapes=[pltpu.VMEM((nseq, HG_HEADS, HG_DIM, HG_DIM), f32)],
        compiler_params=_params(("arbitrary", "arbitrary")),
        name="hgrn_prompt",
    )(h4.reshape(B, L, 4 * HG_WIDTH), lb, gn)


def _ffn_kernel(*refs, decode, tiles_per_seq):
    if decode:
        (x_ref, on_ref, oh_ref, mg_ref, wn_ref, wh_ref, wo_ref, g2_ref, wg_ref, wu_ref, cw_ref, cb_ref, wd_ref,
         cs0_ref, cs1_ref, y_ref, gate_out_ref, x1_ref, xn2_ref, acc_ref) = refs
    else:
        (x_ref, on_ref, oh_ref, mg_ref, wn_ref, wh_ref, wo_ref, g2_ref, wg_ref, wu_ref, cw_ref, cb_ref, wd_ref,
         y_ref, gate_out_ref, x1_ref, xn2_ref, acc_ref, carry_ref) = refs
    i = pl.program_id(0)
    fi = pl.program_id(1)
    tm = x_ref.shape[0]

    @pl.when(fi == 0)
    def _():
        y_a = jnp.dot(on_ref[...], wn_ref[...], preferred_element_type=f32)
        y_b = jnp.dot(oh_ref[...], wh_ref[...], preferred_element_type=f32)
        mg = mg_ref[...]
        merged = mg[:, 0:D_MODEL] * y_a + mg[:, D_MODEL:2 * D_MODEL] * y_b
        x1 = x_ref[...] + jnp.dot(merged.astype(bf16), wo_ref[...], preferred_element_type=f32)
        x1_ref[...] = x1
        xn2_ref[...] = _rmsnorm_rows(x1, g2_ref[...]).astype(bf16)
        acc_ref[...] = jnp.zeros_like(acc_ref)

    xn2 = xn2_ref[...]
    gate = jnp.dot(xn2, wg_ref[...], preferred_element_type=f32)
    up = jnp.dot(xn2, wu_ref[...], preferred_element_type=f32)
    cw = cw_ref[...]
    if decode:
        prev2, prev1 = cs0_ref[...], cs1_ref[...]
        gate_out_ref[...] = gate
    else:
        @pl.when(i % tiles_per_seq == 0)
        def _():
            carry_ref[fi] = jnp.zeros(carry_ref.shape[1:], f32)
        carry = carry_ref[fi]
        rid = lax.broadcasted_iota(i32, gate.shape, 0)
        prev1 = jnp.where(rid == 0, carry[7:8], pltpu.roll(gate, 1, axis=0))
        prev2 = jnp.where(rid == 0, carry[6:7], jnp.where(rid == 1, carry[7:8], pltpu.roll(gate, 2, axis=0)))
        carry_ref[fi, 6:8, :] = gate[tm - 2:tm]
        gate_out_ref[...] = gate[tm - 2:tm]
    conv = cb_ref[...] + cw[0:1] * prev2 + cw[1:2] * prev1 + cw[2:3] * gate
    hmid = (jax.nn.silu(conv) * up).astype(bf16)
    acc_ref[...] += jnp.dot(hmid, wd_ref[...], preferred_element_type=f32)

    @pl.when(fi == pl.num_programs(1) - 1)
    def _():
        y_ref[...] = x1_ref[...] + acc_ref[...]


def _ffn(x2d, o_nsa, o_hg, mgs, wn, wh, wo, g2, wg, wu, cw, cb, wd, *, tm, tf, seq_len=None, conv_state=None):
    n = x2d.shape[0]
    decode = conv_state is not None
    nf = D_FF // tf
    row = lambda width: pl.BlockSpec((tm, width), lambda i, f: (i, 0))
    full = lambda a: pl.BlockSpec(a.shape, lambda i, f: (0,) * a.ndim)
    in_specs = [row(D_MODEL), row(NSA_WIDTH), row(HG_WIDTH), row(2 * D_MODEL), full(wn), full(wh), full(wo), full(g2),
                pl.BlockSpec((D_MODEL, tf), lambda i, f: (0, f)), pl.BlockSpec((D_MODEL, tf), lambda i, f: (0, f)),
                pl.BlockSpec((3, tf), lambda i, f: (0, f)), pl.BlockSpec((1, tf), lambda i, f: (0, f)),
                pl.BlockSpec((tf, D_MODEL), lambda i, f: (f, 0))]
    args = [x2d, o_nsa, o_hg, mgs, wn, wh, wo, g2, wg, wu, cw, cb, wd]
    scratch = [pltpu.VMEM((tm, D_MODEL), f32), pltpu.VMEM((tm, D_MODEL), bf16), pltpu.VMEM((tm, D_MODEL), f32)]
    if decode:
        in_specs += [pl.BlockSpec((tm, tf), lambda i, f: (i, f))] * 2
        args += [conv_state[:, 0], conv_state[:, 1]]
        gate_shape = jax.ShapeDtypeStruct((n, D_FF), f32)
        gate_spec = pl.BlockSpec((tm, tf), lambda i, f: (i, f))
        tiles_per_seq = 1
    else:
        tiles_per_seq = seq_len // tm
        gate_shape = jax.ShapeDtypeStruct((n // tm, 2, D_FF), f32)
        gate_spec = pl.BlockSpec((None, 2, tf), lambda i, f: (i, 0, f))
        scratch.append(pltpu.VMEM((nf, 8, tf), f32))
    return pl.pallas_call(
        functools.partial(_ffn_kernel, decode=decode, tiles_per_seq=tiles_per_seq),
        out_shape=(jax.ShapeDtypeStruct((n, D_MODEL), f32), gate_shape),
        grid=(n // tm, nf),
        in_specs=in_specs,
        out_specs=(row(D_MODEL), gate_spec),
        scratch_shapes=scratch,
        compiler_params=_params(("arbitrary", "arbitrary")),
        name="ffn_decode" if decode else "ffn_prompt",
    )(*args)


def _nsa_decode_kernel(*refs, n_pages, n_win):
    page_refs = refs[1:1 + n_pages]
    (q_ref, kvn_ref, winn_ref, win_all_ref, gate_ref, win_ref, w1_ref, w2_ref, gk0_ref, bias_ref, ex_ref,
     o_ref, win_out_ref, src_ref, ksel_ref, vsel_ref, wcol_ref) = refs[1 + n_pages:]
    b = pl.program_id(0)
    nbatch = win_all_ref.shape[0]
    past = n_pages * PAGE_SIZE
    n_cmp = past // CMP_BLOCK
    n_sel = past // SEL_BLOCK
    heads = NSA_HEADS

    @pl.when(b == 0)
    def _():
        for t in range(2):
            wcol_ref[t * LANES:(t + 1) * LANES, :] = win_all_ref[:, t * LANES:(t + 1) * LANES].T

    for k, page in enumerate(page_refs):
        cols = slice(k * PAGE_SIZE, (k + 1) * PAGE_SIZE)
        _stage_cmp_chunk(src_ref, k, page[0:LANES, :], page[LANES:2 * LANES, :], n_cmp)
        ksel_ref[:, cols] = page[2 * LANES:3 * LANES, :].astype(bf16)
        vsel_ref[:, cols] = page[3 * LANES:4 * LANES, :].astype(bf16)

    q = q_ref[...]
    qf = q.astype(f32)
    hrow = lax.broadcasted_iota(i32, (heads, 1), 0)
    g0 = hrow < NSA_GROUP
    bias = bias_ref[...]
    b_s = bias[:, 0:past]
    b_w = bias[:, past:past + n_win]
    b_c = bias[:, past + n_win:past + n_win + n_cmp]
    b_new = bias[:, past + n_win + LANES:past + n_win + LANES + 1]

    def pick(o2):
        return jnp.where(g0, o2[:, 0:HEAD_DIM], o2[:, HEAD_DIM:2 * HEAD_DIM])

    cmp = _compress(src_ref, w1_ref, w2_ref, gk0_ref[...], n_cmp)
    ck = cmp[:, 0:LANES].astype(bf16)
    cv = cmp[:, LANES:2 * LANES].astype(bf16)
    s_c = lax.dot_general(q, ck, NT, preferred_element_type=f32) + b_c
    e_c, den_c = _softmax_parts(s_c, jnp.full(s_c.shape, True))
    p_c = e_c / jnp.where(den_c > 0, den_c, 1.0)
    o_c = pick(jnp.dot(p_c.astype(bf16), cv, preferred_element_type=f32))

    ps = jnp.concatenate([jnp.sum(p_c[0:NSA_GROUP], axis=0, keepdims=True),
                          jnp.sum(p_c[NSA_GROUP:heads], axis=0, keepdims=True)], axis=0)
    imp = ps[:, 0:n_sel] + ps[:, n_sel:2 * n_sel]
    cur = past // SEL_BLOCK
    jrow = lax.broadcasted_iota(i32, (NSA_KV_HEADS, n_sel), 1)
    forced = (jrow == 0) | (jrow >= cur - 1)
    score_row = jnp.where(forced, FORCED_SCORE, imp)
    n_keep = min(TOP_N, n_sel + 1) - 1
    ri = lax.broadcasted_iota(i32, (n_sel, n_sel), 0)
    ci = lax.broadcasted_iota(i32, (n_sel, n_sel), 1)
    sels = []
    for g in range(NSA_KV_HEADS):
        rowb = jnp.broadcast_to(score_row[g:g + 1], (n_sel, n_sel))
        colb = jnp.sum(jnp.where(ri == ci, rowb, 0.0), axis=-1, keepdims=True)
        better = (colb > rowb) | ((colb == rowb) & (ri < ci))
        cnt = jnp.sum(jnp.where(better, 1.0, 0.0), axis=0, keepdims=True)
        sels.append(jnp.where(cnt < n_keep, 1.0, 0.0))
    sel = jnp.concatenate(sels, axis=0).astype(bf16)
    selx = jnp.dot(sel, ex_ref[...], preferred_element_type=f32)
    mask_s = jnp.where(g0, selx[0:1], selx[1:2]) > 0.5

    def with_new(k_t, v_t, b_past, mask, k_new, v_new):
        s_past = jnp.dot(q, k_t, preferred_element_type=f32) + b_past
        s_new = jnp.sum(qf * k_new, axis=-1, keepdims=True) + b_new
        sm = s_past if mask is None else jnp.where(mask, s_past, NEG)
        m = jnp.maximum(jnp.max(sm, axis=-1, keepdims=True), s_new)
        e = jnp.exp(sm - m)
        e_new = jnp.exp(s_new - m)
        den = jnp.sum(e, axis=-1, keepdims=True) + e_new
        o2 = lax.dot_general(e.astype(bf16), v_t, NT, preferred_element_type=f32) + e_new * v_new
        return pick(o2 / den)

    kvn = kvn_ref[...]
    o_s = with_new(ksel_ref[...], vsel_ref[...], b_s, mask_s, kvn[:, 2 * LANES:3 * LANES], kvn[:, 3 * LANES:4 * LANES])
    winn = winn_ref[...]
    o_w = with_new(win_ref[0:LANES, :].astype(bf16), win_ref[LANES:2 * LANES, :].astype(bf16), b_w, None,
                   winn[:, 0:LANES], winn[:, LANES:2 * LANES])

    gates = gate_ref[...]
    o_ref[...] = (gates[:, 0:1] * o_c + gates[:, 1:2] * o_s + gates[:, 2:3] * o_w).astype(bf16)

    lane_b = lax.broadcasted_iota(i32, wcol_ref.shape, 1)
    new_col = jnp.sum(jnp.where(lane_b == b, wcol_ref[...], 0.0), axis=-1, keepdims=True)
    lane_w = lax.broadcasted_iota(i32, win_ref.shape, 1)
    win_out_ref[...] = jnp.where(lane_w == n_win - 1, new_col, pltpu.roll(win_ref[...], n_win - 1, axis=1))


def _nsa_decode(cache_t, page_table, q_bd, kv32, win32, gates, win_t, w1bd, w2bd, gk0, bias, expand):
    Bs, n_pages = page_table.shape
    n_win = win_t.shape[2]
    past = n_pages * PAGE_SIZE
    n_cmp = past // CMP_BLOCK
    per_b = lambda shape: pl.BlockSpec((None,) + shape, lambda b, pt: (b,) + (0,) * len(shape))
    const = lambda a: pl.BlockSpec(a.shape, lambda b, pt: (0,) * a.ndim)
    page_spec = lambda k: pl.BlockSpec((None, 4 * LANES, PAGE_SIZE), lambda b, pt: (pt[b, k], 0, 0))
    grid_spec = pltpu.PrefetchScalarGridSpec(
        num_scalar_prefetch=1,
        grid=(Bs,),
        in_specs=[page_spec(k) for k in range(n_pages)] + [
            per_b((NSA_HEADS, LANES)), per_b((1, 4 * LANES)), per_b((1, 2 * LANES)), const(win32),
            per_b((NSA_HEADS, 3)), per_b((2 * LANES, n_win)), const(w1bd), const(w2bd), const(gk0), const(bias),
            const(expand)],
        out_specs=(per_b((NSA_HEADS, HEAD_DIM)), per_b((2 * LANES, n_win))),
        scratch_shapes=[pltpu.VMEM((2, _cmp_staging_rows(n_cmp), LANES), f32),
                        pltpu.VMEM((LANES, past), bf16), pltpu.VMEM((LANES, past), bf16),
                        pltpu.VMEM((2 * LANES, Bs), f32)],
    )
    return pl.pallas_call(
        functools.partial(_nsa_decode_kernel, n_pages=n_pages, n_win=n_win),
        out_shape=(jax.ShapeDtypeStruct((Bs, NSA_HEADS, HEAD_DIM), bf16),
                   jax.ShapeDtypeStruct((Bs, 2 * LANES, n_win), f32)),
        grid_spec=grid_spec,
        compiler_params=_params(("arbitrary",)),
        name="nsa_decode",
    )(page_table, *([cache_t] * n_pages), q_bd.reshape(Bs, NSA_HEADS, LANES), kv32.reshape(Bs, 1, 4 * LANES),
      win32.reshape(Bs, 1, 2 * LANES), win32, gates[:, 0:3 * NSA_HEADS].reshape(Bs, NSA_HEADS, 3), win_t,
      w1bd, w2bd, gk0, bias, expand)


def _hgrn_decode_kernel(h4_ref, lb_ref, gn_ref, s_ref, o_ref, s_out_ref, ft_ref):
    b = pl.program_id(0)
    nbatch = h4_ref.shape[0]

    @pl.when(b == 0)
    def _():
        for h in range(HG_HEADS):
            f, _, _ = _hgrn_gates(h4_ref[:, HG_WIDTH + h * HG_DIM:HG_WIDTH + (h + 1) * HG_DIM],
                                  lb_ref[:, h * HG_DIM:(h + 1) * HG_DIM])
            ft_ref[h] = f.T

    lane = lax.broadcasted_iota(i32, (HG_DIM, nbatch), 1)
    row = h4_ref[pl.ds(b, 1), :]
    gn = gn_ref[...]
    for h in range(HG_HEADS):
        part = lambda k: row[:, k * HG_WIDTH + h * HG_DIM:k * HG_WIDTH + (h + 1) * HG_DIM]
        f_col = jnp.sum(jnp.where(lane == b, ft_ref[h], 0.0), axis=-1, keepdims=True)
        s_new = f_col * s_ref[h] + (1.0 - f_col) * part(2)
        s_out_ref[h] = s_new
        q = jnp.broadcast_to(part(0) * HG_SCALE, (8, HG_DIM)).astype(bf16)
        o = jnp.dot(q, s_new.astype(bf16), preferred_element_type=f32)[0:1]
        o_ref[:, h * HG_DIM:(h + 1) * HG_DIM] = _hgrn_out(o, gn, part(3)).astype(bf16)


def _hgrn_decode(h4, lb, gn, state):
    Bs = h4.shape[0]
    sspec = pl.BlockSpec((None, HG_HEADS, HG_DIM, HG_DIM), lambda b: (b, 0, 0, 0))
    return pl.pallas_call(
        _hgrn_decode_kernel,
        out_shape=(jax.ShapeDtypeStruct((Bs, 1, HG_WIDTH), bf16),
                   jax.ShapeDtypeStruct((Bs, HG_HEADS, HG_DIM, HG_DIM), f32)),
        grid=(Bs,),
        in_specs=[pl.BlockSpec(h4.shape, lambda b: (0, 0)), pl.BlockSpec(lb.shape, lambda b: (0, 0)),
                  pl.BlockSpec(gn.shape, lambda b: (0, 0)), sspec],
        out_specs=(pl.BlockSpec((None, 1, HG_WIDTH), lambda b: (b, 0, 0)), sspec),
        scratch_shapes=[pltpu.VMEM((HG_HEADS, HG_DIM, Bs), f32)],
        compiler_params=_params(("arbitrary",)),
        name="hgrn_decode",
    )(h4, lb, gn, state)


def _split_w_in(w):
    idx = np.cumsum(SPLITS)[:-1]
    return jnp.split(w, [int(v) for v in idx], axis=1)


def kernel(x_prompt, x_sample, cache_kv, page_table, state_kv_win, state_hgrn, state_conv, rel_table, hg_lb_logits,
           norm1_g, w_in, q_norm_g, k_norm_g, phi_k_w1, phi_k_w2, phi_v_w1, phi_v_w2, hg_norm_g, w_nsa_out, w_hg_out,
           w_o, norm2_g, w_gate, w_up, conv_w, conv_b, w_down):
    assert w_in.shape[0] == 1, "one layer"
    Bp, L, _ = x_prompt.shape
    Bs = x_sample.shape[0]
    n_pages = page_table.shape[1]
    past = n_pages * PAGE_SIZE
    n_win = state_kv_win.shape[2]
    assert x_sample.shape[1] == 1 and n_win == WINDOW and past % (2 * CMP_BLOCK) == 0 and L % LANES == 0

    wq, wkv, wgl, whq, whf, whi, whg, wmg = _split_w_in(w_in[0])
    wgl = jnp.pad(wgl, ((0, 0), (0, LANES - wgl.shape[1])))
    w_prompt = jnp.concatenate([wq, wkv, whq, whf, whi, whg, wmg, wgl], axis=1).astype(bf16)
    wq_h = wq.reshape(D_MODEL, NSA_HEADS, HEAD_DIM)
    zq = jnp.zeros_like(wq_h)
    in_g0 = (np.arange(NSA_HEADS) < NSA_GROUP)[None, :, None]
    wq_bd = jnp.where(in_g0, jnp.concatenate([wq_h, zq], axis=-1), jnp.concatenate([zq, wq_h], axis=-1))
    w_sample = jnp.concatenate([wq_bd.reshape(D_MODEL, NSA_HEADS * LANES), wkv, whq, whf, whi, whg, wmg, wgl],
                               axis=1).astype(bf16)
    g1 = norm1_g[0][None, :]
    gq = q_norm_g[0]
    gq_prompt = jnp.tile(gq, NSA_HEADS)[None, :]
    gq_sample = jnp.tile(gq, 2 * NSA_HEADS)[None, :]
    gk = [jnp.tile(k_norm_g[0, s], NSA_KV_HEADS)[None, :] for s in range(3)]

    def block_diag(mats):
        n = len(mats)
        out = jnp.zeros(mats[0].shape[:-2] + (n * HEAD_DIM, n * HEAD_DIM), f32)
        for t, m in enumerate(mats):
            out = out.at[..., t * HEAD_DIM:(t + 1) * HEAD_DIM, t * HEAD_DIM:(t + 1) * HEAD_DIM].set(m)
        return out.astype(bf16)

    w1bd = block_diag([phi_k_w1[0], phi_k_w1[0], phi_v_w1[0], phi_v_w1[0]])
    w2bd = block_diag([phi_k_w2[0], phi_k_w2[0], phi_v_w2[0], phi_v_w2[0]])
    lb = jnp.cumsum(jax.nn.softmax(hg_lb_logits.astype(f32), axis=0), axis=0)[0][None, :]
    gn = hg_norm_g[0][None, :]
    ffn_w = (w_nsa_out[0].astype(bf16), w_hg_out[0].astype(bf16), w_o[0].astype(bf16), norm2_g[0][None, :],
             w_gate[0].astype(bf16), w_up[0].astype(bf16), conv_w[0], conv_b[0][None, :], w_down[0].astype(bf16))

    idx_c, idx_t = _prompt_bucket_tables(L)
    tc = _bias_tables(rel_table, jnp.asarray(idx_c)).reshape(L // SEL_BLOCK, NSA_KV_HEADS, NSA_GROUP * SEL_BLOCK, -1)
    tt = _bias_tables(rel_table, jnp.asarray(idx_t)).reshape(5, NSA_KV_HEADS, NSA_GROUP * SEL_BLOCK, LANES)
    bias_dec = _bias_tables(rel_table, jnp.asarray(_decode_bucket_table(past, n_win))[None]).reshape(NSA_HEADS, -1)
    ex_prompt = jnp.asarray(_expand_np(L // SEL_BLOCK, L), dtype=bf16)
    ex_sample = jnp.asarray(_expand_np(past // SEL_BLOCK, past), dtype=bf16)

    xp = x_prompt.reshape(Bp * L, D_MODEL)
    qn, kv_t, win32, katt, h4, mgs, gates = _in_proj(xp, g1, w_prompt, gq_prompt, gk[1], gk[2],
                                                     qw=NSA_WIDTH, q_seg=HEAD_DIM, tm=256, seq_len=L)
    cmp = _compress_prompt(kv_t, w1bd, w2bd, gk[0])
    o_nsa = _nsa_prompt(qn, gates, katt, cmp, tc, tt, ex_prompt, Bp, L)
    o_hg, s_prompt = _hgrn_prompt(h4, lb, gn, Bp, L)
    tm_ffn = 512
    yp, gate_tails = _ffn(xp, o_nsa, o_hg.reshape(Bp * L, HG_WIDTH), mgs, *ffn_w, tm=tm_ffn, tf=D_FF // 2, seq_len=L)
    conv_p = gate_tails.reshape(Bp, L // tm_ffn, 2, D_FF)[:, -1]
    n_keep_win = min(WINDOW, L)
    win_p = win32.reshape(Bp, L, 2, NSA_KV_HEADS, HEAD_DIM)[:, L - n_keep_win:]

    xs = x_sample.reshape(Bs, D_MODEL)
    q_bd, kv32_s, win32_s, _, h4_s, mgs_s, gates_s = _in_proj(xs, g1, w_sample, gq_sample, gk[1], gk[2],
                                                              qw=NSA_HEADS * LANES, q_seg=LANES, tm=Bs)
    cache_t = jnp.swapaxes(cache_kv[0].reshape(cache_kv.shape[1], PAGE_SIZE, 4 * LANES), 1, 2)
    win_t = jnp.swapaxes(state_kv_win[0].reshape(Bs, n_win, 2 * LANES), 1, 2)
    o_nsa_s, win_out_t = _nsa_decode(cache_t, page_table, q_bd, kv32_s, win32_s, gates_s, win_t, w1bd, w2bd, gk[0],
                                     bias_dec, ex_sample)
    win_s = jnp.swapaxes(win_out_t, 1, 2)
    kv32 = jnp.swapaxes(kv_t, 1, 2)
    o_hg_s, s_sample = _hgrn_decode(h4_s, lb, gn, state_hgrn[0])
    ys, gate_s = _ffn(xs, o_nsa_s.reshape(Bs, NSA_WIDTH), o_hg_s.reshape(Bs, HG_WIDTH), mgs_s, *ffn_w,
                      tm=Bs, tf=256, conv_state=state_conv[0])
    conv_s = jnp.stack([state_conv[0][:, 1], gate_s], axis=1)

    kvh = (NSA_KV_HEADS, HEAD_DIM)
    return (yp.reshape(Bp, L, D_MODEL), ys.reshape(Bs, 1, D_MODEL),
            kv32.reshape((1, Bp, L, 4) + kvh), kv32_s.reshape((1, Bs, 1, 4) + kvh),
            win_p[None], win_s.reshape((1, Bs, n_win, 2) + kvh),
            s_prompt[None], s_sample[None], conv_p[None], conv_s[None])
```

```python
import functools
import math

import numpy as np
import jax
import jax.numpy as jnp
from jax import lax
from jax.experimental import pallas as pl
from jax.experimental.pallas import tpu as pltpu

f32, bf16, i32 = jnp.float32, jnp.bfloat16, jnp.int32

D_MODEL = 1024
PAGE_SIZE = 128
NSA_HEADS, NSA_KV_HEADS, NSA_GROUP, HEAD_DIM = 8, 2, 4, 64
CMP_BLOCK, SEL_BLOCK, TOP_N, WINDOW = 32, 64, 16, 512
SCALE = HEAD_DIM ** -0.5
NSA_WIDTH = NSA_HEADS * HEAD_DIM
KV_WIDTH = NSA_KV_HEADS * HEAD_DIM
HG_HEADS, HG_DIM = 4, 128
HG_WIDTH = HG_HEADS * HG_DIM
HG_SCALE = HG_DIM ** -0.5
HG_BLOCK = 16
REL_BUCKETS, REL_MAX_DIST = 32, 128
D_FF = 2816
EPS = 1e-6
SPLITS = (NSA_WIDTH, 6 * KV_WIDTH, 3 * NSA_HEADS, HG_WIDTH, HG_WIDTH, HG_WIDTH, HG_WIDTH, 2 * D_MODEL)
LANES = 128
NEG = -1e30
FORCED_SCORE, INVALID_SCORE = 8.0, -1.0
VMEM_LIMIT = 56 * 1024 * 1024

NT = (((1,), (1,)), ((), ()))
TN = (((0,), (0,)), ((), ()))


def _params(sem):
    return pltpu.CompilerParams(dimension_semantics=sem, vmem_limit_bytes=VMEM_LIMIT)


def _bucket_np(dist):
    n = np.maximum(dist, 0)
    exact = REL_BUCKETS // 2
    ratio = np.log(np.maximum(n, 1).astype(np.float32) / np.float32(exact)) / np.float32(math.log(REL_MAX_DIST / exact))
    big = exact + (ratio.astype(np.float32) * np.float32(REL_BUCKETS - exact)).astype(np.int32)
    return np.where(n < exact, n, np.minimum(big, REL_BUCKETS - 1)).astype(np.int32)


def _cmp_perm(n_cmp):
    half = n_cmp // 2
    c = np.arange(n_cmp)
    return np.where(c < half, 2 * c, 2 * (c - half) + 1)


def _prompt_bucket_tables(L):
    nqb = L // SEL_BLOCK
    n_cmp = L // CMP_BLOCK
    qi = np.arange(SEL_BLOCK)
    c_end = (_cmp_perm(n_cmp) + 1) * CMP_BLOCK - 1
    idx_c = np.stack([_bucket_np((i * SEL_BLOCK + qi)[:, None] - c_end[None, :]) for i in range(nqb)])
    kj = np.arange(SEL_BLOCK)
    def tile(delta):
        return _bucket_np(delta * SEL_BLOCK + qi[:, None] - kj[None, :])
    idx_t = np.stack([np.concatenate([tile(e), tile(e - 1)], axis=1) for e in range(5)])
    return idx_c.astype(np.int32), idx_t.astype(np.int32)


def _decode_bucket_table(past_len, n_win):
    n_cmp = past_len // CMP_BLOCK
    c_end = (_cmp_perm(n_cmp) + 1) * CMP_BLOCK - 1
    d_c = np.zeros((LANES,), np.int64)
    d_c[:n_cmp] = past_len - c_end
    d_s = past_len - np.arange(past_len)
    d_w = n_win - np.arange(n_win)
    d_new = np.zeros((LANES,), np.int64)
    return _bucket_np(np.concatenate([d_s, d_w, d_c, d_new]))[None, :].astype(np.int32)


def _expand_np(n_blocks, n_keys):
    e = (np.arange(n_keys)[None, :] // SEL_BLOCK == np.arange(n_blocks)[:, None])
    return e.astype(np.float32)


def _seg_rmsnorm(x, gain, seg):
    outs = []
    for t in range(x.shape[1] // LANES):
        xt = x[:, t * LANES:(t + 1) * LANES]
        sq = xt * xt
        if seg == LANES:
            r = lax.rsqrt(jnp.sum(sq, axis=-1, keepdims=True) * (1.0 / HEAD_DIM) + EPS)
        else:
            lo = lax.broadcasted_iota(i32, xt.shape, 1) < HEAD_DIM
            s_lo = jnp.sum(jnp.where(lo, sq, 0.0), axis=-1, keepdims=True)
            s_hi = jnp.sum(jnp.where(lo, 0.0, sq), axis=-1, keepdims=True)
            r = jnp.where(lo, lax.rsqrt(s_lo * (1.0 / HEAD_DIM) + EPS), lax.rsqrt(s_hi * (1.0 / HEAD_DIM) + EPS))
        outs.append(xt * r)
    y = outs[0] if len(outs) == 1 else jnp.concatenate(outs, axis=1)
    return y * gain


def _rmsnorm_rows(x, gain):
    return x * lax.rsqrt(jnp.mean(x * x, axis=-1, keepdims=True) + EPS) * gain


def _softmax_parts(s, mask):
    s = jnp.where(mask, s, NEG)
    m = jnp.max(s, axis=-1, keepdims=True)
    e = jnp.where(mask, jnp.exp(s - m), 0.0)
    return e, jnp.sum(e, axis=-1, keepdims=True)


def _bias_table_kernel(tab_ref, idx_ref, out_ref):
    idx = idx_ref[...]
    for h in range(NSA_HEADS):
        acc = jnp.zeros(idx.shape, f32)
        for b in range(REL_BUCKETS):
            acc = jnp.where(idx == b, tab_ref[b, h], acc)
        out_ref[h] = acc


def _bias_tables(rel_table, idx):
    n, rows, w = idx.shape
    return pl.pallas_call(
        _bias_table_kernel,
        out_shape=jax.ShapeDtypeStruct((n, NSA_HEADS, rows, w), f32),
        grid=(n,),
        in_specs=[pl.BlockSpec(memory_space=pltpu.SMEM),
                  pl.BlockSpec((None, rows, w), lambda i: (i, 0, 0))],
        out_specs=pl.BlockSpec((None, NSA_HEADS, rows, w), lambda i: (i, 0, 0, 0)),
        compiler_params=_params(("arbitrary",)),
        name="bias_tables",
    )(rel_table, idx)


def _in_proj_kernel(x_ref, g1_ref, w_ref, gq_ref, gk1_ref, gk2_ref,
                    q_ref, kv_ref, win_ref, katt_ref, h4_ref, mg_ref, gl_ref, *, qw, q_seg, kv_feature_major):
    x = x_ref[...]
    xn = _rmsnorm_rows(x, g1_ref[...]).astype(bf16)

    def proj(c0, width):
        return jnp.dot(xn, w_ref[:, c0:c0 + width], preferred_element_type=f32)

    q_ref[...] = (_seg_rmsnorm(proj(0, qw), gq_ref[...], q_seg) * SCALE).astype(bf16)
    c = qw
    kv = proj(c, 6 * KV_WIDTH)
    c += 6 * KV_WIDTH
    k_sel = _seg_rmsnorm(kv[:, 2 * KV_WIDTH:3 * KV_WIDTH], gk1_ref[...], HEAD_DIM)
    k_win = _seg_rmsnorm(kv[:, 4 * KV_WIDTH:5 * KV_WIDTH], gk2_ref[...], HEAD_DIM)
    v_sel = kv[:, 3 * KV_WIDTH:4 * KV_WIDTH]
    v_win = kv[:, 5 * KV_WIDTH:6 * KV_WIDTH]
    if kv_feature_major:
        kv_ref[0:2 * KV_WIDTH, :] = kv[:, 0:2 * KV_WIDTH].T
        kv_ref[2 * KV_WIDTH:3 * KV_WIDTH, :] = k_sel.T
        kv_ref[3 * KV_WIDTH:4 * KV_WIDTH, :] = v_sel.T
    else:
        kv_ref[:, 0:2 * KV_WIDTH] = kv[:, 0:2 * KV_WIDTH]
        kv_ref[:, 2 * KV_WIDTH:3 * KV_WIDTH] = k_sel
        kv_ref[:, 3 * KV_WIDTH:4 * KV_WIDTH] = v_sel
    win_ref[:, 0:KV_WIDTH] = k_win
    win_ref[:, KV_WIDTH:2 * KV_WIDTH] = v_win
    katt_ref[:, 0:KV_WIDTH] = k_sel.astype(bf16)
    katt_ref[:, KV_WIDTH:2 * KV_WIDTH] = v_sel.astype(bf16)
    katt_ref[:, 2 * KV_WIDTH:3 * KV_WIDTH] = k_win.astype(bf16)
    katt_ref[:, 3 * KV_WIDTH:4 * KV_WIDTH] = v_win.astype(bf16)
    h4_ref[...] = proj(c, 4 * HG_WIDTH)
    c += 4 * HG_WIDTH
    mg_ref[...] = jax.nn.sigmoid(proj(c, 2 * D_MODEL))
    c += 2 * D_MODEL
    gl_ref[...] = jax.nn.sigmoid(proj(c, LANES))


def _in_proj(x2d, g1, w, gq, gk1, gk2, *, qw, q_seg, tm, seq_len=None):
    n = x2d.shape[0]
    cw = w.shape[1]
    row = lambda width: pl.BlockSpec((tm, width), lambda i: (i, 0))
    full = lambda a: pl.BlockSpec(a.shape, lambda i: (0,) * a.ndim)
    widths = (qw, 4 * KV_WIDTH, 2 * KV_WIDTH, 4 * KV_WIDTH, 4 * HG_WIDTH, 2 * D_MODEL, LANES)
    dtypes = (bf16, f32, f32, bf16, f32, f32, f32)
    out_shape = [jax.ShapeDtypeStruct((n, wd), dt) for wd, dt in zip(widths, dtypes)]
    out_specs = [row(wd) for wd in widths]
    if seq_len is not None:
        tiles = seq_len // tm
        out_shape[1] = jax.ShapeDtypeStruct((n // seq_len, 4 * KV_WIDTH, seq_len), f32)
        out_specs[1] = pl.BlockSpec((None, 4 * KV_WIDTH, tm), lambda i: (i // tiles, 0, i % tiles))
    return pl.pallas_call(
        functools.partial(_in_proj_kernel, qw=qw, q_seg=q_seg, kv_feature_major=seq_len is not None),
        out_shape=tuple(out_shape),
        grid=(n // tm,),
        in_specs=[row(D_MODEL), full(g1), pl.BlockSpec((D_MODEL, cw), lambda i: (0, 0)), full(gq), full(gk1), full(gk2)],
        out_specs=tuple(out_specs),
        compiler_params=_params(("arbitrary",)),
        name="in_proj",
    )(x2d, g1, w, gq, gk1, gk2)


CMP_SLOT = CMP_BLOCK + 1


def _cmp_staging_rows(n_cmp):
    return -(-n_cmp * CMP_SLOT // 8) * 8


def _stage_cmp_chunk(src_ref, chunk, kc_t, vc_t, n_cmp):
    per_chunk = LANES // CMP_BLOCK
    for s, t in enumerate((kc_t.T, vc_t.T)):
        for nl in range(per_chunk):
            n = chunk * per_chunk + nl
            slot = n // 2 + (n % 2) * (n_cmp // 2)
            src_ref[s, slot * CMP_SLOT:slot * CMP_SLOT + CMP_BLOCK, :] = t[nl * CMP_BLOCK:(nl + 1) * CMP_BLOCK]


def _compress(src_refs, w1_ref, w2_ref, gk0, n_cmp):
    acc = jnp.zeros((len(src_refs) * n_cmp, 2 * LANES), f32)
    per_dot = 8
    for j0 in range(0, CMP_BLOCK, per_dot):
        xs = jnp.concatenate(
            [jnp.concatenate([src[s, pl.ds(j, n_cmp, stride=CMP_SLOT), :].astype(bf16)
                              for j in range(j0, j0 + per_dot) for s in range(2)], axis=1)
             for src in src_refs], axis=0)
        w = w1_ref[j0 * 2 * LANES:(j0 + per_dot) * 2 * LANES, :]
        acc = acc + jnp.dot(xs, w, preferred_element_type=f32)
    hmid = jax.nn.gelu(acc).astype(bf16)
    y = jnp.dot(hmid, w2_ref[...], preferred_element_type=f32)
    ck = _seg_rmsnorm(y[:, 0:LANES], gk0, HEAD_DIM)
    return jnp.concatenate([ck, y[:, LANES:2 * LANES]], axis=1)


def _compress_prompt_kernel(kv_ref, w1_ref, w2_ref, gk0_ref, out_ref, src_ref, *, n_cmp):
    for c in range(kv_ref.shape[1] // LANES):
        cols = slice(c * LANES, (c + 1) * LANES)
        _stage_cmp_chunk(src_ref, c, kv_ref[0:LANES, cols], kv_ref[LANES:2 * LANES, cols], n_cmp)
    out_ref[...] = _compress([src_ref], w1_ref, w2_ref, gk0_ref[...], n_cmp).astype(bf16)


def _compress_prompt(kv_t, w1bd, w2bd, gk0):
    B, _, L = kv_t.shape
    n_cmp = L // CMP_BLOCK
    return pl.pallas_call(
        functools.partial(_compress_prompt_kernel, n_cmp=n_cmp),
        out_shape=jax.ShapeDtypeStruct((B, n_cmp, 2 * LANES), bf16),
        grid=(B,),
        in_specs=[pl.BlockSpec((None, 2 * LANES, L), lambda b: (b, 0, 0)),
                  pl.BlockSpec(w1bd.shape, lambda b: (0, 0)),
                  pl.BlockSpec(w2bd.shape, lambda b: (0, 0)),
                  pl.BlockSpec(gk0.shape, lambda b: (0, 0))],
        out_specs=pl.BlockSpec((None, n_cmp, 2 * LANES), lambda b: (b, 0, 0)),
        scratch_shapes=[pltpu.VMEM((2, _cmp_staging_rows(n_cmp), LANES), f32)],
        compiler_params=_params(("arbitrary",)),
        name="compress_prompt",
    )(kv_t, w1bd, w2bd, gk0)


def _select_blocks_t(score, valid, n_keep):
    nb = score.shape[0]
    jj = lax.broadcasted_iota(i32, score.shape, 0)
    cnt = jnp.zeros(score.shape, f32)
    for k in range(nb):
        sk = score[k:k + 1, :]
        better = (sk > score) | ((sk == score) & (jj > k))
        cnt = cnt + jnp.where(better, 1.0, 0.0)
    return (cnt < n_keep) & valid


def _sel_tile_classes(n_tiles):
    return sorted({-(-n_tiles * c // 4) for c in range(1, 5)})


def _nsa_prompt_kernel(q_ref, gate_ref, katt_ref, cmp_ref, tc_ref, tt_ref, ex_ref, o_ref,
                       selx_ref, s_ref, os_ref, *, L):
    i = pl.program_id(1)
    n_cmp = L // CMP_BLOCK
    n_sel = L // SEL_BLOCK
    n_tiles = L // LANES
    qb = SEL_BLOCK
    rows = NSA_GROUP * qb
    groups = range(NSA_KV_HEADS)
    gates = gate_ref[...]
    row64 = lax.broadcasted_iota(i32, (qb, LANES), 0)
    lane = lax.broadcasted_iota(i32, (qb, LANES), 1)
    qpos = i * qb + row64

    qs = [jnp.concatenate(
        [q_ref[:, (NSA_GROUP * g + r) * HEAD_DIM:(NSA_GROUP * g + r + 1) * HEAD_DIM] for r in range(NSA_GROUP)],
        axis=0) for g in groups]

    def attend(k_off, v_off, tile0, n_t, mask_fn):
        chunk = 4

        def key_rows(t0, count, off, g):
            r0 = pl.multiple_of((tile0 + t0) * LANES, LANES)
            return katt_ref[pl.ds(r0, count * LANES), off + g * HEAD_DIM:off + (g + 1) * HEAD_DIM]

        def scores(t0, count, mruns):
            dots = [lax.dot_general(qs[g], key_rows(t0, count, k_off, g), NT, preferred_element_type=f32)
                    for g in groups]
            out = list(mruns)
            for j in range(count):
                ta = tile0 + t0 + j
                bias_idx = jnp.clip(i - 2 * ta, 0, 4)
                kpos = ta * LANES + lane
                for g in groups:
                    s = dots[g][:, j * LANES:(j + 1) * LANES] + tt_ref[bias_idx, g]
                    msk = mask_fn(g, ta, kpos)
                    s = jnp.where(msk[None], s.reshape(NSA_GROUP, qb, LANES), NEG).reshape(rows, LANES)
                    s_ref[g, t0 + j] = s
                    out[g] = jnp.maximum(out[g], s)
            return tuple(out)

        def values(t0, count, carry, ms):
            ls, accs = list(carry[0]), list(carry[1])
            ps = []
            for g in groups:
                pj = [jnp.exp(s_ref[g, t0 + j] - ms[g]) for j in range(count)]
                for p in pj:
                    ls[g] = ls[g] + p
                ps.append(jnp.concatenate([p.astype(bf16) for p in pj], axis=1))
            for g in groups:
                accs[g] = accs[g] + jnp.dot(ps[g], key_rows(t0, count, v_off, g), preferred_element_type=f32)
            return tuple(ls), tuple(accs)

        def over_chunks(fn, carry):
            n_full = n_t // chunk
            if n_full > 2:
                carry = lax.fori_loop(0, n_full, lambda c, cr: fn(c * chunk, chunk, cr), carry)
            else:
                for c in range(n_full):
                    carry = fn(c * chunk, chunk, carry)
            if n_t % chunk:
                carry = fn(n_full * chunk, n_t % chunk, carry)
            return carry

        mruns = over_chunks(scores, tuple(jnp.full((rows, LANES), NEG, f32) for _ in groups))
        ms = [jnp.max(m, axis=-1, keepdims=True) for m in mruns]
        init = (tuple(jnp.zeros((rows, LANES), f32) for _ in groups),
                tuple(jnp.zeros((rows, HEAD_DIM), f32) for _ in groups))
        ls, accs = over_chunks(lambda t0, count, cr: values(t0, count, cr, ms), init)
        return [accs[g] / jnp.sum(ls[g], axis=-1, keepdims=True) for g in groups]

    n_wt = min(WINDOW // LANES + 1, n_tiles)
    w0 = jnp.clip((i - WINDOW // qb) // 2, 0, n_tiles - n_wt)
    o_w = attend(2 * KV_WIDTH, 3 * KV_WIDTH, w0, n_wt,
                 lambda g, ta, kpos: (kpos <= qpos) & (qpos - kpos <= WINDOW))

    o_c, imp_t = [], []
    col = lax.broadcasted_iota(i32, (rows, n_cmp), 1)
    rq = lax.broadcasted_iota(i32, (rows, n_cmp), 0) & (qb - 1)
    blk = jnp.where(col < n_cmp // 2, 2 * col, 2 * col - (n_cmp - 1))
    vis = (blk + 1) * CMP_BLOCK - 1 <= i * qb + rq
    for g in groups:
        ck = cmp_ref[:, g * HEAD_DIM:(g + 1) * HEAD_DIM]
        cv = cmp_ref[:, LANES + g * HEAD_DIM:LANES + (g + 1) * HEAD_DIM]
        s = lax.dot_general(qs[g], ck, NT, preferred_element_type=f32) + tc_ref[g]
        e, den = _softmax_parts(s, vis)
        p = e / jnp.where(den > 0, den, 1.0)
        o_c.append(jnp.dot(p.astype(bf16), cv, preferred_element_type=f32))
        ps_t = (p[0:qb] + p[qb:2 * qb] + p[2 * qb:3 * qb] + p[3 * qb:4 * qb]).T
        imp_t.append(ps_t[0:n_sel] + ps_t[n_sel:2 * n_sel])

    imp = jnp.concatenate(imp_t, axis=1)
    jj = lax.broadcasted_iota(i32, imp.shape, 0)
    valid = jj <= i
    forced = valid & ((jj == 0) | (jj >= i - 1))
    score = jnp.where(forced, FORCED_SCORE, jnp.where(valid, imp, INVALID_SCORE))
    sel = _select_blocks_t(score, valid, min(TOP_N, n_sel))
    selx = lax.dot_general(jnp.where(sel, 1.0, 0.0).astype(bf16), ex_ref[...], TN, preferred_element_type=f32)
    for g in groups:
        for t in range(n_tiles):
            selx_ref[g, t] = selx[g * qb:(g + 1) * qb, t * LANES:(t + 1) * LANES]

    need = i // 2 + 1
    prev = 0
    for n_t in _sel_tile_classes(n_tiles):
        @pl.when((need > prev) & (need <= n_t))
        def _(n_t=n_t):
            o_s = attend(0, KV_WIDTH, 0, n_t, lambda g, ta, kpos: (selx_ref[g, ta] > 0.5) & (kpos <= qpos))
            for g in groups:
                os_ref[g] = o_s[g]
        prev = n_t

    for g in groups:
        o_s = os_ref[g]
        for r in range(NSA_GROUP):
            h = NSA_GROUP * g + r
            sl = slice(r * qb, (r + 1) * qb)
            o = (gates[:, 3 * h:3 * h + 1] * o_c[g][sl] + gates[:, 3 * h + 1:3 * h + 2] * o_s[sl]
                 + gates[:, 3 * h + 2:3 * h + 3] * o_w[g][sl])
            o_ref[:, h * HEAD_DIM:(h + 1) * HEAD_DIM] = o.astype(bf16)


def _nsa_prompt(qn, gates, katt, cmp, tc, tt, expand, B, L):
    nqb = L // SEL_BLOCK
    qb = SEL_BLOCK
    n_tiles = L // LANES
    rows = NSA_GROUP * qb
    return pl.pallas_call(
        functools.partial(_nsa_prompt_kernel, L=L),
        out_shape=jax.ShapeDtypeStruct((B * L, NSA_WIDTH), bf16),
        grid=(B, nqb),
        in_specs=[pl.BlockSpec((qb, NSA_WIDTH), lambda b, i: (b * nqb + i, 0)),
                  pl.BlockSpec((qb, LANES), lambda b, i: (b * nqb + i, 0)),
                  pl.BlockSpec((L, 4 * KV_WIDTH), lambda b, i: (b, 0)),
                  pl.BlockSpec((None, L // CMP_BLOCK, 2 * LANES), lambda b, i: (b, 0, 0)),
                  pl.BlockSpec((None,) + tc.shape[1:], lambda b, i: (i, 0, 0, 0)),
                  pl.BlockSpec(tt.shape, lambda b, i: (0, 0, 0, 0)),
                  pl.BlockSpec(expand.shape, lambda b, i: (0, 0))],
        out_specs=pl.BlockSpec((qb, NSA_WIDTH), lambda b, i: (b * nqb + i, 0)),
        scratch_shapes=[pltpu.VMEM((NSA_KV_HEADS, n_tiles, qb, LANES), f32),
                        pltpu.VMEM((NSA_KV_HEADS, n_tiles, rows, LANES), f32),
                        pltpu.VMEM((NSA_KV_HEADS, rows, HEAD_DIM), f32)],
        compiler_params=_params(("arbitrary", "arbitrary")),
        name="nsa_prompt",
    )(qn, gates, katt, cmp, tc, tt, expand)


def _hgrn_gates(hf, lb):
    f = lb + (1.0 - lb) * jax.nn.sigmoid(hf)
    return f, 1.0 - f, jnp.log(f)


def _hgrn_out(o, gn, hgate):
    return _rmsnorm_rows(o, gn) * jax.nn.silu(hgate)


def _hgrn_prompt_kernel(h4_ref, lb_ref, gn_ref, o_ref, s_out_ref, st_ref, *, tc):
    c = pl.program_id(1)
    nb = HG_BLOCK
    nseq = h4_ref.shape[0]

    @pl.when(c == 0)
    def _():
        st_ref[...] = jnp.zeros_like(st_ref)

    t_row = lax.broadcasted_iota(i32, (nb, HG_DIM), 0)
    t_col = lax.broadcasted_iota(i32, (nb, 1), 0)
    gn = gn_ref[...]
    chains = [(sq, h) for sq in range(nseq) for h in range(HG_HEADS)]

    def cumsum_rows(x):
        shift = 1
        while shift < nb:
            x = x + jnp.where(t_row >= shift, pltpu.roll(x, shift, axis=0), 0.0)
            shift *= 2
        return x

    def block(bi, _):
        r0 = pl.multiple_of(bi * nb, nb)

        def sl(sq, h, part):
            return h4_ref[sq, pl.ds(r0, nb), part * HG_WIDTH + h * HG_DIM:part * HG_WIDTH + (h + 1) * HG_DIM]

        qs, ks, vs, bs, sts, os_ = [], [], [], [], [], []
        for sq, h in chains:
            _, k, glog = _hgrn_gates(sl(sq, h, 1), lb_ref[:, h * HG_DIM:(h + 1) * HG_DIM])
            qs.append(sl(sq, h, 0) * HG_SCALE)
            ks.append(k)
            vs.append(sl(sq, h, 2))
            bs.append(cumsum_rows(glog))
        for c_, (sq, h) in enumerate(chains):
            sts.append(st_ref[sq, h])
            os_.append(lax.dot_general((qs[c_] * jnp.exp(bs[c_])).astype(bf16), sts[c_].astype(bf16), NT,
                                       preferred_element_type=f32))
        for c_ in range(len(chains)):
            q, k, v, b, o = qs[c_], ks[c_], vs[c_], bs[c_], os_[c_]
            for s in range(nb):
                w = q * k[s:s + 1] * jnp.exp(jnp.minimum(b - b[s:s + 1], 0.0))
                a = jnp.where(t_col >= s, jnp.sum(w, axis=-1, keepdims=True), 0.0)
                o = o + a * v[s:s + 1]
            os_[c_] = o
        for c_, (sq, h) in enumerate(chains):
            b_last = bs[c_][nb - 1:nb]
            kt = ks[c_] * jnp.exp(b_last - bs[c_])
            upd = lax.dot_general(vs[c_].astype(bf16), kt.astype(bf16), TN, preferred_element_type=f32)
            st_ref[sq, h] = jnp.exp(b_last) * sts[c_] + upd
            o_ref[sq, pl.ds(r0, nb), h * HG_DIM:(h + 1) * HG_DIM] = _hgrn_out(os_[c_], gn, sl(sq, h, 3)).astype(bf16)
        return 0

    lax.fori_loop(0, tc // nb, block, 0)

    @pl.when(c == pl.num_programs(1) - 1)
    def _():
        for sq in range(nseq):
            for h in range(HG_HEADS):
                s_out_ref[sq, h] = st_ref[sq, h].T


def _hgrn_prompt(h4, lb, gn, B, L, tc=256, nseq=4):
    nc = L // tc
    nseq = math.gcd(nseq, B)
    return pl.pallas_call(
        functools.partial(_hgrn_prompt_kernel, tc=tc),
        out_shape=(jax.ShapeDtypeStruct((B, L, HG_WIDTH), bf16),
                   jax.ShapeDtypeStruct((B, HG_HEADS, HG_DIM, HG_DIM), f32)),
        grid=(B // nseq, nc),
        in_specs=[pl.BlockSpec((nseq, tc, 4 * HG_WIDTH), lambda b, c: (b, c, 0)),
                  pl.BlockSpec(lb.shape, lambda b, c: (0, 0)),
                  pl.BlockSpec(gn.shape, lambda b, c: (0, 0))],
        out_specs=(pl.BlockSpec((nseq, tc, HG_WIDTH), lambda b, c: (b, c, 0)),
                   pl.BlockSpec((nseq, HG_HEADS, HG_DIM, HG_DIM), lambda b, c: (b, 0, 0, 0))),
        scratch_shapes=[pltpu.VMEM((nseq, HG_HEADS, HG_DIM, HG_DIM), f32)],
        compiler_params=_params(("arbitrary", "arbitrary")),
        name="hgrn_prompt",
    )(h4.reshape(B, L, 4 * HG_WIDTH), lb, gn)


def _ffn_kernel(*refs, decode, tiles_per_seq):
    if decode:
        (x_ref, on_ref, oh_ref, mg_ref, wn_ref, wh_ref, wo_ref, g2_ref, wg_ref, wu_ref, cw_ref, cb_ref, wd_ref,
         cs0_ref, cs1_ref, y_ref, gate_out_ref, x1_ref, xn2_ref, acc_ref) = refs
    else:
        (x_ref, on_ref, oh_ref, mg_ref, wn_ref, wh_ref, wo_ref, g2_ref, wg_ref, wu_ref, cw_ref, cb_ref, wd_ref,
         y_ref, gate_out_ref, x1_ref, xn2_ref, acc_ref, carry_ref) = refs
    i = pl.program_id(0)
    fi = pl.program_id(1)
    tm = x_ref.shape[0]

    @pl.when(fi == 0)
    def _():
        y_a = jnp.dot(on_ref[...], wn_ref[...], preferred_element_type=f32)
        y_b = jnp.dot(oh_ref[...], wh_ref[...], preferred_element_type=f32)
        mg = mg_ref[...]
        merged = mg[:, 0:D_MODEL] * y_a + mg[:, D_MODEL:2 * D_MODEL] * y_b
        x1 = x_ref[...] + jnp.dot(merged.astype(bf16), wo_ref[...], preferred_element_type=f32)
        x1_ref[...] = x1
        xn2_ref[...] = _rmsnorm_rows(x1, g2_ref[...]).astype(bf16)
        acc_ref[...] = jnp.zeros_like(acc_ref)

    xn2 = xn2_ref[...]
    gate = jnp.dot(xn2, wg_ref[...], preferred_element_type=f32)
    up = jnp.dot(xn2, wu_ref[...], preferred_element_type=f32)
    cw = cw_ref[...]
    if decode:
        prev2, prev1 = cs0_ref[...], cs1_ref[...]
        gate_out_ref[...] = gate
    else:
        @pl.when(i % tiles_per_seq == 0)
        def _():
            carry_ref[fi] = jnp.zeros(carry_ref.shape[1:], f32)
        carry = carry_ref[fi]
        rid = lax.broadcasted_iota(i32, gate.shape, 0)
        prev1 = jnp.where(rid == 0, carry[7:8], pltpu.roll(gate, 1, axis=0))
        prev2 = jnp.where(rid == 0, carry[6:7], jnp.where(rid == 1, carry[7:8], pltpu.roll(gate, 2, axis=0)))
        carry_ref[fi, 6:8, :] = gate[tm - 2:tm]
        gate_out_ref[...] = gate[tm - 2:tm]
    conv = cb_ref[...] + cw[0:1] * prev2 + cw[1:2] * prev1 + cw[2:3] * gate
    hmid = (jax.nn.silu(conv) * up).astype(bf16)
    acc_ref[...] += jnp.dot(hmid, wd_ref[...], preferred_element_type=f32)

    @pl.when(fi == pl.num_programs(1) - 1)
    def _():
        y_ref[...] = x1_ref[...] + acc_ref[...]


def _ffn(x2d, o_nsa, o_hg, mgs, wn, wh, wo, g2, wg, wu, cw, cb, wd, *, tm, tf, seq_len=None, conv_state=None):
    n = x2d.shape[0]
    decode = conv_state is not None
    nf = D_FF // tf
    row = lambda width: pl.BlockSpec((tm, width), lambda i, f: (i, 0))
    full = lambda a: pl.BlockSpec(a.shape, lambda i, f: (0,) * a.ndim)
    in_specs = [row(D_MODEL), row(NSA_WIDTH), row(HG_WIDTH), row(2 * D_MODEL), full(wn), full(wh), full(wo), full(g2),
                pl.BlockSpec((D_MODEL, tf), lambda i, f: (0, f)), pl.BlockSpec((D_MODEL, tf), lambda i, f: (0, f)),
                pl.BlockSpec((3, tf), lambda i, f: (0, f)), pl.BlockSpec((1, tf), lambda i, f: (0, f)),
                pl.BlockSpec((tf, D_MODEL), lambda i, f: (f, 0))]
    args = [x2d, o_nsa, o_hg, mgs, wn, wh, wo, g2, wg, wu, cw, cb, wd]
    scratch = [pltpu.VMEM((tm, D_MODEL), f32), pltpu.VMEM((tm, D_MODEL), bf16), pltpu.VMEM((tm, D_MODEL), f32)]
    if decode:
        in_specs += [pl.BlockSpec((tm, tf), lambda i, f: (i, f))] * 2
        args += [conv_state[:, 0], conv_state[:, 1]]
        gate_shape = jax.ShapeDtypeStruct((n, D_FF), f32)
        gate_spec = pl.BlockSpec((tm, tf), lambda i, f: (i, f))
        tiles_per_seq = 1
    else:
        tiles_per_seq = seq_len // tm
        gate_shape = jax.ShapeDtypeStruct((n // tm, 2, D_FF), f32)
        gate_spec = pl.BlockSpec((None, 2, tf), lambda i, f: (i, 0, f))
        scratch.append(pltpu.VMEM((nf, 8, tf), f32))
    return pl.pallas_call(
        functools.partial(_ffn_kernel, decode=decode, tiles_per_seq=tiles_per_seq),
        out_shape=(jax.ShapeDtypeStruct((n, D_MODEL), f32), gate_shape),
        grid=(n // tm, nf),
        in_specs=in_specs,
        out_specs=(row(D_MODEL), gate_spec),
        scratch_shapes=scratch,
        compiler_params=_params(("arbitrary", "arbitrary")),
        name="ffn_decode" if decode else "ffn_prompt",
    )(*args)


def _nsa_decode_kernel(*refs, n_pages, n_win, nseq):
    page_refs = refs[1:1 + nseq * n_pages]
    (q_ref, kvn_ref, winn_ref, win_all_ref, gate_ref, win_ref, w1_ref, w2_ref, gk0_ref, bias_ref, ex_ref,
     o_ref, win_out_ref, src_ref, ksel_ref, vsel_ref, wcol_ref) = refs[1 + nseq * n_pages:]
    step = pl.program_id(0)
    past = n_pages * PAGE_SIZE
    n_cmp = past // CMP_BLOCK
    n_sel = past // SEL_BLOCK
    heads = NSA_HEADS
    samples = range(nseq)

    @pl.when(step == 0)
    def _():
        for t in range(2):
            wcol_ref[t * LANES:(t + 1) * LANES, :] = win_all_ref[:, t * LANES:(t + 1) * LANES].T

    for sq in samples:
        for k in range(n_pages):
            page = page_refs[sq * n_pages + k]
            cols = slice(k * PAGE_SIZE, (k + 1) * PAGE_SIZE)
            _stage_cmp_chunk(src_ref.at[sq], k, page[0:LANES, :], page[LANES:2 * LANES, :], n_cmp)
            ksel_ref[sq, :, cols] = page[2 * LANES:3 * LANES, :].astype(bf16)
            vsel_ref[sq, :, cols] = page[3 * LANES:4 * LANES, :].astype(bf16)

    hrow = lax.broadcasted_iota(i32, (heads, 1), 0)
    g0 = hrow < NSA_GROUP
    bias = bias_ref[...]
    b_s = bias[:, 0:past]
    b_w = bias[:, past:past + n_win]
    b_c = bias[:, past + n_win:past + n_win + n_cmp]
    b_new = bias[:, past + n_win + LANES:past + n_win + LANES + 1]
    qs = [q_ref[sq] for sq in samples]

    def pick(o2):
        return jnp.where(g0, o2[:, 0:HEAD_DIM], o2[:, HEAD_DIM:2 * HEAD_DIM])

    def with_new(q, k_t, v_t, b_past, mask, k_new, v_new):
        s_past = jnp.dot(q, k_t, preferred_element_type=f32) + b_past
        s_new = jnp.sum(q.astype(f32) * k_new, axis=-1, keepdims=True) + b_new
        sm = s_past if mask is None else jnp.where(mask, s_past, NEG)
        m = jnp.maximum(jnp.max(sm, axis=-1, keepdims=True), s_new)
        e = jnp.exp(sm - m)
        e_new = jnp.exp(s_new - m)
        den = jnp.sum(e, axis=-1, keepdims=True) + e_new
        o2 = lax.dot_general(e.astype(bf16), v_t, NT, preferred_element_type=f32) + e_new * v_new
        return pick(o2 / den)

    o_w = []
    for sq in samples:
        winn = winn_ref[sq]
        o_w.append(with_new(qs[sq], win_ref[sq, 0:LANES, :].astype(bf16), win_ref[sq, LANES:2 * LANES, :].astype(bf16),
                            b_w, None, winn[:, 0:LANES], winn[:, LANES:2 * LANES]))

    cmp = _compress([src_ref.at[sq] for sq in samples], w1_ref, w2_ref, gk0_ref[...], n_cmp)
    o_c, masks = [], []
    cur = past // SEL_BLOCK
    jrow = lax.broadcasted_iota(i32, (NSA_KV_HEADS, n_sel), 1)
    forced = (jrow == 0) | (jrow >= cur - 1)
    n_keep = min(TOP_N, n_sel + 1) - 1
    ri = lax.broadcasted_iota(i32, (n_sel, n_sel), 0)
    ci = lax.broadcasted_iota(i32, (n_sel, n_sel), 1)
    for sq in samples:
        ck = cmp[sq * n_cmp:(sq + 1) * n_cmp, 0:LANES].astype(bf16)
        cv = cmp[sq * n_cmp:(sq + 1) * n_cmp, LANES:2 * LANES].astype(bf16)
        s_c = lax.dot_general(qs[sq], ck, NT, preferred_element_type=f32) + b_c
        e_c, den_c = _softmax_parts(s_c, jnp.full(s_c.shape, True))
        p_c = e_c / jnp.where(den_c > 0, den_c, 1.0)
        o_c.append(pick(jnp.dot(p_c.astype(bf16), cv, preferred_element_type=f32)))

        ps = jnp.concatenate([jnp.sum(p_c[0:NSA_GROUP], axis=0, keepdims=True),
                              jnp.sum(p_c[NSA_GROUP:heads], axis=0, keepdims=True)], axis=0)
        imp = ps[:, 0:n_sel] + ps[:, n_sel:2 * n_sel]
        score_row = jnp.where(forced, FORCED_SCORE, imp)
        sels = []
        for g in range(NSA_KV_HEADS):
            rowb = jnp.broadcast_to(score_row[g:g + 1], (n_sel, n_sel))
            colb = jnp.sum(jnp.where(ri == ci, rowb, 0.0), axis=-1, keepdims=True)
            better = (colb > rowb) | ((colb == rowb) & (ri < ci))
            cnt = jnp.sum(jnp.where(better, 1.0, 0.0), axis=0, keepdims=True)
            sels.append(jnp.where(cnt < n_keep, 1.0, 0.0))
        sel = jnp.concatenate(sels, axis=0).astype(bf16)
        selx = jnp.dot(sel, ex_ref[...], preferred_element_type=f32)
        masks.append(jnp.where(g0, selx[0:1], selx[1:2]) > 0.5)

    lane_b = lax.broadcasted_iota(i32, wcol_ref.shape, 1)
    lane_w = lax.broadcasted_iota(i32, win_ref.shape[1:], 1)
    for sq in samples:
        kvn = kvn_ref[sq]
        o_s = with_new(qs[sq], ksel_ref[sq], vsel_ref[sq], b_s, masks[sq],
                       kvn[:, 2 * LANES:3 * LANES], kvn[:, 3 * LANES:4 * LANES])
        gates = gate_ref[sq]
        o_ref[sq] = (gates[:, 0:1] * o_c[sq] + gates[:, 1:2] * o_s + gates[:, 2:3] * o_w[sq]).astype(bf16)

        new_col = jnp.sum(jnp.where(lane_b == step * nseq + sq, wcol_ref[...], 0.0), axis=-1, keepdims=True)
        win_out_ref[sq] = jnp.where(lane_w == n_win - 1, new_col, pltpu.roll(win_ref[sq], n_win - 1, axis=1))


def _nsa_decode(cache_t, page_table, q_bd, kv32, win32, gates, win_t, w1bd, w2bd, gk0, bias, expand, nseq=2):
    Bs, n_pages = page_table.shape
    n_win = win_t.shape[2]
    past = n_pages * PAGE_SIZE
    n_cmp = past // CMP_BLOCK
    nseq = math.gcd(nseq, Bs)
    per_b = lambda shape: pl.BlockSpec((nseq,) + shape, lambda b, pt: (b,) + (0,) * len(shape))
    const = lambda a: pl.BlockSpec(a.shape, lambda b, pt: (0,) * a.ndim)
    page_spec = lambda sq, k: pl.BlockSpec((None, 4 * LANES, PAGE_SIZE), lambda b, pt: (pt[b * nseq + sq, k], 0, 0))
    grid_spec = pltpu.PrefetchScalarGridSpec(
        num_scalar_prefetch=1,
        grid=(Bs // nseq,),
        in_specs=[page_spec(sq, k) for sq in range(nseq) for k in range(n_pages)] + [
            per_b((NSA_HEADS, LANES)), per_b((1, 4 * LANES)), per_b((1, 2 * LANES)), const(win32),
            per_b((NSA_HEADS, 3)), per_b((2 * LANES, n_win)), const(w1bd), const(w2bd), const(gk0), const(bias),
            const(expand)],
        out_specs=(per_b((NSA_HEADS, HEAD_DIM)), per_b((2 * LANES, n_win))),
        scratch_shapes=[pltpu.VMEM((nseq, 2, _cmp_staging_rows(n_cmp), LANES), f32),
                        pltpu.VMEM((nseq, LANES, past), bf16), pltpu.VMEM((nseq, LANES, past), bf16),
                        pltpu.VMEM((2 * LANES, Bs), f32)],
    )
    return pl.pallas_call(
        functools.partial(_nsa_decode_kernel, n_pages=n_pages, n_win=n_win, nseq=nseq),
        out_shape=(jax.ShapeDtypeStruct((Bs, NSA_HEADS, HEAD_DIM), bf16),
                   jax.ShapeDtypeStruct((Bs, 2 * LANES, n_win), f32)),
        grid_spec=grid_spec,
        compiler_params=_params(("arbitrary",)),
        name="nsa_decode",
    )(page_table, *([cache_t] * (nseq * n_pages)), q_bd.reshape(Bs, NSA_HEADS, LANES), kv32.reshape(Bs, 1, 4 * LANES),
      win32.reshape(Bs, 1, 2 * LANES), win32, gates[:, 0:3 * NSA_HEADS].reshape(Bs, NSA_HEADS, 3), win_t,
      w1bd, w2bd, gk0, bias, expand)


def _hgrn_decode_kernel(h4_ref, lb_ref, gn_ref, s_ref, o_ref, s_out_ref, ft_ref):
    b = pl.program_id(0)
    nbatch = h4_ref.shape[0]

    @pl.when(b == 0)
    def _():
        for h in range(HG_HEADS):
            f, _, _ = _hgrn_gates(h4_ref[:, HG_WIDTH + h * HG_DIM:HG_WIDTH + (h + 1) * HG_DIM],
                                  lb_ref[:, h * HG_DIM:(h + 1) * HG_DIM])
            ft_ref[h] = f.T

    lane = lax.broadcasted_iota(i32, (HG_DIM, nbatch), 1)
    gn = gn_ref[...]
    for sq in range(s_ref.shape[0]):
        sample = b * s_ref.shape[0] + sq
        row = h4_ref[pl.ds(sample, 1), :]
        for h in range(HG_HEADS):
            part = lambda k: row[:, k * HG_WIDTH + h * HG_DIM:k * HG_WIDTH + (h + 1) * HG_DIM]
            f_col = jnp.sum(jnp.where(lane == sample, ft_ref[h], 0.0), axis=-1, keepdims=True)
            s_new = f_col * s_ref[sq, h] + (1.0 - f_col) * part(2)
            s_out_ref[sq, h] = s_new
            q = jnp.broadcast_to(part(0) * HG_SCALE, (8, HG_DIM)).astype(bf16)
            o = jnp.dot(q, s_new.astype(bf16), preferred_element_type=f32)[0:1]
            o_ref[sq, :, h * HG_DIM:(h + 1) * HG_DIM] = _hgrn_out(o, gn, part(3)).astype(bf16)


def _hgrn_decode(h4, lb, gn, state, nseq=4):
    Bs = h4.shape[0]
    nseq = math.gcd(nseq, Bs)
    sspec = pl.BlockSpec((nseq, HG_HEADS, HG_DIM, HG_DIM), lambda b: (b, 0, 0, 0))
    return pl.pallas_call(
        _hgrn_decode_kernel,
        out_shape=(jax.ShapeDtypeStruct((Bs, 1, HG_WIDTH), bf16),
                   jax.ShapeDtypeStruct((Bs, HG_HEADS, HG_DIM, HG_DIM), f32)),
        grid=(Bs // nseq,),
        in_specs=[pl.BlockSpec(h4.shape, lambda b: (0, 0)), pl.BlockSpec(lb.shape, lambda b: (0, 0)),
                  pl.BlockSpec(gn.shape, lambda b: (0, 0)), sspec],
        out_specs=(pl.BlockSpec((nseq, 1, HG_WIDTH), lambda b: (b, 0, 0)), sspec),
        scratch_shapes=[pltpu.VMEM((HG_HEADS, HG_DIM, Bs), f32)],
        compiler_params=_params(("arbitrary",)),
        name="hgrn_decode",
    )(h4, lb, gn, state)


def _split_w_in(w):
    idx = np.cumsum(SPLITS)[:-1]
    return jnp.split(w, [int(v) for v in idx], axis=1)


def kernel(x_prompt, x_sample, cache_kv, page_table, state_kv_win, state_hgrn, state_conv, rel_table, hg_lb_logits,
           norm1_g, w_in, q_norm_g, k_norm_g, phi_k_w1, phi_k_w2, phi_v_w1, phi_v_w2, hg_norm_g, w_nsa_out, w_hg_out,
           w_o, norm2_g, w_gate, w_up, conv_w, conv_b, w_down):
    assert w_in.shape[0] == 1, "one layer"
    Bp, L, _ = x_prompt.shape
    Bs = x_sample.shape[0]
    n_pages = page_table.shape[1]
    past = n_pages * PAGE_SIZE
    n_win = state_kv_win.shape[2]
    assert x_sample.shape[1] == 1 and n_win == WINDOW and past % (2 * CMP_BLOCK) == 0 and L % LANES == 0

    wq, wkv, wgl, whq, whf, whi, whg, wmg = _split_w_in(w_in[0])
    wgl = jnp.pad(wgl, ((0, 0), (0, LANES - wgl.shape[1])))
    w_prompt = jnp.concatenate([wq, wkv, whq, whf, whi, whg, wmg, wgl], axis=1).astype(bf16)
    wq_h = wq.reshape(D_MODEL, NSA_HEADS, HEAD_DIM)
    zq = jnp.zeros_like(wq_h)
    in_g0 = (np.arange(NSA_HEADS) < NSA_GROUP)[None, :, None]
    wq_bd = jnp.where(in_g0, jnp.concatenate([wq_h, zq], axis=-1), jnp.concatenate([zq, wq_h], axis=-1))
    w_sample = jnp.concatenate([wq_bd.reshape(D_MODEL, NSA_HEADS * LANES), wkv, whq, whf, whi, whg, wmg, wgl],
                               axis=1).astype(bf16)
    g1 = norm1_g[0][None, :]
    gq = q_norm_g[0]
    gq_prompt = jnp.tile(gq, NSA_HEADS)[None, :]
    gq_sample = jnp.tile(gq, 2 * NSA_HEADS)[None, :]
    gk = [jnp.tile(k_norm_g[0, s], NSA_KV_HEADS)[None, :] for s in range(3)]

    def block_diag(mats):
        n = len(mats)
        lead = [(0, 0)] * (mats[0].ndim - 2)
        rows = [jnp.pad(m.astype(bf16), lead + [(0, 0), (t * HEAD_DIM, (n - 1 - t) * HEAD_DIM)])
                for t, m in enumerate(mats)]
        return jnp.concatenate(rows, axis=-2)

    w1bd = block_diag([phi_k_w1[0], phi_k_w1[0], phi_v_w1[0], phi_v_w1[0]]).reshape(CMP_BLOCK * 2 * LANES, 2 * LANES)
    w2bd = block_diag([phi_k_w2[0], phi_k_w2[0], phi_v_w2[0], phi_v_w2[0]])
    lb = jnp.cumsum(jax.nn.softmax(hg_lb_logits.astype(f32), axis=0), axis=0)[0][None, :]
    gn = hg_norm_g[0][None, :]
    ffn_w = (w_nsa_out[0].astype(bf16), w_hg_out[0].astype(bf16), w_o[0].astype(bf16), norm2_g[0][None, :],
             w_gate[0].astype(bf16), w_up[0].astype(bf16), conv_w[0], conv_b[0][None, :], w_down[0].astype(bf16))

    idx_c, idx_t = _prompt_bucket_tables(L)
    tc = _bias_tables(rel_table, jnp.asarray(idx_c)).reshape(L // SEL_BLOCK, NSA_KV_HEADS, NSA_GROUP * SEL_BLOCK, -1)
    tt = _bias_tables(rel_table, jnp.asarray(idx_t)).reshape(5, NSA_KV_HEADS, NSA_GROUP * SEL_BLOCK, LANES)
    bias_dec = _bias_tables(rel_table, jnp.asarray(_decode_bucket_table(past, n_win))[None]).reshape(NSA_HEADS, -1)
    ex_prompt = jnp.asarray(_expand_np(L // SEL_BLOCK, L), dtype=bf16)
    ex_sample = jnp.asarray(_expand_np(past // SEL_BLOCK, past), dtype=bf16)

    xp = x_prompt.reshape(Bp * L, D_MODEL)
    qn, kv_t, win32, katt, h4, mgs, gates = _in_proj(xp, g1, w_prompt, gq_prompt, gk[1], gk[2],
                                                     qw=NSA_WIDTH, q_seg=HEAD_DIM, tm=256, seq_len=L)
    cmp = _compress_prompt(kv_t, w1bd, w2bd, gk[0])
    o_nsa = _nsa_prompt(qn, gates, katt, cmp, tc, tt, ex_prompt, Bp, L)
    o_hg, s_prompt = _hgrn_prompt(h4, lb, gn, Bp, L)
    tm_ffn = 512
    yp, gate_tails = _ffn(xp, o_nsa, o_hg.reshape(Bp * L, HG_WIDTH), mgs, *ffn_w, tm=tm_ffn, tf=D_FF // 2, seq_len=L)
    conv_p = gate_tails.reshape(Bp, L // tm_ffn, 2, D_FF)[:, -1]
    n_keep_win = min(WINDOW, L)
    win_p = win32.reshape(Bp, L, 2, NSA_KV_HEADS, HEAD_DIM)[:, L - n_keep_win:]

    xs = x_sample.reshape(Bs, D_MODEL)
    q_bd, kv32_s, win32_s, _, h4_s, mgs_s, gates_s = _in_proj(xs, g1, w_sample, gq_sample, gk[1], gk[2],
                                                              qw=NSA_HEADS * LANES, q_seg=LANES, tm=Bs)
    cache_t = jnp.swapaxes(cache_kv[0].reshape(cache_kv.shape[1], PAGE_SIZE, 4 * LANES), 1, 2)
    win_t = jnp.swapaxes(state_kv_win[0].reshape(Bs, n_win, 2 * LANES), 1, 2)
    o_nsa_s, win_out_t = _nsa_decode(cache_t, page_table, q_bd, kv32_s, win32_s, gates_s, win_t, w1bd, w2bd, gk[0],
                                     bias_dec, ex_sample)
    win_s = jnp.swapaxes(win_out_t, 1, 2)
    kv32 = jnp.swapaxes(kv_t, 1, 2)
    o_hg_s, s_sample = _hgrn_decode(h4_s, lb, gn, state_hgrn[0])
    ys, gate_s = _ffn(xs, o_nsa_s.reshape(Bs, NSA_WIDTH), o_hg_s.reshape(Bs, HG_WIDTH), mgs_s, *ffn_w,
                      tm=Bs, tf=256, conv_state=state_conv[0])
    conv_s = jnp.stack([state_conv[0][:, 1], gate_s], axis=1)

    kvh = (NSA_KV_HEADS, HEAD_DIM)
    return (yp.reshape(Bp, L, D_MODEL), ys.reshape(Bs, 1, D_MODEL),
            kv32.reshape((1, Bp, L, 4) + kvh), kv32_s.reshape((1, Bs, 1, 4) + kvh),
            win_p[None], win_s.reshape((1, Bs, n_win, 2) + kvh),
            s_prompt[None], s_sample[None], conv_p[None], conv_s[None])
```

```python
import functools
import math

import numpy as np
import jax
import jax.numpy as jnp
from jax import lax
from jax.experimental import pallas as pl
from jax.experimental.pallas import tpu as pltpu

f32, bf16, i32 = jnp.float32, jnp.bfloat16, jnp.int32

D_MODEL = 1024
PAGE_SIZE = 128
NSA_HEADS, NSA_KV_HEADS, NSA_GROUP, HEAD_DIM = 8, 2, 4, 64
CMP_BLOCK, SEL_BLOCK, TOP_N, WINDOW = 32, 64, 16, 512
SCALE = HEAD_DIM ** -0.5
NSA_WIDTH = NSA_HEADS * HEAD_DIM
KV_WIDTH = NSA_KV_HEADS * HEAD_DIM
HG_HEADS, HG_DIM = 4, 128
HG_WIDTH = HG_HEADS * HG_DIM
HG_SCALE = HG_DIM ** -0.5
HG_BLOCK = 16
REL_BUCKETS, REL_MAX_DIST = 32, 128
D_FF = 2816
EPS = 1e-6
SPLITS = (NSA_WIDTH, 6 * KV_WIDTH, 3 * NSA_HEADS, HG_WIDTH, HG_WIDTH, HG_WIDTH, HG_WIDTH, 2 * D_MODEL)
LANES = 128
NEG = -1e30
FORCED_SCORE, INVALID_SCORE = 8.0, -1.0
VMEM_LIMIT = 56 * 1024 * 1024

NT = (((1,), (1,)), ((), ()))
TN = (((0,), (0,)), ((), ()))


def _params(sem):
    return pltpu.CompilerParams(dimension_semantics=sem, vmem_limit_bytes=VMEM_LIMIT)


def _bucket_np(dist):
    n = np.maximum(dist, 0)
    exact = REL_BUCKETS // 2
    ratio = np.log(np.maximum(n, 1).astype(np.float32) / np.float32(exact)) / np.float32(math.log(REL_MAX_DIST / exact))
    big = exact + (ratio.astype(np.float32) * np.float32(REL_BUCKETS - exact)).astype(np.int32)
    return np.where(n < exact, n, np.minimum(big, REL_BUCKETS - 1)).astype(np.int32)


def _cmp_perm(n_cmp):
    half = n_cmp // 2
    c = np.arange(n_cmp)
    return np.where(c < half, 2 * c, 2 * (c - half) + 1)


def _prompt_bucket_tables(L):
    nqb = L // SEL_BLOCK
    n_cmp = L // CMP_BLOCK
    qi = np.arange(SEL_BLOCK)
    c_end = (_cmp_perm(n_cmp) + 1) * CMP_BLOCK - 1
    idx_c = np.stack([_bucket_np((i * SEL_BLOCK + qi)[:, None] - c_end[None, :]) for i in range(nqb)])
    kj = np.arange(SEL_BLOCK)
    def tile(delta):
        return _bucket_np(delta * SEL_BLOCK + qi[:, None] - kj[None, :])
    idx_t = np.stack([np.concatenate([tile(e), tile(e - 1)], axis=1) for e in range(5)])
    return idx_c.astype(np.int32), idx_t.astype(np.int32)


def _decode_bucket_table(past_len, n_win):
    n_cmp = past_len // CMP_BLOCK
    c_end = (_cmp_perm(n_cmp) + 1) * CMP_BLOCK - 1
    d_c = np.zeros((LANES,), np.int64)
    d_c[:n_cmp] = past_len - c_end
    d_s = past_len - np.arange(past_len)
    d_w = n_win - np.arange(n_win)
    d_new = np.zeros((LANES,), np.int64)
    return _bucket_np(np.concatenate([d_s, d_w, d_c, d_new]))[None, :].astype(np.int32)


def _expand_np(n_blocks, n_keys):
    e = (np.arange(n_keys)[None, :] // SEL_BLOCK == np.arange(n_blocks)[:, None])
    return e.astype(np.float32)


def _seg_rmsnorm(x, gain, seg):
    outs = []
    for t in range(x.shape[1] // LANES):
        xt = x[:, t * LANES:(t + 1) * LANES]
        sq = xt * xt
        if seg == LANES:
            r = lax.rsqrt(jnp.sum(sq, axis=-1, keepdims=True) * (1.0 / HEAD_DIM) + EPS)
        else:
            lo = lax.broadcasted_iota(i32, xt.shape, 1) < HEAD_DIM
            s_lo = jnp.sum(jnp.where(lo, sq, 0.0), axis=-1, keepdims=True)
            s_hi = jnp.sum(jnp.where(lo, 0.0, sq), axis=-1, keepdims=True)
            r = jnp.where(lo, lax.rsqrt(s_lo * (1.0 / HEAD_DIM) + EPS), lax.rsqrt(s_hi * (1.0 / HEAD_DIM) + EPS))
        outs.append(xt * r)
    y = outs[0] if len(outs) == 1 else jnp.concatenate(outs, axis=1)
    return y * gain


def _rmsnorm_rows(x, gain):
    return x * lax.rsqrt(jnp.mean(x * x, axis=-1, keepdims=True) + EPS) * gain


def _softmax_parts(s, mask):
    s = jnp.where(mask, s, NEG)
    m = jnp.max(s, axis=-1, keepdims=True)
    e = jnp.where(mask, jnp.exp(s - m), 0.0)
    return e, jnp.sum(e, axis=-1, keepdims=True)


def _bias_table_kernel(tab_ref, idx_ref, out_ref):
    idx = idx_ref[...]
    for h in range(NSA_HEADS):
        acc = jnp.zeros(idx.shape, f32)
        for b in range(REL_BUCKETS):
            acc = jnp.where(idx == b, tab_ref[b, h], acc)
        out_ref[h] = acc


def _bias_tables(rel_table, idx):
    n, rows, w = idx.shape
    return pl.pallas_call(
        _bias_table_kernel,
        out_shape=jax.ShapeDtypeStruct((n, NSA_HEADS, rows, w), f32),
        grid=(n,),
        in_specs=[pl.BlockSpec(memory_space=pltpu.SMEM),
                  pl.BlockSpec((None, rows, w), lambda i: (i, 0, 0))],
        out_specs=pl.BlockSpec((None, NSA_HEADS, rows, w), lambda i: (i, 0, 0, 0)),
        compiler_params=_params(("arbitrary",)),
        name="bias_tables",
    )(rel_table, idx)


def _in_proj_kernel(x_ref, g1_ref, w_ref, gq_ref, gk1_ref, gk2_ref,
                    q_ref, kv_ref, win_ref, katt_ref, h4_ref, mg_ref, gl_ref, *, qw, q_seg, kv_feature_major):
    x = x_ref[...]
    xn = _rmsnorm_rows(x, g1_ref[...]).astype(bf16)

    def proj(c0, width):
        return jnp.dot(xn, w_ref[:, c0:c0 + width], preferred_element_type=f32)

    q_ref[...] = (_seg_rmsnorm(proj(0, qw), gq_ref[...], q_seg) * SCALE).astype(bf16)
    c = qw
    kv = proj(c, 6 * KV_WIDTH)
    c += 6 * KV_WIDTH
    k_sel = _seg_rmsnorm(kv[:, 2 * KV_WIDTH:3 * KV_WIDTH], gk1_ref[...], HEAD_DIM)
    k_win = _seg_rmsnorm(kv[:, 4 * KV_WIDTH:5 * KV_WIDTH], gk2_ref[...], HEAD_DIM)
    v_sel = kv[:, 3 * KV_WIDTH:4 * KV_WIDTH]
    v_win = kv[:, 5 * KV_WIDTH:6 * KV_WIDTH]
    if kv_feature_major:
        kv_ref[0:2 * KV_WIDTH, :] = kv[:, 0:2 * KV_WIDTH].T
        kv_ref[2 * KV_WIDTH:3 * KV_WIDTH, :] = k_sel.T
        kv_ref[3 * KV_WIDTH:4 * KV_WIDTH, :] = v_sel.T
    else:
        kv_ref[:, 0:2 * KV_WIDTH] = kv[:, 0:2 * KV_WIDTH]
        kv_ref[:, 2 * KV_WIDTH:3 * KV_WIDTH] = k_sel
        kv_ref[:, 3 * KV_WIDTH:4 * KV_WIDTH] = v_sel
    win_ref[:, 0:KV_WIDTH] = k_win
    win_ref[:, KV_WIDTH:2 * KV_WIDTH] = v_win
    katt_ref[:, 0:KV_WIDTH] = k_sel.astype(bf16)
    katt_ref[:, KV_WIDTH:2 * KV_WIDTH] = v_sel.astype(bf16)
    katt_ref[:, 2 * KV_WIDTH:3 * KV_WIDTH] = k_win.astype(bf16)
    katt_ref[:, 3 * KV_WIDTH:4 * KV_WIDTH] = v_win.astype(bf16)
    h4_ref[...] = proj(c, 4 * HG_WIDTH)
    c += 4 * HG_WIDTH
    mg_ref[...] = jax.nn.sigmoid(proj(c, 2 * D_MODEL))
    c += 2 * D_MODEL
    gl_ref[...] = jax.nn.sigmoid(proj(c, LANES))


def _in_proj(x2d, g1, w, gq, gk1, gk2, *, qw, q_seg, tm, seq_len=None):
    n = x2d.shape[0]
    cw = w.shape[1]
    row = lambda width: pl.BlockSpec((tm, width), lambda i: (i, 0))
    full = lambda a: pl.BlockSpec(a.shape, lambda i: (0,) * a.ndim)
    widths = (qw, 4 * KV_WIDTH, 2 * KV_WIDTH, 4 * KV_WIDTH, 4 * HG_WIDTH, 2 * D_MODEL, LANES)
    dtypes = (bf16, f32, f32, bf16, f32, f32, f32)
    out_shape = [jax.ShapeDtypeStruct((n, wd), dt) for wd, dt in zip(widths, dtypes)]
    out_specs = [row(wd) for wd in widths]
    if seq_len is not None:
        tiles = seq_len // tm
        out_shape[1] = jax.ShapeDtypeStruct((n // seq_len, 4 * KV_WIDTH, seq_len), f32)
        out_specs[1] = pl.BlockSpec((None, 4 * KV_WIDTH, tm), lambda i: (i // tiles, 0, i % tiles))
    return pl.pallas_call(
        functools.partial(_in_proj_kernel, qw=qw, q_seg=q_seg, kv_feature_major=seq_len is not None),
        out_shape=tuple(out_shape),
        grid=(n // tm,),
        in_specs=[row(D_MODEL), full(g1), pl.BlockSpec((D_MODEL, cw), lambda i: (0, 0)), full(gq), full(gk1), full(gk2)],
        out_specs=tuple(out_specs),
        compiler_params=_params(("arbitrary",)),
        name="in_proj",
    )(x2d, g1, w, gq, gk1, gk2)


CMP_SLOT = CMP_BLOCK + 1


def _cmp_staging_rows(n_cmp):
    return -(-n_cmp * CMP_SLOT // 8) * 8


def _stage_cmp_chunk(src_ref, chunk, kc_t, vc_t, n_cmp):
    per_chunk = LANES // CMP_BLOCK
    for s, t in enumerate((kc_t.T, vc_t.T)):
        for nl in range(per_chunk):
            n = chunk * per_chunk + nl
            slot = n // 2 + (n % 2) * (n_cmp // 2)
            src_ref[s, slot * CMP_SLOT:slot * CMP_SLOT + CMP_BLOCK, :] = t[nl * CMP_BLOCK:(nl + 1) * CMP_BLOCK]


def _compress(src_refs, w1_ref, w2_ref, gk0, n_cmp):
    acc = jnp.zeros((len(src_refs) * n_cmp, 2 * LANES), f32)
    per_dot = 8
    for j0 in range(0, CMP_BLOCK, per_dot):
        xs = jnp.concatenate(
            [jnp.concatenate([src[s, pl.ds(j, n_cmp, stride=CMP_SLOT), :].astype(bf16)
                              for j in range(j0, j0 + per_dot) for s in range(2)], axis=1)
             for src in src_refs], axis=0)
        w = w1_ref[j0 * 2 * LANES:(j0 + per_dot) * 2 * LANES, :]
        acc = acc + jnp.dot(xs, w, preferred_element_type=f32)
    hmid = jax.nn.gelu(acc).astype(bf16)
    y = jnp.dot(hmid, w2_ref[...], preferred_element_type=f32)
    ck = _seg_rmsnorm(y[:, 0:LANES], gk0, HEAD_DIM)
    return jnp.concatenate([ck, y[:, LANES:2 * LANES]], axis=1)


def _compress_prompt_kernel(kv_ref, w1_ref, w2_ref, gk0_ref, out_ref, src_ref, *, n_cmp):
    for c in range(kv_ref.shape[1] // LANES):
        cols = slice(c * LANES, (c + 1) * LANES)
        _stage_cmp_chunk(src_ref, c, kv_ref[0:LANES, cols], kv_ref[LANES:2 * LANES, cols], n_cmp)
    out_ref[...] = _compress([src_ref], w1_ref, w2_ref, gk0_ref[...], n_cmp).astype(bf16)


def _compress_prompt(kv_t, w1bd, w2bd, gk0):
    B, _, L = kv_t.shape
    n_cmp = L // CMP_BLOCK
    return pl.pallas_call(
        functools.partial(_compress_prompt_kernel, n_cmp=n_cmp),
        out_shape=jax.ShapeDtypeStruct((B, n_cmp, 2 * LANES), bf16),
        grid=(B,),
        in_specs=[pl.BlockSpec((None, 2 * LANES, L), lambda b: (b, 0, 0)),
                  pl.BlockSpec(w1bd.shape, lambda b: (0, 0)),
                  pl.BlockSpec(w2bd.shape, lambda b: (0, 0)),
                  pl.BlockSpec(gk0.shape, lambda b: (0, 0))],
        out_specs=pl.BlockSpec((None, n_cmp, 2 * LANES), lambda b: (b, 0, 0)),
        scratch_shapes=[pltpu.VMEM((2, _cmp_staging_rows(n_cmp), LANES), f32)],
        compiler_params=_params(("arbitrary",)),
        name="compress_prompt",
    )(kv_t, w1bd, w2bd, gk0)


def _select_blocks_t(score, valid, n_keep):
    nb = score.shape[0]
    jj = lax.broadcasted_iota(i32, score.shape, 0)
    cnt = jnp.zeros(score.shape, f32)
    for k in range(nb):
        sk = score[k:k + 1, :]
        better = (sk > score) | ((sk == score) & (jj > k))
        cnt = cnt + jnp.where(better, 1.0, 0.0)
    return (cnt < n_keep) & valid


def _sel_tile_classes(n_tiles):
    return sorted({-(-n_tiles * c // 4) for c in range(1, 5)})


def _nsa_prompt_kernel(q_ref, gate_ref, katt_ref, cmp_ref, tc_ref, tt_ref, ex_ref, o_ref,
                       selx_ref, s_ref, os_ref, *, L):
    i = pl.program_id(1)
    n_cmp = L // CMP_BLOCK
    n_sel = L // SEL_BLOCK
    n_tiles = L // LANES
    qb = SEL_BLOCK
    rows = NSA_GROUP * qb
    units = [(sq, g) for sq in range(q_ref.shape[0]) for g in range(NSA_KV_HEADS)]
    nu = range(len(units))
    row64 = lax.broadcasted_iota(i32, (qb, LANES), 0)
    lane = lax.broadcasted_iota(i32, (qb, LANES), 1)
    qpos = i * qb + row64

    qs = [jnp.concatenate(
        [q_ref[sq, :, (NSA_GROUP * g + r) * HEAD_DIM:(NSA_GROUP * g + r + 1) * HEAD_DIM] for r in range(NSA_GROUP)],
        axis=0) for sq, g in units]

    def attend(k_off, v_off, tile0, n_t, mask_fn):
        chunk = 4

        def key_rows(t0, count, off, u):
            sq, g = units[u]
            r0 = pl.multiple_of((tile0 + t0) * LANES, LANES)
            return katt_ref[sq, pl.ds(r0, count * LANES), off + g * HEAD_DIM:off + (g + 1) * HEAD_DIM]

        def scores(t0, count, mruns):
            dots = [lax.dot_general(qs[u], key_rows(t0, count, k_off, u), NT, preferred_element_type=f32)
                    for u in nu]
            out = list(mruns)
            for j in range(count):
                ta = tile0 + t0 + j
                bias_idx = jnp.clip(i - 2 * ta, 0, 4)
                kpos = ta * LANES + lane
                for u in nu:
                    s = dots[u][:, j * LANES:(j + 1) * LANES] + tt_ref[bias_idx, units[u][1]]
                    msk = mask_fn(u, ta, kpos)
                    s = jnp.where(msk[None], s.reshape(NSA_GROUP, qb, LANES), NEG).reshape(rows, LANES)
                    s_ref[u, t0 + j] = s
                    out[u] = jnp.maximum(out[u], s)
            return tuple(out)

        def values(t0, count, carry, ms):
            ls, accs = list(carry[0]), list(carry[1])
            ps = []
            for u in nu:
                pj = [jnp.exp(s_ref[u, t0 + j] - ms[u]) for j in range(count)]
                for p in pj:
                    ls[u] = ls[u] + p
                ps.append(jnp.concatenate([p.astype(bf16) for p in pj], axis=1))
            for u in nu:
                accs[u] = accs[u] + jnp.dot(ps[u], key_rows(t0, count, v_off, u), preferred_element_type=f32)
            return tuple(ls), tuple(accs)

        def over_chunks(fn, carry):
            n_full = n_t // chunk
            if n_full > 2:
                carry = lax.fori_loop(0, n_full, lambda c, cr: fn(c * chunk, chunk, cr), carry)
            else:
                for c in range(n_full):
                    carry = fn(c * chunk, chunk, carry)
            if n_t % chunk:
                carry = fn(n_full * chunk, n_t % chunk, carry)
            return carry

        mruns = over_chunks(scores, tuple(jnp.full((rows, LANES), NEG, f32) for _ in nu))
        ms = [jnp.max(m, axis=-1, keepdims=True) for m in mruns]
        init = (tuple(jnp.zeros((rows, LANES), f32) for _ in nu),
                tuple(jnp.zeros((rows, HEAD_DIM), f32) for _ in nu))
        ls, accs = over_chunks(lambda t0, count, cr: values(t0, count, cr, ms), init)
        return [accs[u] / jnp.sum(ls[u], axis=-1, keepdims=True) for u in nu]

    n_wt = min(WINDOW // LANES + 1, n_tiles)
    w0 = jnp.clip((i - WINDOW // qb) // 2, 0, n_tiles - n_wt)
    o_w = attend(2 * KV_WIDTH, 3 * KV_WIDTH, w0, n_wt,
                 lambda u, ta, kpos: (kpos <= qpos) & (qpos - kpos <= WINDOW))

    o_c, imp_t = [], []
    col = lax.broadcasted_iota(i32, (rows, n_cmp), 1)
    rq = lax.broadcasted_iota(i32, (rows, n_cmp), 0) & (qb - 1)
    blk = jnp.where(col < n_cmp // 2, 2 * col, 2 * col - (n_cmp - 1))
    vis = (blk + 1) * CMP_BLOCK - 1 <= i * qb + rq
    for u, (sq, g) in enumerate(units):
        ck = cmp_ref[sq, :, g * HEAD_DIM:(g + 1) * HEAD_DIM]
        cv = cmp_ref[sq, :, LANES + g * HEAD_DIM:LANES + (g + 1) * HEAD_DIM]
        s = lax.dot_general(qs[u], ck, NT, preferred_element_type=f32) + tc_ref[g]
        e, den = _softmax_parts(s, vis)
        p = e / jnp.where(den > 0, den, 1.0)
        o_c.append(jnp.dot(p.astype(bf16), cv, preferred_element_type=f32))
        ps_t = (p[0:qb] + p[qb:2 * qb] + p[2 * qb:3 * qb] + p[3 * qb:4 * qb]).T
        imp_t.append(ps_t[0:n_sel] + ps_t[n_sel:2 * n_sel])

    imp = jnp.concatenate(imp_t, axis=1)
    jj = lax.broadcasted_iota(i32, imp.shape, 0)
    valid = jj <= i
    forced = valid & ((jj == 0) | (jj >= i - 1))
    score = jnp.where(forced, FORCED_SCORE, jnp.where(valid, imp, INVALID_SCORE))
    sel = _select_blocks_t(score, valid, min(TOP_N, n_sel))
    selx = lax.dot_general(jnp.where(sel, 1.0, 0.0).astype(bf16), ex_ref[...], TN, preferred_element_type=f32)
    for u in nu:
        for t in range(n_tiles):
            selx_ref[u, t] = selx[u * qb:(u + 1) * qb, t * LANES:(t + 1) * LANES]

    need = i // 2 + 1
    prev = 0
    for n_t in _sel_tile_classes(n_tiles):
        @pl.when((need > prev) & (need <= n_t))
        def _(n_t=n_t):
            o_s = attend(0, KV_WIDTH, 0, n_t, lambda u, ta, kpos: (selx_ref[u, ta] > 0.5) & (kpos <= qpos))
            for u in nu:
                os_ref[u] = o_s[u]
        prev = n_t

    for u, (sq, g) in enumerate(units):
        o_s = os_ref[u]
        gates = gate_ref[sq]
        for r in range(NSA_GROUP):
            h = NSA_GROUP * g + r
            sl = slice(r * qb, (r + 1) * qb)
            o = (gates[:, 3 * h:3 * h + 1] * o_c[u][sl] + gates[:, 3 * h + 1:3 * h + 2] * o_s[sl]
                 + gates[:, 3 * h + 2:3 * h + 3] * o_w[u][sl])
            o_ref[sq, :, h * HEAD_DIM:(h + 1) * HEAD_DIM] = o.astype(bf16)


def _nsa_prompt(qn, gates, katt, cmp, tc, tt, expand, B, L, nseq=2):
    nqb = L // SEL_BLOCK
    qb = SEL_BLOCK
    n_tiles = L // LANES
    rows = NSA_GROUP * qb
    nseq = math.gcd(nseq, B)
    n_units = nseq * NSA_KV_HEADS
    seq3 = lambda a: a.reshape(B, L, a.shape[-1])
    return pl.pallas_call(
        functools.partial(_nsa_prompt_kernel, L=L),
        out_shape=jax.ShapeDtypeStruct((B, L, NSA_WIDTH), bf16),
        grid=(B // nseq, nqb),
        in_specs=[pl.BlockSpec((nseq, qb, NSA_WIDTH), lambda b, i: (b, i, 0)),
                  pl.BlockSpec((nseq, qb, LANES), lambda b, i: (b, i, 0)),
                  pl.BlockSpec((nseq, L, 4 * KV_WIDTH), lambda b, i: (b, 0, 0)),
                  pl.BlockSpec((nseq, L // CMP_BLOCK, 2 * LANES), lambda b, i: (b, 0, 0)),
                  pl.BlockSpec((None,) + tc.shape[1:], lambda b, i: (i, 0, 0, 0)),
                  pl.BlockSpec(tt.shape, lambda b, i: (0, 0, 0, 0)),
                  pl.BlockSpec(expand.shape, lambda b, i: (0, 0))],
        out_specs=pl.BlockSpec((nseq, qb, NSA_WIDTH), lambda b, i: (b, i, 0)),
        scratch_shapes=[pltpu.VMEM((n_units, n_tiles, qb, LANES), f32),
                        pltpu.VMEM((n_units, n_tiles, rows, LANES), f32),
                        pltpu.VMEM((n_units, rows, HEAD_DIM), f32)],
        compiler_params=_params(("arbitrary", "arbitrary")),
        name="nsa_prompt",
    )(seq3(qn), seq3(gates), seq3(katt), cmp, tc, tt, expand).reshape(B * L, NSA_WIDTH)


def _hgrn_gates(hf, lb):
    f = lb + (1.0 - lb) * jax.nn.sigmoid(hf)
    return f, 1.0 - f, jnp.log(f)


def _hgrn_out(o, gn, hgate):
    return _rmsnorm_rows(o, gn) * jax.nn.silu(hgate)


def _hgrn_prompt_kernel(h4_ref, lb_ref, gn_ref, o_ref, s_out_ref, st_ref, *, tc):
    c = pl.program_id(1)
    nb = HG_BLOCK
    nseq = h4_ref.shape[0]

    @pl.when(c == 0)
    def _():
        st_ref[...] = jnp.zeros_like(st_ref)

    t_row = lax.broadcasted_iota(i32, (nb, HG_DIM), 0)
    t_col = lax.broadcasted_iota(i32, (nb, 1), 0)
    gn = gn_ref[...]
    chains = [(sq, h) for sq in range(nseq) for h in range(HG_HEADS)]

    def cumsum_rows(x):
        shift = 1
        while shift < nb:
            x = x + jnp.where(t_row >= shift, pltpu.roll(x, shift, axis=0), 0.0)
            shift *= 2
        return x

    def block(bi, _):
        r0 = pl.multiple_of(bi * nb, nb)

        def sl(sq, h, part):
            return h4_ref[sq, pl.ds(r0, nb), part * HG_WIDTH + h * HG_DIM:part * HG_WIDTH + (h + 1) * HG_DIM]

        qs, ks, vs, bs, sts, os_ = [], [], [], [], [], []
        for sq, h in chains:
            _, k, glog = _hgrn_gates(sl(sq, h, 1), lb_ref[:, h * HG_DIM:(h + 1) * HG_DIM])
            qs.append(sl(sq, h, 0) * HG_SCALE)
            ks.append(k)
            vs.append(sl(sq, h, 2))
            bs.append(cumsum_rows(glog))
        for c_, (sq, h) in enumerate(chains):
            sts.append(st_ref[sq, h])
            os_.append(lax.dot_general((qs[c_] * jnp.exp(bs[c_])).astype(bf16), sts[c_].astype(bf16), NT,
                                       preferred_element_type=f32))
        for c_ in range(len(chains)):
            q, k, v, b, o = qs[c_], ks[c_], vs[c_], bs[c_], os_[c_]
            for s in range(nb):
                w = q * k[s:s + 1] * jnp.exp(jnp.minimum(b - b[s:s + 1], 0.0))
                a = jnp.where(t_col >= s, jnp.sum(w, axis=-1, keepdims=True), 0.0)
                o = o + a * v[s:s + 1]
            os_[c_] = o
        for c_, (sq, h) in enumerate(chains):
            b_last = bs[c_][nb - 1:nb]
            kt = ks[c_] * jnp.exp(b_last - bs[c_])
            upd = lax.dot_general(vs[c_].astype(bf16), kt.astype(bf16), TN, preferred_element_type=f32)
            st_ref[sq, h] = jnp.exp(b_last) * sts[c_] + upd
            o_ref[sq, pl.ds(r0, nb), h * HG_DIM:(h + 1) * HG_DIM] = _hgrn_out(os_[c_], gn, sl(sq, h, 3)).astype(bf16)
        return 0

    lax.fori_loop(0, tc // nb, block, 0)

    @pl.when(c == pl.num_programs(1) - 1)
    def _():
        for sq in range(nseq):
            for h in range(HG_HEADS):
                s_out_ref[sq, h] = st_ref[sq, h].T


def _hgrn_prompt(h4, lb, gn, B, L, tc=256, nseq=4):
    nc = L // tc
    nseq = math.gcd(nseq, B)
    return pl.pallas_call(
        functools.partial(_hgrn_prompt_kernel, tc=tc),
        out_shape=(jax.ShapeDtypeStruct((B, L, HG_WIDTH), bf16),
                   jax.ShapeDtypeStruct((B, HG_HEADS, HG_DIM, HG_DIM), f32)),
        grid=(B // nseq, nc),
        in_specs=[pl.BlockSpec((nseq, tc, 4 * HG_WIDTH), lambda b, c: (b, c, 0)),
                  pl.BlockSpec(lb.shape, lambda b, c: (0, 0)),
                  pl.BlockSpec(gn.shape, lambda b, c: (0, 0))],
        out_specs=(pl.BlockSpec((nseq, tc, HG_WIDTH), lambda b, c: (b, c, 0)),
                   pl.BlockSpec((nseq, HG_HEADS, HG_DIM, HG_DIM), lambda b, c: (b, 0, 0, 0))),
        scratch_shapes=[pltpu.VMEM((nseq, HG_HEADS, HG_DIM, HG_DIM), f32)],
        compiler_params=_params(("arbitrary", "arbitrary")),
        name="hgrn_prompt",
    )(h4.reshape(B, L, 4 * HG_WIDTH), lb, gn)


def _ffn_kernel(*refs, decode, tiles_per_seq):
    if decode:
        (x_ref, on_ref, oh_ref, mg_ref, wn_ref, wh_ref, wo_ref, g2_ref, wg_ref, wu_ref, cw_ref, cb_ref, wd_ref,
         cs0_ref, cs1_ref, y_ref, gate_out_ref, x1_ref, xn2_ref, acc_ref) = refs
    else:
        (x_ref, on_ref, oh_ref, mg_ref, wn_ref, wh_ref, wo_ref, g2_ref, wg_ref, wu_ref, cw_ref, cb_ref, wd_ref,
         y_ref, gate_out_ref, x1_ref, xn2_ref, acc_ref, carry_ref) = refs
    i = pl.program_id(0)
    fi = pl.program_id(1)
    tm = x_ref.shape[0]

    @pl.when(fi == 0)
    def _():
        y_a = jnp.dot(on_ref[...], wn_ref[...], preferred_element_type=f32)
        y_b = jnp.dot(oh_ref[...], wh_ref[...], preferred_element_type=f32)
        mg = mg_ref[...]
        merged = mg[:, 0:D_MODEL] * y_a + mg[:, D_MODEL:2 * D_MODEL] * y_b
        x1 = x_ref[...] + jnp.dot(merged.astype(bf16), wo_ref[...], preferred_element_type=f32)
        x1_ref[...] = x1
        xn2_ref[...] = _rmsnorm_rows(x1, g2_ref[...]).astype(bf16)
        acc_ref[...] = jnp.zeros_like(acc_ref)

    xn2 = xn2_ref[...]
    gate = jnp.dot(xn2, wg_ref[...], preferred_element_type=f32)
    up = jnp.dot(xn2, wu_ref[...], preferred_element_type=f32)
    cw = cw_ref[...]
    if decode:
        prev2, prev1 = cs0_ref[...], cs1_ref[...]
        gate_out_ref[...] = gate
    else:
        @pl.when(i % tiles_per_seq == 0)
        def _():
            carry_ref[fi] = jnp.zeros(carry_ref.shape[1:], f32)
        carry = carry_ref[fi]
        rid = lax.broadcasted_iota(i32, gate.shape, 0)
        prev1 = jnp.where(rid == 0, carry[7:8], pltpu.roll(gate, 1, axis=0))
        prev2 = jnp.where(rid == 0, carry[6:7], jnp.where(rid == 1, carry[7:8], pltpu.roll(gate, 2, axis=0)))
        carry_ref[fi, 6:8, :] = gate[tm - 2:tm]
        gate_out_ref[...] = gate[tm - 2:tm]
    conv = cb_ref[...] + cw[0:1] * prev2 + cw[1:2] * prev1 + cw[2:3] * gate
    hmid = (jax.nn.silu(conv) * up).astype(bf16)
    acc_ref[...] += jnp.dot(hmid, wd_ref[...], preferred_element_type=f32)

    @pl.when(fi == pl.num_programs(1) - 1)
    def _():
        y_ref[...] = x1_ref[...] + acc_ref[...]


def _ffn(x2d, o_nsa, o_hg, mgs, wn, wh, wo, g2, wg, wu, cw, cb, wd, *, tm, tf, seq_len=None, conv_state=None):
    n = x2d.shape[0]
    decode = conv_state is not None
    nf = D_FF // tf
    row = lambda width: pl.BlockSpec((tm, width), lambda i, f: (i, 0))
    full = lambda a: pl.BlockSpec(a.shape, lambda i, f: (0,) * a.ndim)
    in_specs = [row(D_MODEL), row(NSA_WIDTH), row(HG_WIDTH), row(2 * D_MODEL), full(wn), full(wh), full(wo), full(g2),
                pl.BlockSpec((D_MODEL, tf), lambda i, f: (0, f)), pl.BlockSpec((D_MODEL, tf), lambda i, f: (0, f)),
                pl.BlockSpec((3, tf), lambda i, f: (0, f)), pl.BlockSpec((1, tf), lambda i, f: (0, f)),
                pl.BlockSpec((tf, D_MODEL), lambda i, f: (f, 0))]
    args = [x2d, o_nsa, o_hg, mgs, wn, wh, wo, g2, wg, wu, cw, cb, wd]
    scratch = [pltpu.VMEM((tm, D_MODEL), f32), pltpu.VMEM((tm, D_MODEL), bf16), pltpu.VMEM((tm, D_MODEL), f32)]
    if decode:
        in_specs += [pl.BlockSpec((tm, tf), lambda i, f: (i, f))] * 2
        args += [conv_state[:, 0], conv_state[:, 1]]
        gate_shape = jax.ShapeDtypeStruct((n, D_FF), f32)
        gate_spec = pl.BlockSpec((tm, tf), lambda i, f: (i, f))
        tiles_per_seq = 1
    else:
        tiles_per_seq = seq_len // tm
        gate_shape = jax.ShapeDtypeStruct((n // tm, 2, D_FF), f32)
        gate_spec = pl.BlockSpec((None, 2, tf), lambda i, f: (i, 0, f))
        scratch.append(pltpu.VMEM((nf, 8, tf), f32))
    return pl.pallas_call(
        functools.partial(_ffn_kernel, decode=decode, tiles_per_seq=tiles_per_seq),
        out_shape=(jax.ShapeDtypeStruct((n, D_MODEL), f32), gate_shape),
        grid=(n // tm, nf),
        in_specs=in_specs,
        out_specs=(row(D_MODEL), gate_spec),
        scratch_shapes=scratch,
        compiler_params=_params(("arbitrary", "arbitrary")),
        name="ffn_decode" if decode else "ffn_prompt",
    )(*args)


def _nsa_decode_kernel(*refs, n_pages, n_win, nseq):
    page_refs = refs[1:1 + nseq * n_pages]
    (q_ref, kvn_ref, winn_ref, win_all_ref, gate_ref, win_ref, w1_ref, w2_ref, gk0_ref, bias_ref, ex_ref,
     o_ref, win_out_ref, src_ref, ksel_ref, vsel_ref, wcol_ref) = refs[1 + nseq * n_pages:]
    step = pl.program_id(0)
    past = n_pages * PAGE_SIZE
    n_cmp = past // CMP_BLOCK
    n_sel = past // SEL_BLOCK
    heads = NSA_HEADS
    samples = range(nseq)

    @pl.when(step == 0)
    def _():
        for t in range(2):
            wcol_ref[t * LANES:(t + 1) * LANES, :] = win_all_ref[:, t * LANES:(t + 1) * LANES].T

    for sq in samples:
        for k in range(n_pages):
            page = page_refs[sq * n_pages + k]
            cols = slice(k * PAGE_SIZE, (k + 1) * PAGE_SIZE)
            _stage_cmp_chunk(src_ref.at[sq], k, page[0:LANES, :], page[LANES:2 * LANES, :], n_cmp)
            ksel_ref[sq, :, cols] = page[2 * LANES:3 * LANES, :].astype(bf16)
            vsel_ref[sq, :, cols] = page[3 * LANES:4 * LANES, :].astype(bf16)

    hrow = lax.broadcasted_iota(i32, (heads, 1), 0)
    g0 = hrow < NSA_GROUP
    bias = bias_ref[...]
    b_s = bias[:, 0:past]
    b_w = bias[:, past:past + n_win]
    b_c = bias[:, past + n_win:past + n_win + n_cmp]
    b_new = bias[:, past + n_win + LANES:past + n_win + LANES + 1]
    qs = [q_ref[sq] for sq in samples]

    def pick(o2):
        return jnp.where(g0, o2[:, 0:HEAD_DIM], o2[:, HEAD_DIM:2 * HEAD_DIM])

    def with_new(q, k_t, v_t, b_past, mask, k_new, v_new):
        s_past = jnp.dot(q, k_t, preferred_element_type=f32) + b_past
        s_new = jnp.sum(q.astype(f32) * k_new, axis=-1, keepdims=True) + b_new
        sm = s_past if mask is None else jnp.where(mask, s_past, NEG)
        m = jnp.maximum(jnp.max(sm, axis=-1, keepdims=True), s_new)
        e = jnp.exp(sm - m)
        e_new = jnp.exp(s_new - m)
        den = jnp.sum(e, axis=-1, keepdims=True) + e_new
        o2 = lax.dot_general(e.astype(bf16), v_t, NT, preferred_element_type=f32) + e_new * v_new
        return pick(o2 / den)

    o_w = []
    for sq in samples:
        winn = winn_ref[sq]
        o_w.append(with_new(qs[sq], win_ref[sq, 0:LANES, :].astype(bf16), win_ref[sq, LANES:2 * LANES, :].astype(bf16),
                            b_w, None, winn[:, 0:LANES], winn[:, LANES:2 * LANES]))

    cmp = _compress([src_ref.at[sq] for sq in samples], w1_ref, w2_ref, gk0_ref[...], n_cmp)
    o_c, masks = [], []
    cur = past // SEL_BLOCK
    jrow = lax.broadcasted_iota(i32, (NSA_KV_HEADS, n_sel), 1)
    forced = (jrow == 0) | (jrow >= cur - 1)
    n_keep = min(TOP_N, n_sel + 1) - 1
    ri = lax.broadcasted_iota(i32, (n_sel, n_sel), 0)
    ci = lax.broadcasted_iota(i32, (n_sel, n_sel), 1)
    for sq in samples:
        ck = cmp[sq * n_cmp:(sq + 1) * n_cmp, 0:LANES].astype(bf16)
        cv = cmp[sq * n_cmp:(sq + 1) * n_cmp, LANES:2 * LANES].astype(bf16)
        s_c = lax.dot_general(qs[sq], ck, NT, preferred_element_type=f32) + b_c
        e_c, den_c = _softmax_parts(s_c, jnp.full(s_c.shape, True))
        p_c = e_c / jnp.where(den_c > 0, den_c, 1.0)
        o_c.append(pick(jnp.dot(p_c.astype(bf16), cv, preferred_element_type=f32)))

        ps = jnp.concatenate([jnp.sum(p_c[0:NSA_GROUP], axis=0, keepdims=True),
                              jnp.sum(p_c[NSA_GROUP:heads], axis=0, keepdims=True)], axis=0)
        imp = ps[:, 0:n_sel] + ps[:, n_sel:2 * n_sel]
        score_row = jnp.where(forced, FORCED_SCORE, imp)
        sels = []
        for g in range(NSA_KV_HEADS):
            rowb = jnp.broadcast_to(score_row[g:g + 1], (n_sel, n_sel))
            colb = jnp.sum(jnp.where(ri == ci, rowb, 0.0), axis=-1, keepdims=True)
            better = (colb > rowb) | ((colb == rowb) & (ri < ci))
            cnt = jnp.sum(jnp.where(better, 1.0, 0.0), axis=0, keepdims=True)
            sels.append(jnp.where(cnt < n_keep, 1.0, 0.0))
        sel = jnp.concatenate(sels, axis=0).astype(bf16)
        selx = jnp.dot(sel, ex_ref[...], preferred_element_type=f32)
        masks.append(jnp.where(g0, selx[0:1], selx[1:2]) > 0.5)

    lane_b = lax.broadcasted_iota(i32, wcol_ref.shape, 1)
    lane_w = lax.broadcasted_iota(i32, win_ref.shape[1:], 1)
    for sq in samples:
        kvn = kvn_ref[sq]
        o_s = with_new(qs[sq], ksel_ref[sq], vsel_ref[sq], b_s, masks[sq],
                       kvn[:, 2 * LANES:3 * LANES], kvn[:, 3 * LANES:4 * LANES])
        gates = gate_ref[sq]
        o_ref[sq] = (gates[:, 0:1] * o_c[sq] + gates[:, 1:2] * o_s + gates[:, 2:3] * o_w[sq]).astype(bf16)

        new_col = jnp.sum(jnp.where(lane_b == step * nseq + sq, wcol_ref[...], 0.0), axis=-1, keepdims=True)
        win_out_ref[sq] = jnp.where(lane_w == n_win - 1, new_col, pltpu.roll(win_ref[sq], n_win - 1, axis=1))


def _nsa_decode(cache_t, page_table, q_bd, kv32, win32, gates, win_t, w1bd, w2bd, gk0, bias, expand, nseq=2):
    Bs, n_pages = page_table.shape
    n_win = win_t.shape[2]
    past = n_pages * PAGE_SIZE
    n_cmp = past // CMP_BLOCK
    nseq = math.gcd(nseq, Bs)
    per_b = lambda shape: pl.BlockSpec((nseq,) + shape, lambda b, pt: (b,) + (0,) * len(shape))
    const = lambda a: pl.BlockSpec(a.shape, lambda b, pt: (0,) * a.ndim)
    page_spec = lambda sq, k: pl.BlockSpec((None, 4 * LANES, PAGE_SIZE), lambda b, pt: (pt[b * nseq + sq, k], 0, 0))
    grid_spec = pltpu.PrefetchScalarGridSpec(
        num_scalar_prefetch=1,
        grid=(Bs // nseq,),
        in_specs=[page_spec(sq, k) for sq in range(nseq) for k in range(n_pages)] + [
            per_b((NSA_HEADS, LANES)), per_b((1, 4 * LANES)), per_b((1, 2 * LANES)), const(win32),
            per_b((NSA_HEADS, 3)), per_b((2 * LANES, n_win)), const(w1bd), const(w2bd), const(gk0), const(bias),
            const(expand)],
        out_specs=(per_b((NSA_HEADS, HEAD_DIM)), per_b((2 * LANES, n_win))),
        scratch_shapes=[pltpu.VMEM((nseq, 2, _cmp_staging_rows(n_cmp), LANES), f32),
                        pltpu.VMEM((nseq, LANES, past), bf16), pltpu.VMEM((nseq, LANES, past), bf16),
                        pltpu.VMEM((2 * LANES, Bs), f32)],
    )
    return pl.pallas_call(
        functools.partial(_nsa_decode_kernel, n_pages=n_pages, n_win=n_win, nseq=nseq),
        out_shape=(jax.ShapeDtypeStruct((Bs, NSA_HEADS, HEAD_DIM), bf16),
                   jax.ShapeDtypeStruct((Bs, 2 * LANES, n_win), f32)),
        grid_spec=grid_spec,
        compiler_params=_params(("arbitrary",)),
        name="nsa_decode",
    )(page_table, *([cache_t] * (nseq * n_pages)), q_bd.reshape(Bs, NSA_HEADS, LANES), kv32.reshape(Bs, 1, 4 * LANES),
      win32.reshape(Bs, 1, 2 * LANES), win32, gates[:, 0:3 * NSA_HEADS].reshape(Bs, NSA_HEADS, 3), win_t,
      w1bd, w2bd, gk0, bias, expand)


def _hgrn_decode_kernel(h4_ref, lb_ref, gn_ref, s_ref, o_ref, s_out_ref, ft_ref):
    b = pl.program_id(0)
    nbatch = h4_ref.shape[0]

    @pl.when(b == 0)
    def _():
        for h in range(HG_HEADS):
            f, _, _ = _hgrn_gates(h4_ref[:, HG_WIDTH + h * HG_DIM:HG_WIDTH + (h + 1) * HG_DIM],
                                  lb_ref[:, h * HG_DIM:(h + 1) * HG_DIM])
            ft_ref[h] = f.T

    lane = lax.broadcasted_iota(i32, (HG_DIM, nbatch), 1)
    gn = gn_ref[...]
    for sq in range(s_ref.shape[0]):
        sample = b * s_ref.shape[0] + sq
        row = h4_ref[pl.ds(sample, 1), :]
        for h in range(HG_HEADS):
            part = lambda k: row[:, k * HG_WIDTH + h * HG_DIM:k * HG_WIDTH + (h + 1) * HG_DIM]
            f_col = jnp.sum(jnp.where(lane == sample, ft_ref[h], 0.0), axis=-1, keepdims=True)
            s_new = f_col * s_ref[sq, h] + (1.0 - f_col) * part(2)
            s_out_ref[sq, h] = s_new
            q = jnp.broadcast_to(part(0) * HG_SCALE, (8, HG_DIM)).astype(bf16)
            o = jnp.dot(q, s_new.astype(bf16), preferred_element_type=f32)[0:1]
            o_ref[sq, :, h * HG_DIM:(h + 1) * HG_DIM] = _hgrn_out(o, gn, part(3)).astype(bf16)


def _hgrn_decode(h4, lb, gn, state, nseq=4):
    Bs = h4.shape[0]
    nseq = math.gcd(nseq, Bs)
    sspec = pl.BlockSpec((nseq, HG_HEADS, HG_DIM, HG_DIM), lambda b: (b, 0, 0, 0))
    return pl.pallas_call(
        _hgrn_decode_kernel,
        out_shape=(jax.ShapeDtypeStruct((Bs, 1, HG_WIDTH), bf16),
                   jax.ShapeDtypeStruct((Bs, HG_HEADS, HG_DIM, HG_DIM), f32)),
        grid=(Bs // nseq,),
        in_specs=[pl.BlockSpec(h4.shape, lambda b: (0, 0)), pl.BlockSpec(lb.shape, lambda b: (0, 0)),
                  pl.BlockSpec(gn.shape, lambda b: (0, 0)), sspec],
        out_specs=(pl.BlockSpec((nseq, 1, HG_WIDTH), lambda b: (b, 0, 0)), sspec),
        scratch_shapes=[pltpu.VMEM((HG_HEADS, HG_DIM, Bs), f32)],
        compiler_params=_params(("arbitrary",)),
        name="hgrn_decode",
    )(h4, lb, gn, state)


def _split_w_in(w):
    idx = np.cumsum(SPLITS)[:-1]
    return jnp.split(w, [int(v) for v in idx], axis=1)


def kernel(x_prompt, x_sample, cache_kv, page_table, state_kv_win, state_hgrn, state_conv, rel_table, hg_lb_logits,
           norm1_g, w_in, q_norm_g, k_norm_g, phi_k_w1, phi_k_w2, phi_v_w1, phi_v_w2, hg_norm_g, w_nsa_out, w_hg_out,
           w_o, norm2_g, w_gate, w_up, conv_w, conv_b, w_down):
    assert w_in.shape[0] == 1, "one layer"
    Bp, L, _ = x_prompt.shape
    Bs = x_sample.shape[0]
    n_pages = page_table.shape[1]
    past = n_pages * PAGE_SIZE
    n_win = state_kv_win.shape[2]
    assert x_sample.shape[1] == 1 and n_win == WINDOW and past % (2 * CMP_BLOCK) == 0 and L % LANES == 0

    wq, wkv, wgl, whq, whf, whi, whg, wmg = _split_w_in(w_in[0])
    wgl = jnp.pad(wgl, ((0, 0), (0, LANES - wgl.shape[1])))
    w_prompt = jnp.concatenate([wq, wkv, whq, whf, whi, whg, wmg, wgl], axis=1).astype(bf16)
    wq_h = wq.reshape(D_MODEL, NSA_HEADS, HEAD_DIM)
    zq = jnp.zeros_like(wq_h)
    in_g0 = (np.arange(NSA_HEADS) < NSA_GROUP)[None, :, None]
    wq_bd = jnp.where(in_g0, jnp.concatenate([wq_h, zq], axis=-1), jnp.concatenate([zq, wq_h], axis=-1))
    w_sample = jnp.concatenate([wq_bd.reshape(D_MODEL, NSA_HEADS * LANES), wkv, whq, whf, whi, whg, wmg, wgl],
                               axis=1).astype(bf16)
    g1 = norm1_g[0][None, :]
    gq = q_norm_g[0]
    gq_prompt = jnp.tile(gq, NSA_HEADS)[None, :]
    gq_sample = jnp.tile(gq, 2 * NSA_HEADS)[None, :]
    gk = [jnp.tile(k_norm_g[0, s], NSA_KV_HEADS)[None, :] for s in range(3)]

    def block_diag(mats):
        n = len(mats)
        lead = [(0, 0)] * (mats[0].ndim - 2)
        rows = [jnp.pad(m.astype(bf16), lead + [(0, 0), (t * HEAD_DIM, (n - 1 - t) * HEAD_DIM)])
                for t, m in enumerate(mats)]
        return jnp.concatenate(rows, axis=-2)

    w1bd = block_diag([phi_k_w1[0], phi_k_w1[0], phi_v_w1[0], phi_v_w1[0]]).reshape(CMP_BLOCK * 2 * LANES, 2 * LANES)
    w2bd = block_diag([phi_k_w2[0], phi_k_w2[0], phi_v_w2[0], phi_v_w2[0]])
    lb = jnp.cumsum(jax.nn.softmax(hg_lb_logits.astype(f32), axis=0), axis=0)[0][None, :]
    gn = hg_norm_g[0][None, :]
    ffn_w = (w_nsa_out[0].astype(bf16), w_hg_out[0].astype(bf16), w_o[0].astype(bf16), norm2_g[0][None, :],
             w_gate[0].astype(bf16), w_up[0].astype(bf16), conv_w[0], conv_b[0][None, :], w_down[0].astype(bf16))

    idx_c, idx_t = _prompt_bucket_tables(L)
    tc = _bias_tables(rel_table, jnp.asarray(idx_c)).reshape(L // SEL_BLOCK, NSA_KV_HEADS, NSA_GROUP * SEL_BLOCK, -1)
    tt = _bias_tables(rel_table, jnp.asarray(idx_t)).reshape(5, NSA_KV_HEADS, NSA_GROUP * SEL_BLOCK, LANES)
    bias_dec = _bias_tables(rel_table, jnp.asarray(_decode_bucket_table(past, n_win))[None]).reshape(NSA_HEADS, -1)
    ex_prompt = jnp.asarray(_expand_np(L // SEL_BLOCK, L), dtype=bf16)
    ex_sample = jnp.asarray(_expand_np(past // SEL_BLOCK, past), dtype=bf16)

    xp = x_prompt.reshape(Bp * L, D_MODEL)
    qn, kv_t, win32, katt, h4, mgs, gates = _in_proj(xp, g1, w_prompt, gq_prompt, gk[1], gk[2],
                                                     qw=NSA_WIDTH, q_seg=HEAD_DIM, tm=256, seq_len=L)
    cmp = _compress_prompt(kv_t, w1bd, w2bd, gk[0])
    o_nsa = _nsa_prompt(qn, gates, katt, cmp, tc, tt, ex_prompt, Bp, L)
    o_hg, s_prompt = _hgrn_prompt(h4, lb, gn, Bp, L)
    tm_ffn = 512
    yp, gate_tails = _ffn(xp, o_nsa, o_hg.reshape(Bp * L, HG_WIDTH), mgs, *ffn_w, tm=tm_ffn, tf=D_FF // 2, seq_len=L)
    conv_p = gate_tails.reshape(Bp, L // tm_ffn, 2, D_FF)[:, -1]
    n_keep_win = min(WINDOW, L)
    win_p = win32.reshape(Bp, L, 2, NSA_KV_HEADS, HEAD_DIM)[:, L - n_keep_win:]

    xs = x_sample.reshape(Bs, D_MODEL)
    q_bd, kv32_s, win32_s, _, h4_s, mgs_s, gates_s = _in_proj(xs, g1, w_sample, gq_sample, gk[1], gk[2],
                                                              qw=NSA_HEADS * LANES, q_seg=LANES, tm=Bs)
    cache_t = jnp.swapaxes(cache_kv[0].reshape(cache_kv.shape[1], PAGE_SIZE, 4 * LANES), 1, 2)
    win_t = jnp.swapaxes(state_kv_win[0].reshape(Bs, n_win, 2 * LANES), 1, 2)
    o_nsa_s, win_out_t = _nsa_decode(cache_t, page_table, q_bd, kv32_s, win32_s, gates_s, win_t, w1bd, w2bd, gk[0],
                                     bias_dec, ex_sample)
    win_s = jnp.swapaxes(win_out_t, 1, 2)
    kv32 = jnp.swapaxes(kv_t, 1, 2)
    o_hg_s, s_sample = _hgrn_decode(h4_s, lb, gn, state_hgrn[0])
    ys, gate_s = _ffn(xs, o_nsa_s.reshape(Bs, NSA_WIDTH), o_hg_s.reshape(Bs, HG_WIDTH), mgs_s, *ffn_w,
                      tm=Bs, tf=256, conv_state=state_conv[0])
    conv_s = jnp.stack([state_conv[0][:, 1], gate_s], axis=1)

    kvh = (NSA_KV_HEADS, HEAD_DIM)
    return (yp.reshape(Bp, L, D_MODEL), ys.reshape(Bs, 1, D_MODEL),
            kv32.reshape((1, Bp, L, 4) + kvh), kv32_s.reshape((1, Bs, 1, 4) + kvh),
            win_p[None], win_s.reshape((1, Bs, n_win, 2) + kvh),
            s_prompt[None], s_sample[None], conv_p[None], conv_s[None])
```

```python
import functools
import math

import numpy as np
import jax
import jax.numpy as jnp
from jax import lax
from jax.experimental import pallas as pl
from jax.experimental.pallas import tpu as pltpu

f32, bf16, i32 = jnp.float32, jnp.bfloat16, jnp.int32

D_MODEL = 1024
PAGE_SIZE = 128
NSA_HEADS, NSA_KV_HEADS, NSA_GROUP, HEAD_DIM = 8, 2, 4, 64
CMP_BLOCK, SEL_BLOCK, TOP_N, WINDOW = 32, 64, 16, 512
SCALE = HEAD_DIM ** -0.5
NSA_WIDTH = NSA_HEADS * HEAD_DIM
KV_WIDTH = NSA_KV_HEADS * HEAD_DIM
HG_HEADS, HG_DIM = 4, 128
HG_WIDTH = HG_HEADS * HG_DIM
HG_SCALE = HG_DIM ** -0.5
HG_BLOCK = 16
FFN_SLICE = 768
REL_BUCKETS, REL_MAX_DIST = 32, 128
D_FF = 2816
EPS = 1e-6
SPLITS = (NSA_WIDTH, 6 * KV_WIDTH, 3 * NSA_HEADS, HG_WIDTH, HG_WIDTH, HG_WIDTH, HG_WIDTH, 2 * D_MODEL)
LANES = 128
NEG = -1e30
FORCED_SCORE, INVALID_SCORE = 8.0, -1.0
VMEM_LIMIT = 56 * 1024 * 1024

NT = (((1,), (1,)), ((), ()))
TN = (((0,), (0,)), ((), ()))


def _params(sem):
    return pltpu.CompilerParams(dimension_semantics=sem, vmem_limit_bytes=VMEM_LIMIT)


def _bucket_np(dist):
    n = np.maximum(dist, 0)
    exact = REL_BUCKETS // 2
    ratio = np.log(np.maximum(n, 1).astype(np.float32) / np.float32(exact)) / np.float32(math.log(REL_MAX_DIST / exact))
    big = exact + (ratio.astype(np.float32) * np.float32(REL_BUCKETS - exact)).astype(np.int32)
    return np.where(n < exact, n, np.minimum(big, REL_BUCKETS - 1)).astype(np.int32)


def _cmp_perm(n_cmp):
    half = n_cmp // 2
    c = np.arange(n_cmp)
    return np.where(c < half, 2 * c, 2 * (c - half) + 1)


def _prompt_bucket_tables(L):
    nqb = L // SEL_BLOCK
    n_cmp = L // CMP_BLOCK
    qi = np.arange(SEL_BLOCK)
    c_end = (_cmp_perm(n_cmp) + 1) * CMP_BLOCK - 1
    idx_c = np.stack([_bucket_np((i * SEL_BLOCK + qi)[:, None] - c_end[None, :]) for i in range(nqb)])
    kj = np.arange(SEL_BLOCK)
    def tile(delta):
        return _bucket_np(delta * SEL_BLOCK + qi[:, None] - kj[None, :])
    idx_t = np.stack([np.concatenate([tile(e), tile(e - 1)], axis=1) for e in range(5)])
    return idx_c.astype(np.int32), idx_t.astype(np.int32)


def _decode_bucket_table(past_len, n_win):
    n_cmp = past_len // CMP_BLOCK
    c_end = (_cmp_perm(n_cmp) + 1) * CMP_BLOCK - 1
    d_c = np.zeros((LANES,), np.int64)
    d_c[:n_cmp] = past_len - c_end
    d_s = past_len - np.arange(past_len)
    d_w = n_win - np.arange(n_win)
    d_new = np.zeros((LANES,), np.int64)
    return _bucket_np(np.concatenate([d_s, d_w, d_c, d_new]))[None, :].astype(np.int32)


def _expand_np(n_blocks, n_keys):
    e = (np.arange(n_keys)[None, :] // SEL_BLOCK == np.arange(n_blocks)[:, None])
    return e.astype(np.float32)


def _seg_rmsnorm(x, gain, seg):
    outs = []
    for t in range(x.shape[1] // LANES):
        xt = x[:, t * LANES:(t + 1) * LANES]
        sq = xt * xt
        if seg == LANES:
            r = lax.rsqrt(jnp.sum(sq, axis=-1, keepdims=True) * (1.0 / HEAD_DIM) + EPS)
        else:
            lo = lax.broadcasted_iota(i32, xt.shape, 1) < HEAD_DIM
            s_lo = jnp.sum(jnp.where(lo, sq, 0.0), axis=-1, keepdims=True)
            s_hi = jnp.sum(jnp.where(lo, 0.0, sq), axis=-1, keepdims=True)
            r = jnp.where(lo, lax.rsqrt(s_lo * (1.0 / HEAD_DIM) + EPS), lax.rsqrt(s_hi * (1.0 / HEAD_DIM) + EPS))
        outs.append(xt * r)
    y = outs[0] if len(outs) == 1 else jnp.concatenate(outs, axis=1)
    return y * gain


def _rmsnorm_rows(x, gain):
    return x * lax.rsqrt(jnp.mean(x * x, axis=-1, keepdims=True) + EPS) * gain


def _softmax_parts(s, mask):
    s = jnp.where(mask, s, NEG)
    m = jnp.max(s, axis=-1, keepdims=True)
    e = jnp.where(mask, jnp.exp(s - m), 0.0)
    return e, jnp.sum(e, axis=-1, keepdims=True)


def _bias_table_kernel(tab_ref, idx_ref, out_ref):
    idx = idx_ref[...]
    for h in range(NSA_HEADS):
        acc = jnp.zeros(idx.shape, f32)
        for b in range(REL_BUCKETS):
            acc = jnp.where(idx == b, tab_ref[b, h], acc)
        out_ref[h] = acc


def _bias_tables(rel_table, idx):
    n, rows, w = idx.shape
    return pl.pallas_call(
        _bias_table_kernel,
        out_shape=jax.ShapeDtypeStruct((n, NSA_HEADS, rows, w), f32),
        grid=(n,),
        in_specs=[pl.BlockSpec(memory_space=pltpu.SMEM),
                  pl.BlockSpec((None, rows, w), lambda i: (i, 0, 0))],
        out_specs=pl.BlockSpec((None, NSA_HEADS, rows, w), lambda i: (i, 0, 0, 0)),
        compiler_params=_params(("arbitrary",)),
        name="bias_tables",
    )(rel_table, idx)


def _in_proj_kernel(x_ref, g1_ref, w_ref, gq_ref, gk1_ref, gk2_ref,
                    q_ref, kv_ref, win_ref, katt_ref, h4_ref, mg_ref, gl_ref, *, qw, q_seg, kv_feature_major):
    x = x_ref[...]
    xn = _rmsnorm_rows(x, g1_ref[...]).astype(bf16)

    def proj(c0, width):
        return jnp.dot(xn, w_ref[:, c0:c0 + width], preferred_element_type=f32)

    q_ref[...] = (_seg_rmsnorm(proj(0, qw), gq_ref[...], q_seg) * SCALE).astype(bf16)
    c = qw
    kv = proj(c, 6 * KV_WIDTH)
    c += 6 * KV_WIDTH
    k_sel = _seg_rmsnorm(kv[:, 2 * KV_WIDTH:3 * KV_WIDTH], gk1_ref[...], HEAD_DIM)
    k_win = _seg_rmsnorm(kv[:, 4 * KV_WIDTH:5 * KV_WIDTH], gk2_ref[...], HEAD_DIM)
    v_sel = kv[:, 3 * KV_WIDTH:4 * KV_WIDTH]
    v_win = kv[:, 5 * KV_WIDTH:6 * KV_WIDTH]
    if kv_feature_major:
        kv_ref[0:2 * KV_WIDTH, :] = kv[:, 0:2 * KV_WIDTH].T
        kv_ref[2 * KV_WIDTH:3 * KV_WIDTH, :] = k_sel.T
        kv_ref[3 * KV_WIDTH:4 * KV_WIDTH, :] = v_sel.T
    else:
        kv_ref[:, 0:2 * KV_WIDTH] = kv[:, 0:2 * KV_WIDTH]
        kv_ref[:, 2 * KV_WIDTH:3 * KV_WIDTH] = k_sel
        kv_ref[:, 3 * KV_WIDTH:4 * KV_WIDTH] = v_sel
    win_ref[:, 0:KV_WIDTH] = k_win
    win_ref[:, KV_WIDTH:2 * KV_WIDTH] = v_win
    katt_ref[:, 0:KV_WIDTH] = k_sel.astype(bf16)
    katt_ref[:, KV_WIDTH:2 * KV_WIDTH] = v_sel.astype(bf16)
    katt_ref[:, 2 * KV_WIDTH:3 * KV_WIDTH] = k_win.astype(bf16)
    katt_ref[:, 3 * KV_WIDTH:4 * KV_WIDTH] = v_win.astype(bf16)
    h4_ref[...] = proj(c, 4 * HG_WIDTH)
    c += 4 * HG_WIDTH
    mg_ref[...] = jax.nn.sigmoid(proj(c, 2 * D_MODEL))
    c += 2 * D_MODEL
    gl_ref[...] = jax.nn.sigmoid(proj(c, LANES))


def _in_proj(x2d, g1, w, gq, gk1, gk2, *, qw, q_seg, tm, seq_len=None):
    n = x2d.shape[0]
    cw = w.shape[1]
    row = lambda width: pl.BlockSpec((tm, width), lambda i: (i, 0))
    full = lambda a: pl.BlockSpec(a.shape, lambda i: (0,) * a.ndim)
    widths = (qw, 4 * KV_WIDTH, 2 * KV_WIDTH, 4 * KV_WIDTH, 4 * HG_WIDTH, 2 * D_MODEL, LANES)
    dtypes = (bf16, f32, f32, bf16, f32, f32, f32)
    out_shape = [jax.ShapeDtypeStruct((n, wd), dt) for wd, dt in zip(widths, dtypes)]
    out_specs = [row(wd) for wd in widths]
    if seq_len is not None:
        tiles = seq_len // tm
        out_shape[1] = jax.ShapeDtypeStruct((n // seq_len, 4 * KV_WIDTH, seq_len), f32)
        out_specs[1] = pl.BlockSpec((None, 4 * KV_WIDTH, tm), lambda i: (i // tiles, 0, i % tiles))
    return pl.pallas_call(
        functools.partial(_in_proj_kernel, qw=qw, q_seg=q_seg, kv_feature_major=seq_len is not None),
        out_shape=tuple(out_shape),
        grid=(n // tm,),
        in_specs=[row(D_MODEL), full(g1),
                  pl.BlockSpec((D_MODEL, cw), lambda i: (0, 0), pipeline_mode=pl.Buffered(1)),
                  full(gq), full(gk1), full(gk2)],
        out_specs=tuple(out_specs),
        compiler_params=_params(("arbitrary",)),
        name="in_proj",
    )(x2d, g1, w, gq, gk1, gk2)


CMP_SLOT = CMP_BLOCK + 1


def _cmp_staging_rows(n_cmp):
    return -(-n_cmp * CMP_SLOT // 8) * 8


def _stage_cmp_chunk(src_ref, chunk, kc_t, vc_t, n_cmp):
    per_chunk = LANES // CMP_BLOCK
    for s, t in enumerate((kc_t.T, vc_t.T)):
        for nl in range(per_chunk):
            n = chunk * per_chunk + nl
            slot = n // 2 + (n % 2) * (n_cmp // 2)
            src_ref[s, slot * CMP_SLOT:slot * CMP_SLOT + CMP_BLOCK, :] = t[nl * CMP_BLOCK:(nl + 1) * CMP_BLOCK]


def _compress(src_refs, w1_ref, w2_ref, gk0, n_cmp):
    acc = jnp.zeros((len(src_refs) * n_cmp, 2 * LANES), f32)
    per_dot = 8
    for j0 in range(0, CMP_BLOCK, per_dot):
        xs = jnp.concatenate(
            [jnp.concatenate([src[s, pl.ds(j, n_cmp, stride=CMP_SLOT), :].astype(bf16)
                              for j in range(j0, j0 + per_dot) for s in range(2)], axis=1)
             for src in src_refs], axis=0)
        w = w1_ref[j0 * 2 * LANES:(j0 + per_dot) * 2 * LANES, :]
        acc = acc + jnp.dot(xs, w, preferred_element_type=f32)
    hmid = jax.nn.gelu(acc).astype(bf16)
    y = jnp.dot(hmid, w2_ref[...], preferred_element_type=f32)
    ck = _seg_rmsnorm(y[:, 0:LANES], gk0, HEAD_DIM)
    return jnp.concatenate([ck, y[:, LANES:2 * LANES]], axis=1)


def _compress_prompt_kernel(kv_ref, w1_ref, w2_ref, gk0_ref, out_ref, src_ref, *, n_cmp):
    for c in range(kv_ref.shape[1] // LANES):
        cols = slice(c * LANES, (c + 1) * LANES)
        _stage_cmp_chunk(src_ref, c, kv_ref[0:LANES, cols], kv_ref[LANES:2 * LANES, cols], n_cmp)
    out_ref[...] = _compress([src_ref], w1_ref, w2_ref, gk0_ref[...], n_cmp).astype(bf16)


def _compress_prompt(kv_t, w1bd, w2bd, gk0):
    B, _, L = kv_t.shape
    n_cmp = L // CMP_BLOCK
    return pl.pallas_call(
        functools.partial(_compress_prompt_kernel, n_cmp=n_cmp),
        out_shape=jax.ShapeDtypeStruct((B, n_cmp, 2 * LANES), bf16),
        grid=(B,),
        in_specs=[pl.BlockSpec((None, 2 * LANES, L), lambda b: (b, 0, 0)),
                  pl.BlockSpec(w1bd.shape, lambda b: (0, 0)),
                  pl.BlockSpec(w2bd.shape, lambda b: (0, 0)),
                  pl.BlockSpec(gk0.shape, lambda b: (0, 0))],
        out_specs=pl.BlockSpec((None, n_cmp, 2 * LANES), lambda b: (b, 0, 0)),
        scratch_shapes=[pltpu.VMEM((2, _cmp_staging_rows(n_cmp), LANES), f32)],
        compiler_params=_params(("arbitrary",)),
        name="compress_prompt",
    )(kv_t, w1bd, w2bd, gk0)


def _select_blocks_t(score, valid, n_keep):
    nb = score.shape[0]
    jj = lax.broadcasted_iota(i32, score.shape, 0)
    cnt = jnp.zeros(score.shape, f32)
    for k in range(nb):
        sk = score[k:k + 1, :]
        better = (sk > score) | ((sk == score) & (jj > k))
        cnt = cnt + jnp.where(better, 1.0, 0.0)
    return (cnt < n_keep) & valid


def _sel_tile_classes(n_tiles):
    return sorted({-(-n_tiles * c // 4) for c in range(1, 5)})


def _nsa_prompt_kernel(q_ref, gate_ref, katt_ref, cmp_ref, tc_ref, tt_ref, ex_ref, o_ref,
                       selx_ref, s_ref, os_ref, part_ref, gsel_ref, *, L):
    i = pl.program_id(1)
    n_cmp = L // CMP_BLOCK
    n_sel = L // SEL_BLOCK
    n_tiles = L // LANES
    qb = SEL_BLOCK
    rows = NSA_GROUP * qb
    units = [(sq, g) for sq in range(q_ref.shape[0]) for g in range(NSA_KV_HEADS)]
    nu = range(len(units))
    row64 = lax.broadcasted_iota(i32, (qb, LANES), 0)
    lane = lax.broadcasted_iota(i32, (qb, LANES), 1)
    qpos = i * qb + row64

    qs = [jnp.concatenate(
        [q_ref[sq, :, (NSA_GROUP * g + r) * HEAD_DIM:(NSA_GROUP * g + r + 1) * HEAD_DIM] for r in range(NSA_GROUP)],
        axis=0) for sq, g in units]

    def attend(k_off, v_off, tile0, n_t, mask_fn):
        chunk = 4

        def key_rows(t0, count, off, u):
            sq, g = units[u]
            r0 = pl.multiple_of((tile0 + t0) * LANES, LANES)
            return katt_ref[sq, pl.ds(r0, count * LANES), off + g * HEAD_DIM:off + (g + 1) * HEAD_DIM]

        def scores(t0, count, mruns):
            dots = [lax.dot_general(qs[u], key_rows(t0, count, k_off, u), NT, preferred_element_type=f32)
                    for u in nu]
            out = list(mruns)
            for j in range(count):
                ta = tile0 + t0 + j
                bias_idx = jnp.clip(i - 2 * ta, 0, 4)
                kpos = ta * LANES + lane
                for u in nu:
                    s = dots[u][:, j * LANES:(j + 1) * LANES] + tt_ref[bias_idx, units[u][1]]
                    msk = mask_fn(u, ta, kpos)
                    s = jnp.where(msk[None], s.reshape(NSA_GROUP, qb, LANES), NEG).reshape(rows, LANES)
                    s_ref[u, t0 + j] = s
                    out[u] = jnp.maximum(out[u], s)
            return tuple(out)

        def values(t0, count, carry, ms):
            ls, accs = list(carry[0]), list(carry[1])
            ps = []
            for u in nu:
                pj = [jnp.exp(s_ref[u, t0 + j] - ms[u]) for j in range(count)]
                for p in pj:
                    ls[u] = ls[u] + p
                ps.append(jnp.concatenate([p.astype(bf16) for p in pj], axis=1))
            for u in nu:
                accs[u] = accs[u] + jnp.dot(ps[u], key_rows(t0, count, v_off, u), preferred_element_type=f32)
            return tuple(ls), tuple(accs)

        def over_chunks(fn, carry):
            n_full = n_t // chunk
            if n_full > 2:
                carry = lax.fori_loop(0, n_full, lambda c, cr: fn(c * chunk, chunk, cr), carry)
            else:
                for c in range(n_full):
                    carry = fn(c * chunk, chunk, carry)
            if n_t % chunk:
                carry = fn(n_full * chunk, n_t % chunk, carry)
            return carry

        mruns = over_chunks(scores, tuple(jnp.full((rows, LANES), NEG, f32) for _ in nu))
        ms = [jnp.max(m, axis=-1, keepdims=True) for m in mruns]
        init = (tuple(jnp.zeros((rows, LANES), f32) for _ in nu),
                tuple(jnp.zeros((rows, HEAD_DIM), f32) for _ in nu))
        ls, accs = over_chunks(lambda t0, count, cr: values(t0, count, cr, ms), init)
        return [accs[u] / jnp.sum(ls[u], axis=-1, keepdims=True) for u in nu]

    n_wt = min(WINDOW // LANES + 1, n_tiles)
    w0 = jnp.clip((i - WINDOW // qb) // 2, 0, n_tiles - n_wt)
    o_w = attend(2 * KV_WIDTH, 3 * KV_WIDTH, w0, n_wt,
                 lambda u, ta, kpos: (kpos <= qpos) & (qpos - kpos <= WINDOW))

    o_c, imp_t = [], []
    col = lax.broadcasted_iota(i32, (rows, n_cmp), 1)
    rq = lax.broadcasted_iota(i32, (rows, n_cmp), 0) & (qb - 1)
    blk = jnp.where(col < n_cmp // 2, 2 * col, 2 * col - (n_cmp - 1))
    vis = (blk + 1) * CMP_BLOCK - 1 <= i * qb + rq
    for u, (sq, g) in enumerate(units):
        ck = cmp_ref[sq, :, g * HEAD_DIM:(g + 1) * HEAD_DIM]
        cv = cmp_ref[sq, :, LANES + g * HEAD_DIM:LANES + (g + 1) * HEAD_DIM]
        s = lax.dot_general(qs[u], ck, NT, preferred_element_type=f32) + tc_ref[g]
        e, den = _softmax_parts(s, vis)
        p = e / jnp.where(den > 0, den, 1.0)
        o_c.append(jnp.dot(p.astype(bf16), cv, preferred_element_type=f32))
        ps_t = (p[0:qb] + p[qb:2 * qb] + p[2 * qb:3 * qb] + p[3 * qb:4 * qb]).T
        imp_t.append(ps_t[0:n_sel] + ps_t[n_sel:2 * n_sel])

    imp = jnp.concatenate(imp_t, axis=1)
    jj = lax.broadcasted_iota(i32, imp.shape, 0)
    valid = jj <= i
    forced = valid & ((jj == 0) | (jj >= i - 1))
    score = jnp.where(forced, FORCED_SCORE, jnp.where(valid, imp, INVALID_SCORE))
    sel = _select_blocks_t(score, valid, min(TOP_N, n_sel))
    selx = lax.dot_general(jnp.where(sel, 1.0, 0.0).astype(bf16), ex_ref[...], TN, preferred_element_type=f32)
    for u in nu:
        for t in range(n_tiles):
            selx_ref[u, t] = selx[u * qb:(u + 1) * qb, t * LANES:(t + 1) * LANES]

    for u, (sq, g) in enumerate(units):
        gates = gate_ref[sq]
        for r in range(NSA_GROUP):
            h = NSA_GROUP * g + r
            hs = slice(h * HEAD_DIM, (h + 1) * HEAD_DIM)
            sl = slice(r * qb, (r + 1) * qb)
            part_ref[sq, :, hs] = gates[:, 3 * h:3 * h + 1] * o_c[u][sl] + gates[:, 3 * h + 2:3 * h + 3] * o_w[u][sl]
            gsel_ref[sq, :, hs] = jnp.broadcast_to(gates[:, 3 * h + 1:3 * h + 2], (qb, HEAD_DIM))

    need = i // 2 + 1
    prev = 0
    for n_t in _sel_tile_classes(n_tiles):
        @pl.when((need > prev) & (need <= n_t))
        def _(n_t=n_t):
            o_s = attend(0, KV_WIDTH, 0, n_t, lambda u, ta, kpos: (selx_ref[u, ta] > 0.5) & (kpos <= qpos))
            for u in nu:
                os_ref[u] = o_s[u]
        prev = n_t

    for u, (sq, g) in enumerate(units):
        o_s = os_ref[u]
        for r in range(NSA_GROUP):
            hs = slice((NSA_GROUP * g + r) * HEAD_DIM, (NSA_GROUP * g + r + 1) * HEAD_DIM)
            o = part_ref[sq, :, hs] + gsel_ref[sq, :, hs] * o_s[r * qb:(r + 1) * qb]
            o_ref[sq, :, hs] = o.astype(bf16)


def _nsa_prompt(qn, gates, katt, cmp, tc, tt, expand, B, L, nseq=2):
    nqb = L // SEL_BLOCK
    qb = SEL_BLOCK
    n_tiles = L // LANES
    rows = NSA_GROUP * qb
    nseq = math.gcd(nseq, B)
    n_units = nseq * NSA_KV_HEADS
    seq3 = lambda a: a.reshape(B, L, a.shape[-1])
    return pl.pallas_call(
        functools.partial(_nsa_prompt_kernel, L=L),
        out_shape=jax.ShapeDtypeStruct((B, L, NSA_WIDTH), bf16),
        grid=(B // nseq, nqb),
        in_specs=[pl.BlockSpec((nseq, qb, NSA_WIDTH), lambda b, i: (b, i, 0)),
                  pl.BlockSpec((nseq, qb, LANES), lambda b, i: (b, i, 0)),
                  pl.BlockSpec((nseq, L, 4 * KV_WIDTH), lambda b, i: (b, 0, 0)),
                  pl.BlockSpec((nseq, L // CMP_BLOCK, 2 * LANES), lambda b, i: (b, 0, 0)),
                  pl.BlockSpec((None,) + tc.shape[1:], lambda b, i: (i, 0, 0, 0)),
                  pl.BlockSpec(tt.shape, lambda b, i: (0, 0, 0, 0)),
                  pl.BlockSpec(expand.shape, lambda b, i: (0, 0))],
        out_specs=pl.BlockSpec((nseq, qb, NSA_WIDTH), lambda b, i: (b, i, 0)),
        scratch_shapes=[pltpu.VMEM((n_units, n_tiles, qb, LANES), f32),
                        pltpu.VMEM((n_units, n_tiles, rows, LANES), f32),
                        pltpu.VMEM((n_units, rows, HEAD_DIM), f32),
                        pltpu.VMEM((nseq, qb, NSA_WIDTH), f32), pltpu.VMEM((nseq, qb, NSA_WIDTH), f32)],
        compiler_params=_params(("arbitrary", "arbitrary")),
        name="nsa_prompt",
    )(seq3(qn), seq3(gates), seq3(katt), cmp, tc, tt, expand).reshape(B * L, NSA_WIDTH)


def _hgrn_gates(hf, lb):
    f = lb + (1.0 - lb) * jax.nn.sigmoid(hf)
    return f, 1.0 - f, jnp.log(f)


def _hgrn_out(o, gn, hgate):
    return _rmsnorm_rows(o, gn) * jax.nn.silu(hgate)


def _hgrn_prompt_kernel(h4_ref, lb_ref, gn_ref, o_ref, s_out_ref, st_ref, *, tc):
    c = pl.program_id(1)
    nb = HG_BLOCK
    nseq = h4_ref.shape[0]

    @pl.when(c == 0)
    def _():
        st_ref[...] = jnp.zeros_like(st_ref)

    t_row = lax.broadcasted_iota(i32, (nb, HG_DIM), 0)
    t_col = lax.broadcasted_iota(i32, (nb, 1), 0)
    gn = gn_ref[...]
    chains = [(sq, h) for sq in range(nseq) for h in range(HG_HEADS)]

    def cumsum_rows(x):
        shift = 1
        while shift < nb:
            x = x + jnp.where(t_row >= shift, pltpu.roll(x, shift, axis=0), 0.0)
            shift *= 2
        return x

    def block(bi, _):
        r0 = pl.multiple_of(bi * nb, nb)

        def sl(sq, h, part):
            return h4_ref[sq, pl.ds(r0, nb), part * HG_WIDTH + h * HG_DIM:part * HG_WIDTH + (h + 1) * HG_DIM]

        qs, ks, vs, bs, sts, os_ = [], [], [], [], [], []
        for sq, h in chains:
            _, k, glog = _hgrn_gates(sl(sq, h, 1), lb_ref[:, h * HG_DIM:(h + 1) * HG_DIM])
            qs.append(sl(sq, h, 0) * HG_SCALE)
            ks.append(k)
            vs.append(sl(sq, h, 2))
            bs.append(cumsum_rows(glog))
        for c_, (sq, h) in enumerate(chains):
            sts.append(st_ref[sq, h])
            os_.append(lax.dot_general((qs[c_] * jnp.exp(bs[c_])).astype(bf16), sts[c_].astype(bf16), NT,
                                       preferred_element_type=f32))
        for c_ in range(len(chains)):
            q, k, v, b, o = qs[c_], ks[c_], vs[c_], bs[c_], os_[c_]
            parts = [o[r0_:r0_ + 8] for r0_ in range(0, nb, 8)]
            for s in range(nb):
                for gi in range(s // 8, nb // 8):
                    rs = slice(gi * 8, gi * 8 + 8)
                    w = q[rs] * k[s:s + 1] * jnp.exp(b[rs] - b[s:s + 1])
                    a = jnp.sum(w, axis=-1, keepdims=True)
                    if gi == s // 8:
                        a = jnp.where(t_col[rs] >= s, a, 0.0)
                    parts[gi] = parts[gi] + a * v[s:s + 1]
            os_[c_] = jnp.concatenate(parts, axis=0)
        for c_, (sq, h) in enumerate(chains):
            b_last = bs[c_][nb - 1:nb]
            kt = ks[c_] * jnp.exp(b_last - bs[c_])
            upd = lax.dot_general(vs[c_].astype(bf16), kt.astype(bf16), TN, preferred_element_type=f32)
            st_ref[sq, h] = jnp.exp(b_last) * sts[c_] + upd
            o_ref[sq, pl.ds(r0, nb), h * HG_DIM:(h + 1) * HG_DIM] = _hgrn_out(os_[c_], gn, sl(sq, h, 3)).astype(bf16)
        return 0

    lax.fori_loop(0, tc // nb, block, 0)

    @pl.when(c == pl.num_programs(1) - 1)
    def _():
        for sq in range(nseq):
            for h in range(HG_HEADS):
                s_out_ref[sq, h] = st_ref[sq, h].T


def _hgrn_prompt(h4, lb, gn, B, L, tc=256, nseq=4):
    nc = L // tc
    nseq = math.gcd(nseq, B)
    return pl.pallas_call(
        functools.partial(_hgrn_prompt_kernel, tc=tc),
        out_shape=(jax.ShapeDtypeStruct((B, L, HG_WIDTH), bf16),
                   jax.ShapeDtypeStruct((B, HG_HEADS, HG_DIM, HG_DIM), f32)),
        grid=(B // nseq, nc),
        in_specs=[pl.BlockSpec((nseq, tc, 4 * HG_WIDTH), lambda b, c: (b, c, 0)),
                  pl.BlockSpec(lb.shape, lambda b, c: (0, 0)),
                  pl.BlockSpec(gn.shape, lambda b, c: (0, 0))],
        out_specs=(pl.BlockSpec((nseq, tc, HG_WIDTH), lambda b, c: (b, c, 0)),
                   pl.BlockSpec((nseq, HG_HEADS, HG_DIM, HG_DIM), lambda b, c: (b, 0, 0, 0))),
        scratch_shapes=[pltpu.VMEM((nseq, HG_HEADS, HG_DIM, HG_DIM), f32)],
        compiler_params=_params(("arbitrary", "arbitrary")),
        name="hgrn_prompt",
    )(h4.reshape(B, L, 4 * HG_WIDTH), lb, gn)


def _ffn_kernel(*refs, decode, tiles_per_seq):
    if decode:
        (x_ref, on_ref, oh_ref, mg_ref, wn_ref, wh_ref, wo_ref, g2_ref, wg_ref, wu_ref, cw_ref, cb_ref, wd_ref,
         cs0_ref, cs1_ref, y_ref, gate_out_ref, x1_ref, xn2_ref, acc_ref) = refs
    else:
        (x_ref, on_ref, oh_ref, mg_ref, wn_ref, wh_ref, wo_ref, g2_ref, wg_ref, wu_ref, cw_ref, cb_ref, wd_ref,
         y_ref, gate_out_ref, x1_ref, xn2_ref, acc_ref, carry_ref) = refs
    i = pl.program_id(0)
    fi = pl.program_id(1)
    tm = x_ref.shape[0]

    @pl.when(fi == 0)
    def _():
        y_a = jnp.dot(on_ref[...], wn_ref[...], preferred_element_type=f32)
        y_b = jnp.dot(oh_ref[...], wh_ref[...], preferred_element_type=f32)
        mg = mg_ref[...]
        merged = mg[:, 0:D_MODEL] * y_a + mg[:, D_MODEL:2 * D_MODEL] * y_b
        x1 = x_ref[...] + jnp.dot(merged.astype(bf16), wo_ref[...], preferred_element_type=f32)
        x1_ref[...] = x1
        xn2_ref[...] = _rmsnorm_rows(x1, g2_ref[...]).astype(bf16)
        acc_ref[...] = jnp.zeros_like(acc_ref)

    xn2 = xn2_ref[...]
    tf = wg_ref.shape[1]
    if not decode:
        @pl.when(i % tiles_per_seq == 0)
        def _():
            carry_ref[fi] = jnp.zeros(carry_ref.shape[1:], f32)

    acc = acc_ref[...]
    for a in range(0, tf, FFN_SLICE):
        cs = slice(a, min(a + FFN_SLICE, tf))
        gate = jnp.dot(xn2, wg_ref[:, cs], preferred_element_type=f32)
        up = jnp.dot(xn2, wu_ref[:, cs], preferred_element_type=f32)
        cw = cw_ref[:, cs]
        if decode:
            prev2, prev1 = cs0_ref[:, cs], cs1_ref[:, cs]
            gate_out_ref[:, cs] = gate
        else:
            carry = carry_ref[fi, :, cs]
            rid = lax.broadcasted_iota(i32, gate.shape, 0)
            prev1 = jnp.where(rid == 0, carry[7:8], pltpu.roll(gate, 1, axis=0))
            prev2 = jnp.where(rid == 0, carry[6:7], jnp.where(rid == 1, carry[7:8], pltpu.roll(gate, 2, axis=0)))
            carry_ref[fi, 6:8, cs] = gate[tm - 2:tm]
            gate_out_ref[:, cs] = gate[tm - 2:tm]
        conv = cb_ref[:, cs] + cw[0:1] * prev2 + cw[1:2] * prev1 + cw[2:3] * gate
        hmid = (jax.nn.silu(conv) * up).astype(bf16)
        acc = acc + jnp.dot(hmid, wd_ref[cs, :], preferred_element_type=f32)
    acc_ref[...] = acc

    @pl.when(fi == pl.num_programs(1) - 1)
    def _():
        y_ref[...] = x1_ref[...] + acc_ref[...]


def _ffn(x2d, o_nsa, o_hg, mgs, wn, wh, wo, g2, wg, wu, cw, cb, wd, *, tm, tf, seq_len=None, conv_state=None):
    n = x2d.shape[0]
    decode = conv_state is not None
    nf = D_FF // tf
    row = lambda width: pl.BlockSpec((tm, width), lambda i, f: (i, 0))
    full = lambda a: pl.BlockSpec(a.shape, lambda i, f: (0,) * a.ndim)
    in_specs = [row(D_MODEL), row(NSA_WIDTH), row(HG_WIDTH), row(2 * D_MODEL), full(wn), full(wh), full(wo), full(g2),
                pl.BlockSpec((D_MODEL, tf), lambda i, f: (0, f)), pl.BlockSpec((D_MODEL, tf), lambda i, f: (0, f)),
                pl.BlockSpec((3, tf), lambda i, f: (0, f)), pl.BlockSpec((1, tf), lambda i, f: (0, f)),
                pl.BlockSpec((tf, D_MODEL), lambda i, f: (f, 0))]
    args = [x2d, o_nsa, o_hg, mgs, wn, wh, wo, g2, wg, wu, cw, cb, wd]
    scratch = [pltpu.VMEM((tm, D_MODEL), f32), pltpu.VMEM((tm, D_MODEL), bf16), pltpu.VMEM((tm, D_MODEL), f32)]
    if decode:
        in_specs += [pl.BlockSpec((tm, tf), lambda i, f: (i, f))] * 2
        args += [conv_state[:, 0], conv_state[:, 1]]
        gate_shape = jax.ShapeDtypeStruct((n, D_FF), f32)
        gate_spec = pl.BlockSpec((tm, tf), lambda i, f: (i, f))
        tiles_per_seq = 1
    else:
        tiles_per_seq = seq_len // tm
        gate_shape = jax.ShapeDtypeStruct((n // tm, 2, D_FF), f32)
        gate_spec = pl.BlockSpec((None, 2, tf), lambda i, f: (i, 0, f))
        scratch.append(pltpu.VMEM((nf, 8, tf), f32))
    return pl.pallas_call(
        functools.partial(_ffn_kernel, decode=decode, tiles_per_seq=tiles_per_seq),
        out_shape=(jax.ShapeDtypeStruct((n, D_MODEL), f32), gate_shape),
        grid=(n // tm, nf),
        in_specs=in_specs,
        out_specs=(row(D_MODEL), gate_spec),
        scratch_shapes=scratch,
        compiler_params=_params(("arbitrary", "arbitrary")),
        name="ffn_decode" if decode else "ffn_prompt",
    )(*args)


def _nsa_decode_kernel(*refs, n_pages, n_win, nseq):
    page_refs = refs[1:1 + nseq * n_pages]
    (q_ref, kvn_ref, winn_ref, win_all_ref, gate_ref, win_ref, w1_ref, w2_ref, gk0_ref, bias_ref, ex_ref,
     o_ref, win_out_ref, src_ref, ksel_ref, vsel_ref, wcol_ref) = refs[1 + nseq * n_pages:]
    step = pl.program_id(0)
    past = n_pages * PAGE_SIZE
    n_cmp = past // CMP_BLOCK
    n_sel = past // SEL_BLOCK
    heads = NSA_HEADS
    samples = range(nseq)

    @pl.when(step == 0)
    def _():
        for t in range(2):
            wcol_ref[t * LANES:(t + 1) * LANES, :] = win_all_ref[:, t * LANES:(t + 1) * LANES].T

    for sq in samples:
        for k in range(n_pages):
            page = page_refs[sq * n_pages + k]
            cols = slice(k * PAGE_SIZE, (k + 1) * PAGE_SIZE)
            _stage_cmp_chunk(src_ref.at[sq], k, page[0:LANES, :], page[LANES:2 * LANES, :], n_cmp)
            ksel_ref[sq, :, cols] = page[2 * LANES:3 * LANES, :].astype(bf16)
            vsel_ref[sq, :, cols] = page[3 * LANES:4 * LANES, :].astype(bf16)

    hrow = lax.broadcasted_iota(i32, (heads, 1), 0)
    g0 = hrow < NSA_GROUP
    bias = bias_ref[...]
    b_s = bias[:, 0:past]
    b_w = bias[:, past:past + n_win]
    b_c = bias[:, past + n_win:past + n_win + n_cmp]
    b_new = bias[:, past + n_win + LANES:past + n_win + LANES + 1]
    qs = [q_ref[sq] for sq in samples]

    def pick(o2):
        return jnp.where(g0, o2[:, 0:HEAD_DIM], o2[:, HEAD_DIM:2 * HEAD_DIM])

    def with_new(q, k_t, v_t, b_past, mask, k_new, v_new):
        s_past = jnp.dot(q, k_t, preferred_element_type=f32) + b_past
        s_new = jnp.sum(q.astype(f32) * k_new, axis=-1, keepdims=True) + b_new
        sm = s_past if mask is None else jnp.where(mask, s_past, NEG)
        m = jnp.maximum(jnp.max(sm, axis=-1, keepdims=True), s_new)
        e = jnp.exp(sm - m)
        e_new = jnp.exp(s_new - m)
        den = jnp.sum(e, axis=-1, keepdims=True) + e_new
        o2 = lax.dot_general(e.astype(bf16), v_t, NT, preferred_element_type=f32) + e_new * v_new
        return pick(o2 / den)

    o_w = []
    for sq in samples:
        winn = winn_ref[sq]
        o_w.append(with_new(qs[sq], win_ref[sq, 0:LANES, :].astype(bf16), win_ref[sq, LANES:2 * LANES, :].astype(bf16),
                            b_w, None, winn[:, 0:LANES], winn[:, LANES:2 * LANES]))

    cmp = _compress([src_ref.at[sq] for sq in samples], w1_ref, w2_ref, gk0_ref[...], n_cmp)
    o_c, masks = [], []
    cur = past // SEL_BLOCK
    jrow = lax.broadcasted_iota(i32, (NSA_KV_HEADS, n_sel), 1)
    forced = (jrow == 0) | (jrow >= cur - 1)
    n_keep = min(TOP_N, n_sel + 1) - 1
    ri = lax.broadcasted_iota(i32, (n_sel, n_sel), 0)
    ci = lax.broadcasted_iota(i32, (n_sel, n_sel), 1)
    for sq in samples:
        ck = cmp[sq * n_cmp:(sq + 1) * n_cmp, 0:LANES].astype(bf16)
        cv = cmp[sq * n_cmp:(sq + 1) * n_cmp, LANES:2 * LANES].astype(bf16)
        s_c = lax.dot_general(qs[sq], ck, NT, preferred_element_type=f32) + b_c
        e_c, den_c = _softmax_parts(s_c, jnp.full(s_c.shape, True))
        p_c = e_c / jnp.where(den_c > 0, den_c, 1.0)
        o_c.append(pick(jnp.dot(p_c.astype(bf16), cv, preferred_element_type=f32)))

        ps = jnp.concatenate([jnp.sum(p_c[0:NSA_GROUP], axis=0, keepdims=True),
                              jnp.sum(p_c[NSA_GROUP:heads], axis=0, keepdims=True)], axis=0)
        imp = ps[:, 0:n_sel] + ps[:, n_sel:2 * n_sel]
        score_row = jnp.where(forced, FORCED_SCORE, imp)
        sels = []
        for g in range(NSA_KV_HEADS):
            rowb = jnp.broadcast_to(score_row[g:g + 1], (n_sel, n_sel))
            colb = jnp.sum(jnp.where(ri == ci, rowb, 0.0), axis=-1, keepdims=True)
            better = (colb > rowb) | ((colb == rowb) & (ri < ci))
            cnt = jnp.sum(jnp.where(better, 1.0, 0.0), axis=0, keepdims=True)
            sels.append(jnp.where(cnt < n_keep, 1.0, 0.0))
        sel = jnp.concatenate(sels, axis=0).astype(bf16)
        selx = jnp.dot(sel, ex_ref[...], preferred_element_type=f32)
        masks.append(jnp.where(g0, selx[0:1], selx[1:2]) > 0.5)

    lane_b = lax.broadcasted_iota(i32, wcol_ref.shape, 1)
    lane_w = lax.broadcasted_iota(i32, win_ref.shape[1:], 1)
    for sq in samples:
        kvn = kvn_ref[sq]
        o_s = with_new(qs[sq], ksel_ref[sq], vsel_ref[sq], b_s, masks[sq],
                       kvn[:, 2 * LANES:3 * LANES], kvn[:, 3 * LANES:4 * LANES])
        gates = gate_ref[sq]
        o_ref[sq] = (gates[:, 0:1] * o_c[sq] + gates[:, 1:2] * o_s + gates[:, 2:3] * o_w[sq]).astype(bf16)

        new_col = jnp.sum(jnp.where(lane_b == step * nseq + sq, wcol_ref[...], 0.0), axis=-1, keepdims=True)
        win_out_ref[sq] = jnp.where(lane_w == n_win - 1, new_col, pltpu.roll(win_ref[sq], n_win - 1, axis=1))


def _nsa_decode(cache_t, page_table, q_bd, kv32, win32, gates, win_t, w1bd, w2bd, gk0, bias, expand, nseq=2):
    Bs, n_pages = page_table.shape
    n_win = win_t.shape[2]
    past = n_pages * PAGE_SIZE
    n_cmp = past // CMP_BLOCK
    nseq = math.gcd(nseq, Bs)
    per_b = lambda shape: pl.BlockSpec((nseq,) + shape, lambda b, pt: (b,) + (0,) * len(shape))
    const = lambda a: pl.BlockSpec(a.shape, lambda b, pt: (0,) * a.ndim)
    page_spec = lambda sq, k: pl.BlockSpec((None, 4 * LANES, PAGE_SIZE), lambda b, pt: (pt[b * nseq + sq, k], 0, 0))
    grid_spec = pltpu.PrefetchScalarGridSpec(
        num_scalar_prefetch=1,
        grid=(Bs // nseq,),
        in_specs=[page_spec(sq, k) for sq in range(nseq) for k in range(n_pages)] + [
            per_b((NSA_HEADS, LANES)), per_b((1, 4 * LANES)), per_b((1, 2 * LANES)), const(win32),
            per_b((NSA_HEADS, 3)), per_b((2 * LANES, n_win)), const(w1bd), const(w2bd), const(gk0), const(bias),
            const(expand)],
        out_specs=(per_b((NSA_HEADS, HEAD_DIM)), per_b((2 * LANES, n_win))),
        scratch_shapes=[pltpu.VMEM((nseq, 2, _cmp_staging_rows(n_cmp), LANES), f32),
                        pltpu.VMEM((nseq, LANES, past), bf16), pltpu.VMEM((nseq, LANES, past), bf16),
                        pltpu.VMEM((2 * LANES, Bs), f32)],
    )
    return pl.pallas_call(
        functools.partial(_nsa_decode_kernel, n_pages=n_pages, n_win=n_win, nseq=nseq),
        out_shape=(jax.ShapeDtypeStruct((Bs, NSA_HEADS, HEAD_DIM), bf16),
                   jax.ShapeDtypeStruct((Bs, 2 * LANES, n_win), f32)),
        grid_spec=grid_spec,
        compiler_params=_params(("arbitrary",)),
        name="nsa_decode",
    )(page_table, *([cache_t] * (nseq * n_pages)), q_bd.reshape(Bs, NSA_HEADS, LANES), kv32.reshape(Bs, 1, 4 * LANES),
      win32.reshape(Bs, 1, 2 * LANES), win32, gates[:, 0:3 * NSA_HEADS].reshape(Bs, NSA_HEADS, 3), win_t,
      w1bd, w2bd, gk0, bias, expand)


def _hgrn_decode_kernel(h4_ref, lb_ref, gn_ref, s_ref, o_ref, s_out_ref, ft_ref):
    b = pl.program_id(0)
    nbatch = h4_ref.shape[0]

    @pl.when(b == 0)
    def _():
        for h in range(HG_HEADS):
            f, _, _ = _hgrn_gates(h4_ref[:, HG_WIDTH + h * HG_DIM:HG_WIDTH + (h + 1) * HG_DIM],
                                  lb_ref[:, h * HG_DIM:(h + 1) * HG_DIM])
            ft_ref[h] = f.T

    lane = lax.broadcasted_iota(i32, (HG_DIM, nbatch), 1)
    gn = gn_ref[...]
    for sq in range(s_ref.shape[0]):
        sample = b * s_ref.shape[0] + sq
        row = h4_ref[pl.ds(sample, 1), :]
        for h in range(HG_HEADS):
            part = lambda k: row[:, k * HG_WIDTH + h * HG_DIM:k * HG_WIDTH + (h + 1) * HG_DIM]
            f_col = jnp.sum(jnp.where(lane == sample, ft_ref[h], 0.0), axis=-1, keepdims=True)
            s_new = f_col * s_ref[sq, h] + (1.0 - f_col) * part(2)
            s_out_ref[sq, h] = s_new
            q = jnp.broadcast_to(part(0) * HG_SCALE, (8, HG_DIM)).astype(bf16)
            o = jnp.dot(q, s_new.astype(bf16), preferred_element_type=f32)[0:1]
            o_ref[sq, :, h * HG_DIM:(h + 1) * HG_DIM] = _hgrn_out(o, gn, part(3)).astype(bf16)


def _hgrn_decode(h4, lb, gn, state, nseq=4):
    Bs = h4.shape[0]
    nseq = math.gcd(nseq, Bs)
    sspec = pl.BlockSpec((nseq, HG_HEADS, HG_DIM, HG_DIM), lambda b: (b, 0, 0, 0))
    return pl.pallas_call(
        _hgrn_decode_kernel,
        out_shape=(jax.ShapeDtypeStruct((Bs, 1, HG_WIDTH), bf16),
                   jax.ShapeDtypeStruct((Bs, HG_HEADS, HG_DIM, HG_DIM), f32)),
        grid=(Bs // nseq,),
        in_specs=[pl.BlockSpec(h4.shape, lambda b: (0, 0)), pl.BlockSpec(lb.shape, lambda b: (0, 0)),
                  pl.BlockSpec(gn.shape, lambda b: (0, 0)), sspec],
        out_specs=(pl.BlockSpec((nseq, 1, HG_WIDTH), lambda b: (b, 0, 0)), sspec),
        scratch_shapes=[pltpu.VMEM((HG_HEADS, HG_DIM, Bs), f32)],
        compiler_params=_params(("arbitrary",)),
        name="hgrn_decode",
    )(h4, lb, gn, state)


def _split_w_in(w):
    idx = np.cumsum(SPLITS)[:-1]
    return jnp.split(w, [int(v) for v in idx], axis=1)


def kernel(x_prompt, x_sample, cache_kv, page_table, state_kv_win, state_hgrn, state_conv, rel_table, hg_lb_logits,
           norm1_g, w_in, q_norm_g, k_norm_g, phi_k_w1, phi_k_w2, phi_v_w1, phi_v_w2, hg_norm_g, w_nsa_out, w_hg_out,
           w_o, norm2_g, w_gate, w_up, conv_w, conv_b, w_down):
    assert w_in.shape[0] == 1, "one layer"
    Bp, L, _ = x_prompt.shape
    Bs = x_sample.shape[0]
    n_pages = page_table.shape[1]
    past = n_pages * PAGE_SIZE
    n_win = state_kv_win.shape[2]
    assert x_sample.shape[1] == 1 and n_win == WINDOW and past % (2 * CMP_BLOCK) == 0 and L % LANES == 0

    wq, wkv, wgl, whq, whf, whi, whg, wmg = _split_w_in(w_in[0])
    wgl = jnp.pad(wgl, ((0, 0), (0, LANES - wgl.shape[1])))
    w_prompt = jnp.concatenate([wq, wkv, whq, whf, whi, whg, wmg, wgl], axis=1).astype(bf16)
    wq_h = wq.reshape(D_MODEL, NSA_HEADS, HEAD_DIM)
    zq = jnp.zeros_like(wq_h)
    in_g0 = (np.arange(NSA_HEADS) < NSA_GROUP)[None, :, None]
    wq_bd = jnp.where(in_g0, jnp.concatenate([wq_h, zq], axis=-1), jnp.concatenate([zq, wq_h], axis=-1))
    w_sample = jnp.concatenate([wq_bd.reshape(D_MODEL, NSA_HEADS * LANES), wkv, whq, whf, whi, whg, wmg, wgl],
                               axis=1).astype(bf16)
    g1 = norm1_g[0][None, :]
    gq = q_norm_g[0]
    gq_prompt = jnp.tile(gq, NSA_HEADS)[None, :]
    gq_sample = jnp.tile(gq, 2 * NSA_HEADS)[None, :]
    gk = [jnp.tile(k_norm_g[0, s], NSA_KV_HEADS)[None, :] for s in range(3)]

    def block_diag(mats):
        n = len(mats)
        lead = [(0, 0)] * (mats[0].ndim - 2)
        rows = [jnp.pad(m.astype(bf16), lead + [(0, 0), (t * HEAD_DIM, (n - 1 - t) * HEAD_DIM)])
                for t, m in enumerate(mats)]
        return jnp.concatenate(rows, axis=-2)

    w1bd = block_diag([phi_k_w1[0], phi_k_w1[0], phi_v_w1[0], phi_v_w1[0]]).reshape(CMP_BLOCK * 2 * LANES, 2 * LANES)
    w2bd = block_diag([phi_k_w2[0], phi_k_w2[0], phi_v_w2[0], phi_v_w2[0]])
    lb = jnp.cumsum(jax.nn.softmax(hg_lb_logits.astype(f32), axis=0), axis=0)[0][None, :]
    gn = hg_norm_g[0][None, :]
    ffn_w = (w_nsa_out[0].astype(bf16), w_hg_out[0].astype(bf16), w_o[0].astype(bf16), norm2_g[0][None, :],
             w_gate[0].astype(bf16), w_up[0].astype(bf16), conv_w[0], conv_b[0][None, :], w_down[0].astype(bf16))

    idx_c, idx_t = _prompt_bucket_tables(L)
    tc = _bias_tables(rel_table, jnp.asarray(idx_c)).reshape(L // SEL_BLOCK, NSA_KV_HEADS, NSA_GROUP * SEL_BLOCK, -1)
    tt = _bias_tables(rel_table, jnp.asarray(idx_t)).reshape(5, NSA_KV_HEADS, NSA_GROUP * SEL_BLOCK, LANES)
    bias_dec = _bias_tables(rel_table, jnp.asarray(_decode_bucket_table(past, n_win))[None]).reshape(NSA_HEADS, -1)
    ex_prompt = jnp.asarray(_expand_np(L // SEL_BLOCK, L), dtype=bf16)
    ex_sample = jnp.asarray(_expand_np(past // SEL_BLOCK, past), dtype=bf16)

    xp = x_prompt.reshape(Bp * L, D_MODEL)
    qn, kv_t, win32, katt, h4, mgs, gates = _in_proj(xp, g1, w_prompt, gq_prompt, gk[1], gk[2],
                                                     qw=NSA_WIDTH, q_seg=HEAD_DIM, tm=512, seq_len=L)
    cmp = _compress_prompt(kv_t, w1bd, w2bd, gk[0])
    o_nsa = _nsa_prompt(qn, gates, katt, cmp, tc, tt, ex_prompt, Bp, L)
    o_hg, s_prompt = _hgrn_prompt(h4, lb, gn, Bp, L)
    tm_ffn = 512
    yp, gate_tails = _ffn(xp, o_nsa, o_hg.reshape(Bp * L, HG_WIDTH), mgs, *ffn_w, tm=tm_ffn, tf=D_FF // 2, seq_len=L)
    conv_p = gate_tails.reshape(Bp, L // tm_ffn, 2, D_FF)[:, -1]
    n_keep_win = min(WINDOW, L)
    win_p = win32.reshape(Bp, L, 2, NSA_KV_HEADS, HEAD_DIM)[:, L - n_keep_win:]

    xs = x_sample.reshape(Bs, D_MODEL)
    q_bd, kv32_s, win32_s, _, h4_s, mgs_s, gates_s = _in_proj(xs, g1, w_sample, gq_sample, gk[1], gk[2],
                                                              qw=NSA_HEADS * LANES, q_seg=LANES, tm=Bs)
    cache_t = jnp.swapaxes(cache_kv[0].reshape(cache_kv.shape[1], PAGE_SIZE, 4 * LANES), 1, 2)
    win_t = jnp.swapaxes(state_kv_win[0].reshape(Bs, n_win, 2 * LANES), 1, 2)
    o_nsa_s, win_out_t = _nsa_decode(cache_t, page_table, q_bd, kv32_s, win32_s, gates_s, win_t, w1bd, w2bd, gk[0],
                                     bias_dec, ex_sample)
    win_s = jnp.swapaxes(win_out_t, 1, 2)
    kv32 = jnp.swapaxes(kv_t, 1, 2)
    o_hg_s, s_sample = _hgrn_decode(h4_s, lb, gn, state_hgrn[0])
    ys, gate_s = _ffn(xs, o_nsa_s.reshape(Bs, NSA_WIDTH), o_hg_s.reshape(Bs, HG_WIDTH), mgs_s, *ffn_w,
                      tm=Bs, tf=256, conv_state=state_conv[0])
    conv_s = jnp.stack([state_conv[0][:, 1], gate_s], axis=1)

    kvh = (NSA_KV_HEADS, HEAD_DIM)
    return (yp.reshape(Bp, L, D_MODEL), ys.reshape(Bs, 1, D_MODEL),
            kv32.reshape((1, Bp, L, 4) + kvh), kv32_s.reshape((1, Bs, 1, 4) + kvh),
            win_p[None], win_s.reshape((1, Bs, n_win, 2) + kvh),
            s_prompt[None], s_sample[None], conv_p[None], conv_s[None])
```

```python
import functools
import math

import numpy as np
import jax
import jax.numpy as jnp
from jax import lax
from jax.experimental import pallas as pl
from jax.experimental.pallas import tpu as pltpu

f32, bf16, i32 = jnp.float32, jnp.bfloat16, jnp.int32

D_MODEL = 1024
PAGE_SIZE = 128
NSA_HEADS, NSA_KV_HEADS, NSA_GROUP, HEAD_DIM = 8, 2, 4, 64
CMP_BLOCK, SEL_BLOCK, TOP_N, WINDOW = 32, 64, 16, 512
SCALE = HEAD_DIM ** -0.5
NSA_WIDTH = NSA_HEADS * HEAD_DIM
KV_WIDTH = NSA_KV_HEADS * HEAD_DIM
HG_HEADS, HG_DIM = 4, 128
HG_WIDTH = HG_HEADS * HG_DIM
HG_SCALE = HG_DIM ** -0.5
HG_BLOCK = 16
FFN_SLICE = 768
REL_BUCKETS, REL_MAX_DIST = 32, 128
D_FF = 2816
EPS = 1e-6
SPLITS = (NSA_WIDTH, 6 * KV_WIDTH, 3 * NSA_HEADS, HG_WIDTH, HG_WIDTH, HG_WIDTH, HG_WIDTH, 2 * D_MODEL)
LANES = 128
NEG = -1e30
FORCED_SCORE, INVALID_SCORE = 8.0, -1.0
VMEM_LIMIT = 56 * 1024 * 1024

NT = (((1,), (1,)), ((), ()))
TN = (((0,), (0,)), ((), ()))


def _params(sem):
    return pltpu.CompilerParams(dimension_semantics=sem, vmem_limit_bytes=VMEM_LIMIT)


def _bucket_np(dist):
    n = np.maximum(dist, 0)
    exact = REL_BUCKETS // 2
    ratio = np.log(np.maximum(n, 1).astype(np.float32) / np.float32(exact)) / np.float32(math.log(REL_MAX_DIST / exact))
    big = exact + (ratio.astype(np.float32) * np.float32(REL_BUCKETS - exact)).astype(np.int32)
    return np.where(n < exact, n, np.minimum(big, REL_BUCKETS - 1)).astype(np.int32)


def _cmp_perm(n_cmp):
    half = n_cmp // 2
    c = np.arange(n_cmp)
    return np.where(c < half, 2 * c, 2 * (c - half) + 1)


def _prompt_bucket_tables(L):
    nqb = L // SEL_BLOCK
    n_cmp = L // CMP_BLOCK
    qi = np.arange(SEL_BLOCK)
    c_end = (_cmp_perm(n_cmp) + 1) * CMP_BLOCK - 1
    idx_c = np.stack([_bucket_np((i * SEL_BLOCK + qi)[:, None] - c_end[None, :]) for i in range(nqb)])
    kj = np.arange(SEL_BLOCK)
    def tile(delta):
        return _bucket_np(delta * SEL_BLOCK + qi[:, None] - kj[None, :])
    idx_t = np.stack([np.concatenate([tile(e), tile(e - 1)], axis=1) for e in range(5)])
    return idx_c.astype(np.int32), idx_t.astype(np.int32)


def _decode_bucket_table(past_len, n_win):
    n_cmp = past_len // CMP_BLOCK
    c_end = (_cmp_perm(n_cmp) + 1) * CMP_BLOCK - 1
    d_c = np.zeros((LANES,), np.int64)
    d_c[:n_cmp] = past_len - c_end
    d_s = past_len - np.arange(past_len)
    d_w = n_win - np.arange(n_win)
    d_new = np.zeros((LANES,), np.int64)
    return _bucket_np(np.concatenate([d_s, d_w, d_c, d_new]))[None, :].astype(np.int32)


def _expand_np(n_blocks, n_keys):
    e = (np.arange(n_keys)[None, :] // SEL_BLOCK == np.arange(n_blocks)[:, None])
    return e.astype(np.float32)


def _seg_rmsnorm(x, gain, seg):
    outs = []
    for t in range(x.shape[1] // LANES):
        xt = x[:, t * LANES:(t + 1) * LANES]
        sq = xt * xt
        if seg == LANES:
            r = lax.rsqrt(jnp.sum(sq, axis=-1, keepdims=True) * (1.0 / HEAD_DIM) + EPS)
        else:
            lo = lax.broadcasted_iota(i32, xt.shape, 1) < HEAD_DIM
            s_lo = jnp.sum(jnp.where(lo, sq, 0.0), axis=-1, keepdims=True)
            s_hi = jnp.sum(jnp.where(lo, 0.0, sq), axis=-1, keepdims=True)
            r = jnp.where(lo, lax.rsqrt(s_lo * (1.0 / HEAD_DIM) + EPS), lax.rsqrt(s_hi * (1.0 / HEAD_DIM) + EPS))
        outs.append(xt * r)
    y = outs[0] if len(outs) == 1 else jnp.concatenate(outs, axis=1)
    return y * gain


def _rmsnorm_rows(x, gain):
    return x * lax.rsqrt(jnp.mean(x * x, axis=-1, keepdims=True) + EPS) * gain


def _softmax_parts(s, mask):
    s = jnp.where(mask, s, NEG)
    m = jnp.max(s, axis=-1, keepdims=True)
    e = jnp.where(mask, jnp.exp(s - m), 0.0)
    return e, jnp.sum(e, axis=-1, keepdims=True)


def _bias_table_kernel(tab_ref, idx_ref, out_ref):
    idx = idx_ref[...]
    rows = idx.shape[0]
    for h in range(NSA_HEADS):
        acc = jnp.zeros(idx.shape, f32)
        for b in range(REL_BUCKETS):
            acc = jnp.where(idx == b, tab_ref[b, h], acc)
        r = h % NSA_GROUP
        out_ref[h // NSA_GROUP, r * rows:(r + 1) * rows, :] = acc


def _bias_tables(rel_table, idx):
    n, rows, w = idx.shape
    return pl.pallas_call(
        _bias_table_kernel,
        out_shape=jax.ShapeDtypeStruct((n, NSA_KV_HEADS, NSA_GROUP * rows, w), f32),
        grid=(n,),
        in_specs=[pl.BlockSpec(memory_space=pltpu.SMEM),
                  pl.BlockSpec((None, rows, w), lambda i: (i, 0, 0))],
        out_specs=pl.BlockSpec((None, NSA_KV_HEADS, NSA_GROUP * rows, w), lambda i: (i, 0, 0, 0)),
        compiler_params=_params(("arbitrary",)),
        name="bias_tables",
    )(rel_table, idx)


def _in_proj_kernel(x_ref, g1_ref, w_ref, gq_ref, gk1_ref, gk2_ref,
                    q_ref, kv_ref, win_ref, katt_ref, h4_ref, mg_ref, gl_ref, *, qw, q_seg, kv_feature_major):
    x = x_ref[...]
    xn = _rmsnorm_rows(x, g1_ref[...]).astype(bf16)

    def proj(c0, width):
        return jnp.dot(xn, w_ref[:, c0:c0 + width], preferred_element_type=f32)

    q_ref[...] = (_seg_rmsnorm(proj(0, qw), gq_ref[...], q_seg) * SCALE).astype(bf16)
    c = qw
    kv = proj(c, 6 * KV_WIDTH)
    c += 6 * KV_WIDTH
    k_sel = _seg_rmsnorm(kv[:, 2 * KV_WIDTH:3 * KV_WIDTH], gk1_ref[...], HEAD_DIM)
    k_win = _seg_rmsnorm(kv[:, 4 * KV_WIDTH:5 * KV_WIDTH], gk2_ref[...], HEAD_DIM)
    v_sel = kv[:, 3 * KV_WIDTH:4 * KV_WIDTH]
    v_win = kv[:, 5 * KV_WIDTH:6 * KV_WIDTH]
    if kv_feature_major:
        kv_ref[0:2 * KV_WIDTH, :] = kv[:, 0:2 * KV_WIDTH].T
        kv_ref[2 * KV_WIDTH:3 * KV_WIDTH, :] = k_sel.T
        kv_ref[3 * KV_WIDTH:4 * KV_WIDTH, :] = v_sel.T
    else:
        kv_ref[:, 0:2 * KV_WIDTH] = kv[:, 0:2 * KV_WIDTH]
        kv_ref[:, 2 * KV_WIDTH:3 * KV_WIDTH] = k_sel
        kv_ref[:, 3 * KV_WIDTH:4 * KV_WIDTH] = v_sel
    if kv_feature_major:
        win_ref[0:KV_WIDTH, :] = k_win.T
        win_ref[KV_WIDTH:2 * KV_WIDTH, :] = v_win.T
    else:
        win_ref[:, 0:KV_WIDTH] = k_win
        win_ref[:, KV_WIDTH:2 * KV_WIDTH] = v_win
    katt_ref[:, 0:KV_WIDTH] = k_sel.astype(bf16)
    katt_ref[:, KV_WIDTH:2 * KV_WIDTH] = v_sel.astype(bf16)
    katt_ref[:, 2 * KV_WIDTH:3 * KV_WIDTH] = k_win.astype(bf16)
    katt_ref[:, 3 * KV_WIDTH:4 * KV_WIDTH] = v_win.astype(bf16)
    h4_ref[...] = proj(c, 4 * HG_WIDTH)
    c += 4 * HG_WIDTH
    mg_ref[...] = jax.nn.sigmoid(proj(c, 2 * D_MODEL))
    c += 2 * D_MODEL
    gl_ref[...] = jax.nn.sigmoid(proj(c, LANES))


def _in_proj(x2d, g1, w, gq, gk1, gk2, *, qw, q_seg, tm, seq_len=None):
    n = x2d.shape[0]
    cw = w.shape[1]
    row = lambda width: pl.BlockSpec((tm, width), lambda i: (i, 0))
    full = lambda a: pl.BlockSpec(a.shape, lambda i: (0,) * a.ndim)
    widths = (qw, 4 * KV_WIDTH, 2 * KV_WIDTH, 4 * KV_WIDTH, 4 * HG_WIDTH, 2 * D_MODEL, LANES)
    dtypes = (bf16, f32, f32, bf16, f32, f32, f32)
    out_shape = [jax.ShapeDtypeStruct((n, wd), dt) for wd, dt in zip(widths, dtypes)]
    out_specs = [row(wd) for wd in widths]
    if seq_len is not None:
        tiles = seq_len // tm
        for o_, wd in ((1, 4 * KV_WIDTH), (2, 2 * KV_WIDTH)):
            out_shape[o_] = jax.ShapeDtypeStruct((n // seq_len, wd, seq_len), f32)
            out_specs[o_] = pl.BlockSpec((None, wd, tm), lambda i: (i // tiles, 0, i % tiles))
    return pl.pallas_call(
        functools.partial(_in_proj_kernel, qw=qw, q_seg=q_seg, kv_feature_major=seq_len is not None),
        out_shape=tuple(out_shape),
        grid=(n // tm,),
        in_specs=[row(D_MODEL), full(g1),
                  pl.BlockSpec((D_MODEL, cw), lambda i: (0, 0), pipeline_mode=pl.Buffered(1)),
                  full(gq), full(gk1), full(gk2)],
        out_specs=tuple(out_specs),
        compiler_params=_params(("arbitrary",)),
        name="in_proj",
    )(x2d, g1, w, gq, gk1, gk2)


CMP_SLOT = CMP_BLOCK + 1


def _cmp_staging_rows(n_cmp):
    return -(-n_cmp * CMP_SLOT // 8) * 8


def _stage_cmp_chunk(src_ref, chunk, kc_t, vc_t, n_cmp):
    per_chunk = LANES // CMP_BLOCK
    for s, t in enumerate((kc_t.T, vc_t.T)):
        for nl in range(per_chunk):
            n = chunk * per_chunk + nl
            slot = n // 2 + (n % 2) * (n_cmp // 2)
            src_ref[s, slot * CMP_SLOT:slot * CMP_SLOT + CMP_BLOCK, :] = t[nl * CMP_BLOCK:(nl + 1) * CMP_BLOCK]


def _compress(src_refs, w1_ref, w2_ref, gk0, n_cmp):
    acc = jnp.zeros((len(src_refs) * n_cmp, 2 * LANES), f32)
    per_dot = 8
    for j0 in range(0, CMP_BLOCK, per_dot):
        xs = jnp.concatenate(
            [jnp.concatenate([src[s, pl.ds(j, n_cmp, stride=CMP_SLOT), :].astype(bf16)
                              for j in range(j0, j0 + per_dot) for s in range(2)], axis=1)
             for src in src_refs], axis=0)
        w = w1_ref[j0 * 2 * LANES:(j0 + per_dot) * 2 * LANES, :]
        acc = acc + jnp.dot(xs, w, preferred_element_type=f32)
    hmid = jax.nn.gelu(acc).astype(bf16)
    y = jnp.dot(hmid, w2_ref[...], preferred_element_type=f32)
    ck = _seg_rmsnorm(y[:, 0:LANES], gk0, HEAD_DIM)
    return jnp.concatenate([ck, y[:, LANES:2 * LANES]], axis=1)


def _compress_prompt_kernel(kv_ref, w1_ref, w2_ref, gk0_ref, out_ref, src_ref, *, n_cmp):
    for c in range(kv_ref.shape[1] // LANES):
        cols = slice(c * LANES, (c + 1) * LANES)
        _stage_cmp_chunk(src_ref, c, kv_ref[0:LANES, cols], kv_ref[LANES:2 * LANES, cols], n_cmp)
    out_ref[...] = _compress([src_ref], w1_ref, w2_ref, gk0_ref[...], n_cmp).astype(bf16)


def _compress_prompt(kv_t, w1bd, w2bd, gk0):
    B, _, L = kv_t.shape
    n_cmp = L // CMP_BLOCK
    return pl.pallas_call(
        functools.partial(_compress_prompt_kernel, n_cmp=n_cmp),
        out_shape=jax.ShapeDtypeStruct((B, n_cmp, 2 * LANES), bf16),
        grid=(B,),
        in_specs=[pl.BlockSpec((None, 2 * LANES, L), lambda b: (b, 0, 0)),
                  pl.BlockSpec(w1bd.shape, lambda b: (0, 0)),
                  pl.BlockSpec(w2bd.shape, lambda b: (0, 0)),
                  pl.BlockSpec(gk0.shape, lambda b: (0, 0))],
        out_specs=pl.BlockSpec((None, n_cmp, 2 * LANES), lambda b: (b, 0, 0)),
        scratch_shapes=[pltpu.VMEM((2, _cmp_staging_rows(n_cmp), LANES), f32)],
        compiler_params=_params(("arbitrary",)),
        name="compress_prompt",
    )(kv_t, w1bd, w2bd, gk0)


def _select_blocks_t(score, valid, n_keep):
    nb = score.shape[0]
    jj = lax.broadcasted_iota(i32, score.shape, 0)
    cnt = jnp.zeros(score.shape, f32)
    for k in range(nb):
        sk = score[k:k + 1, :]
        better = (sk > score) | ((sk == score) & (jj > k))
        cnt = cnt + jnp.where(better, 1.0, 0.0)
    return (cnt < n_keep) & valid


def _sel_tile_classes(n_tiles):
    return sorted({-(-n_tiles * c // 4) for c in range(1, 5)})


def _nsa_prompt_kernel(q_ref, gate_ref, katt_ref, cmp_ref, tc_ref, tt_ref, ex_ref, o_ref,
                       selx_ref, s_ref, os_ref, part_ref, gsel_ref, *, L):
    i = pl.program_id(1)
    n_cmp = L // CMP_BLOCK
    n_sel = L // SEL_BLOCK
    n_tiles = L // LANES
    qb = SEL_BLOCK
    rows = NSA_GROUP * qb
    units = [(sq, g) for sq in range(q_ref.shape[0]) for g in range(NSA_KV_HEADS)]
    nu = range(len(units))
    row64 = lax.broadcasted_iota(i32, (qb, LANES), 0)
    lane = lax.broadcasted_iota(i32, (qb, LANES), 1)
    qpos = i * qb + row64

    qs = [jnp.concatenate(
        [q_ref[sq, :, (NSA_GROUP * g + r) * HEAD_DIM:(NSA_GROUP * g + r + 1) * HEAD_DIM] for r in range(NSA_GROUP)],
        axis=0) for sq, g in units]

    def attend(k_off, v_off, tile0, n_t, mask_fn):
        chunk = 4

        def key_rows(t0, count, off, u):
            sq, g = units[u]
            r0 = pl.multiple_of((tile0 + t0) * LANES, LANES)
            return katt_ref[sq, pl.ds(r0, count * LANES), off + g * HEAD_DIM:off + (g + 1) * HEAD_DIM]

        def scores(t0, count, mruns):
            dots = [lax.dot_general(qs[u], key_rows(t0, count, k_off, u), NT, preferred_element_type=f32)
                    for u in nu]
            out = list(mruns)
            for j in range(count):
                ta = tile0 + t0 + j
                bias_idx = jnp.clip(i - 2 * ta, 0, 4)
                kpos = ta * LANES + lane
                for u in nu:
                    s = dots[u][:, j * LANES:(j + 1) * LANES] + tt_ref[bias_idx, units[u][1]]
                    msk = mask_fn(u, ta, kpos)
                    s = jnp.where(msk[None], s.reshape(NSA_GROUP, qb, LANES), NEG).reshape(rows, LANES)
                    s_ref[u, t0 + j] = s
                    out[u] = jnp.maximum(out[u], s)
            return tuple(out)

        def values(t0, count, carry, ms):
            ls, accs = list(carry[0]), list(carry[1])
            ps = []
            for u in nu:
                pj = [jnp.exp(s_ref[u, t0 + j] - ms[u]) for j in range(count)]
                for p in pj:
                    ls[u] = ls[u] + p
                ps.append(jnp.concatenate([p.astype(bf16) for p in pj], axis=1))
            for u in nu:
                accs[u] = accs[u] + jnp.dot(ps[u], key_rows(t0, count, v_off, u), preferred_element_type=f32)
            return tuple(ls), tuple(accs)

        def over_chunks(fn, carry):
            n_full = n_t // chunk
            if n_full > 2:
                carry = lax.fori_loop(0, n_full, lambda c, cr: fn(c * chunk, chunk, cr), carry)
            else:
                for c in range(n_full):
                    carry = fn(c * chunk, chunk, carry)
            if n_t % chunk:
                carry = fn(n_full * chunk, n_t % chunk, carry)
            return carry

        mruns = over_chunks(scores, tuple(jnp.full((rows, LANES), NEG, f32) for _ in nu))
        ms = [jnp.max(m, axis=-1, keepdims=True) for m in mruns]
        init = (tuple(jnp.zeros((rows, LANES), f32) for _ in nu),
                tuple(jnp.zeros((rows, HEAD_DIM), f32) for _ in nu))
        ls, accs = over_chunks(lambda t0, count, cr: values(t0, count, cr, ms), init)
        return [accs[u] / jnp.sum(ls[u], axis=-1, keepdims=True) for u in nu]

    n_wt = min(WINDOW // LANES + 1, n_tiles)
    w0 = jnp.clip((i - WINDOW // qb) // 2, 0, n_tiles - n_wt)
    o_w = attend(2 * KV_WIDTH, 3 * KV_WIDTH, w0, n_wt,
                 lambda u, ta, kpos: (kpos <= qpos) & (qpos - kpos <= WINDOW))

    o_c, imp_t = [], []
    col = lax.broadcasted_iota(i32, (rows, n_cmp), 1)
    rq = lax.broadcasted_iota(i32, (rows, n_cmp), 0) & (qb - 1)
    blk = jnp.where(col < n_cmp // 2, 2 * col, 2 * col - (n_cmp - 1))
    vis = (blk + 1) * CMP_BLOCK - 1 <= i * qb + rq
    for u, (sq, g) in enumerate(units):
        ck = cmp_ref[sq, :, g * HEAD_DIM:(g + 1) * HEAD_DIM]
        cv = cmp_ref[sq, :, LANES + g * HEAD_DIM:LANES + (g + 1) * HEAD_DIM]
        s = lax.dot_general(qs[u], ck, NT, preferred_element_type=f32) + tc_ref[g]
        e, den = _softmax_parts(s, vis)
        p = e / jnp.where(den > 0, den, 1.0)
        o_c.append(jnp.dot(p.astype(bf16), cv, preferred_element_type=f32))
        ps_t = (p[0:qb] + p[qb:2 * qb] + p[2 * qb:3 * qb] + p[3 * qb:4 * qb]).T
        imp_t.append(ps_t[0:n_sel] + ps_t[n_sel:2 * n_sel])

    imp = jnp.concatenate(imp_t, axis=1)
    jj = lax.broadcasted_iota(i32, imp.shape, 0)
    valid = jj <= i
    forced = valid & ((jj == 0) | (jj >= i - 1))
    score = jnp.where(forced, FORCED_SCORE, jnp.where(valid, imp, INVALID_SCORE))
    sel = _select_blocks_t(score, valid, min(TOP_N, n_sel))
    selx = lax.dot_general(jnp.where(sel, 1.0, 0.0).astype(bf16), ex_ref[...], TN, preferred_element_type=f32)
    for u in nu:
        for t in range(n_tiles):
            selx_ref[u, t] = selx[u * qb:(u + 1) * qb, t * LANES:(t + 1) * LANES]

    for u, (sq, g) in enumerate(units):
        gates = gate_ref[sq]
        for r in range(NSA_GROUP):
            h = NSA_GROUP * g + r
            hs = slice(h * HEAD_DIM, (h + 1) * HEAD_DIM)
            sl = slice(r * qb, (r + 1) * qb)
            part_ref[sq, :, hs] = gates[:, 3 * h:3 * h + 1] * o_c[u][sl] + gates[:, 3 * h + 2:3 * h + 3] * o_w[u][sl]
            gsel_ref[sq, :, hs] = jnp.broadcast_to(gates[:, 3 * h + 1:3 * h + 2], (qb, HEAD_DIM))

    need = i // 2 + 1
    prev = 0
    for n_t in _sel_tile_classes(n_tiles):
        @pl.when((need > prev) & (need <= n_t))
        def _(n_t=n_t):
            o_s = attend(0, KV_WIDTH, 0, n_t, lambda u, ta, kpos: (selx_ref[u, ta] > 0.5) & (kpos <= qpos))
            for u in nu:
                os_ref[u] = o_s[u]
        prev = n_t

    for u, (sq, g) in enumerate(units):
        o_s = os_ref[u]
        for r in range(NSA_GROUP):
            hs = slice((NSA_GROUP * g + r) * HEAD_DIM, (NSA_GROUP * g + r + 1) * HEAD_DIM)
            o = part_ref[sq, :, hs] + gsel_ref[sq, :, hs] * o_s[r * qb:(r + 1) * qb]
            o_ref[sq, :, hs] = o.astype(bf16)


def _nsa_prompt(qn, gates, katt, cmp, tc, tt, expand, B, L, nseq=2):
    nqb = L // SEL_BLOCK
    qb = SEL_BLOCK
    n_tiles = L // LANES
    rows = NSA_GROUP * qb
    nseq = math.gcd(nseq, B)
    n_units = nseq * NSA_KV_HEADS
    seq3 = lambda a: a.reshape(B, L, a.shape[-1])
    return pl.pallas_call(
        functools.partial(_nsa_prompt_kernel, L=L),
        out_shape=jax.ShapeDtypeStruct((B, L, NSA_WIDTH), bf16),
        grid=(B // nseq, nqb),
        in_specs=[pl.BlockSpec((nseq, qb, NSA_WIDTH), lambda b, i: (b, i, 0)),
                  pl.BlockSpec((nseq, qb, LANES), lambda b, i: (b, i, 0)),
                  pl.BlockSpec((nseq, L, 4 * KV_WIDTH), lambda b, i: (b, 0, 0)),
                  pl.BlockSpec((nseq, L // CMP_BLOCK, 2 * LANES), lambda b, i: (b, 0, 0)),
                  pl.BlockSpec((None,) + tc.shape[1:], lambda b, i: (i, 0, 0, 0)),
                  pl.BlockSpec(tt.shape, lambda b, i: (0, 0, 0, 0)),
                  pl.BlockSpec(expand.shape, lambda b, i: (0, 0))],
        out_specs=pl.BlockSpec((nseq, qb, NSA_WIDTH), lambda b, i: (b, i, 0)),
        scratch_shapes=[pltpu.VMEM((n_units, n_tiles, qb, LANES), f32),
                        pltpu.VMEM((n_units, n_tiles, rows, LANES), f32),
                        pltpu.VMEM((n_units, rows, HEAD_DIM), f32),
                        pltpu.VMEM((nseq, qb, NSA_WIDTH), f32), pltpu.VMEM((nseq, qb, NSA_WIDTH), f32)],
        compiler_params=_params(("arbitrary", "arbitrary")),
        name="nsa_prompt",
    )(seq3(qn), seq3(gates), seq3(katt), cmp, tc, tt, expand).reshape(B * L, NSA_WIDTH)


def _hgrn_gates(hf, lb):
    f = lb + (1.0 - lb) * jax.nn.sigmoid(hf)
    return f, 1.0 - f, jnp.log(f)


def _hgrn_out(o, gn, hgate):
    return _rmsnorm_rows(o, gn) * jax.nn.silu(hgate)


def _hgrn_prompt_kernel(h4_ref, lb_ref, gn_ref, o_ref, s_out_ref, st_ref, *, tc):
    c = pl.program_id(1)
    nb = HG_BLOCK
    nseq = h4_ref.shape[0]

    @pl.when(c == 0)
    def _():
        st_ref[...] = jnp.zeros_like(st_ref)

    t_row = lax.broadcasted_iota(i32, (nb, HG_DIM), 0)
    t_col = lax.broadcasted_iota(i32, (nb, 1), 0)
    gn = gn_ref[...]
    chains = [(sq, h) for sq in range(nseq) for h in range(HG_HEADS)]

    def cumsum_rows(x):
        shift = 1
        while shift < nb:
            x = x + jnp.where(t_row >= shift, pltpu.roll(x, shift, axis=0), 0.0)
            shift *= 2
        return x

    def block(bi, _):
        r0 = pl.multiple_of(bi * nb, nb)

        def sl(sq, h, part):
            return h4_ref[sq, pl.ds(r0, nb), part * HG_WIDTH + h * HG_DIM:part * HG_WIDTH + (h + 1) * HG_DIM]

        qs, ks, vs, bs, sts, os_ = [], [], [], [], [], []
        for sq, h in chains:
            _, k, glog = _hgrn_gates(sl(sq, h, 1), lb_ref[:, h * HG_DIM:(h + 1) * HG_DIM])
            qs.append(sl(sq, h, 0) * HG_SCALE)
            ks.append(k)
            vs.append(sl(sq, h, 2))
            bs.append(cumsum_rows(glog))
        for c_, (sq, h) in enumerate(chains):
            sts.append(st_ref[sq, h])
            os_.append(lax.dot_general((qs[c_] * jnp.exp(bs[c_])).astype(bf16), sts[c_].astype(bf16), NT,
                                       preferred_element_type=f32))
        for c_ in range(len(chains)):
            q, k, v, b, o = qs[c_], ks[c_], vs[c_], bs[c_], os_[c_]
            parts = [o[r0_:r0_ + 8] for r0_ in range(0, nb, 8)]
            for s in range(nb):
                for gi in range(s // 8, nb // 8):
                    rs = slice(gi * 8, gi * 8 + 8)
                    w = q[rs] * k[s:s + 1] * jnp.exp(b[rs] - b[s:s + 1])
                    a = jnp.sum(w, axis=-1, keepdims=True)
                    if gi == s // 8:
                        a = jnp.where(t_col[rs] >= s, a, 0.0)
                    parts[gi] = parts[gi] + a * v[s:s + 1]
            os_[c_] = jnp.concatenate(parts, axis=0)
        for c_, (sq, h) in enumerate(chains):
            b_last = bs[c_][nb - 1:nb]
            kt = ks[c_] * jnp.exp(b_last - bs[c_])
            upd = lax.dot_general(vs[c_].astype(bf16), kt.astype(bf16), TN, preferred_element_type=f32)
            st_ref[sq, h] = jnp.exp(b_last) * sts[c_] + upd
            o_ref[sq, pl.ds(r0, nb), h * HG_DIM:(h + 1) * HG_DIM] = _hgrn_out(os_[c_], gn, sl(sq, h, 3)).astype(bf16)
        return 0

    lax.fori_loop(0, tc // nb, block, 0)

    @pl.when(c == pl.num_programs(1) - 1)
    def _():
        for sq in range(nseq):
            for h in range(HG_HEADS):
                s_out_ref[sq, h] = st_ref[sq, h].T


def _hgrn_prompt(h4, lb, gn, B, L, tc=256, nseq=4):
    nc = L // tc
    nseq = math.gcd(nseq, B)
    return pl.pallas_call(
        functools.partial(_hgrn_prompt_kernel, tc=tc),
        out_shape=(jax.ShapeDtypeStruct((B, L, HG_WIDTH), bf16),
                   jax.ShapeDtypeStruct((B, HG_HEADS, HG_DIM, HG_DIM), f32)),
        grid=(B // nseq, nc),
        in_specs=[pl.BlockSpec((nseq, tc, 4 * HG_WIDTH), lambda b, c: (b, c, 0)),
                  pl.BlockSpec(lb.shape, lambda b, c: (0, 0)),
                  pl.BlockSpec(gn.shape, lambda b, c: (0, 0))],
        out_specs=(pl.BlockSpec((nseq, tc, HG_WIDTH), lambda b, c: (b, c, 0)),
                   pl.BlockSpec((nseq, HG_HEADS, HG_DIM, HG_DIM), lambda b, c: (b, 0, 0, 0))),
        scratch_shapes=[pltpu.VMEM((nseq, HG_HEADS, HG_DIM, HG_DIM), f32)],
        compiler_params=_params(("arbitrary", "arbitrary")),
        name="hgrn_prompt",
    )(h4.reshape(B, L, 4 * HG_WIDTH), lb, gn)


def _ffn_kernel(*refs, decode, tiles_per_seq):
    if decode:
        (x_ref, on_ref, oh_ref, mg_ref, wn_ref, wh_ref, wo_ref, g2_ref, wg_ref, wu_ref, cw_ref, cb_ref, wd_ref,
         cs0_ref, cs1_ref, y_ref, gate_out_ref, x1_ref, xn2_ref, acc_ref) = refs
    else:
        (x_ref, on_ref, oh_ref, mg_ref, wn_ref, wh_ref, wo_ref, g2_ref, wg_ref, wu_ref, cw_ref, cb_ref, wd_ref,
         y_ref, gate_out_ref, x1_ref, xn2_ref, acc_ref, carry_ref) = refs
    i = pl.program_id(0)
    fi = pl.program_id(1)
    tm = x_ref.shape[0]

    @pl.when(fi == 0)
    def _():
        y_a = jnp.dot(on_ref[...], wn_ref[...], preferred_element_type=f32)
        y_b = jnp.dot(oh_ref[...], wh_ref[...], preferred_element_type=f32)
        mg = mg_ref[...]
        merged = mg[:, 0:D_MODEL] * y_a + mg[:, D_MODEL:2 * D_MODEL] * y_b
        x1 = x_ref[...] + jnp.dot(merged.astype(bf16), wo_ref[...], preferred_element_type=f32)
        x1_ref[...] = x1
        xn2_ref[...] = _rmsnorm_rows(x1, g2_ref[...]).astype(bf16)
        acc_ref[...] = jnp.zeros_like(acc_ref)

    xn2 = xn2_ref[...]
    tf = wg_ref.shape[1]
    if not decode:
        @pl.when(i % tiles_per_seq == 0)
        def _():
            carry_ref[fi] = jnp.zeros(carry_ref.shape[1:], f32)

    acc = acc_ref[...]
    for a in range(0, tf, FFN_SLICE):
        cs = slice(a, min(a + FFN_SLICE, tf))
        gate = jnp.dot(xn2, wg_ref[:, cs], preferred_element_type=f32)
        up = jnp.dot(xn2, wu_ref[:, cs], preferred_element_type=f32)
        cw = cw_ref[:, cs]
        if decode:
            prev2, prev1 = cs0_ref[:, cs], cs1_ref[:, cs]
            gate_out_ref[:, cs] = gate
        else:
            carry = carry_ref[fi, :, cs]
            rid = lax.broadcasted_iota(i32, gate.shape, 0)
            prev1 = jnp.where(rid == 0, carry[7:8], pltpu.roll(gate, 1, axis=0))
            prev2 = jnp.where(rid == 0, carry[6:7], jnp.where(rid == 1, carry[7:8], pltpu.roll(gate, 2, axis=0)))
            carry_ref[fi, 6:8, cs] = gate[tm - 2:tm]
            gate_out_ref[:, cs] = gate[tm - 2:tm]
        conv = cb_ref[:, cs] + cw[0:1] * prev2 + cw[1:2] * prev1 + cw[2:3] * gate
        hmid = (jax.nn.silu(conv) * up).astype(bf16)
        acc = acc + jnp.dot(hmid, wd_ref[cs, :], preferred_element_type=f32)
    acc_ref[...] = acc

    @pl.when(fi == pl.num_programs(1) - 1)
    def _():
        y_ref[...] = x1_ref[...] + acc_ref[...]


def _ffn(x2d, o_nsa, o_hg, mgs, wn, wh, wo, g2, wg, wu, cw, cb, wd, *, tm, tf, seq_len=None, conv_state=None):
    n = x2d.shape[0]
    decode = conv_state is not None
    nf = D_FF // tf
    row = lambda width: pl.BlockSpec((tm, width), lambda i, f: (i, 0))
    full = lambda a: pl.BlockSpec(a.shape, lambda i, f: (0,) * a.ndim)
    in_specs = [row(D_MODEL), row(NSA_WIDTH), row(HG_WIDTH), row(2 * D_MODEL), full(wn), full(wh), full(wo), full(g2),
                pl.BlockSpec((D_MODEL, tf), lambda i, f: (0, f)), pl.BlockSpec((D_MODEL, tf), lambda i, f: (0, f)),
                pl.BlockSpec((3, tf), lambda i, f: (0, f)), pl.BlockSpec((1, tf), lambda i, f: (0, f)),
                pl.BlockSpec((tf, D_MODEL), lambda i, f: (f, 0))]
    args = [x2d, o_nsa, o_hg, mgs, wn, wh, wo, g2, wg, wu, cw, cb, wd]
    scratch = [pltpu.VMEM((tm, D_MODEL), f32), pltpu.VMEM((tm, D_MODEL), bf16), pltpu.VMEM((tm, D_MODEL), f32)]
    if decode:
        in_specs += [pl.BlockSpec((tm, tf), lambda i, f: (i, f))] * 2
        args += [conv_state[:, 0], conv_state[:, 1]]
        gate_shape = jax.ShapeDtypeStruct((n, D_FF), f32)
        gate_spec = pl.BlockSpec((tm, tf), lambda i, f: (i, f))
        tiles_per_seq = 1
    else:
        tiles_per_seq = seq_len // tm
        gate_shape = jax.ShapeDtypeStruct((n // tm, 2, D_FF), f32)
        gate_spec = pl.BlockSpec((None, 2, tf), lambda i, f: (i, 0, f))
        scratch.append(pltpu.VMEM((nf, 8, tf), f32))
    return pl.pallas_call(
        functools.partial(_ffn_kernel, decode=decode, tiles_per_seq=tiles_per_seq),
        out_shape=(jax.ShapeDtypeStruct((n, D_MODEL), f32), gate_shape),
        grid=(n // tm, nf),
        in_specs=in_specs,
        out_specs=(row(D_MODEL), gate_spec),
        scratch_shapes=scratch,
        compiler_params=_params(("arbitrary", "arbitrary")),
        name="ffn_decode" if decode else "ffn_prompt",
    )(*args)


def _nsa_decode_kernel(*refs, n_pages, n_win, nseq):
    page_refs = refs[1:1 + nseq * n_pages]
    (q_ref, kvn_ref, winn_ref, win_all_ref, gate_ref, win_ref, w1_ref, w2_ref, gk0_ref, bias_ref, ex_ref,
     o_ref, win_out_ref, src_ref, ksel_ref, vsel_ref, wcol_ref) = refs[1 + nseq * n_pages:]
    step = pl.program_id(0)
    past = n_pages * PAGE_SIZE
    n_cmp = past // CMP_BLOCK
    n_sel = past // SEL_BLOCK
    heads = NSA_HEADS
    samples = range(nseq)

    @pl.when(step == 0)
    def _():
        for t in range(2):
            wcol_ref[t * LANES:(t + 1) * LANES, :] = win_all_ref[:, t * LANES:(t + 1) * LANES].T

    for sq in samples:
        for k in range(n_pages):
            page = page_refs[sq * n_pages + k]
            cols = slice(k * PAGE_SIZE, (k + 1) * PAGE_SIZE)
            _stage_cmp_chunk(src_ref.at[sq], k, page[0:LANES, :], page[LANES:2 * LANES, :], n_cmp)
            ksel_ref[sq, :, cols] = page[2 * LANES:3 * LANES, :].astype(bf16)
            vsel_ref[sq, :, cols] = page[3 * LANES:4 * LANES, :].astype(bf16)

    hrow = lax.broadcasted_iota(i32, (heads, 1), 0)
    g0 = hrow < NSA_GROUP
    bias = bias_ref[...]
    b_s = bias[:, 0:past]
    b_w = bias[:, past:past + n_win]
    b_c = bias[:, past + n_win:past + n_win + n_cmp]
    b_new = bias[:, past + n_win + LANES:past + n_win + LANES + 1]
    qs = [q_ref[sq] for sq in samples]

    def pick(o2):
        return jnp.where(g0, o2[:, 0:HEAD_DIM], o2[:, HEAD_DIM:2 * HEAD_DIM])

    def scores(q, k_t, b_past, k_new):
        s_past = jnp.dot(q, k_t, preferred_element_type=f32) + b_past
        s_new = jnp.sum(q.astype(f32) * k_new, axis=-1, keepdims=True) + b_new
        return s_past, s_new

    def attend(s, mask, v_t, v_new):
        s_past, s_new = s
        sm = s_past if mask is None else jnp.where(mask, s_past, NEG)
        m = jnp.maximum(jnp.max(sm, axis=-1, keepdims=True), s_new)
        e = jnp.exp(sm - m)
        e_new = jnp.exp(s_new - m)
        den = jnp.sum(e, axis=-1, keepdims=True) + e_new
        o2 = lax.dot_general(e.astype(bf16), v_t, NT, preferred_element_type=f32) + e_new * v_new
        return pick(o2 / den)

    lane_b = lax.broadcasted_iota(i32, wcol_ref.shape, 1)
    lane_w = lax.broadcasted_iota(i32, win_ref.shape[1:], 1)
    o_w, s_sel = [], []
    for sq in samples:
        winn, kvn = winn_ref[sq], kvn_ref[sq]
        s_sel.append(scores(qs[sq], ksel_ref[sq], b_s, kvn[:, 2 * LANES:3 * LANES]))
        s_win = scores(qs[sq], win_ref[sq, 0:LANES, :].astype(bf16), b_w, winn[:, 0:LANES])
        o_w.append(attend(s_win, None, win_ref[sq, LANES:2 * LANES, :].astype(bf16), winn[:, LANES:2 * LANES]))
        new_col = jnp.sum(jnp.where(lane_b == step * nseq + sq, wcol_ref[...], 0.0), axis=-1, keepdims=True)
        win_out_ref[sq] = jnp.where(lane_w == n_win - 1, new_col, pltpu.roll(win_ref[sq], n_win - 1, axis=1))

    cmp = _compress([src_ref.at[sq] for sq in samples], w1_ref, w2_ref, gk0_ref[...], n_cmp)
    o_c, masks = [], []
    cur = past // SEL_BLOCK
    jrow = lax.broadcasted_iota(i32, (NSA_KV_HEADS, n_sel), 1)
    forced = (jrow == 0) | (jrow >= cur - 1)
    n_keep = min(TOP_N, n_sel + 1) - 1
    ri = lax.broadcasted_iota(i32, (n_sel, n_sel), 0)
    ci = lax.broadcasted_iota(i32, (n_sel, n_sel), 1)
    for sq in samples:
        ck = cmp[sq * n_cmp:(sq + 1) * n_cmp, 0:LANES].astype(bf16)
        cv = cmp[sq * n_cmp:(sq + 1) * n_cmp, LANES:2 * LANES].astype(bf16)
        s_c = lax.dot_general(qs[sq], ck, NT, preferred_element_type=f32) + b_c
        e_c, den_c = _softmax_parts(s_c, jnp.full(s_c.shape, True))
        p_c = e_c / jnp.where(den_c > 0, den_c, 1.0)
        o_c.append(pick(jnp.dot(p_c.astype(bf16), cv, preferred_element_type=f32)))

        ps = jnp.concatenate([jnp.sum(p_c[0:NSA_GROUP], axis=0, keepdims=True),
                              jnp.sum(p_c[NSA_GROUP:heads], axis=0, keepdims=True)], axis=0)
        imp = ps[:, 0:n_sel] + ps[:, n_sel:2 * n_sel]
        score_row = jnp.where(forced, FORCED_SCORE, imp)
        sels = []
        for g in range(NSA_KV_HEADS):
            rowb = jnp.broadcast_to(score_row[g:g + 1], (n_sel, n_sel))
            colb = jnp.sum(jnp.where(ri == ci, rowb, 0.0), axis=-1, keepdims=True)
            better = (colb > rowb) | ((colb == rowb) & (ri < ci))
            cnt = jnp.sum(jnp.where(better, 1.0, 0.0), axis=0, keepdims=True)
            sels.append(jnp.where(cnt < n_keep, 1.0, 0.0))
        sel = jnp.concatenate(sels, axis=0).astype(bf16)
        selx = jnp.dot(sel, ex_ref[...], preferred_element_type=f32)
        masks.append(jnp.where(g0, selx[0:1], selx[1:2]) > 0.5)

    for sq in samples:
        o_s = attend(s_sel[sq], masks[sq], vsel_ref[sq], kvn_ref[sq][:, 3 * LANES:4 * LANES])
        gates = gate_ref[sq]
        o_ref[sq] = (gates[:, 0:1] * o_c[sq] + gates[:, 1:2] * o_s + gates[:, 2:3] * o_w[sq]).astype(bf16)


def _nsa_decode(cache_t, page_table, q_bd, kv32, win32, gates, win_t, w1bd, w2bd, gk0, bias, expand, nseq=2):
    Bs, n_pages = page_table.shape
    n_win = win_t.shape[2]
    past = n_pages * PAGE_SIZE
    n_cmp = past // CMP_BLOCK
    nseq = math.gcd(nseq, Bs)
    per_b = lambda shape: pl.BlockSpec((nseq,) + shape, lambda b, pt: (b,) + (0,) * len(shape))
    const = lambda a: pl.BlockSpec(a.shape, lambda b, pt: (0,) * a.ndim)
    page_spec = lambda sq, k: pl.BlockSpec((None, 4 * LANES, PAGE_SIZE), lambda b, pt: (pt[b * nseq + sq, k], 0, 0))
    grid_spec = pltpu.PrefetchScalarGridSpec(
        num_scalar_prefetch=1,
        grid=(Bs // nseq,),
        in_specs=[page_spec(sq, k) for sq in range(nseq) for k in range(n_pages)] + [
            per_b((NSA_HEADS, LANES)), per_b((1, 4 * LANES)), per_b((1, 2 * LANES)), const(win32),
            per_b((NSA_HEADS, 3)), per_b((2 * LANES, n_win)), const(w1bd), const(w2bd), const(gk0), const(bias),
            const(expand)],
        out_specs=(per_b((NSA_HEADS, HEAD_DIM)), per_b((2 * LANES, n_win))),
        scratch_shapes=[pltpu.VMEM((nseq, 2, _cmp_staging_rows(n_cmp), LANES), f32),
                        pltpu.VMEM((nseq, LANES, past), bf16), pltpu.VMEM((nseq, LANES, past), bf16),
                        pltpu.VMEM((2 * LANES, Bs), f32)],
    )
    return pl.pallas_call(
        functools.partial(_nsa_decode_kernel, n_pages=n_pages, n_win=n_win, nseq=nseq),
        out_shape=(jax.ShapeDtypeStruct((Bs, NSA_HEADS, HEAD_DIM), bf16),
                   jax.ShapeDtypeStruct((Bs, 2 * LANES, n_win), f32)),
        grid_spec=grid_spec,
        compiler_params=_params(("arbitrary",)),
        name="nsa_decode",
    )(page_table, *([cache_t] * (nseq * n_pages)), q_bd.reshape(Bs, NSA_HEADS, LANES), kv32.reshape(Bs, 1, 4 * LANES),
      win32.reshape(Bs, 1, 2 * LANES), win32, gates[:, 0:3 * NSA_HEADS].reshape(Bs, NSA_HEADS, 3), win_t,
      w1bd, w2bd, gk0, bias, expand)


def _hgrn_decode_kernel(h4_ref, lb_ref, gn_ref, s_ref, o_ref, s_out_ref, ft_ref):
    b = pl.program_id(0)
    nbatch = h4_ref.shape[0]

    @pl.when(b == 0)
    def _():
        for h in range(HG_HEADS):
            f, _, _ = _hgrn_gates(h4_ref[:, HG_WIDTH + h * HG_DIM:HG_WIDTH + (h + 1) * HG_DIM],
                                  lb_ref[:, h * HG_DIM:(h + 1) * HG_DIM])
            ft_ref[h] = f.T

    lane = lax.broadcasted_iota(i32, (HG_DIM, nbatch), 1)
    gn = gn_ref[...]
    for sq in range(s_ref.shape[0]):
        sample = b * s_ref.shape[0] + sq
        row = h4_ref[pl.ds(sample, 1), :]
        for h in range(HG_HEADS):
            part = lambda k: row[:, k * HG_WIDTH + h * HG_DIM:k * HG_WIDTH + (h + 1) * HG_DIM]
            f_col = jnp.sum(jnp.where(lane == sample, ft_ref[h], 0.0), axis=-1, keepdims=True)
            s_new = f_col * s_ref[sq, h] + (1.0 - f_col) * part(2)
            s_out_ref[sq, h] = s_new
            q = jnp.broadcast_to(part(0) * HG_SCALE, (8, HG_DIM)).astype(bf16)
            o = jnp.dot(q, s_new.astype(bf16), preferred_element_type=f32)[0:1]
            o_ref[sq, :, h * HG_DIM:(h + 1) * HG_DIM] = _hgrn_out(o, gn, part(3)).astype(bf16)


def _hgrn_decode(h4, lb, gn, state, nseq=4):
    Bs = h4.shape[0]
    nseq = math.gcd(nseq, Bs)
    sspec = pl.BlockSpec((nseq, HG_HEADS, HG_DIM, HG_DIM), lambda b: (b, 0, 0, 0))
    return pl.pallas_call(
        _hgrn_decode_kernel,
        out_shape=(jax.ShapeDtypeStruct((Bs, 1, HG_WIDTH), bf16),
                   jax.ShapeDtypeStruct((Bs, HG_HEADS, HG_DIM, HG_DIM), f32)),
        grid=(Bs // nseq,),
        in_specs=[pl.BlockSpec(h4.shape, lambda b: (0, 0)), pl.BlockSpec(lb.shape, lambda b: (0, 0)),
                  pl.BlockSpec(gn.shape, lambda b: (0, 0)), sspec],
        out_specs=(pl.BlockSpec((nseq, 1, HG_WIDTH), lambda b: (b, 0, 0)), sspec),
        scratch_shapes=[pltpu.VMEM((HG_HEADS, HG_DIM, Bs), f32)],
        compiler_params=_params(("arbitrary",)),
        name="hgrn_decode",
    )(h4, lb, gn, state)


def _split_w_in(w):
    idx = np.cumsum(SPLITS)[:-1]
    return jnp.split(w, [int(v) for v in idx], axis=1)


def kernel(x_prompt, x_sample, cache_kv, page_table, state_kv_win, state_hgrn, state_conv, rel_table, hg_lb_logits,
           norm1_g, w_in, q_norm_g, k_norm_g, phi_k_w1, phi_k_w2, phi_v_w1, phi_v_w2, hg_norm_g, w_nsa_out, w_hg_out,
           w_o, norm2_g, w_gate, w_up, conv_w, conv_b, w_down):
    assert w_in.shape[0] == 1, "one layer"
    Bp, L, _ = x_prompt.shape
    Bs = x_sample.shape[0]
    n_pages = page_table.shape[1]
    past = n_pages * PAGE_SIZE
    n_win = state_kv_win.shape[2]
    assert x_sample.shape[1] == 1 and n_win == WINDOW and past % (2 * CMP_BLOCK) == 0 and L % LANES == 0

    wq, wkv, wgl, whq, whf, whi, whg, wmg = _split_w_in(w_in[0])
    wgl = jnp.pad(wgl, ((0, 0), (0, LANES - wgl.shape[1])))
    w_prompt = jnp.concatenate([wq, wkv, whq, whf, whi, whg, wmg, wgl], axis=1).astype(bf16)
    wq_h = wq.reshape(D_MODEL, NSA_HEADS, HEAD_DIM)
    zq = jnp.zeros_like(wq_h)
    in_g0 = (np.arange(NSA_HEADS) < NSA_GROUP)[None, :, None]
    wq_bd = jnp.where(in_g0, jnp.concatenate([wq_h, zq], axis=-1), jnp.concatenate([zq, wq_h], axis=-1))
    w_sample = jnp.concatenate([wq_bd.reshape(D_MODEL, NSA_HEADS * LANES), wkv, whq, whf, whi, whg, wmg, wgl],
                               axis=1).astype(bf16)
    g1 = norm1_g[0][None, :]
    gq = q_norm_g[0]
    gq_prompt = jnp.tile(gq, NSA_HEADS)[None, :]
    gq_sample = jnp.tile(gq, 2 * NSA_HEADS)[None, :]
    gk = [jnp.tile(k_norm_g[0, s], NSA_KV_HEADS)[None, :] for s in range(3)]

    def block_diag(mats):
        n = len(mats)
        lead = [(0, 0)] * (mats[0].ndim - 2)
        rows = [jnp.pad(m.astype(bf16), lead + [(0, 0), (t * HEAD_DIM, (n - 1 - t) * HEAD_DIM)])
                for t, m in enumerate(mats)]
        return jnp.concatenate(rows, axis=-2)

    w1bd = block_diag([phi_k_w1[0], phi_k_w1[0], phi_v_w1[0], phi_v_w1[0]]).reshape(CMP_BLOCK * 2 * LANES, 2 * LANES)
    w2bd = block_diag([phi_k_w2[0], phi_k_w2[0], phi_v_w2[0], phi_v_w2[0]])
    lb = jnp.cumsum(jax.nn.softmax(hg_lb_logits.astype(f32), axis=0), axis=0)[0][None, :]
    gn = hg_norm_g[0][None, :]
    ffn_w = (w_nsa_out[0].astype(bf16), w_hg_out[0].astype(bf16), w_o[0].astype(bf16), norm2_g[0][None, :],
             w_gate[0].astype(bf16), w_up[0].astype(bf16), conv_w[0], conv_b[0][None, :], w_down[0].astype(bf16))

    idx_c, idx_t = _prompt_bucket_tables(L)
    tc = _bias_tables(rel_table, jnp.asarray(idx_c))
    tt = _bias_tables(rel_table, jnp.asarray(idx_t))
    bias_dec = _bias_tables(rel_table, jnp.asarray(_decode_bucket_table(past, n_win))[None]).reshape(NSA_HEADS, -1)
    ex_prompt = jnp.asarray(_expand_np(L // SEL_BLOCK, L), dtype=bf16)
    ex_sample = jnp.asarray(_expand_np(past // SEL_BLOCK, past), dtype=bf16)

    xp = x_prompt.reshape(Bp * L, D_MODEL)
    qn, kv_t, win32, katt, h4, mgs, gates = _in_proj(xp, g1, w_prompt, gq_prompt, gk[1], gk[2],
                                                     qw=NSA_WIDTH, q_seg=HEAD_DIM, tm=512, seq_len=L)
    cmp = _compress_prompt(kv_t, w1bd, w2bd, gk[0])
    o_nsa = _nsa_prompt(qn, gates, katt, cmp, tc, tt, ex_prompt, Bp, L)
    o_hg, s_prompt = _hgrn_prompt(h4, lb, gn, Bp, L)
    tm_ffn = 512
    yp, gate_tails = _ffn(xp, o_nsa, o_hg.reshape(Bp * L, HG_WIDTH), mgs, *ffn_w, tm=tm_ffn, tf=D_FF // 2, seq_len=L)
    conv_p = gate_tails.reshape(Bp, L // tm_ffn, 2, D_FF)[:, -1]
    n_keep_win = min(WINDOW, L)
    win_p = jnp.swapaxes(win32[:, :, L - n_keep_win:], 1, 2).reshape(Bp, n_keep_win, 2, NSA_KV_HEADS, HEAD_DIM)

    xs = x_sample.reshape(Bs, D_MODEL)
    q_bd, kv32_s, win32_s, _, h4_s, mgs_s, gates_s = _in_proj(xs, g1, w_sample, gq_sample, gk[1], gk[2],
                                                              qw=NSA_HEADS * LANES, q_seg=LANES, tm=Bs)
    cache_t = jnp.swapaxes(cache_kv[0].reshape(cache_kv.shape[1], PAGE_SIZE, 4 * LANES), 1, 2)
    win_t = jnp.swapaxes(state_kv_win[0].reshape(Bs, n_win, 2 * LANES), 1, 2)
    o_nsa_s, win_out_t = _nsa_decode(cache_t, page_table, q_bd, kv32_s, win32_s, gates_s, win_t, w1bd, w2bd, gk[0],
                                     bias_dec, ex_sample)
    win_s = jnp.swapaxes(win_out_t, 1, 2)
    kv32 = jnp.swapaxes(kv_t, 1, 2)
    o_hg_s, s_sample = _hgrn_decode(h4_s, lb, gn, state_hgrn[0])
    ys, gate_s = _ffn(xs, o_nsa_s.reshape(Bs, NSA_WIDTH), o_hg_s.reshape(Bs, HG_WIDTH), mgs_s, *ffn_w,
                      tm=Bs, tf=256, conv_state=state_conv[0])
    conv_s = jnp.stack([state_conv[0][:, 1], gate_s], axis=1)

    kvh = (NSA_KV_HEADS, HEAD_DIM)
    return (yp.reshape(Bp, L, D_MODEL), ys.reshape(Bs, 1, D_MODEL),
            kv32.reshape((1, Bp, L, 4) + kvh), kv32_s.reshape((1, Bs, 1, 4) + kvh),
            win_p[None], win_s.reshape((1, Bs, n_win, 2) + kvh),
            s_prompt[None], s_sample[None], conv_p[None], conv_s[None])
```

```python
import functools
import math

import numpy as np
import jax
import jax.numpy as jnp
from jax import lax
from jax.experimental import pallas as pl
from jax.experimental.pallas import tpu as pltpu

f32, bf16, i32 = jnp.float32, jnp.bfloat16, jnp.int32

D_MODEL = 1024
PAGE_SIZE = 128
NSA_HEADS, NSA_KV_HEADS, NSA_GROUP, HEAD_DIM = 8, 2, 4, 64
CMP_BLOCK, SEL_BLOCK, TOP_N, WINDOW = 32, 64, 16, 512
SCALE = HEAD_DIM ** -0.5
NSA_WIDTH = NSA_HEADS * HEAD_DIM
KV_WIDTH = NSA_KV_HEADS * HEAD_DIM
HG_HEADS, HG_DIM = 4, 128
HG_WIDTH = HG_HEADS * HG_DIM
HG_SCALE = HG_DIM ** -0.5
HG_BLOCK = 16
SEL_CLASSES = 8
SUBLANES = 8
FFN_SLICE = 768
REL_BUCKETS, REL_MAX_DIST = 32, 128
D_FF = 2816
EPS = 1e-6
SPLITS = (NSA_WIDTH, 6 * KV_WIDTH, 3 * NSA_HEADS, HG_WIDTH, HG_WIDTH, HG_WIDTH, HG_WIDTH, 2 * D_MODEL)
LANES = 128
NEG = -1e30
FORCED_SCORE, INVALID_SCORE = 8.0, -1.0
VMEM_LIMIT = 56 * 1024 * 1024

NT = (((1,), (1,)), ((), ()))
TN = (((0,), (0,)), ((), ()))


def _params(sem):
    return pltpu.CompilerParams(dimension_semantics=sem, vmem_limit_bytes=VMEM_LIMIT)


def _bucket_np(dist):
    n = np.maximum(dist, 0)
    exact = REL_BUCKETS // 2
    ratio = np.log(np.maximum(n, 1).astype(np.float32) / np.float32(exact)) / np.float32(math.log(REL_MAX_DIST / exact))
    big = exact + (ratio.astype(np.float32) * np.float32(REL_BUCKETS - exact)).astype(np.int32)
    return np.where(n < exact, n, np.minimum(big, REL_BUCKETS - 1)).astype(np.int32)


def _cmp_perm(n_cmp):
    half = n_cmp // 2
    c = np.arange(n_cmp)
    return np.where(c < half, 2 * c, 2 * (c - half) + 1)


def _prompt_bucket_tables(L):
    nqb = L // SEL_BLOCK
    n_cmp = L // CMP_BLOCK
    qi = np.arange(SEL_BLOCK)
    c_end = (_cmp_perm(n_cmp) + 1) * CMP_BLOCK - 1
    idx_c = np.stack([_bucket_np((i * SEL_BLOCK + qi)[:, None] - c_end[None, :]) for i in range(nqb)])
    kj = np.arange(SEL_BLOCK)
    def tile(delta):
        return _bucket_np(delta * SEL_BLOCK + qi[:, None] - kj[None, :])
    idx_t = np.stack([np.concatenate([tile(e), tile(e - 1)], axis=1) for e in range(5)])
    return idx_c.astype(np.int32), idx_t.astype(np.int32)


def _decode_bucket_table(past_len, n_win):
    n_cmp = past_len // CMP_BLOCK
    c_end = (_cmp_perm(n_cmp) + 1) * CMP_BLOCK - 1
    d_c = np.zeros((LANES,), np.int64)
    d_c[:n_cmp] = past_len - c_end
    d_s = past_len - np.arange(past_len)
    d_w = n_win - np.arange(n_win)
    d_new = np.zeros((LANES,), np.int64)
    return _bucket_np(np.concatenate([d_s, d_w, d_c, d_new]))[None, :].astype(np.int32)


def _expand_np(n_blocks, n_keys):
    e = (np.arange(n_keys)[None, :] // SEL_BLOCK == np.arange(n_blocks)[:, None])
    return e.astype(np.float32)


def _seg_rmsnorm(x, gain, seg):
    outs = []
    for t in range(x.shape[1] // LANES):
        xt = x[:, t * LANES:(t + 1) * LANES]
        sq = xt * xt
        if seg == LANES:
            r = lax.rsqrt(jnp.sum(sq, axis=-1, keepdims=True) * (1.0 / HEAD_DIM) + EPS)
        else:
            lo = lax.broadcasted_iota(i32, xt.shape, 1) < HEAD_DIM
            s_lo = jnp.sum(jnp.where(lo, sq, 0.0), axis=-1, keepdims=True)
            s_hi = jnp.sum(jnp.where(lo, 0.0, sq), axis=-1, keepdims=True)
            r = jnp.where(lo, lax.rsqrt(s_lo * (1.0 / HEAD_DIM) + EPS), lax.rsqrt(s_hi * (1.0 / HEAD_DIM) + EPS))
        outs.append(xt * r)
    y = outs[0] if len(outs) == 1 else jnp.concatenate(outs, axis=1)
    return y * gain


def _rmsnorm_rows(x, gain):
    return x * lax.rsqrt(jnp.mean(x * x, axis=-1, keepdims=True) + EPS) * gain


def _softmax_parts(s, mask):
    s = jnp.where(mask, s, NEG)
    m = jnp.max(s, axis=-1, keepdims=True)
    e = jnp.where(mask, jnp.exp(s - m), 0.0)
    return e, jnp.sum(e, axis=-1, keepdims=True)


def _bias_table_kernel(tab_ref, idx_ref, out_ref):
    idx = idx_ref[...]
    rows = idx.shape[0]
    for h in range(NSA_HEADS):
        acc = jnp.zeros(idx.shape, f32)
        for b in range(REL_BUCKETS):
            acc = jnp.where(idx == b, tab_ref[b, h], acc)
        r = h % NSA_GROUP
        out_ref[h // NSA_GROUP, r * rows:(r + 1) * rows, :] = acc


def _bias_tables(rel_table, idx):
    n, rows, w = idx.shape
    return pl.pallas_call(
        _bias_table_kernel,
        out_shape=jax.ShapeDtypeStruct((n, NSA_KV_HEADS, NSA_GROUP * rows, w), f32),
        grid=(n,),
        in_specs=[pl.BlockSpec(memory_space=pltpu.SMEM),
                  pl.BlockSpec((None, rows, w), lambda i: (i, 0, 0))],
        out_specs=pl.BlockSpec((None, NSA_KV_HEADS, NSA_GROUP * rows, w), lambda i: (i, 0, 0, 0)),
        compiler_params=_params(("arbitrary",)),
        name="bias_tables",
    )(rel_table, idx)


def _in_proj_kernel(x_ref, g1_ref, w_ref, gq_ref, gk1_ref, gk2_ref,
                    q_ref, kv_ref, win_ref, katt_ref, h4_ref, mg_ref, gl_ref, *, qw, q_seg, kv_feature_major):
    x = x_ref[...]
    xn = _rmsnorm_rows(x, g1_ref[...]).astype(bf16)

    def proj(c0, width):
        return jnp.dot(xn, w_ref[:, c0:c0 + width], preferred_element_type=f32)

    q_ref[...] = (_seg_rmsnorm(proj(0, qw), gq_ref[...], q_seg) * SCALE).astype(bf16)
    c = qw
    kv = proj(c, 6 * KV_WIDTH)
    c += 6 * KV_WIDTH
    k_sel = _seg_rmsnorm(kv[:, 2 * KV_WIDTH:3 * KV_WIDTH], gk1_ref[...], HEAD_DIM)
    k_win = _seg_rmsnorm(kv[:, 4 * KV_WIDTH:5 * KV_WIDTH], gk2_ref[...], HEAD_DIM)
    v_sel = kv[:, 3 * KV_WIDTH:4 * KV_WIDTH]
    v_win = kv[:, 5 * KV_WIDTH:6 * KV_WIDTH]
    if kv_feature_major:
        kv_ref[0:2 * KV_WIDTH, :] = kv[:, 0:2 * KV_WIDTH].T
        kv_ref[2 * KV_WIDTH:3 * KV_WIDTH, :] = k_sel.T
        kv_ref[3 * KV_WIDTH:4 * KV_WIDTH, :] = v_sel.T
    else:
        kv_ref[:, 0:2 * KV_WIDTH] = kv[:, 0:2 * KV_WIDTH]
        kv_ref[:, 2 * KV_WIDTH:3 * KV_WIDTH] = k_sel
        kv_ref[:, 3 * KV_WIDTH:4 * KV_WIDTH] = v_sel
    if kv_feature_major:
        win_ref[0:KV_WIDTH, :] = k_win.T
        win_ref[KV_WIDTH:2 * KV_WIDTH, :] = v_win.T
    else:
        win_ref[:, 0:KV_WIDTH] = k_win
        win_ref[:, KV_WIDTH:2 * KV_WIDTH] = v_win
    katt_ref[:, 0:KV_WIDTH] = k_sel.astype(bf16)
    katt_ref[:, KV_WIDTH:2 * KV_WIDTH] = v_sel.astype(bf16)
    katt_ref[:, 2 * KV_WIDTH:3 * KV_WIDTH] = k_win.astype(bf16)
    katt_ref[:, 3 * KV_WIDTH:4 * KV_WIDTH] = v_win.astype(bf16)
    h4_ref[...] = proj(c, 4 * HG_WIDTH)
    c += 4 * HG_WIDTH
    mg_ref[...] = jax.nn.sigmoid(proj(c, 2 * D_MODEL))
    c += 2 * D_MODEL
    gl_ref[...] = jax.nn.sigmoid(proj(c, LANES))


def _in_proj(x2d, g1, w, gq, gk1, gk2, *, qw, q_seg, tm, seq_len=None):
    n = x2d.shape[0]
    cw = w.shape[1]
    row = lambda width: pl.BlockSpec((tm, width), lambda i: (i, 0))
    full = lambda a: pl.BlockSpec(a.shape, lambda i: (0,) * a.ndim)
    widths = (qw, 4 * KV_WIDTH, 2 * KV_WIDTH, 4 * KV_WIDTH, 4 * HG_WIDTH, 2 * D_MODEL, LANES)
    dtypes = (bf16, f32, f32, bf16, f32, f32, f32)
    out_shape = [jax.ShapeDtypeStruct((n, wd), dt) for wd, dt in zip(widths, dtypes)]
    out_specs = [row(wd) for wd in widths]
    if seq_len is not None:
        tiles = seq_len // tm
        for o_, wd in ((1, 4 * KV_WIDTH), (2, 2 * KV_WIDTH)):
            out_shape[o_] = jax.ShapeDtypeStruct((n // seq_len, wd, seq_len), f32)
            out_specs[o_] = pl.BlockSpec((None, wd, tm), lambda i: (i // tiles, 0, i % tiles))
    return pl.pallas_call(
        functools.partial(_in_proj_kernel, qw=qw, q_seg=q_seg, kv_feature_major=seq_len is not None),
        out_shape=tuple(out_shape),
        grid=(n // tm,),
        in_specs=[row(D_MODEL), full(g1),
                  pl.BlockSpec((D_MODEL, cw), lambda i: (0, 0), pipeline_mode=pl.Buffered(1)),
                  full(gq), full(gk1), full(gk2)],
        out_specs=tuple(out_specs),
        compiler_params=_params(("arbitrary",)),
        name="in_proj",
    )(x2d, g1, w, gq, gk1, gk2)


CMP_SLOT = CMP_BLOCK + 1


def _cmp_staging_rows(n_cmp):
    return -(-n_cmp * CMP_SLOT // 8) * 8


def _stage_cmp_chunk(src_ref, chunk, kc_t, vc_t, n_cmp):
    per_chunk = LANES // CMP_BLOCK
    for s, t in enumerate((kc_t.T, vc_t.T)):
        for nl in range(per_chunk):
            n = chunk * per_chunk + nl
            slot = n // 2 + (n % 2) * (n_cmp // 2)
            src_ref[s, slot * CMP_SLOT:slot * CMP_SLOT + CMP_BLOCK, :] = t[nl * CMP_BLOCK:(nl + 1) * CMP_BLOCK]


def _compress(src_refs, w1_ref, w2_ref, gk0, n_cmp):
    acc = jnp.zeros((len(src_refs) * n_cmp, 2 * LANES), f32)
    per_dot = 8
    for j0 in range(0, CMP_BLOCK, per_dot):
        xs = jnp.concatenate(
            [jnp.concatenate([src[s, pl.ds(j, n_cmp, stride=CMP_SLOT), :].astype(bf16)
                              for j in range(j0, j0 + per_dot) for s in range(2)], axis=1)
             for src in src_refs], axis=0)
        w = w1_ref[j0 * 2 * LANES:(j0 + per_dot) * 2 * LANES, :]
        acc = acc + jnp.dot(xs, w, preferred_element_type=f32)
    hmid = jax.nn.gelu(acc).astype(bf16)
    y = jnp.dot(hmid, w2_ref[...], preferred_element_type=f32)
    ck = _seg_rmsnorm(y[:, 0:LANES], gk0, HEAD_DIM)
    return jnp.concatenate([ck, y[:, LANES:2 * LANES]], axis=1)


def _compress_prompt_kernel(kv_ref, w1_ref, w2_ref, gk0_ref, out_ref, src_ref, *, n_cmp):
    for c in range(kv_ref.shape[1] // LANES):
        cols = slice(c * LANES, (c + 1) * LANES)
        _stage_cmp_chunk(src_ref, c, kv_ref[0:LANES, cols], kv_ref[LANES:2 * LANES, cols], n_cmp)
    out_ref[...] = _compress([src_ref], w1_ref, w2_ref, gk0_ref[...], n_cmp).astype(bf16)


def _compress_prompt(kv_t, w1bd, w2bd, gk0):
    B, _, L = kv_t.shape
    n_cmp = L // CMP_BLOCK
    return pl.pallas_call(
        functools.partial(_compress_prompt_kernel, n_cmp=n_cmp),
        out_shape=jax.ShapeDtypeStruct((B, n_cmp, 2 * LANES), bf16),
        grid=(B,),
        in_specs=[pl.BlockSpec((None, 2 * LANES, L), lambda b: (b, 0, 0)),
                  pl.BlockSpec(w1bd.shape, lambda b: (0, 0)),
                  pl.BlockSpec(w2bd.shape, lambda b: (0, 0)),
                  pl.BlockSpec(gk0.shape, lambda b: (0, 0))],
        out_specs=pl.BlockSpec((None, n_cmp, 2 * LANES), lambda b: (b, 0, 0)),
        scratch_shapes=[pltpu.VMEM((2, _cmp_staging_rows(n_cmp), LANES), f32)],
        compiler_params=_params(("arbitrary",)),
        name="compress_prompt",
    )(kv_t, w1bd, w2bd, gk0)


def _select_blocks_t(score, valid, n_keep):
    nb = score.shape[0]
    jj = lax.broadcasted_iota(i32, score.shape, 0)
    cnt = jnp.zeros(score.shape, f32)
    for k in range(nb):
        sk = score[k:k + 1, :]
        better = (sk > score) | ((sk == score) & (jj > k))
        cnt = cnt + jnp.where(better, 1.0, 0.0)
    return (cnt < n_keep) & valid


def _sel_tile_classes(n_tiles):
    return sorted({-(-n_tiles * c // SEL_CLASSES) for c in range(1, SEL_CLASSES + 1)})


def _nsa_prompt_kernel(q_ref, gate_ref, katt_ref, cmp_ref, tc_ref, tt_ref, ex_ref, o_ref,
                       selx_ref, s_ref, os_ref, part_ref, gsel_ref, *, L):
    i = pl.program_id(1)
    n_cmp = L // CMP_BLOCK
    n_sel = L // SEL_BLOCK
    n_tiles = L // LANES
    qb = SEL_BLOCK
    rows = NSA_GROUP * qb
    units = [(sq, g) for sq in range(q_ref.shape[0]) for g in range(NSA_KV_HEADS)]
    nu = range(len(units))
    row64 = lax.broadcasted_iota(i32, (qb, LANES), 0)
    lane = lax.broadcasted_iota(i32, (qb, LANES), 1)
    qpos = i * qb + row64

    qs = [jnp.concatenate(
        [q_ref[sq, :, (NSA_GROUP * g + r) * HEAD_DIM:(NSA_GROUP * g + r + 1) * HEAD_DIM] for r in range(NSA_GROUP)],
        axis=0) for sq, g in units]

    def attend(k_off, v_off, tile0, n_t, mask_fn):
        chunk = 4

        def key_rows(t0, count, off, u):
            sq, g = units[u]
            r0 = pl.multiple_of((tile0 + t0) * LANES, LANES)
            return katt_ref[sq, pl.ds(r0, count * LANES), off + g * HEAD_DIM:off + (g + 1) * HEAD_DIM]

        def scores(t0, count, mruns):
            dots = [lax.dot_general(qs[u], key_rows(t0, count, k_off, u), NT, preferred_element_type=f32)
                    for u in nu]
            out = list(mruns)
            for j in range(count):
                ta = tile0 + t0 + j
                bias_idx = jnp.clip(i - 2 * ta, 0, 4)
                kpos = ta * LANES + lane
                for u in nu:
                    s = dots[u][:, j * LANES:(j + 1) * LANES] + tt_ref[bias_idx, units[u][1]]
                    msk = mask_fn(u, ta, kpos)
                    s = jnp.where(msk[None], s.reshape(NSA_GROUP, qb, LANES), NEG).reshape(rows, LANES)
                    s_ref[u, t0 + j] = s
                    out[u] = jnp.maximum(out[u], s)
            return tuple(out)

        def values(t0, count, carry, ms):
            ls, accs = list(carry[0]), list(carry[1])
            ps = []
            for u in nu:
                pj = [jnp.exp(s_ref[u, t0 + j] - ms[u]) for j in range(count)]
                for p in pj:
                    ls[u] = ls[u] + p
                ps.append(jnp.concatenate([p.astype(bf16) for p in pj], axis=1))
            for u in nu:
                accs[u] = accs[u] + jnp.dot(ps[u], key_rows(t0, count, v_off, u), preferred_element_type=f32)
            return tuple(ls), tuple(accs)

        def over_chunks(fn, carry):
            for t0 in range(0, n_t, chunk):
                carry = fn(t0, min(chunk, n_t - t0), carry)
            return carry

        mruns = over_chunks(scores, tuple(jnp.full((rows, LANES), NEG, f32) for _ in nu))
        ms = [jnp.max(m, axis=-1, keepdims=True) for m in mruns]
        init = (tuple(jnp.zeros((rows, LANES), f32) for _ in nu),
                tuple(jnp.zeros((rows, HEAD_DIM), f32) for _ in nu))
        ls, accs = over_chunks(lambda t0, count, cr: values(t0, count, cr, ms), init)
        return [accs[u] / jnp.sum(ls[u], axis=-1, keepdims=True) for u in nu]

    n_wt = min(WINDOW // LANES + 1, n_tiles)
    w0 = jnp.clip((i - WINDOW // qb) // 2, 0, n_tiles - n_wt)
    o_w = attend(2 * KV_WIDTH, 3 * KV_WIDTH, w0, n_wt,
                 lambda u, ta, kpos: (kpos <= qpos) & (qpos - kpos <= WINDOW))

    o_c, imp_t = [], []
    col = lax.broadcasted_iota(i32, (rows, n_cmp), 1)
    rq = lax.broadcasted_iota(i32, (rows, n_cmp), 0) & (qb - 1)
    blk = jnp.where(col < n_cmp // 2, 2 * col, 2 * col - (n_cmp - 1))
    vis = (blk + 1) * CMP_BLOCK - 1 <= i * qb + rq
    for u, (sq, g) in enumerate(units):
        ck = cmp_ref[sq, :, g * HEAD_DIM:(g + 1) * HEAD_DIM]
        cv = cmp_ref[sq, :, LANES + g * HEAD_DIM:LANES + (g + 1) * HEAD_DIM]
        s = lax.dot_general(qs[u], ck, NT, preferred_element_type=f32) + tc_ref[g]
        e, den = _softmax_parts(s, vis)
        p = e / jnp.where(den > 0, den, 1.0)
        o_c.append(jnp.dot(p.astype(bf16), cv, preferred_element_type=f32))
        ps_t = (p[0:qb] + p[qb:2 * qb] + p[2 * qb:3 * qb] + p[3 * qb:4 * qb]).T
        imp_t.append(ps_t[0:n_sel] + ps_t[n_sel:2 * n_sel])

    imp = jnp.concatenate(imp_t, axis=1)
    jj = lax.broadcasted_iota(i32, imp.shape, 0)
    valid = jj <= i
    forced = valid & ((jj == 0) | (jj >= i - 1))
    score = jnp.where(forced, FORCED_SCORE, jnp.where(valid, imp, INVALID_SCORE))
    sel = _select_blocks_t(score, valid, min(TOP_N, n_sel))
    selx = lax.dot_general(jnp.where(sel, 1.0, 0.0).astype(bf16), ex_ref[...], TN, preferred_element_type=f32)
    for u in nu:
        for t in range(n_tiles):
            selx_ref[u, t] = selx[u * qb:(u + 1) * qb, t * LANES:(t + 1) * LANES]

    for u, (sq, g) in enumerate(units):
        gates = gate_ref[sq]
        for r in range(NSA_GROUP):
            h = NSA_GROUP * g + r
            hs = slice(h * HEAD_DIM, (h + 1) * HEAD_DIM)
            sl = slice(r * qb, (r + 1) * qb)
            part_ref[sq, :, hs] = gates[:, 3 * h:3 * h + 1] * o_c[u][sl] + gates[:, 3 * h + 2:3 * h + 3] * o_w[u][sl]
            gsel_ref[sq, :, hs] = jnp.broadcast_to(gates[:, 3 * h + 1:3 * h + 2], (qb, HEAD_DIM))

    need = i // 2 + 1
    prev = 0
    for n_t in _sel_tile_classes(n_tiles):
        @pl.when((need > prev) & (need <= n_t))
        def _(n_t=n_t):
            o_s = attend(0, KV_WIDTH, 0, n_t, lambda u, ta, kpos: (selx_ref[u, ta] > 0.5) & (kpos <= qpos))
            for u in nu:
                os_ref[u] = o_s[u]
        prev = n_t

    for u, (sq, g) in enumerate(units):
        o_s = os_ref[u]
        for r in range(NSA_GROUP):
            hs = slice((NSA_GROUP * g + r) * HEAD_DIM, (NSA_GROUP * g + r + 1) * HEAD_DIM)
            o = part_ref[sq, :, hs] + gsel_ref[sq, :, hs] * o_s[r * qb:(r + 1) * qb]
            o_ref[sq, :, hs] = o.astype(bf16)


def _nsa_prompt(qn, gates, katt, cmp, tc, tt, expand, B, L, nseq=2):
    nqb = L // SEL_BLOCK
    qb = SEL_BLOCK
    n_tiles = L // LANES
    rows = NSA_GROUP * qb
    nseq = math.gcd(nseq, B)
    n_units = nseq * NSA_KV_HEADS
    seq3 = lambda a: a.reshape(B, L, a.shape[-1])
    return pl.pallas_call(
        functools.partial(_nsa_prompt_kernel, L=L),
        out_shape=jax.ShapeDtypeStruct((B, L, NSA_WIDTH), bf16),
        grid=(B // nseq, nqb),
        in_specs=[pl.BlockSpec((nseq, qb, NSA_WIDTH), lambda b, i: (b, i, 0)),
                  pl.BlockSpec((nseq, qb, LANES), lambda b, i: (b, i, 0)),
                  pl.BlockSpec((nseq, L, 4 * KV_WIDTH), lambda b, i: (b, 0, 0)),
                  pl.BlockSpec((nseq, L // CMP_BLOCK, 2 * LANES), lambda b, i: (b, 0, 0)),
                  pl.BlockSpec((None,) + tc.shape[1:], lambda b, i: (i, 0, 0, 0)),
                  pl.BlockSpec(tt.shape, lambda b, i: (0, 0, 0, 0)),
                  pl.BlockSpec(expand.shape, lambda b, i: (0, 0))],
        out_specs=pl.BlockSpec((nseq, qb, NSA_WIDTH), lambda b, i: (b, i, 0)),
        scratch_shapes=[pltpu.VMEM((n_units, n_tiles, qb, LANES), f32),
                        pltpu.VMEM((n_units, n_tiles, rows, LANES), f32),
                        pltpu.VMEM((n_units, rows, HEAD_DIM), f32),
                        pltpu.VMEM((nseq, qb, NSA_WIDTH), f32), pltpu.VMEM((nseq, qb, NSA_WIDTH), f32)],
        compiler_params=_params(("arbitrary", "arbitrary")),
        name="nsa_prompt",
    )(seq3(qn), seq3(gates), seq3(katt), cmp, tc, tt, expand).reshape(B * L, NSA_WIDTH)


def _hgrn_gates(hf, lb):
    f = lb + (1.0 - lb) * jax.nn.sigmoid(hf)
    return f, 1.0 - f, jnp.log(f)


def _hgrn_out(o, gn, hgate):
    return _rmsnorm_rows(o, gn) * jax.nn.silu(hgate)


def _hgrn_prompt_kernel(h4_ref, lb_ref, gn_ref, o_ref, s_out_ref, st_ref, *, tc):
    c = pl.program_id(1)
    nb = HG_BLOCK
    nseq = h4_ref.shape[0]

    @pl.when(c == 0)
    def _():
        st_ref[...] = jnp.zeros_like(st_ref)

    t_row = lax.broadcasted_iota(i32, (nb, HG_DIM), 0)
    t_col = lax.broadcasted_iota(i32, (nb, 1), 0)
    gn = gn_ref[...]
    chains = [(sq, h) for sq in range(nseq) for h in range(HG_HEADS)]

    def cumsum_rows(x):
        shift = 1
        while shift < nb:
            x = x + jnp.where(t_row >= shift, pltpu.roll(x, shift, axis=0), 0.0)
            shift *= 2
        return x

    def block(bi, _):
        r0 = pl.multiple_of(bi * nb, nb)

        def sl(sq, h, part):
            return h4_ref[sq, pl.ds(r0, nb), part * HG_WIDTH + h * HG_DIM:part * HG_WIDTH + (h + 1) * HG_DIM]

        qs, ks, vs, bs, sts, os_ = [], [], [], [], [], []
        for sq, h in chains:
            _, k, glog = _hgrn_gates(sl(sq, h, 1), lb_ref[:, h * HG_DIM:(h + 1) * HG_DIM])
            qs.append(sl(sq, h, 0) * HG_SCALE)
            ks.append(k)
            vs.append(sl(sq, h, 2))
            bs.append(cumsum_rows(glog))
        for c_, (sq, h) in enumerate(chains):
            sts.append(st_ref[sq, h])
            os_.append(lax.dot_general((qs[c_] * jnp.exp(bs[c_])).astype(bf16), sts[c_].astype(bf16), NT,
                                       preferred_element_type=f32))
        for c_ in range(len(chains)):
            q, k, v, b, o = qs[c_], ks[c_], vs[c_], bs[c_], os_[c_]
            parts = [o[r0_:r0_ + SUBLANES] for r0_ in range(0, nb, SUBLANES)]
            for s in range(nb):
                for gi in range(s // SUBLANES, nb // SUBLANES):
                    rs = slice(gi * SUBLANES, (gi + 1) * SUBLANES)
                    w = q[rs] * k[s:s + 1] * jnp.exp(b[rs] - b[s:s + 1])
                    a = jnp.sum(w, axis=-1, keepdims=True)
                    if gi == s // SUBLANES:
                        a = jnp.where(t_col[rs] >= s, a, 0.0)
                    parts[gi] = parts[gi] + a * v[s:s + 1]
            os_[c_] = jnp.concatenate(parts, axis=0)
        for c_, (sq, h) in enumerate(chains):
            b_last = bs[c_][nb - 1:nb]
            kt = ks[c_] * jnp.exp(b_last - bs[c_])
            upd = lax.dot_general(vs[c_].astype(bf16), kt.astype(bf16), TN, preferred_element_type=f32)
            st_ref[sq, h] = jnp.exp(b_last) * sts[c_] + upd
            o_ref[sq, pl.ds(r0, nb), h * HG_DIM:(h + 1) * HG_DIM] = _hgrn_out(os_[c_], gn, sl(sq, h, 3)).astype(bf16)
        return 0

    lax.fori_loop(0, tc // nb, block, 0, unroll=2)

    @pl.when(c == pl.num_programs(1) - 1)
    def _():
        for sq in range(nseq):
            for h in range(HG_HEADS):
                s_out_ref[sq, h] = st_ref[sq, h].T


def _hgrn_prompt(h4, lb, gn, B, L, tc=256, nseq=4):
    nc = L // tc
    nseq = math.gcd(nseq, B)
    return pl.pallas_call(
        functools.partial(_hgrn_prompt_kernel, tc=tc),
        out_shape=(jax.ShapeDtypeStruct((B, L, HG_WIDTH), bf16),
                   jax.ShapeDtypeStruct((B, HG_HEADS, HG_DIM, HG_DIM), f32)),
        grid=(B // nseq, nc),
        in_specs=[pl.BlockSpec((nseq, tc, 4 * HG_WIDTH), lambda b, c: (b, c, 0)),
                  pl.BlockSpec(lb.shape, lambda b, c: (0, 0)),
                  pl.BlockSpec(gn.shape, lambda b, c: (0, 0))],
        out_specs=(pl.BlockSpec((nseq, tc, HG_WIDTH), lambda b, c: (b, c, 0)),
                   pl.BlockSpec((nseq, HG_HEADS, HG_DIM, HG_DIM), lambda b, c: (b, 0, 0, 0))),
        scratch_shapes=[pltpu.VMEM((nseq, HG_HEADS, HG_DIM, HG_DIM), f32)],
        compiler_params=_params(("arbitrary", "arbitrary")),
        name="hgrn_prompt",
    )(h4.reshape(B, L, 4 * HG_WIDTH), lb, gn)


def _ffn_kernel(*refs, decode, tiles_per_seq):
    if decode:
        (x_ref, on_ref, oh_ref, mg_ref, wn_ref, wh_ref, wo_ref, g2_ref, wg_ref, wu_ref, cw_ref, cb_ref, wd_ref,
         cs0_ref, cs1_ref, y_ref, gate_out_ref, x1_ref, xn2_ref, acc_ref) = refs
    else:
        (x_ref, on_ref, oh_ref, mg_ref, wn_ref, wh_ref, wo_ref, g2_ref, wg_ref, wu_ref, cw_ref, cb_ref, wd_ref,
         y_ref, gate_out_ref, x1_ref, xn2_ref, acc_ref, carry_ref) = refs
    i = pl.program_id(0)
    fi = pl.program_id(1)
    tm = x_ref.shape[0]

    @pl.when(fi == 0)
    def _():
        y_a = jnp.dot(on_ref[...], wn_ref[...], preferred_element_type=f32)
        y_b = jnp.dot(oh_ref[...], wh_ref[...], preferred_element_type=f32)
        mg = mg_ref[...]
        merged = mg[:, 0:D_MODEL] * y_a + mg[:, D_MODEL:2 * D_MODEL] * y_b
        x1 = x_ref[...] + jnp.dot(merged.astype(bf16), wo_ref[...], preferred_element_type=f32)
        x1_ref[...] = x1
        xn2_ref[...] = _rmsnorm_rows(x1, g2_ref[...]).astype(bf16)
        acc_ref[...] = jnp.zeros_like(acc_ref)

    xn2 = xn2_ref[...]
    tf = wg_ref.shape[1]
    if not decode:
        @pl.when(i % tiles_per_seq == 0)
        def _():
            carry_ref[fi] = jnp.zeros(carry_ref.shape[1:], f32)

    acc = acc_ref[...]
    for a in range(0, tf, FFN_SLICE):
        cs = slice(a, min(a + FFN_SLICE, tf))
        gate = jnp.dot(xn2, wg_ref[:, cs], preferred_element_type=f32)
        up = jnp.dot(xn2, wu_ref[:, cs], preferred_element_type=f32)
        cw = cw_ref[:, cs]
        if decode:
            prev2, prev1 = cs0_ref[:, cs], cs1_ref[:, cs]
            gate_out_ref[:, cs] = gate
        else:
            carry = carry_ref[fi, :, cs]
            rid = lax.broadcasted_iota(i32, gate.shape, 0)
            prev1 = jnp.where(rid == 0, carry[7:8], pltpu.roll(gate, 1, axis=0))
            prev2 = jnp.where(rid == 0, carry[6:7], jnp.where(rid == 1, carry[7:8], pltpu.roll(gate, 2, axis=0)))
            carry_ref[fi, 6:8, cs] = gate[tm - 2:tm]
            gate_out_ref[:, cs] = gate[tm - 2:tm]
        conv = cb_ref[:, cs] + cw[0:1] * prev2 + cw[1:2] * prev1 + cw[2:3] * gate
        hmid = (jax.nn.silu(conv) * up).astype(bf16)
        acc = acc + jnp.dot(hmid, wd_ref[cs, :], preferred_element_type=f32)
    acc_ref[...] = acc

    @pl.when(fi == pl.num_programs(1) - 1)
    def _():
        y_ref[...] = x1_ref[...] + acc_ref[...]


def _ffn(x2d, o_nsa, o_hg, mgs, wn, wh, wo, g2, wg, wu, cw, cb, wd, *, tm, tf, seq_len=None, conv_state=None):
    n = x2d.shape[0]
    decode = conv_state is not None
    nf = D_FF // tf
    row = lambda width: pl.BlockSpec((tm, width), lambda i, f: (i, 0))
    full = lambda a: pl.BlockSpec(a.shape, lambda i, f: (0,) * a.ndim)
    in_specs = [row(D_MODEL), row(NSA_WIDTH), row(HG_WIDTH), row(2 * D_MODEL), full(wn), full(wh), full(wo), full(g2),
                pl.BlockSpec((D_MODEL, tf), lambda i, f: (0, f)), pl.BlockSpec((D_MODEL, tf), lambda i, f: (0, f)),
                pl.BlockSpec((3, tf), lambda i, f: (0, f)), pl.BlockSpec((1, tf), lambda i, f: (0, f)),
                pl.BlockSpec((tf, D_MODEL), lambda i, f: (f, 0))]
    args = [x2d, o_nsa, o_hg, mgs, wn, wh, wo, g2, wg, wu, cw, cb, wd]
    scratch = [pltpu.VMEM((tm, D_MODEL), f32), pltpu.VMEM((tm, D_MODEL), bf16), pltpu.VMEM((tm, D_MODEL), f32)]
    if decode:
        in_specs += [pl.BlockSpec((tm, tf), lambda i, f: (i, f))] * 2
        args += [conv_state[:, 0], conv_state[:, 1]]
        gate_shape = jax.ShapeDtypeStruct((n, D_FF), f32)
        gate_spec = pl.BlockSpec((tm, tf), lambda i, f: (i, f))
        tiles_per_seq = 1
    else:
        tiles_per_seq = seq_len // tm
        gate_shape = jax.ShapeDtypeStruct((n // tm, 2, D_FF), f32)
        gate_spec = pl.BlockSpec((None, 2, tf), lambda i, f: (i, 0, f))
        scratch.append(pltpu.VMEM((nf, 8, tf), f32))
    return pl.pallas_call(
        functools.partial(_ffn_kernel, decode=decode, tiles_per_seq=tiles_per_seq),
        out_shape=(jax.ShapeDtypeStruct((n, D_MODEL), f32), gate_shape),
        grid=(n // tm, nf),
        in_specs=in_specs,
        out_specs=(row(D_MODEL), gate_spec),
        scratch_shapes=scratch,
        compiler_params=_params(("arbitrary", "arbitrary")),
        name="ffn_decode" if decode else "ffn_prompt",
    )(*args)


def _nsa_decode_kernel(*refs, n_pages, n_win, nseq):
    page_refs = refs[1:1 + nseq * n_pages]
    (q_ref, kvn_ref, winn_ref, win_all_ref, gate_ref, win_ref, w1_ref, w2_ref, gk0_ref, bias_ref, ex_ref,
     o_ref, win_out_ref, src_ref, ksel_ref, vsel_ref, wcol_ref) = refs[1 + nseq * n_pages:]
    step = pl.program_id(0)
    past = n_pages * PAGE_SIZE
    n_cmp = past // CMP_BLOCK
    n_sel = past // SEL_BLOCK
    heads = NSA_HEADS
    samples = range(nseq)

    @pl.when(step == 0)
    def _():
        for t in range(2):
            wcol_ref[t * LANES:(t + 1) * LANES, :] = win_all_ref[:, t * LANES:(t + 1) * LANES].T

    for sq in samples:
        for k in range(n_pages):
            page = page_refs[sq * n_pages + k]
            cols = slice(k * PAGE_SIZE, (k + 1) * PAGE_SIZE)
            _stage_cmp_chunk(src_ref.at[sq], k, page[0:LANES, :], page[LANES:2 * LANES, :], n_cmp)
            ksel_ref[sq, :, cols] = page[2 * LANES:3 * LANES, :].astype(bf16)
            vsel_ref[sq, :, cols] = page[3 * LANES:4 * LANES, :].astype(bf16)

    hrow = lax.broadcasted_iota(i32, (heads, 1), 0)
    g0 = hrow < NSA_GROUP
    bias = bias_ref[...]
    b_s = bias[:, 0:past]
    b_w = bias[:, past:past + n_win]
    b_c = bias[:, past + n_win:past + n_win + n_cmp]
    b_new = bias[:, past + n_win + LANES:past + n_win + LANES + 1]
    qs = [q_ref[sq] for sq in samples]

    def pick(o2):
        return jnp.where(g0, o2[:, 0:HEAD_DIM], o2[:, HEAD_DIM:2 * HEAD_DIM])

    def scores(q, k_t, b_past, k_new):
        s_past = jnp.dot(q, k_t, preferred_element_type=f32) + b_past
        s_new = jnp.sum(q.astype(f32) * k_new, axis=-1, keepdims=True) + b_new
        return s_past, s_new

    def attend(s, mask, v_t, v_new):
        s_past, s_new = s
        sm = s_past if mask is None else jnp.where(mask, s_past, NEG)
        m = jnp.maximum(jnp.max(sm, axis=-1, keepdims=True), s_new)
        e = jnp.exp(sm - m)
        e_new = jnp.exp(s_new - m)
        den = jnp.sum(e, axis=-1, keepdims=True) + e_new
        o2 = lax.dot_general(e.astype(bf16), v_t, NT, preferred_element_type=f32) + e_new * v_new
        return pick(o2 / den)

    lane_b = lax.broadcasted_iota(i32, wcol_ref.shape, 1)
    lane_w = lax.broadcasted_iota(i32, win_ref.shape[1:], 1)
    o_w, s_sel = [], []
    for sq in samples:
        winn, kvn = winn_ref[sq], kvn_ref[sq]
        s_sel.append(scores(qs[sq], ksel_ref[sq], b_s, kvn[:, 2 * LANES:3 * LANES]))
        s_win = scores(qs[sq], win_ref[sq, 0:LANES, :].astype(bf16), b_w, winn[:, 0:LANES])
        o_w.append(attend(s_win, None, win_ref[sq, LANES:2 * LANES, :].astype(bf16), winn[:, LANES:2 * LANES]))
        new_col = jnp.sum(jnp.where(lane_b == step * nseq + sq, wcol_ref[...], 0.0), axis=-1, keepdims=True)
        win_out_ref[sq] = jnp.where(lane_w == n_win - 1, new_col, pltpu.roll(win_ref[sq], n_win - 1, axis=1))

    cmp = _compress([src_ref.at[sq] for sq in samples], w1_ref, w2_ref, gk0_ref[...], n_cmp)
    o_c, masks = [], []
    cur = past // SEL_BLOCK
    jrow = lax.broadcasted_iota(i32, (NSA_KV_HEADS, n_sel), 1)
    forced = (jrow == 0) | (jrow >= cur - 1)
    n_keep = min(TOP_N, n_sel + 1) - 1
    ri = lax.broadcasted_iota(i32, (n_sel, n_sel), 0)
    ci = lax.broadcasted_iota(i32, (n_sel, n_sel), 1)
    for sq in samples:
        ck = cmp[sq * n_cmp:(sq + 1) * n_cmp, 0:LANES].astype(bf16)
        cv = cmp[sq * n_cmp:(sq + 1) * n_cmp, LANES:2 * LANES].astype(bf16)
        s_c = lax.dot_general(qs[sq], ck, NT, preferred_element_type=f32) + b_c
        e_c, den_c = _softmax_parts(s_c, jnp.full(s_c.shape, True))
        p_c = e_c / jnp.where(den_c > 0, den_c, 1.0)
        o_c.append(pick(jnp.dot(p_c.astype(bf16), cv, preferred_element_type=f32)))

        ps = jnp.concatenate([jnp.sum(p_c[0:NSA_GROUP], axis=0, keepdims=True),
                              jnp.sum(p_c[NSA_GROUP:heads], axis=0, keepdims=True)], axis=0)
        imp = ps[:, 0:n_sel] + ps[:, n_sel:2 * n_sel]
        score_row = jnp.where(forced, FORCED_SCORE, imp)
        sels = []
        for g in range(NSA_KV_HEADS):
            rowb = jnp.broadcast_to(score_row[g:g + 1], (n_sel, n_sel))
            colb = jnp.sum(jnp.where(ri == ci, rowb, 0.0), axis=-1, keepdims=True)
            better = (colb > rowb) | ((colb == rowb) & (ri < ci))
            cnt = jnp.sum(jnp.where(better, 1.0, 0.0), axis=0, keepdims=True)
            sels.append(jnp.where(cnt < n_keep, 1.0, 0.0))
        sel = jnp.concatenate(sels, axis=0).astype(bf16)
        selx = jnp.dot(sel, ex_ref[...], preferred_element_type=f32)
        masks.append(jnp.where(g0, selx[0:1], selx[1:2]) > 0.5)

    for sq in samples:
        o_s = attend(s_sel[sq], masks[sq], vsel_ref[sq], kvn_ref[sq][:, 3 * LANES:4 * LANES])
        gates = gate_ref[sq]
        o_ref[sq] = (gates[:, 0:1] * o_c[sq] + gates[:, 1:2] * o_s + gates[:, 2:3] * o_w[sq]).astype(bf16)


def _nsa_decode(cache_t, page_table, q_bd, kv32, win32, gates, win_t, w1bd, w2bd, gk0, bias, expand, nseq=2):
    Bs, n_pages = page_table.shape
    n_win = win_t.shape[2]
    past = n_pages * PAGE_SIZE
    n_cmp = past // CMP_BLOCK
    nseq = math.gcd(nseq, Bs)
    per_b = lambda shape: pl.BlockSpec((nseq,) + shape, lambda b, pt: (b,) + (0,) * len(shape))
    const = lambda a: pl.BlockSpec(a.shape, lambda b, pt: (0,) * a.ndim)
    page_spec = lambda sq, k: pl.BlockSpec((None, 4 * LANES, PAGE_SIZE), lambda b, pt: (pt[b * nseq + sq, k], 0, 0))
    grid_spec = pltpu.PrefetchScalarGridSpec(
        num_scalar_prefetch=1,
        grid=(Bs // nseq,),
        in_specs=[page_spec(sq, k) for sq in range(nseq) for k in range(n_pages)] + [
            per_b((NSA_HEADS, LANES)), per_b((1, 4 * LANES)), per_b((1, 2 * LANES)), const(win32),
            per_b((NSA_HEADS, 3)), per_b((2 * LANES, n_win)), const(w1bd), const(w2bd), const(gk0), const(bias),
            const(expand)],
        out_specs=(per_b((NSA_HEADS, HEAD_DIM)), per_b((2 * LANES, n_win))),
        scratch_shapes=[pltpu.VMEM((nseq, 2, _cmp_staging_rows(n_cmp), LANES), f32),
                        pltpu.VMEM((nseq, LANES, past), bf16), pltpu.VMEM((nseq, LANES, past), bf16),
                        pltpu.VMEM((2 * LANES, Bs), f32)],
    )
    return pl.pallas_call(
        functools.partial(_nsa_decode_kernel, n_pages=n_pages, n_win=n_win, nseq=nseq),
        out_shape=(jax.ShapeDtypeStruct((Bs, NSA_HEADS, HEAD_DIM), bf16),
                   jax.ShapeDtypeStruct((Bs, 2 * LANES, n_win), f32)),
        grid_spec=grid_spec,
        compiler_params=_params(("arbitrary",)),
        name="nsa_decode",
    )(page_table, *([cache_t] * (nseq * n_pages)), q_bd.reshape(Bs, NSA_HEADS, LANES), kv32.reshape(Bs, 1, 4 * LANES),
      win32.reshape(Bs, 1, 2 * LANES), win32, gates[:, 0:3 * NSA_HEADS].reshape(Bs, NSA_HEADS, 3), win_t,
      w1bd, w2bd, gk0, bias, expand)


def _hgrn_decode_kernel(h4_ref, lb_ref, gn_ref, s_ref, o_ref, s_out_ref, ft_ref):
    b = pl.program_id(0)
    nbatch = h4_ref.shape[0]

    @pl.when(b == 0)
    def _():
        for h in range(HG_HEADS):
            f, _, _ = _hgrn_gates(h4_ref[:, HG_WIDTH + h * HG_DIM:HG_WIDTH + (h + 1) * HG_DIM],
                                  lb_ref[:, h * HG_DIM:(h + 1) * HG_DIM])
            ft_ref[h] = f.T

    lane = lax.broadcasted_iota(i32, (HG_DIM, nbatch), 1)
    gn = gn_ref[...]
    for sq in range(s_ref.shape[0]):
        sample = b * s_ref.shape[0] + sq
        row = h4_ref[pl.ds(sample, 1), :]
        for h in range(HG_HEADS):
            part = lambda k: row[:, k * HG_WIDTH + h * HG_DIM:k * HG_WIDTH + (h + 1) * HG_DIM]
            f_col = jnp.sum(jnp.where(lane == sample, ft_ref[h], 0.0), axis=-1, keepdims=True)
            s_new = f_col * s_ref[sq, h] + (1.0 - f_col) * part(2)
            s_out_ref[sq, h] = s_new
            q = jnp.broadcast_to(part(0) * HG_SCALE, (8, HG_DIM)).astype(bf16)
            o = jnp.dot(q, s_new.astype(bf16), preferred_element_type=f32)[0:1]
            o_ref[sq, :, h * HG_DIM:(h + 1) * HG_DIM] = _hgrn_out(o, gn, part(3)).astype(bf16)


def _hgrn_decode(h4, lb, gn, state, nseq=4):
    Bs = h4.shape[0]
    nseq = math.gcd(nseq, Bs)
    sspec = pl.BlockSpec((nseq, HG_HEADS, HG_DIM, HG_DIM), lambda b: (b, 0, 0, 0))
    return pl.pallas_call(
        _hgrn_decode_kernel,
        out_shape=(jax.ShapeDtypeStruct((Bs, 1, HG_WIDTH), bf16),
                   jax.ShapeDtypeStruct((Bs, HG_HEADS, HG_DIM, HG_DIM), f32)),
        grid=(Bs // nseq,),
        in_specs=[pl.BlockSpec(h4.shape, lambda b: (0, 0)), pl.BlockSpec(lb.shape, lambda b: (0, 0)),
                  pl.BlockSpec(gn.shape, lambda b: (0, 0)), sspec],
        out_specs=(pl.BlockSpec((nseq, 1, HG_WIDTH), lambda b: (b, 0, 0)), sspec),
        scratch_shapes=[pltpu.VMEM((HG_HEADS, HG_DIM, Bs), f32)],
        compiler_params=_params(("arbitrary",)),
        name="hgrn_decode",
    )(h4, lb, gn, state)


def _split_w_in(w):
    idx = np.cumsum(SPLITS)[:-1]
    return jnp.split(w, [int(v) for v in idx], axis=1)


def kernel(x_prompt, x_sample, cache_kv, page_table, state_kv_win, state_hgrn, state_conv, rel_table, hg_lb_logits,
           norm1_g, w_in, q_norm_g, k_norm_g, phi_k_w1, phi_k_w2, phi_v_w1, phi_v_w2, hg_norm_g, w_nsa_out, w_hg_out,
           w_o, norm2_g, w_gate, w_up, conv_w, conv_b, w_down):
    assert w_in.shape[0] == 1, "one layer"
    Bp, L, _ = x_prompt.shape
    Bs = x_sample.shape[0]
    n_pages = page_table.shape[1]
    past = n_pages * PAGE_SIZE
    n_win = state_kv_win.shape[2]
    assert x_sample.shape[1] == 1 and n_win == WINDOW and past % (2 * CMP_BLOCK) == 0 and L % LANES == 0

    wq, wkv, wgl, whq, whf, whi, whg, wmg = _split_w_in(w_in[0])
    wgl = jnp.pad(wgl, ((0, 0), (0, LANES - wgl.shape[1])))
    w_prompt = jnp.concatenate([wq, wkv, whq, whf, whi, whg, wmg, wgl], axis=1).astype(bf16)
    wq_h = wq.reshape(D_MODEL, NSA_HEADS, HEAD_DIM)
    zq = jnp.zeros_like(wq_h)
    in_g0 = (np.arange(NSA_HEADS) < NSA_GROUP)[None, :, None]
    wq_bd = jnp.where(in_g0, jnp.concatenate([wq_h, zq], axis=-1), jnp.concatenate([zq, wq_h], axis=-1))
    w_sample = jnp.concatenate([wq_bd.reshape(D_MODEL, NSA_HEADS * LANES), wkv, whq, whf, whi, whg, wmg, wgl],
                               axis=1).astype(bf16)
    g1 = norm1_g[0][None, :]
    gq = q_norm_g[0]
    gq_prompt = jnp.tile(gq, NSA_HEADS)[None, :]
    gq_sample = jnp.tile(gq, 2 * NSA_HEADS)[None, :]
    gk = [jnp.tile(k_norm_g[0, s], NSA_KV_HEADS)[None, :] for s in range(3)]

    def block_diag(mats):
        n = len(mats)
        lead = [(0, 0)] * (mats[0].ndim - 2)
        rows = [jnp.pad(m.astype(bf16), lead + [(0, 0), (t * HEAD_DIM, (n - 1 - t) * HEAD_DIM)])
                for t, m in enumerate(mats)]
        return jnp.concatenate(rows, axis=-2)

    w1bd = block_diag([phi_k_w1[0], phi_k_w1[0], phi_v_w1[0], phi_v_w1[0]]).reshape(CMP_BLOCK * 2 * LANES, 2 * LANES)
    w2bd = block_diag([phi_k_w2[0], phi_k_w2[0], phi_v_w2[0], phi_v_w2[0]])
    lb = jnp.cumsum(jax.nn.softmax(hg_lb_logits.astype(f32), axis=0), axis=0)[0][None, :]
    gn = hg_norm_g[0][None, :]
    ffn_w = (w_nsa_out[0].astype(bf16), w_hg_out[0].astype(bf16), w_o[0].astype(bf16), norm2_g[0][None, :],
             w_gate[0].astype(bf16), w_up[0].astype(bf16), conv_w[0], conv_b[0][None, :], w_down[0].astype(bf16))

    idx_c, idx_t = _prompt_bucket_tables(L)
    tc = _bias_tables(rel_table, jnp.asarray(idx_c))
    tt = _bias_tables(rel_table, jnp.asarray(idx_t))
    bias_dec = _bias_tables(rel_table, jnp.asarray(_decode_bucket_table(past, n_win))[None]).reshape(NSA_HEADS, -1)
    ex_prompt = jnp.asarray(_expand_np(L // SEL_BLOCK, L), dtype=bf16)
    ex_sample = jnp.asarray(_expand_np(past // SEL_BLOCK, past), dtype=bf16)

    xp = x_prompt.reshape(Bp * L, D_MODEL)
    qn, kv_t, win32, katt, h4, mgs, gates = _in_proj(xp, g1, w_prompt, gq_prompt, gk[1], gk[2],
                                                     qw=NSA_WIDTH, q_seg=HEAD_DIM, tm=512, seq_len=L)
    cmp = _compress_prompt(kv_t, w1bd, w2bd, gk[0])
    o_nsa = _nsa_prompt(qn, gates, katt, cmp, tc, tt, ex_prompt, Bp, L)
    o_hg, s_prompt = _hgrn_prompt(h4, lb, gn, Bp, L)
    tm_ffn = 512
    yp, gate_tails = _ffn(xp, o_nsa, o_hg.reshape(Bp * L, HG_WIDTH), mgs, *ffn_w, tm=tm_ffn, tf=D_FF // 2, seq_len=L)
    conv_p = gate_tails.reshape(Bp, L // tm_ffn, 2, D_FF)[:, -1]
    n_keep_win = min(WINDOW, L)
    win_p = jnp.swapaxes(win32[:, :, L - n_keep_win:], 1, 2).reshape(Bp, n_keep_win, 2, NSA_KV_HEADS, HEAD_DIM)

    xs = x_sample.reshape(Bs, D_MODEL)
    q_bd, kv32_s, win32_s, _, h4_s, mgs_s, gates_s = _in_proj(xs, g1, w_sample, gq_sample, gk[1], gk[2],
                                                              qw=NSA_HEADS * LANES, q_seg=LANES, tm=Bs)
    cache_t = jnp.swapaxes(cache_kv[0].reshape(cache_kv.shape[1], PAGE_SIZE, 4 * LANES), 1, 2)
    win_t = jnp.swapaxes(state_kv_win[0].reshape(Bs, n_win, 2 * LANES), 1, 2)
    o_nsa_s, win_out_t = _nsa_decode(cache_t, page_table, q_bd, kv32_s, win32_s, gates_s, win_t, w1bd, w2bd, gk[0],
                                     bias_dec, ex_sample)
    win_s = jnp.swapaxes(win_out_t, 1, 2)
    kv32 = jnp.swapaxes(kv_t, 1, 2)
    o_hg_s, s_sample = _hgrn_decode(h4_s, lb, gn, state_hgrn[0])
    ys, gate_s = _ffn(xs, o_nsa_s.reshape(Bs, NSA_WIDTH), o_hg_s.reshape(Bs, HG_WIDTH), mgs_s, *ffn_w,
                      tm=Bs, tf=256, conv_state=state_conv[0])
    conv_s = jnp.stack([state_conv[0][:, 1], gate_s], axis=1)

    kvh = (NSA_KV_HEADS, HEAD_DIM)
    return (yp.reshape(Bp, L, D_MODEL), ys.reshape(Bs, 1, D_MODEL),
            kv32.reshape((1, Bp, L, 4) + kvh), kv32_s.reshape((1, Bs, 1, 4) + kvh),
            win_p[None], win_s.reshape((1, Bs, n_win, 2) + kvh),
            s_prompt[None], s_sample[None], conv_p[None], conv_s[None])
```

```python
import functools
import math

import numpy as np
import jax
import jax.numpy as jnp
from jax import lax
from jax.experimental import pallas as pl
from jax.experimental.pallas import tpu as pltpu

f32, bf16, i32 = jnp.float32, jnp.bfloat16, jnp.int32

D_MODEL = 1024
PAGE_SIZE = 128
NSA_HEADS, NSA_KV_HEADS, NSA_GROUP, HEAD_DIM = 8, 2, 4, 64
CMP_BLOCK, SEL_BLOCK, TOP_N, WINDOW = 32, 64, 16, 512
SCALE = HEAD_DIM ** -0.5
NSA_WIDTH = NSA_HEADS * HEAD_DIM
KV_WIDTH = NSA_KV_HEADS * HEAD_DIM
HG_HEADS, HG_DIM = 4, 128
HG_WIDTH = HG_HEADS * HG_DIM
HG_SCALE = HG_DIM ** -0.5
HG_BLOCK = 16
SEL_CLASSES = 8
SUBLANES = 8
FFN_SLICE = 768
REL_BUCKETS, REL_MAX_DIST = 32, 128
D_FF = 2816
EPS = 1e-6
SPLITS = (NSA_WIDTH, 6 * KV_WIDTH, 3 * NSA_HEADS, HG_WIDTH, HG_WIDTH, HG_WIDTH, HG_WIDTH, 2 * D_MODEL)
LANES = 128
NEG = -1e30
FORCED_SCORE, INVALID_SCORE = 8.0, -1.0
VMEM_LIMIT = 56 * 1024 * 1024

NT = (((1,), (1,)), ((), ()))
TN = (((0,), (0,)), ((), ()))


def _params(sem):
    return pltpu.CompilerParams(dimension_semantics=sem, vmem_limit_bytes=VMEM_LIMIT)


def _bucket_np(dist):
    n = np.maximum(dist, 0)
    exact = REL_BUCKETS // 2
    ratio = np.log(np.maximum(n, 1).astype(np.float32) / np.float32(exact)) / np.float32(math.log(REL_MAX_DIST / exact))
    big = exact + (ratio.astype(np.float32) * np.float32(REL_BUCKETS - exact)).astype(np.int32)
    return np.where(n < exact, n, np.minimum(big, REL_BUCKETS - 1)).astype(np.int32)


def _cmp_perm(n_cmp):
    half = n_cmp // 2
    c = np.arange(n_cmp)
    return np.where(c < half, 2 * c, 2 * (c - half) + 1)


def _prompt_bucket_tables(L):
    nqb = L // SEL_BLOCK
    n_cmp = L // CMP_BLOCK
    qi = np.arange(SEL_BLOCK)
    c_end = (_cmp_perm(n_cmp) + 1) * CMP_BLOCK - 1
    idx_c = np.stack([_bucket_np((i * SEL_BLOCK + qi)[:, None] - c_end[None, :]) for i in range(nqb)])
    kj = np.arange(SEL_BLOCK)
    def tile(delta):
        return _bucket_np(delta * SEL_BLOCK + qi[:, None] - kj[None, :])
    idx_t = np.stack([np.concatenate([tile(e), tile(e - 1)], axis=1) for e in range(5)])
    return idx_c.astype(np.int32), idx_t.astype(np.int32)


def _decode_bucket_table(past_len, n_win):
    n_cmp = past_len // CMP_BLOCK
    c_end = (_cmp_perm(n_cmp) + 1) * CMP_BLOCK - 1
    d_c = np.zeros((LANES,), np.int64)
    d_c[:n_cmp] = past_len - c_end
    d_s = past_len - np.arange(past_len)
    d_w = n_win - np.arange(n_win)
    d_new = np.zeros((LANES,), np.int64)
    return _bucket_np(np.concatenate([d_s, d_w, d_c, d_new]))[None, :].astype(np.int32)


def _expand_np(n_blocks, n_keys):
    e = (np.arange(n_keys)[None, :] // SEL_BLOCK == np.arange(n_blocks)[:, None])
    return e.astype(np.float32)


def _seg_rmsnorm(x, gain, seg):
    outs = []
    for t in range(x.shape[1] // LANES):
        xt = x[:, t * LANES:(t + 1) * LANES]
        sq = xt * xt
        if seg == LANES:
            r = lax.rsqrt(jnp.sum(sq, axis=-1, keepdims=True) * (1.0 / HEAD_DIM) + EPS)
        else:
            lo = lax.broadcasted_iota(i32, xt.shape, 1) < HEAD_DIM
            s_lo = jnp.sum(jnp.where(lo, sq, 0.0), axis=-1, keepdims=True)
            s_hi = jnp.sum(jnp.where(lo, 0.0, sq), axis=-1, keepdims=True)
            r = jnp.where(lo, lax.rsqrt(s_lo * (1.0 / HEAD_DIM) + EPS), lax.rsqrt(s_hi * (1.0 / HEAD_DIM) + EPS))
        outs.append(xt * r)
    y = outs[0] if len(outs) == 1 else jnp.concatenate(outs, axis=1)
    return y * gain


def _rmsnorm_rows(x, gain):
    return x * lax.rsqrt(jnp.mean(x * x, axis=-1, keepdims=True) + EPS) * gain


def _softmax_parts(s, mask):
    s = jnp.where(mask, s, NEG)
    m = jnp.max(s, axis=-1, keepdims=True)
    e = jnp.where(mask, jnp.exp(s - m), 0.0)
    return e, jnp.sum(e, axis=-1, keepdims=True)


def _bias_table_kernel(tab_ref, idx_ref, out_ref):
    idx = idx_ref[...]
    rows = idx.shape[0]
    for h in range(NSA_HEADS):
        acc = jnp.zeros(idx.shape, f32)
        for b in range(REL_BUCKETS):
            acc = jnp.where(idx == b, tab_ref[b, h], acc)
        r = h % NSA_GROUP
        out_ref[h // NSA_GROUP, r * rows:(r + 1) * rows, :] = acc


def _bias_tables(rel_table, idx):
    n, rows, w = idx.shape
    return pl.pallas_call(
        _bias_table_kernel,
        out_shape=jax.ShapeDtypeStruct((n, NSA_KV_HEADS, NSA_GROUP * rows, w), f32),
        grid=(n,),
        in_specs=[pl.BlockSpec(memory_space=pltpu.SMEM),
                  pl.BlockSpec((None, rows, w), lambda i: (i, 0, 0))],
        out_specs=pl.BlockSpec((None, NSA_KV_HEADS, NSA_GROUP * rows, w), lambda i: (i, 0, 0, 0)),
        compiler_params=_params(("arbitrary",)),
        name="bias_tables",
    )(rel_table, idx)


def _in_proj_kernel(x_ref, g1_ref, w_ref, gq_ref, gk1_ref, gk2_ref,
                    q_ref, kv_ref, win_ref, katt_ref, h4_ref, mg_ref, gl_ref, *, qw, q_seg, kv_feature_major):
    x = x_ref[...]
    xn = _rmsnorm_rows(x, g1_ref[...]).astype(bf16)

    def proj(c0, width):
        return jnp.dot(xn, w_ref[:, c0:c0 + width], preferred_element_type=f32)

    q_ref[...] = (_seg_rmsnorm(proj(0, qw), gq_ref[...], q_seg) * SCALE).astype(bf16)
    c = qw
    kv = proj(c, 6 * KV_WIDTH)
    c += 6 * KV_WIDTH
    k_sel = _seg_rmsnorm(kv[:, 2 * KV_WIDTH:3 * KV_WIDTH], gk1_ref[...], HEAD_DIM)
    k_win = _seg_rmsnorm(kv[:, 4 * KV_WIDTH:5 * KV_WIDTH], gk2_ref[...], HEAD_DIM)
    v_sel = kv[:, 3 * KV_WIDTH:4 * KV_WIDTH]
    v_win = kv[:, 5 * KV_WIDTH:6 * KV_WIDTH]
    if kv_feature_major:
        kv_ref[0:2 * KV_WIDTH, :] = kv[:, 0:2 * KV_WIDTH].T
        kv_ref[2 * KV_WIDTH:3 * KV_WIDTH, :] = k_sel.T
        kv_ref[3 * KV_WIDTH:4 * KV_WIDTH, :] = v_sel.T
    else:
        kv_ref[:, 0:2 * KV_WIDTH] = kv[:, 0:2 * KV_WIDTH]
        kv_ref[:, 2 * KV_WIDTH:3 * KV_WIDTH] = k_sel
        kv_ref[:, 3 * KV_WIDTH:4 * KV_WIDTH] = v_sel
    if kv_feature_major:
        win_ref[0:KV_WIDTH, :] = k_win.T
        win_ref[KV_WIDTH:2 * KV_WIDTH, :] = v_win.T
    else:
        win_ref[:, 0:KV_WIDTH] = k_win
        win_ref[:, KV_WIDTH:2 * KV_WIDTH] = v_win
    katt_ref[:, 0:KV_WIDTH] = k_sel.astype(bf16)
    katt_ref[:, KV_WIDTH:2 * KV_WIDTH] = v_sel.astype(bf16)
    katt_ref[:, 2 * KV_WIDTH:3 * KV_WIDTH] = k_win.astype(bf16)
    katt_ref[:, 3 * KV_WIDTH:4 * KV_WIDTH] = v_win.astype(bf16)
    h4_ref[...] = proj(c, 4 * HG_WIDTH)
    c += 4 * HG_WIDTH
    mg_ref[...] = jax.nn.sigmoid(proj(c, 2 * D_MODEL))
    c += 2 * D_MODEL
    gl_ref[...] = jax.nn.sigmoid(proj(c, LANES))


def _in_proj(x2d, g1, w, gq, gk1, gk2, *, qw, q_seg, tm, seq_len=None):
    n = x2d.shape[0]
    cw = w.shape[1]
    row = lambda width: pl.BlockSpec((tm, width), lambda i: (i, 0))
    full = lambda a: pl.BlockSpec(a.shape, lambda i: (0,) * a.ndim)
    widths = (qw, 4 * KV_WIDTH, 2 * KV_WIDTH, 4 * KV_WIDTH, 4 * HG_WIDTH, 2 * D_MODEL, LANES)
    dtypes = (bf16, f32, f32, bf16, f32, f32, f32)
    out_shape = [jax.ShapeDtypeStruct((n, wd), dt) for wd, dt in zip(widths, dtypes)]
    out_specs = [row(wd) for wd in widths]
    if seq_len is not None:
        tiles = seq_len // tm
        for o_, wd in ((1, 4 * KV_WIDTH), (2, 2 * KV_WIDTH)):
            out_shape[o_] = jax.ShapeDtypeStruct((n // seq_len, wd, seq_len), f32)
            out_specs[o_] = pl.BlockSpec((None, wd, tm), lambda i: (i // tiles, 0, i % tiles))
    return pl.pallas_call(
        functools.partial(_in_proj_kernel, qw=qw, q_seg=q_seg, kv_feature_major=seq_len is not None),
        out_shape=tuple(out_shape),
        grid=(n // tm,),
        in_specs=[row(D_MODEL), full(g1),
                  pl.BlockSpec((D_MODEL, cw), lambda i: (0, 0), pipeline_mode=pl.Buffered(1)),
                  full(gq), full(gk1), full(gk2)],
        out_specs=tuple(out_specs),
        compiler_params=_params(("arbitrary",)),
        name="in_proj",
    )(x2d, g1, w, gq, gk1, gk2)


CMP_SLOT = CMP_BLOCK + 1


def _cmp_staging_rows(n_cmp):
    return -(-n_cmp * CMP_SLOT // 8) * 8


def _stage_cmp_chunk(src_ref, chunk, kc_t, vc_t, n_cmp):
    per_chunk = LANES // CMP_BLOCK
    for s, t in enumerate((kc_t.astype(bf16).T.astype(f32), vc_t.astype(bf16).T.astype(f32))):
        for nl in range(per_chunk):
            n = chunk * per_chunk + nl
            slot = n // 2 + (n % 2) * (n_cmp // 2)
            src_ref[s, slot * CMP_SLOT:slot * CMP_SLOT + CMP_BLOCK, :] = t[nl * CMP_BLOCK:(nl + 1) * CMP_BLOCK]


def _compress(src_refs, w1_ref, w2_ref, gk0, n_cmp):
    acc = jnp.zeros((len(src_refs) * n_cmp, 2 * LANES), f32)
    per_dot = 8
    for j0 in range(0, CMP_BLOCK, per_dot):
        xs = jnp.concatenate(
            [jnp.concatenate([src[s, pl.ds(j, n_cmp, stride=CMP_SLOT), :].astype(bf16)
                              for j in range(j0, j0 + per_dot) for s in range(2)], axis=1)
             for src in src_refs], axis=0)
        w = w1_ref[j0 * 2 * LANES:(j0 + per_dot) * 2 * LANES, :]
        acc = acc + jnp.dot(xs, w, preferred_element_type=f32)
    hmid = jax.nn.gelu(acc).astype(bf16)
    y = jnp.dot(hmid, w2_ref[...], preferred_element_type=f32)
    ck = _seg_rmsnorm(y[:, 0:LANES], gk0, HEAD_DIM)
    return jnp.concatenate([ck, y[:, LANES:2 * LANES]], axis=1)


def _compress_prompt_kernel(kv_ref, w1_ref, w2_ref, gk0_ref, out_ref, src_ref, *, n_cmp):
    for c in range(kv_ref.shape[1] // LANES):
        cols = slice(c * LANES, (c + 1) * LANES)
        _stage_cmp_chunk(src_ref, c, kv_ref[0:LANES, cols], kv_ref[LANES:2 * LANES, cols], n_cmp)
    out_ref[...] = _compress([src_ref], w1_ref, w2_ref, gk0_ref[...], n_cmp).astype(bf16)


def _compress_prompt(kv_t, w1bd, w2bd, gk0):
    B, _, L = kv_t.shape
    n_cmp = L // CMP_BLOCK
    return pl.pallas_call(
        functools.partial(_compress_prompt_kernel, n_cmp=n_cmp),
        out_shape=jax.ShapeDtypeStruct((B, n_cmp, 2 * LANES), bf16),
        grid=(B,),
        in_specs=[pl.BlockSpec((None, 2 * LANES, L), lambda b: (b, 0, 0)),
                  pl.BlockSpec(w1bd.shape, lambda b: (0, 0)),
                  pl.BlockSpec(w2bd.shape, lambda b: (0, 0)),
                  pl.BlockSpec(gk0.shape, lambda b: (0, 0))],
        out_specs=pl.BlockSpec((None, n_cmp, 2 * LANES), lambda b: (b, 0, 0)),
        scratch_shapes=[pltpu.VMEM((2, _cmp_staging_rows(n_cmp), LANES), f32)],
        compiler_params=_params(("arbitrary",)),
        name="compress_prompt",
    )(kv_t, w1bd, w2bd, gk0)


def _select_blocks_t(score, valid, n_keep):
    nb = score.shape[0]
    jj = lax.broadcasted_iota(i32, score.shape, 0)
    cnt = jnp.zeros(score.shape, f32)
    for k in range(nb):
        sk = score[k:k + 1, :]
        better = (sk > score) | ((sk == score) & (jj > k))
        cnt = cnt + jnp.where(better, 1.0, 0.0)
    return (cnt < n_keep) & valid


def _sel_tile_classes(n_tiles):
    return sorted({-(-n_tiles * c // SEL_CLASSES) for c in range(1, SEL_CLASSES + 1)})


def _nsa_prompt_kernel(q_ref, gate_ref, katt_ref, cmp_ref, tc_ref, tt_ref, ex_ref, o_ref,
                       selx_ref, s_ref, os_ref, part_ref, gsel_ref, *, L):
    i = pl.program_id(1)
    n_cmp = L // CMP_BLOCK
    n_sel = L // SEL_BLOCK
    n_tiles = L // LANES
    qb = SEL_BLOCK
    rows = NSA_GROUP * qb
    units = [(sq, g) for sq in range(q_ref.shape[0]) for g in range(NSA_KV_HEADS)]
    nu = range(len(units))
    row64 = lax.broadcasted_iota(i32, (qb, LANES), 0)
    lane = lax.broadcasted_iota(i32, (qb, LANES), 1)
    qpos = i * qb + row64

    qs = [jnp.concatenate(
        [q_ref[sq, :, (NSA_GROUP * g + r) * HEAD_DIM:(NSA_GROUP * g + r + 1) * HEAD_DIM] for r in range(NSA_GROUP)],
        axis=0) for sq, g in units]

    def attend(k_off, v_off, tile0, n_t, mask_fn):
        chunk = 4

        def key_rows(t0, count, off, u):
            sq, g = units[u]
            r0 = pl.multiple_of((tile0 + t0) * LANES, LANES)
            return katt_ref[sq, pl.ds(r0, count * LANES), off + g * HEAD_DIM:off + (g + 1) * HEAD_DIM]

        def scores(t0, count, mruns):
            dots = [lax.dot_general(qs[u], key_rows(t0, count, k_off, u), NT, preferred_element_type=f32)
                    for u in nu]
            out = list(mruns)
            for j in range(count):
                ta = tile0 + t0 + j
                bias_idx = jnp.clip(i - 2 * ta, 0, 4)
                kpos = ta * LANES + lane
                for u in nu:
                    s = dots[u][:, j * LANES:(j + 1) * LANES] + tt_ref[bias_idx, units[u][1]]
                    msk = mask_fn(u, ta, kpos)
                    s = jnp.where(msk[None], s.reshape(NSA_GROUP, qb, LANES), NEG).reshape(rows, LANES)
                    s_ref[u, t0 + j] = s
                    out[u] = jnp.maximum(out[u], s)
            return tuple(out)

        def values(t0, count, carry, ms):
            ls, accs = list(carry[0]), list(carry[1])
            ps = []
            for u in nu:
                pj = [jnp.exp(s_ref[u, t0 + j] - ms[u]) for j in range(count)]
                for p in pj:
                    ls[u] = ls[u] + p
                ps.append(jnp.concatenate([p.astype(bf16) for p in pj], axis=1))
            for u in nu:
                accs[u] = accs[u] + jnp.dot(ps[u], key_rows(t0, count, v_off, u), preferred_element_type=f32)
            return tuple(ls), tuple(accs)

        def over_chunks(fn, carry):
            for t0 in range(0, n_t, chunk):
                carry = fn(t0, min(chunk, n_t - t0), carry)
            return carry

        mruns = over_chunks(scores, tuple(jnp.full((rows, LANES), NEG, f32) for _ in nu))
        ms = [jnp.max(m, axis=-1, keepdims=True) for m in mruns]
        init = (tuple(jnp.zeros((rows, LANES), f32) for _ in nu),
                tuple(jnp.zeros((rows, HEAD_DIM), f32) for _ in nu))
        ls, accs = over_chunks(lambda t0, count, cr: values(t0, count, cr, ms), init)
        return [accs[u] / jnp.sum(ls[u], axis=-1, keepdims=True) for u in nu]

    n_wt = min(WINDOW // LANES + 1, n_tiles)
    w0 = jnp.clip((i - WINDOW // qb) // 2, 0, n_tiles - n_wt)
    o_w = attend(2 * KV_WIDTH, 3 * KV_WIDTH, w0, n_wt,
                 lambda u, ta, kpos: (kpos <= qpos) & (qpos - kpos <= WINDOW))

    o_c, imp_t = [], []
    col = lax.broadcasted_iota(i32, (rows, n_cmp), 1)
    rq = lax.broadcasted_iota(i32, (rows, n_cmp), 0) & (qb - 1)
    blk = jnp.where(col < n_cmp // 2, 2 * col, 2 * col - (n_cmp - 1))
    vis = (blk + 1) * CMP_BLOCK - 1 <= i * qb + rq
    for u, (sq, g) in enumerate(units):
        ck = cmp_ref[sq, :, g * HEAD_DIM:(g + 1) * HEAD_DIM]
        cv = cmp_ref[sq, :, LANES + g * HEAD_DIM:LANES + (g + 1) * HEAD_DIM]
        s = lax.dot_general(qs[u], ck, NT, preferred_element_type=f32) + tc_ref[g]
        e, den = _softmax_parts(s, vis)
        p = e / jnp.where(den > 0, den, 1.0)
        o_c.append(jnp.dot(p.astype(bf16), cv, preferred_element_type=f32))
        ps_t = (p[0:qb] + p[qb:2 * qb] + p[2 * qb:3 * qb] + p[3 * qb:4 * qb]).T
        imp_t.append(ps_t[0:n_sel] + ps_t[n_sel:2 * n_sel])

    imp = jnp.concatenate(imp_t, axis=1)
    jj = lax.broadcasted_iota(i32, imp.shape, 0)
    valid = jj <= i
    forced = valid & ((jj == 0) | (jj >= i - 1))
    score = jnp.where(forced, FORCED_SCORE, jnp.where(valid, imp, INVALID_SCORE))
    sel = _select_blocks_t(score, valid, min(TOP_N, n_sel))
    selx = lax.dot_general(jnp.where(sel, 1.0, 0.0).astype(bf16), ex_ref[...], TN, preferred_element_type=f32)
    for u in nu:
        for t in range(n_tiles):
            selx_ref[u, t] = selx[u * qb:(u + 1) * qb, t * LANES:(t + 1) * LANES]

    for u, (sq, g) in enumerate(units):
        gates = gate_ref[sq]
        for r in range(NSA_GROUP):
            h = NSA_GROUP * g + r
            hs = slice(h * HEAD_DIM, (h + 1) * HEAD_DIM)
            sl = slice(r * qb, (r + 1) * qb)
            part_ref[sq, :, hs] = gates[:, 3 * h:3 * h + 1] * o_c[u][sl] + gates[:, 3 * h + 2:3 * h + 3] * o_w[u][sl]
            gsel_ref[sq, :, hs] = jnp.broadcast_to(gates[:, 3 * h + 1:3 * h + 2], (qb, HEAD_DIM))

    need = i // 2 + 1
    prev = 0
    for n_t in _sel_tile_classes(n_tiles):
        @pl.when((need > prev) & (need <= n_t))
        def _(n_t=n_t):
            o_s = attend(0, KV_WIDTH, 0, n_t, lambda u, ta, kpos: (selx_ref[u, ta] > 0.5) & (kpos <= qpos))
            for u in nu:
                os_ref[u] = o_s[u]
        prev = n_t

    for u, (sq, g) in enumerate(units):
        o_s = os_ref[u]
        for r in range(NSA_GROUP):
            hs = slice((NSA_GROUP * g + r) * HEAD_DIM, (NSA_GROUP * g + r + 1) * HEAD_DIM)
            o = part_ref[sq, :, hs] + gsel_ref[sq, :, hs] * o_s[r * qb:(r + 1) * qb]
            o_ref[sq, :, hs] = o.astype(bf16)


def _nsa_prompt(qn, gates, katt, cmp, tc, tt, expand, B, L, nseq=2):
    nqb = L // SEL_BLOCK
    qb = SEL_BLOCK
    n_tiles = L // LANES
    rows = NSA_GROUP * qb
    nseq = math.gcd(nseq, B)
    n_units = nseq * NSA_KV_HEADS
    seq3 = lambda a: a.reshape(B, L, a.shape[-1])
    return pl.pallas_call(
        functools.partial(_nsa_prompt_kernel, L=L),
        out_shape=jax.ShapeDtypeStruct((B, L, NSA_WIDTH), bf16),
        grid=(B // nseq, nqb),
        in_specs=[pl.BlockSpec((nseq, qb, NSA_WIDTH), lambda b, i: (b, i, 0)),
                  pl.BlockSpec((nseq, qb, LANES), lambda b, i: (b, i, 0)),
                  pl.BlockSpec((nseq, L, 4 * KV_WIDTH), lambda b, i: (b, 0, 0)),
                  pl.BlockSpec((nseq, L // CMP_BLOCK, 2 * LANES), lambda b, i: (b, 0, 0)),
                  pl.BlockSpec((None,) + tc.shape[1:], lambda b, i: (i, 0, 0, 0)),
                  pl.BlockSpec(tt.shape, lambda b, i: (0, 0, 0, 0)),
                  pl.BlockSpec(expand.shape, lambda b, i: (0, 0))],
        out_specs=pl.BlockSpec((nseq, qb, NSA_WIDTH), lambda b, i: (b, i, 0)),
        scratch_shapes=[pltpu.VMEM((n_units, n_tiles, qb, LANES), f32),
                        pltpu.VMEM((n_units, n_tiles, rows, LANES), f32),
                        pltpu.VMEM((n_units, rows, HEAD_DIM), f32),
                        pltpu.VMEM((nseq, qb, NSA_WIDTH), f32), pltpu.VMEM((nseq, qb, NSA_WIDTH), f32)],
        compiler_params=_params(("arbitrary", "arbitrary")),
        name="nsa_prompt",
    )(seq3(qn), seq3(gates), seq3(katt), cmp, tc, tt, expand).reshape(B * L, NSA_WIDTH)


def _hgrn_gates(hf, lb):
    f = lb + (1.0 - lb) * jax.nn.sigmoid(hf)
    return f, 1.0 - f, jnp.log(f)


def _hgrn_out(o, gn, hgate):
    return _rmsnorm_rows(o, gn) * jax.nn.silu(hgate)


def _hgrn_prompt_kernel(h4_ref, lb_ref, gn_ref, o_ref, s_out_ref, st_ref, *, tc):
    c = pl.program_id(1)
    nb = HG_BLOCK
    nseq = h4_ref.shape[0]

    @pl.when(c == 0)
    def _():
        st_ref[...] = jnp.zeros_like(st_ref)

    t_row = lax.broadcasted_iota(i32, (nb, HG_DIM), 0)
    t_col = lax.broadcasted_iota(i32, (nb, 1), 0)
    gn = gn_ref[...]
    chains = [(sq, h) for sq in range(nseq) for h in range(HG_HEADS)]

    def cumsum_rows(x):
        shift = 1
        while shift < nb:
            x = x + jnp.where(t_row >= shift, pltpu.roll(x, shift, axis=0), 0.0)
            shift *= 2
        return x

    def block(bi, _):
        r0 = pl.multiple_of(bi * nb, nb)

        def sl(sq, h, part):
            return h4_ref[sq, pl.ds(r0, nb), part * HG_WIDTH + h * HG_DIM:part * HG_WIDTH + (h + 1) * HG_DIM]

        qs, ks, vs, bs, sts, os_ = [], [], [], [], [], []
        for sq, h in chains:
            _, k, glog = _hgrn_gates(sl(sq, h, 1), lb_ref[:, h * HG_DIM:(h + 1) * HG_DIM])
            qs.append(sl(sq, h, 0) * HG_SCALE)
            ks.append(k)
            vs.append(sl(sq, h, 2))
            bs.append(cumsum_rows(glog))
        for c_, (sq, h) in enumerate(chains):
            sts.append(st_ref[sq, h])
            os_.append(lax.dot_general((qs[c_] * jnp.exp(bs[c_])).astype(bf16), sts[c_].astype(bf16), NT,
                                       preferred_element_type=f32))
        for c_ in range(len(chains)):
            q, k, v, b, o = qs[c_], ks[c_], vs[c_], bs[c_], os_[c_]
            parts = [o[r0_:r0_ + SUBLANES] for r0_ in range(0, nb, SUBLANES)]
            for s in range(nb):
                for gi in range(s // SUBLANES, nb // SUBLANES):
                    rs = slice(gi * SUBLANES, (gi + 1) * SUBLANES)
                    w = q[rs] * k[s:s + 1] * jnp.exp(b[rs] - b[s:s + 1])
                    a = jnp.sum(w, axis=-1, keepdims=True)
                    if gi == s // SUBLANES:
                        a = jnp.where(t_col[rs] >= s, a, 0.0)
                    parts[gi] = parts[gi] + a * v[s:s + 1]
            os_[c_] = jnp.concatenate(parts, axis=0)
        for c_, (sq, h) in enumerate(chains):
            b_last = bs[c_][nb - 1:nb]
            kt = ks[c_] * jnp.exp(b_last - bs[c_])
            upd = lax.dot_general(vs[c_].astype(bf16), kt.astype(bf16), TN, preferred_element_type=f32)
            st_ref[sq, h] = jnp.exp(b_last) * sts[c_] + upd
            o_ref[sq, pl.ds(r0, nb), h * HG_DIM:(h + 1) * HG_DIM] = _hgrn_out(os_[c_], gn, sl(sq, h, 3)).astype(bf16)
        return 0

    lax.fori_loop(0, tc // nb, block, 0, unroll=2)

    @pl.when(c == pl.num_programs(1) - 1)
    def _():
        for sq in range(nseq):
            for h in range(HG_HEADS):
                s_out_ref[sq, h] = st_ref[sq, h].T


def _hgrn_prompt(h4, lb, gn, B, L, tc=256, nseq=4):
    nc = L // tc
    nseq = math.gcd(nseq, B)
    return pl.pallas_call(
        functools.partial(_hgrn_prompt_kernel, tc=tc),
        out_shape=(jax.ShapeDtypeStruct((B, L, HG_WIDTH), bf16),
                   jax.ShapeDtypeStruct((B, HG_HEADS, HG_DIM, HG_DIM), f32)),
        grid=(B // nseq, nc),
        in_specs=[pl.BlockSpec((nseq, tc, 4 * HG_WIDTH), lambda b, c: (b, c, 0)),
                  pl.BlockSpec(lb.shape, lambda b, c: (0, 0)),
                  pl.BlockSpec(gn.shape, lambda b, c: (0, 0))],
        out_specs=(pl.BlockSpec((nseq, tc, HG_WIDTH), lambda b, c: (b, c, 0)),
                   pl.BlockSpec((nseq, HG_HEADS, HG_DIM, HG_DIM), lambda b, c: (b, 0, 0, 0))),
        scratch_shapes=[pltpu.VMEM((nseq, HG_HEADS, HG_DIM, HG_DIM), f32)],
        compiler_params=_params(("arbitrary", "arbitrary")),
        name="hgrn_prompt",
    )(h4.reshape(B, L, 4 * HG_WIDTH), lb, gn)


def _ffn_kernel(*refs, decode, tiles_per_seq):
    if decode:
        (x_ref, on_ref, oh_ref, mg_ref, wn_ref, wh_ref, wo_ref, g2_ref, wg_ref, wu_ref, cw_ref, cb_ref, wd_ref,
         cs0_ref, cs1_ref, y_ref, gate_out_ref, x1_ref, xn2_ref, acc_ref) = refs
    else:
        (x_ref, on_ref, oh_ref, mg_ref, wn_ref, wh_ref, wo_ref, g2_ref, wg_ref, wu_ref, cw_ref, cb_ref, wd_ref,
         y_ref, gate_out_ref, x1_ref, xn2_ref, acc_ref, carry_ref) = refs
    i = pl.program_id(0)
    fi = pl.program_id(1)
    tm = x_ref.shape[0]

    @pl.when(fi == 0)
    def _():
        y_a = jnp.dot(on_ref[...], wn_ref[...], preferred_element_type=f32)
        y_b = jnp.dot(oh_ref[...], wh_ref[...], preferred_element_type=f32)
        mg = mg_ref[...]
        merged = mg[:, 0:D_MODEL] * y_a + mg[:, D_MODEL:2 * D_MODEL] * y_b
        x1 = x_ref[...] + jnp.dot(merged.astype(bf16), wo_ref[...], preferred_element_type=f32)
        x1_ref[...] = x1
        xn2_ref[...] = _rmsnorm_rows(x1, g2_ref[...]).astype(bf16)
        acc_ref[...] = jnp.zeros_like(acc_ref)

    xn2 = xn2_ref[...]
    tf = wg_ref.shape[1]
    if not decode:
        @pl.when(i % tiles_per_seq == 0)
        def _():
            carry_ref[fi] = jnp.zeros(carry_ref.shape[1:], f32)

    acc = acc_ref[...]
    for a in range(0, tf, FFN_SLICE):
        cs = slice(a, min(a + FFN_SLICE, tf))
        gate = jnp.dot(xn2, wg_ref[:, cs], preferred_element_type=f32)
        up = jnp.dot(xn2, wu_ref[:, cs], preferred_element_type=f32)
        cw = cw_ref[:, cs]
        if decode:
            prev2, prev1 = cs0_ref[:, cs], cs1_ref[:, cs]
            gate_out_ref[:, cs] = gate
        else:
            carry = carry_ref[fi, :, cs]
            rid = lax.broadcasted_iota(i32, gate.shape, 0)
            prev1 = jnp.where(rid == 0, carry[7:8], pltpu.roll(gate, 1, axis=0))
            prev2 = jnp.where(rid == 0, carry[6:7], jnp.where(rid == 1, carry[7:8], pltpu.roll(gate, 2, axis=0)))
            carry_ref[fi, 6:8, cs] = gate[tm - 2:tm]
            gate_out_ref[:, cs] = gate[tm - 2:tm]
        conv = cb_ref[:, cs] + cw[0:1] * prev2 + cw[1:2] * prev1 + cw[2:3] * gate
        hmid = (jax.nn.silu(conv) * up).astype(bf16)
        acc = acc + jnp.dot(hmid, wd_ref[cs, :], preferred_element_type=f32)
    acc_ref[...] = acc

    @pl.when(fi == pl.num_programs(1) - 1)
    def _():
        y_ref[...] = x1_ref[...] + acc_ref[...]


def _ffn(x2d, o_nsa, o_hg, mgs, wn, wh, wo, g2, wg, wu, cw, cb, wd, *, tm, tf, seq_len=None, conv_state=None):
    n = x2d.shape[0]
    decode = conv_state is not None
    nf = D_FF // tf
    row = lambda width: pl.BlockSpec((tm, width), lambda i, f: (i, 0))
    full = lambda a: pl.BlockSpec(a.shape, lambda i, f: (0,) * a.ndim)
    in_specs = [row(D_MODEL), row(NSA_WIDTH), row(HG_WIDTH), row(2 * D_MODEL), full(wn), full(wh), full(wo), full(g2),
                pl.BlockSpec((D_MODEL, tf), lambda i, f: (0, f)), pl.BlockSpec((D_MODEL, tf), lambda i, f: (0, f)),
                pl.BlockSpec((3, tf), lambda i, f: (0, f)), pl.BlockSpec((1, tf), lambda i, f: (0, f)),
                pl.BlockSpec((tf, D_MODEL), lambda i, f: (f, 0))]
    args = [x2d, o_nsa, o_hg, mgs, wn, wh, wo, g2, wg, wu, cw, cb, wd]
    scratch = [pltpu.VMEM((tm, D_MODEL), f32), pltpu.VMEM((tm, D_MODEL), bf16), pltpu.VMEM((tm, D_MODEL), f32)]
    if decode:
        in_specs += [pl.BlockSpec((tm, tf), lambda i, f: (i, f))] * 2
        args += [conv_state[:, 0], conv_state[:, 1]]
        gate_shape = jax.ShapeDtypeStruct((n, D_FF), f32)
        gate_spec = pl.BlockSpec((tm, tf), lambda i, f: (i, f))
        tiles_per_seq = 1
    else:
        tiles_per_seq = seq_len // tm
        gate_shape = jax.ShapeDtypeStruct((n // tm, 2, D_FF), f32)
        gate_spec = pl.BlockSpec((None, 2, tf), lambda i, f: (i, 0, f))
        scratch.append(pltpu.VMEM((nf, 8, tf), f32))
    return pl.pallas_call(
        functools.partial(_ffn_kernel, decode=decode, tiles_per_seq=tiles_per_seq),
        out_shape=(jax.ShapeDtypeStruct((n, D_MODEL), f32), gate_shape),
        grid=(n // tm, nf),
        in_specs=in_specs,
        out_specs=(row(D_MODEL), gate_spec),
        scratch_shapes=scratch,
        compiler_params=_params(("arbitrary", "arbitrary")),
        name="ffn_decode" if decode else "ffn_prompt",
    )(*args)


def _nsa_decode_kernel(*refs, n_pages, n_win, nseq):
    page_refs = refs[1:1 + nseq * n_pages]
    (q_ref, kvn_ref, winn_ref, win_all_ref, win_ref, w1_ref, w2_ref, gk0_ref, bias_ref,
     ow_ref, win_out_ref, cmp_ref, ssel_ref, snew_ref, vsel_ref, src_ref, ksel_ref, wcol_ref) = refs[1 + nseq * n_pages:]
    step = pl.program_id(0)
    past = n_pages * PAGE_SIZE
    n_cmp = past // CMP_BLOCK
    heads = NSA_HEADS
    samples = range(nseq)

    @pl.when(step == 0)
    def _():
        for t in range(2):
            wcol_ref[t * LANES:(t + 1) * LANES, :] = win_all_ref[:, t * LANES:(t + 1) * LANES].T

    for sq in samples:
        for k in range(n_pages):
            page = page_refs[sq * n_pages + k]
            cols = slice(k * PAGE_SIZE, (k + 1) * PAGE_SIZE)
            _stage_cmp_chunk(src_ref.at[sq], k, page[0:LANES, :], page[LANES:2 * LANES, :], n_cmp)
            ksel_ref[sq, :, cols] = page[2 * LANES:3 * LANES, :].astype(bf16)
            vsel_ref[sq, :, cols] = page[3 * LANES:4 * LANES, :].astype(bf16)

    hrow = lax.broadcasted_iota(i32, (heads, 1), 0)
    g0 = hrow < NSA_GROUP
    bias = bias_ref[...]
    b_s = bias[:, 0:past]
    b_w = bias[:, past:past + n_win]
    b_new = bias[:, past + n_win + LANES:past + n_win + LANES + 1]
    qs = [q_ref[sq] for sq in samples]
    attend = _decode_attend

    def scores(q, k_t, b_past, k_new):
        s_past = jnp.dot(q, k_t, preferred_element_type=f32) + b_past
        s_new = jnp.sum(q.astype(f32) * k_new, axis=-1, keepdims=True) + b_new
        return s_past, s_new

    lane_b = lax.broadcasted_iota(i32, wcol_ref.shape, 1)
    lane_w = lax.broadcasted_iota(i32, win_ref.shape[1:], 1)
    for sq in samples:
        winn, kvn = winn_ref[sq], kvn_ref[sq]
        s_past, s_new = scores(qs[sq], ksel_ref[sq], b_s, kvn[:, 2 * LANES:3 * LANES])
        ssel_ref[sq] = s_past
        snew_ref[sq] = jnp.broadcast_to(s_new, (heads, LANES))
        s_win = scores(qs[sq], win_ref[sq, 0:LANES, :].astype(bf16), b_w, winn[:, 0:LANES])
        ow_ref[sq] = attend(s_win, None, win_ref[sq, LANES:2 * LANES, :].astype(bf16), winn[:, LANES:2 * LANES],
                            g0)
        new_col = jnp.sum(jnp.where(lane_b == step * nseq + sq, wcol_ref[...], 0.0), axis=-1, keepdims=True)
        win_out_ref[sq] = jnp.where(lane_w == n_win - 1, new_col, pltpu.roll(win_ref[sq], n_win - 1, axis=1))

    cmp = _compress([src_ref.at[sq] for sq in samples], w1_ref, w2_ref, gk0_ref[...], n_cmp)
    for sq in samples:
        cmp_ref[sq] = cmp[sq * n_cmp:(sq + 1) * n_cmp].astype(bf16)


def _decode_attend(s, mask, v_t, v_new, g0):
    s_past, s_new = s
    sm = s_past if mask is None else jnp.where(mask, s_past, NEG)
    m = jnp.maximum(jnp.max(sm, axis=-1, keepdims=True), s_new)
    e = jnp.exp(sm - m)
    e_new = jnp.exp(s_new - m)
    den = jnp.sum(e, axis=-1, keepdims=True) + e_new
    o2 = (lax.dot_general(e.astype(bf16), v_t, NT, preferred_element_type=f32) + e_new * v_new) / den
    return jnp.where(g0, o2[:, 0:HEAD_DIM], o2[:, HEAD_DIM:2 * HEAD_DIM])


def _nsa_decode_tail_kernel(q_ref, cmp_ref, ssel_ref, snew_ref, vsel_ref, ow_ref, kvn_ref, gate_ref, bias_ref, ex_ref,
                            o_ref, *, n_sel):
    samples = range(q_ref.shape[0])
    heads = NSA_HEADS
    n_cmp = cmp_ref.shape[1]
    g0 = lax.broadcasted_iota(i32, (heads, 1), 0) < NSA_GROUP
    b_c = bias_ref[...]
    cur = n_sel
    jrow = lax.broadcasted_iota(i32, (NSA_KV_HEADS, n_sel), 1)
    forced = (jrow == 0) | (jrow >= cur - 1)
    n_keep = min(TOP_N, n_sel + 1) - 1
    ri = lax.broadcasted_iota(i32, (n_sel, n_sel), 0)
    ci = lax.broadcasted_iota(i32, (n_sel, n_sel), 1)

    s_c = [lax.dot_general(q_ref[sq], cmp_ref[sq, :, 0:LANES], NT, preferred_element_type=f32) + b_c for sq in samples]
    p_c = []
    for sq in samples:
        e_c, den_c = _softmax_parts(s_c[sq], jnp.full(s_c[sq].shape, True))
        p_c.append(e_c / jnp.where(den_c > 0, den_c, 1.0))
    o_c = [jnp.dot(p_c[sq].astype(bf16), cmp_ref[sq, :, LANES:2 * LANES], preferred_element_type=f32) for sq in samples]
    sels = []
    for sq in samples:
        ps = jnp.concatenate([jnp.sum(p_c[sq][0:NSA_GROUP], axis=0, keepdims=True),
                              jnp.sum(p_c[sq][NSA_GROUP:heads], axis=0, keepdims=True)], axis=0)
        imp = ps[:, 0:n_sel] + ps[:, n_sel:2 * n_sel]
        score_row = jnp.where(forced, FORCED_SCORE, imp)
        rows = []
        for g in range(NSA_KV_HEADS):
            rowb = jnp.broadcast_to(score_row[g:g + 1], (n_sel, n_sel))
            colb = jnp.sum(jnp.where(ri == ci, rowb, 0.0), axis=-1, keepdims=True)
            better = (colb > rowb) | ((colb == rowb) & (ri < ci))
            cnt = jnp.sum(jnp.where(better, 1.0, 0.0), axis=0, keepdims=True)
            rows.append(jnp.where(cnt < n_keep, 1.0, 0.0))
        sels.append(jnp.concatenate(rows, axis=0).astype(bf16))
    selx = [jnp.dot(sels[sq], ex_ref[...], preferred_element_type=f32) for sq in samples]
    for sq in samples:
        mask = jnp.where(g0, selx[sq][0:1], selx[sq][1:2]) > 0.5
        o_s = _decode_attend((ssel_ref[sq], snew_ref[sq][:, 0:1]), mask, vsel_ref[sq],
                             kvn_ref[sq][:, 3 * LANES:4 * LANES], g0)
        gates = gate_ref[sq]
        o_cg = jnp.where(g0, o_c[sq][:, 0:HEAD_DIM], o_c[sq][:, HEAD_DIM:2 * HEAD_DIM])
        o_ref[sq] = (gates[:, 0:1] * o_cg + gates[:, 1:2] * o_s + gates[:, 2:3] * ow_ref[sq]).astype(bf16)


def _nsa_decode(cache_t, page_table, q_bd, kv32, win32, gates, win_t, w1bd, w2bd, gk0, bias, expand,
                nseq=2, nseq_tail=8):
    Bs, n_pages = page_table.shape
    n_win = win_t.shape[2]
    past = n_pages * PAGE_SIZE
    n_cmp = past // CMP_BLOCK
    nseq = math.gcd(nseq, Bs)
    per_b = lambda shape: pl.BlockSpec((nseq,) + shape, lambda b, pt: (b,) + (0,) * len(shape))
    const = lambda a: pl.BlockSpec(a.shape, lambda b, pt: (0,) * a.ndim)
    page_spec = lambda sq, k: pl.BlockSpec((None, 4 * LANES, PAGE_SIZE), lambda b, pt: (pt[b * nseq + sq, k], 0, 0))
    q3 = q_bd.reshape(Bs, NSA_HEADS, LANES)
    kvn3 = kv32.reshape(Bs, 1, 4 * LANES)
    gates3 = gates[:, 0:3 * NSA_HEADS].reshape(Bs, NSA_HEADS, 3)
    grid_spec = pltpu.PrefetchScalarGridSpec(
        num_scalar_prefetch=1,
        grid=(Bs // nseq,),
        in_specs=[page_spec(sq, k) for sq in range(nseq) for k in range(n_pages)] + [
            per_b((NSA_HEADS, LANES)), per_b((1, 4 * LANES)), per_b((1, 2 * LANES)), const(win32),
            per_b((2 * LANES, n_win)), const(w1bd), const(w2bd), const(gk0), const(bias)],
        out_specs=(per_b((NSA_HEADS, HEAD_DIM)), per_b((2 * LANES, n_win)), per_b((n_cmp, 2 * LANES)),
                   per_b((NSA_HEADS, past)), per_b((NSA_HEADS, LANES)), per_b((LANES, past))),
        scratch_shapes=[pltpu.VMEM((nseq, 2, _cmp_staging_rows(n_cmp), LANES), f32),
                        pltpu.VMEM((nseq, LANES, past), bf16), pltpu.VMEM((2 * LANES, Bs), f32)],
    )
    o_w, win_out, cmp, s_sel, s_new, v_sel = pl.pallas_call(
        functools.partial(_nsa_decode_kernel, n_pages=n_pages, n_win=n_win, nseq=nseq),
        out_shape=(jax.ShapeDtypeStruct((Bs, NSA_HEADS, HEAD_DIM), f32),
                   jax.ShapeDtypeStruct((Bs, 2 * LANES, n_win), f32),
                   jax.ShapeDtypeStruct((Bs, n_cmp, 2 * LANES), bf16),
                   jax.ShapeDtypeStruct((Bs, NSA_HEADS, past), f32),
                   jax.ShapeDtypeStruct((Bs, NSA_HEADS, LANES), f32),
                   jax.ShapeDtypeStruct((Bs, LANES, past), bf16)),
        grid_spec=grid_spec,
        compiler_params=_params(("arbitrary",)),
        name="nsa_decode",
    )(page_table, *([cache_t] * (nseq * n_pages)), q3, kvn3, win32.reshape(Bs, 1, 2 * LANES), win32, win_t,
      w1bd, w2bd, gk0, bias)

    ns = math.gcd(nseq_tail, Bs)
    per_t = lambda shape: pl.BlockSpec((ns,) + shape, lambda b: (b,) + (0,) * len(shape))
    const_t = lambda a: pl.BlockSpec(a.shape, lambda b: (0,) * a.ndim)
    b_c = bias[:, past + n_win:past + n_win + n_cmp]
    o = pl.pallas_call(
        functools.partial(_nsa_decode_tail_kernel, n_sel=past // SEL_BLOCK),
        out_shape=jax.ShapeDtypeStruct((Bs, NSA_HEADS, HEAD_DIM), bf16),
        grid=(Bs // ns,),
        in_specs=[per_t((NSA_HEADS, LANES)), per_t((n_cmp, 2 * LANES)), per_t((NSA_HEADS, past)),
                  per_t((NSA_HEADS, LANES)), per_t((LANES, past)), per_t((NSA_HEADS, HEAD_DIM)),
                  per_t((1, 4 * LANES)), per_t((NSA_HEADS, 3)), const_t(b_c), const_t(expand)],
        out_specs=per_t((NSA_HEADS, HEAD_DIM)),
        compiler_params=_params(("arbitrary",)),
        name="nsa_decode_select",
    )(q3, cmp, s_sel, s_new, v_sel, o_w, kvn3, gates3, b_c, expand)
    return o, win_out


def _hgrn_decode_kernel(h4_ref, lb_ref, gn_ref, s_ref, o_ref, s_out_ref, ft_ref):
    b = pl.program_id(0)
    nbatch = h4_ref.shape[0]

    @pl.when(b == 0)
    def _():
        for h in range(HG_HEADS):
            f, _, _ = _hgrn_gates(h4_ref[:, HG_WIDTH + h * HG_DIM:HG_WIDTH + (h + 1) * HG_DIM],
                                  lb_ref[:, h * HG_DIM:(h + 1) * HG_DIM])
            ft_ref[h] = f.T

    lane = lax.broadcasted_iota(i32, (HG_DIM, nbatch), 1)
    gn = gn_ref[...]
    for sq in range(s_ref.shape[0]):
        sample = b * s_ref.shape[0] + sq
        row = h4_ref[pl.ds(sample, 1), :]
        for h in range(HG_HEADS):
            part = lambda k: row[:, k * HG_WIDTH + h * HG_DIM:k * HG_WIDTH + (h + 1) * HG_DIM]
            f_col = jnp.sum(jnp.where(lane == sample, ft_ref[h], 0.0), axis=-1, keepdims=True)
            s_new = f_col * s_ref[sq, h] + (1.0 - f_col) * part(2)
            s_out_ref[sq, h] = s_new
            q = jnp.broadcast_to(part(0) * HG_SCALE, (8, HG_DIM)).astype(bf16)
            o = jnp.dot(q, s_new.astype(bf16), preferred_element_type=f32)[0:1]
            o_ref[sq, :, h * HG_DIM:(h + 1) * HG_DIM] = _hgrn_out(o, gn, part(3)).astype(bf16)


def _hgrn_decode(h4, lb, gn, state, nseq=4):
    Bs = h4.shape[0]
    nseq = math.gcd(nseq, Bs)
    sspec = pl.BlockSpec((nseq, HG_HEADS, HG_DIM, HG_DIM), lambda b: (b, 0, 0, 0))
    return pl.pallas_call(
        _hgrn_decode_kernel,
        out_shape=(jax.ShapeDtypeStruct((Bs, 1, HG_WIDTH), bf16),
                   jax.ShapeDtypeStruct((Bs, HG_HEADS, HG_DIM, HG_DIM), f32)),
        grid=(Bs // nseq,),
        in_specs=[pl.BlockSpec(h4.shape, lambda b: (0, 0)), pl.BlockSpec(lb.shape, lambda b: (0, 0)),
                  pl.BlockSpec(gn.shape, lambda b: (0, 0)), sspec],
        out_specs=(pl.BlockSpec((nseq, 1, HG_WIDTH), lambda b: (b, 0, 0)), sspec),
        scratch_shapes=[pltpu.VMEM((HG_HEADS, HG_DIM, Bs), f32)],
        compiler_params=_params(("arbitrary",)),
        name="hgrn_decode",
    )(h4, lb, gn, state)


def _split_w_in(w):
    idx = np.cumsum(SPLITS)[:-1]
    return jnp.split(w, [int(v) for v in idx], axis=1)


def kernel(x_prompt, x_sample, cache_kv, page_table, state_kv_win, state_hgrn, state_conv, rel_table, hg_lb_logits,
           norm1_g, w_in, q_norm_g, k_norm_g, phi_k_w1, phi_k_w2, phi_v_w1, phi_v_w2, hg_norm_g, w_nsa_out, w_hg_out,
           w_o, norm2_g, w_gate, w_up, conv_w, conv_b, w_down):
    assert w_in.shape[0] == 1, "one layer"
    Bp, L, _ = x_prompt.shape
    Bs = x_sample.shape[0]
    n_pages = page_table.shape[1]
    past = n_pages * PAGE_SIZE
    n_win = state_kv_win.shape[2]
    assert x_sample.shape[1] == 1 and n_win == WINDOW and past % (2 * CMP_BLOCK) == 0 and L % LANES == 0

    wq, wkv, wgl, whq, whf, whi, whg, wmg = _split_w_in(w_in[0])
    wgl = jnp.pad(wgl, ((0, 0), (0, LANES - wgl.shape[1])))
    w_prompt = jnp.concatenate([wq, wkv, whq, whf, whi, whg, wmg, wgl], axis=1).astype(bf16)
    wq_h = wq.reshape(D_MODEL, NSA_HEADS, HEAD_DIM)
    zq = jnp.zeros_like(wq_h)
    in_g0 = (np.arange(NSA_HEADS) < NSA_GROUP)[None, :, None]
    wq_bd = jnp.where(in_g0, jnp.concatenate([wq_h, zq], axis=-1), jnp.concatenate([zq, wq_h], axis=-1))
    w_sample = jnp.concatenate([wq_bd.reshape(D_MODEL, NSA_HEADS * LANES), wkv, whq, whf, whi, whg, wmg, wgl],
                               axis=1).astype(bf16)
    g1 = norm1_g[0][None, :]
    gq = q_norm_g[0]
    gq_prompt = jnp.tile(gq, NSA_HEADS)[None, :]
    gq_sample = jnp.tile(gq, 2 * NSA_HEADS)[None, :]
    gk = [jnp.tile(k_norm_g[0, s], NSA_KV_HEADS)[None, :] for s in range(3)]

    def block_diag(mats):
        n = len(mats)
        lead = [(0, 0)] * (mats[0].ndim - 2)
        rows = [jnp.pad(m.astype(bf16), lead + [(0, 0), (t * HEAD_DIM, (n - 1 - t) * HEAD_DIM)])
                for t, m in enumerate(mats)]
        return jnp.concatenate(rows, axis=-2)

    w1bd = block_diag([phi_k_w1[0], phi_k_w1[0], phi_v_w1[0], phi_v_w1[0]]).reshape(CMP_BLOCK * 2 * LANES, 2 * LANES)
    w2bd = block_diag([phi_k_w2[0], phi_k_w2[0], phi_v_w2[0], phi_v_w2[0]])
    lb = jnp.cumsum(jax.nn.softmax(hg_lb_logits.astype(f32), axis=0), axis=0)[0][None, :]
    gn = hg_norm_g[0][None, :]
    ffn_w = (w_nsa_out[0].astype(bf16), w_hg_out[0].astype(bf16), w_o[0].astype(bf16), norm2_g[0][None, :],
             w_gate[0].astype(bf16), w_up[0].astype(bf16), conv_w[0], conv_b[0][None, :], w_down[0].astype(bf16))

    idx_c, idx_t = _prompt_bucket_tables(L)
    tc = _bias_tables(rel_table, jnp.asarray(idx_c))
    tt = _bias_tables(rel_table, jnp.asarray(idx_t))
    bias_dec = _bias_tables(rel_table, jnp.asarray(_decode_bucket_table(past, n_win))[None]).reshape(NSA_HEADS, -1)
    ex_prompt = jnp.asarray(_expand_np(L // SEL_BLOCK, L), dtype=bf16)
    ex_sample = jnp.asarray(_expand_np(past // SEL_BLOCK, past), dtype=bf16)

    xp = x_prompt.reshape(Bp * L, D_MODEL)
    qn, kv_t, win32, katt, h4, mgs, gates = _in_proj(xp, g1, w_prompt, gq_prompt, gk[1], gk[2],
                                                     qw=NSA_WIDTH, q_seg=HEAD_DIM, tm=512, seq_len=L)
    cmp = _compress_prompt(kv_t, w1bd, w2bd, gk[0])
    o_nsa = _nsa_prompt(qn, gates, katt, cmp, tc, tt, ex_prompt, Bp, L)
    o_hg, s_prompt = _hgrn_prompt(h4, lb, gn, Bp, L)
    tm_ffn = 512
    yp, gate_tails = _ffn(xp, o_nsa, o_hg.reshape(Bp * L, HG_WIDTH), mgs, *ffn_w, tm=tm_ffn, tf=D_FF // 2, seq_len=L)
    conv_p = gate_tails.reshape(Bp, L // tm_ffn, 2, D_FF)[:, -1]
    n_keep_win = min(WINDOW, L)
    win_p = jnp.swapaxes(win32[:, :, L - n_keep_win:], 1, 2).reshape(Bp, n_keep_win, 2, NSA_KV_HEADS, HEAD_DIM)

    xs = x_sample.reshape(Bs, D_MODEL)
    q_bd, kv32_s, win32_s, _, h4_s, mgs_s, gates_s = _in_proj(xs, g1, w_sample, gq_sample, gk[1], gk[2],
                                                              qw=NSA_HEADS * LANES, q_seg=LANES, tm=Bs)
    cache_t = jnp.swapaxes(cache_kv[0].reshape(cache_kv.shape[1], PAGE_SIZE, 4 * LANES), 1, 2)
    win_t = jnp.swapaxes(state_kv_win[0].reshape(Bs, n_win, 2 * LANES), 1, 2)
    o_nsa_s, win_out_t = _nsa_decode(cache_t, page_table, q_bd, kv32_s, win32_s, gates_s, win_t, w1bd, w2bd, gk[0],
                                     bias_dec, ex_sample)
    win_s = jnp.swapaxes(win_out_t, 1, 2)
    kv32 = jnp.swapaxes(kv_t, 1, 2)
    o_hg_s, s_sample = _hgrn_decode(h4_s, lb, gn, state_hgrn[0])
    ys, gate_s = _ffn(xs, o_nsa_s.reshape(Bs, NSA_WIDTH), o_hg_s.reshape(Bs, HG_WIDTH), mgs_s, *ffn_w,
                      tm=Bs, tf=256, conv_state=state_conv[0])
    conv_s = jnp.stack([state_conv[0][:, 1], gate_s], axis=1)

    kvh = (NSA_KV_HEADS, HEAD_DIM)
    return (yp.reshape(Bp, L, D_MODEL), ys.reshape(Bs, 1, D_MODEL),
            kv32.reshape((1, Bp, L, 4) + kvh), kv32_s.reshape((1, Bs, 1, 4) + kvh),
            win_p[None], win_s.reshape((1, Bs, n_win, 2) + kvh),
            s_prompt[None], s_sample[None], conv_p[None], conv_s[None])
```

```python
import functools
import math

import numpy as np
import jax
import jax.numpy as jnp
from jax import lax
from jax.experimental import pallas as pl
from jax.experimental.pallas import tpu as pltpu

f32, bf16, i32 = jnp.float32, jnp.bfloat16, jnp.int32

D_MODEL = 1024
PAGE_SIZE = 128
NSA_HEADS, NSA_KV_HEADS, NSA_GROUP, HEAD_DIM = 8, 2, 4, 64
CMP_BLOCK, SEL_BLOCK, TOP_N, WINDOW = 32, 64, 16, 512
SCALE = HEAD_DIM ** -0.5
NSA_WIDTH = NSA_HEADS * HEAD_DIM
KV_WIDTH = NSA_KV_HEADS * HEAD_DIM
HG_HEADS, HG_DIM = 4, 128
HG_WIDTH = HG_HEADS * HG_DIM
HG_SCALE = HG_DIM ** -0.5
HG_BLOCK = 16
SEL_CLASSES = 8
SUBLANES = 8
FFN_SLICE = 768
REL_BUCKETS, REL_MAX_DIST = 32, 128
D_FF = 2816
EPS = 1e-6
SPLITS = (NSA_WIDTH, 6 * KV_WIDTH, 3 * NSA_HEADS, HG_WIDTH, HG_WIDTH, HG_WIDTH, HG_WIDTH, 2 * D_MODEL)
LANES = 128
NEG = -1e30
FORCED_SCORE, INVALID_SCORE = 8.0, -1.0
VMEM_LIMIT = 56 * 1024 * 1024

NT = (((1,), (1,)), ((), ()))
TN = (((0,), (0,)), ((), ()))


def _params(sem):
    return pltpu.CompilerParams(dimension_semantics=sem, vmem_limit_bytes=VMEM_LIMIT)


def _bucket_np(dist):
    n = np.maximum(dist, 0)
    exact = REL_BUCKETS // 2
    ratio = np.log(np.maximum(n, 1).astype(np.float32) / np.float32(exact)) / np.float32(math.log(REL_MAX_DIST / exact))
    big = exact + (ratio.astype(np.float32) * np.float32(REL_BUCKETS - exact)).astype(np.int32)
    return np.where(n < exact, n, np.minimum(big, REL_BUCKETS - 1)).astype(np.int32)


def _cmp_perm(n_cmp):
    half = n_cmp // 2
    c = np.arange(n_cmp)
    return np.where(c < half, 2 * c, 2 * (c - half) + 1)


def _prompt_bucket_tables(L):
    nqb = L // SEL_BLOCK
    n_cmp = L // CMP_BLOCK
    qi = np.arange(SEL_BLOCK)
    c_end = (_cmp_perm(n_cmp) + 1) * CMP_BLOCK - 1
    idx_c = np.stack([_bucket_np((i * SEL_BLOCK + qi)[:, None] - c_end[None, :]) for i in range(nqb)])
    kj = np.arange(SEL_BLOCK)
    def tile(delta):
        return _bucket_np(delta * SEL_BLOCK + qi[:, None] - kj[None, :])
    idx_t = np.stack([np.concatenate([tile(e), tile(e - 1)], axis=1) for e in range(5)])
    return idx_c.astype(np.int32), idx_t.astype(np.int32)


def _decode_bucket_table(past_len, n_win):
    n_cmp = past_len // CMP_BLOCK
    c_end = (_cmp_perm(n_cmp) + 1) * CMP_BLOCK - 1
    d_c = np.zeros((LANES,), np.int64)
    d_c[:n_cmp] = past_len - c_end
    d_s = past_len - np.arange(past_len)
    d_w = n_win - np.arange(n_win)
    d_new = np.zeros((LANES,), np.int64)
    return _bucket_np(np.concatenate([d_s, d_w, d_c, d_new]))[None, :].astype(np.int32)


def _expand_np(n_blocks, n_keys):
    e = (np.arange(n_keys)[None, :] // SEL_BLOCK == np.arange(n_blocks)[:, None])
    return e.astype(np.float32)


def _seg_rmsnorm(x, gain, seg):
    outs = []
    for t in range(x.shape[1] // LANES):
        xt = x[:, t * LANES:(t + 1) * LANES]
        sq = xt * xt
        if seg == LANES:
            r = lax.rsqrt(jnp.sum(sq, axis=-1, keepdims=True) * (1.0 / HEAD_DIM) + EPS)
        else:
            lo = lax.broadcasted_iota(i32, xt.shape, 1) < HEAD_DIM
            s_lo = jnp.sum(jnp.where(lo, sq, 0.0), axis=-1, keepdims=True)
            s_hi = jnp.sum(jnp.where(lo, 0.0, sq), axis=-1, keepdims=True)
            r = jnp.where(lo, lax.rsqrt(s_lo * (1.0 / HEAD_DIM) + EPS), lax.rsqrt(s_hi * (1.0 / HEAD_DIM) + EPS))
        outs.append(xt * r)
    y = outs[0] if len(outs) == 1 else jnp.concatenate(outs, axis=1)
    return y * gain


def _rmsnorm_rows(x, gain):
    return x * lax.rsqrt(jnp.mean(x * x, axis=-1, keepdims=True) + EPS) * gain


def _softmax_parts(s, mask):
    s = jnp.where(mask, s, NEG)
    m = jnp.max(s, axis=-1, keepdims=True)
    e = jnp.where(mask, jnp.exp(s - m), 0.0)
    return e, jnp.sum(e, axis=-1, keepdims=True)


def _bias_table_kernel(tab_ref, idx_ref, out_ref):
    idx = idx_ref[...]
    rows = idx.shape[0]
    for h in range(NSA_HEADS):
        acc = jnp.zeros(idx.shape, f32)
        for b in range(REL_BUCKETS):
            acc = jnp.where(idx == b, tab_ref[b, h], acc)
        r = h % NSA_GROUP
        out_ref[h // NSA_GROUP, r * rows:(r + 1) * rows, :] = acc


def _bias_tables(rel_table, idx):
    n, rows, w = idx.shape
    return pl.pallas_call(
        _bias_table_kernel,
        out_shape=jax.ShapeDtypeStruct((n, NSA_KV_HEADS, NSA_GROUP * rows, w), f32),
        grid=(n,),
        in_specs=[pl.BlockSpec(memory_space=pltpu.SMEM),
                  pl.BlockSpec((None, rows, w), lambda i: (i, 0, 0))],
        out_specs=pl.BlockSpec((None, NSA_KV_HEADS, NSA_GROUP * rows, w), lambda i: (i, 0, 0, 0)),
        compiler_params=_params(("arbitrary",)),
        name="bias_tables",
    )(rel_table, idx)


def _in_proj_kernel(x_ref, g1_ref, w_ref, gq_ref, gk1_ref, gk2_ref,
                    q_ref, kv_ref, win_ref, katt_ref, h4_ref, mg_ref, gl_ref, *, qw, q_seg, kv_feature_major):
    x = x_ref[...]
    xn = _rmsnorm_rows(x, g1_ref[...]).astype(bf16)

    def proj(c0, width):
        return lax.dot_general(xn, w_ref[c0:c0 + width, :], NT, preferred_element_type=f32)

    q_ref[...] = (_seg_rmsnorm(proj(0, qw), gq_ref[...], q_seg) * SCALE).astype(bf16)
    c = qw
    kv = proj(c, 6 * KV_WIDTH)
    c += 6 * KV_WIDTH
    k_sel = _seg_rmsnorm(kv[:, 2 * KV_WIDTH:3 * KV_WIDTH], gk1_ref[...], HEAD_DIM)
    k_win = _seg_rmsnorm(kv[:, 4 * KV_WIDTH:5 * KV_WIDTH], gk2_ref[...], HEAD_DIM)
    v_sel = kv[:, 3 * KV_WIDTH:4 * KV_WIDTH]
    v_win = kv[:, 5 * KV_WIDTH:6 * KV_WIDTH]
    if kv_feature_major:
        kv_ref[0:2 * KV_WIDTH, :] = kv[:, 0:2 * KV_WIDTH].T
        kv_ref[2 * KV_WIDTH:3 * KV_WIDTH, :] = k_sel.T
        kv_ref[3 * KV_WIDTH:4 * KV_WIDTH, :] = v_sel.T
    else:
        kv_ref[:, 0:2 * KV_WIDTH] = kv[:, 0:2 * KV_WIDTH]
        kv_ref[:, 2 * KV_WIDTH:3 * KV_WIDTH] = k_sel
        kv_ref[:, 3 * KV_WIDTH:4 * KV_WIDTH] = v_sel
    if kv_feature_major:
        win_ref[0:KV_WIDTH, :] = k_win.T
        win_ref[KV_WIDTH:2 * KV_WIDTH, :] = v_win.T
    else:
        win_ref[:, 0:KV_WIDTH] = k_win
        win_ref[:, KV_WIDTH:2 * KV_WIDTH] = v_win
    katt_ref[:, 0:KV_WIDTH] = k_sel.astype(bf16)
    katt_ref[:, KV_WIDTH:2 * KV_WIDTH] = v_sel.astype(bf16)
    katt_ref[:, 2 * KV_WIDTH:3 * KV_WIDTH] = k_win.astype(bf16)
    katt_ref[:, 3 * KV_WIDTH:4 * KV_WIDTH] = v_win.astype(bf16)
    h4_ref[...] = proj(c, 4 * HG_WIDTH)
    c += 4 * HG_WIDTH
    mg_ref[...] = jax.nn.sigmoid(proj(c, 2 * D_MODEL))
    c += 2 * D_MODEL
    gl_ref[...] = jax.nn.sigmoid(proj(c, LANES))


def _in_proj(x2d, g1, w, gq, gk1, gk2, *, qw, q_seg, tm, seq_len=None):
    n = x2d.shape[0]
    row = lambda width: pl.BlockSpec((tm, width), lambda i: (i, 0))
    full = lambda a: pl.BlockSpec(a.shape, lambda i: (0,) * a.ndim)
    widths = (qw, 4 * KV_WIDTH, 2 * KV_WIDTH, 4 * KV_WIDTH, 4 * HG_WIDTH, 2 * D_MODEL, LANES)
    dtypes = (bf16, f32, f32, bf16, f32, f32, f32)
    out_shape = [jax.ShapeDtypeStruct((n, wd), dt) for wd, dt in zip(widths, dtypes)]
    out_specs = [row(wd) for wd in widths]
    if seq_len is not None:
        tiles = seq_len // tm
        for o_, wd in ((1, 4 * KV_WIDTH), (2, 2 * KV_WIDTH)):
            out_shape[o_] = jax.ShapeDtypeStruct((n // seq_len, wd, seq_len), f32)
            out_specs[o_] = pl.BlockSpec((None, wd, tm), lambda i: (i // tiles, 0, i % tiles))
    return pl.pallas_call(
        functools.partial(_in_proj_kernel, qw=qw, q_seg=q_seg, kv_feature_major=seq_len is not None),
        out_shape=tuple(out_shape),
        grid=(n // tm,),
        in_specs=[row(D_MODEL), full(g1),
                  pl.BlockSpec(w.shape, lambda i: (0, 0), pipeline_mode=pl.Buffered(1)),
                  full(gq), full(gk1), full(gk2)],
        out_specs=tuple(out_specs),
        compiler_params=_params(("arbitrary",)),
        name="in_proj",
    )(x2d, g1, w, gq, gk1, gk2)


CMP_SLOT = CMP_BLOCK + 1


def _cmp_staging_rows(n_cmp):
    return -(-n_cmp * CMP_SLOT // 8) * 8


def _stage_cmp_chunk(src_ref, chunk, kc_t, vc_t, n_cmp):
    per_chunk = LANES // CMP_BLOCK
    for s, t in enumerate((kc_t.astype(bf16).T.astype(f32), vc_t.astype(bf16).T.astype(f32))):
        for nl in range(per_chunk):
            n = chunk * per_chunk + nl
            slot = n // 2 + (n % 2) * (n_cmp // 2)
            src_ref[s, slot * CMP_SLOT:slot * CMP_SLOT + CMP_BLOCK, :] = t[nl * CMP_BLOCK:(nl + 1) * CMP_BLOCK]


def _compress(src_refs, w1_ref, w2_ref, gk0, n_cmp):
    acc = jnp.zeros((len(src_refs) * n_cmp, 2 * LANES), f32)
    per_dot = 8
    for j0 in range(0, CMP_BLOCK, per_dot):
        xs = jnp.concatenate(
            [jnp.concatenate([src[s, pl.ds(j, n_cmp, stride=CMP_SLOT), :].astype(bf16)
                              for j in range(j0, j0 + per_dot) for s in range(2)], axis=1)
             for src in src_refs], axis=0)
        w = w1_ref[j0 * 2 * LANES:(j0 + per_dot) * 2 * LANES, :]
        acc = acc + jnp.dot(xs, w, preferred_element_type=f32)
    hmid = jax.nn.gelu(acc).astype(bf16)
    y = jnp.dot(hmid, w2_ref[...], preferred_element_type=f32)
    ck = _seg_rmsnorm(y[:, 0:LANES], gk0, HEAD_DIM)
    return jnp.concatenate([ck, y[:, LANES:2 * LANES]], axis=1)


def _compress_prompt_kernel(kv_ref, w1_ref, w2_ref, gk0_ref, out_ref, src_ref, *, n_cmp):
    for c in range(kv_ref.shape[1] // LANES):
        cols = slice(c * LANES, (c + 1) * LANES)
        _stage_cmp_chunk(src_ref, c, kv_ref[0:LANES, cols], kv_ref[LANES:2 * LANES, cols], n_cmp)
    out_ref[...] = _compress([src_ref], w1_ref, w2_ref, gk0_ref[...], n_cmp).astype(bf16)


def _compress_prompt(kv_t, w1bd, w2bd, gk0):
    B, _, L = kv_t.shape
    n_cmp = L // CMP_BLOCK
    return pl.pallas_call(
        functools.partial(_compress_prompt_kernel, n_cmp=n_cmp),
        out_shape=jax.ShapeDtypeStruct((B, n_cmp, 2 * LANES), bf16),
        grid=(B,),
        in_specs=[pl.BlockSpec((None, 2 * LANES, L), lambda b: (b, 0, 0)),
                  pl.BlockSpec(w1bd.shape, lambda b: (0, 0)),
                  pl.BlockSpec(w2bd.shape, lambda b: (0, 0)),
                  pl.BlockSpec(gk0.shape, lambda b: (0, 0))],
        out_specs=pl.BlockSpec((None, n_cmp, 2 * LANES), lambda b: (b, 0, 0)),
        scratch_shapes=[pltpu.VMEM((2, _cmp_staging_rows(n_cmp), LANES), f32)],
        compiler_params=_params(("arbitrary",)),
        name="compress_prompt",
    )(kv_t, w1bd, w2bd, gk0)


def _select_blocks_t(score, valid, n_keep):
    nb = score.shape[0]
    jj = lax.broadcasted_iota(i32, score.shape, 0)
    cnt = jnp.zeros(score.shape, f32)
    for k in range(nb):
        sk = score[k:k + 1, :]
        better = (sk > score) | ((sk == score) & (jj > k))
        cnt = cnt + jnp.where(better, 1.0, 0.0)
    return (cnt < n_keep) & valid


def _sel_tile_classes(n_tiles):
    return sorted({-(-n_tiles * c // SEL_CLASSES) for c in range(1, SEL_CLASSES + 1)})


def _nsa_prompt_kernel(q_ref, gate_ref, katt_ref, cmp_ref, tc_ref, tt_ref, ex_ref, o_ref,
                       selx_ref, s_ref, os_ref, part_ref, gsel_ref, *, L):
    i = pl.program_id(1)
    n_cmp = L // CMP_BLOCK
    n_sel = L // SEL_BLOCK
    n_tiles = L // LANES
    qb = SEL_BLOCK
    rows = NSA_GROUP * qb
    units = [(sq, g) for sq in range(q_ref.shape[0]) for g in range(NSA_KV_HEADS)]
    nu = range(len(units))
    row64 = lax.broadcasted_iota(i32, (qb, LANES), 0)
    lane = lax.broadcasted_iota(i32, (qb, LANES), 1)
    qpos = i * qb + row64

    qs = [jnp.concatenate(
        [q_ref[sq, :, (NSA_GROUP * g + r) * HEAD_DIM:(NSA_GROUP * g + r + 1) * HEAD_DIM] for r in range(NSA_GROUP)],
        axis=0) for sq, g in units]

    def attend(k_off, v_off, tile0, n_t, mask_fn):
        chunk = 4

        def key_rows(t0, count, off, u):
            sq, g = units[u]
            r0 = pl.multiple_of((tile0 + t0) * LANES, LANES)
            return katt_ref[sq, pl.ds(r0, count * LANES), off + g * HEAD_DIM:off + (g + 1) * HEAD_DIM]

        def scores(t0, count, mruns):
            dots = [lax.dot_general(qs[u], key_rows(t0, count, k_off, u), NT, preferred_element_type=f32)
                    for u in nu]
            out = list(mruns)
            for j in range(count):
                ta = tile0 + t0 + j
                bias_idx = jnp.clip(i - 2 * ta, 0, 4)
                kpos = ta * LANES + lane
                for u in nu:
                    s = dots[u][:, j * LANES:(j + 1) * LANES] + tt_ref[bias_idx, units[u][1]]
                    msk = mask_fn(u, ta, kpos)
                    s = jnp.where(msk[None], s.reshape(NSA_GROUP, qb, LANES), NEG).reshape(rows, LANES)
                    s_ref[u, t0 + j] = s
                    out[u] = jnp.maximum(out[u], s)
            return tuple(out)

        def values(t0, count, carry, ms):
            ls, accs = list(carry[0]), list(carry[1])
            ps = []
            for u in nu:
                pj = [jnp.exp(s_ref[u, t0 + j] - ms[u]) for j in range(count)]
                for p in pj:
                    ls[u] = ls[u] + p
                ps.append(jnp.concatenate([p.astype(bf16) for p in pj], axis=1))
            for u in nu:
                accs[u] = accs[u] + jnp.dot(ps[u], key_rows(t0, count, v_off, u), preferred_element_type=f32)
            return tuple(ls), tuple(accs)

        def over_chunks(fn, carry):
            for t0 in range(0, n_t, chunk):
                carry = fn(t0, min(chunk, n_t - t0), carry)
            return carry

        mruns = over_chunks(scores, tuple(jnp.full((rows, LANES), NEG, f32) for _ in nu))
        ms = [jnp.max(m, axis=-1, keepdims=True) for m in mruns]
        init = (tuple(jnp.zeros((rows, LANES), f32) for _ in nu),
                tuple(jnp.zeros((rows, HEAD_DIM), f32) for _ in nu))
        ls, accs = over_chunks(lambda t0, count, cr: values(t0, count, cr, ms), init)
        return [accs[u] / jnp.sum(ls[u], axis=-1, keepdims=True) for u in nu]

    n_wt = min(WINDOW // LANES + 1, n_tiles)
    w0 = jnp.clip((i - WINDOW // qb) // 2, 0, n_tiles - n_wt)
    o_w = attend(2 * KV_WIDTH, 3 * KV_WIDTH, w0, n_wt,
                 lambda u, ta, kpos: (kpos <= qpos) & (qpos - kpos <= WINDOW))

    o_c, imp_t = [], []
    col = lax.broadcasted_iota(i32, (rows, n_cmp), 1)
    rq = lax.broadcasted_iota(i32, (rows, n_cmp), 0) & (qb - 1)
    blk = jnp.where(col < n_cmp // 2, 2 * col, 2 * col - (n_cmp - 1))
    vis = (blk + 1) * CMP_BLOCK - 1 <= i * qb + rq
    for u, (sq, g) in enumerate(units):
        ck = cmp_ref[sq, :, g * HEAD_DIM:(g + 1) * HEAD_DIM]
        cv = cmp_ref[sq, :, LANES + g * HEAD_DIM:LANES + (g + 1) * HEAD_DIM]
        s = lax.dot_general(qs[u], ck, NT, preferred_element_type=f32) + tc_ref[g]
        e, den = _softmax_parts(s, vis)
        p = e / jnp.where(den > 0, den, 1.0)
        o_c.append(jnp.dot(p.astype(bf16), cv, preferred_element_type=f32))
        ps_t = (p[0:qb] + p[qb:2 * qb] + p[2 * qb:3 * qb] + p[3 * qb:4 * qb]).T
        imp_t.append(ps_t[0:n_sel] + ps_t[n_sel:2 * n_sel])

    imp = jnp.concatenate(imp_t, axis=1)
    jj = lax.broadcasted_iota(i32, imp.shape, 0)
    valid = jj <= i
    forced = valid & ((jj == 0) | (jj >= i - 1))
    score = jnp.where(forced, FORCED_SCORE, jnp.where(valid, imp, INVALID_SCORE))
    sel = _select_blocks_t(score, valid, min(TOP_N, n_sel))
    selx = lax.dot_general(jnp.where(sel, 1.0, 0.0).astype(bf16), ex_ref[...], TN, preferred_element_type=f32)
    for u in nu:
        for t in range(n_tiles):
            selx_ref[u, t] = selx[u * qb:(u + 1) * qb, t * LANES:(t + 1) * LANES]

    for u, (sq, g) in enumerate(units):
        gates = gate_ref[sq]
        for r in range(NSA_GROUP):
            h = NSA_GROUP * g + r
            hs = slice(h * HEAD_DIM, (h + 1) * HEAD_DIM)
            sl = slice(r * qb, (r + 1) * qb)
            part_ref[sq, :, hs] = gates[:, 3 * h:3 * h + 1] * o_c[u][sl] + gates[:, 3 * h + 2:3 * h + 3] * o_w[u][sl]
            gsel_ref[sq, :, hs] = jnp.broadcast_to(gates[:, 3 * h + 1:3 * h + 2], (qb, HEAD_DIM))

    need = i // 2 + 1
    prev = 0
    for n_t in _sel_tile_classes(n_tiles):
        @pl.when((need > prev) & (need <= n_t))
        def _(n_t=n_t):
            o_s = attend(0, KV_WIDTH, 0, n_t, lambda u, ta, kpos: (selx_ref[u, ta] > 0.5) & (kpos <= qpos))
            for u in nu:
                os_ref[u] = o_s[u]
        prev = n_t

    for u, (sq, g) in enumerate(units):
        o_s = os_ref[u]
        for r in range(NSA_GROUP):
            hs = slice((NSA_GROUP * g + r) * HEAD_DIM, (NSA_GROUP * g + r + 1) * HEAD_DIM)
            o = part_ref[sq, :, hs] + gsel_ref[sq, :, hs] * o_s[r * qb:(r + 1) * qb]
            o_ref[sq, :, hs] = o.astype(bf16)


def _nsa_prompt(qn, gates, katt, cmp, tc, tt, expand, B, L, nseq=2):
    nqb = L // SEL_BLOCK
    qb = SEL_BLOCK
    n_tiles = L // LANES
    rows = NSA_GROUP * qb
    nseq = math.gcd(nseq, B)
    n_units = nseq * NSA_KV_HEADS
    seq3 = lambda a: a.reshape(B, L, a.shape[-1])
    return pl.pallas_call(
        functools.partial(_nsa_prompt_kernel, L=L),
        out_shape=jax.ShapeDtypeStruct((B, L, NSA_WIDTH), bf16),
        grid=(B // nseq, nqb),
        in_specs=[pl.BlockSpec((nseq, qb, NSA_WIDTH), lambda b, i: (b, i, 0)),
                  pl.BlockSpec((nseq, qb, LANES), lambda b, i: (b, i, 0)),
                  pl.BlockSpec((nseq, L, 4 * KV_WIDTH), lambda b, i: (b, 0, 0)),
                  pl.BlockSpec((nseq, L // CMP_BLOCK, 2 * LANES), lambda b, i: (b, 0, 0)),
                  pl.BlockSpec((None,) + tc.shape[1:], lambda b, i: (i, 0, 0, 0)),
                  pl.BlockSpec(tt.shape, lambda b, i: (0, 0, 0, 0)),
                  pl.BlockSpec(expand.shape, lambda b, i: (0, 0))],
        out_specs=pl.BlockSpec((nseq, qb, NSA_WIDTH), lambda b, i: (b, i, 0)),
        scratch_shapes=[pltpu.VMEM((n_units, n_tiles, qb, LANES), f32),
                        pltpu.VMEM((n_units, n_tiles, rows, LANES), f32),
                        pltpu.VMEM((n_units, rows, HEAD_DIM), f32),
                        pltpu.VMEM((nseq, qb, NSA_WIDTH), f32), pltpu.VMEM((nseq, qb, NSA_WIDTH), f32)],
        compiler_params=_params(("arbitrary", "arbitrary")),
        name="nsa_prompt",
    )(seq3(qn), seq3(gates), seq3(katt), cmp, tc, tt, expand).reshape(B * L, NSA_WIDTH)


def _hgrn_gates(hf, lb):
    f = lb + (1.0 - lb) * jax.nn.sigmoid(hf)
    return f, 1.0 - f, jnp.log(f)


def _hgrn_out(o, gn, hgate):
    return _rmsnorm_rows(o, gn) * jax.nn.silu(hgate)


def _hgrn_prompt_kernel(h4_ref, lb_ref, gn_ref, o_ref, s_out_ref, st_ref, *, tc):
    c = pl.program_id(1)
    nb = HG_BLOCK
    nseq = h4_ref.shape[0]

    @pl.when(c == 0)
    def _():
        st_ref[...] = jnp.zeros_like(st_ref)

    t_row = lax.broadcasted_iota(i32, (nb, HG_DIM), 0)
    t_col = lax.broadcasted_iota(i32, (nb, 1), 0)
    gn = gn_ref[...]
    chains = [(sq, h) for sq in range(nseq) for h in range(HG_HEADS)]

    def cumsum_rows(x):
        shift = 1
        while shift < nb:
            x = x + jnp.where(t_row >= shift, pltpu.roll(x, shift, axis=0), 0.0)
            shift *= 2
        return x

    def block(bi, _):
        r0 = pl.multiple_of(bi * nb, nb)

        def sl(sq, h, part):
            return h4_ref[sq, pl.ds(r0, nb), part * HG_WIDTH + h * HG_DIM:part * HG_WIDTH + (h + 1) * HG_DIM]

        qs, ks, vs, bs, sts, os_ = [], [], [], [], [], []
        for sq, h in chains:
            _, k, glog = _hgrn_gates(sl(sq, h, 1), lb_ref[:, h * HG_DIM:(h + 1) * HG_DIM])
            qs.append(sl(sq, h, 0) * HG_SCALE)
            ks.append(k)
            vs.append(sl(sq, h, 2))
            bs.append(cumsum_rows(glog))
        for c_, (sq, h) in enumerate(chains):
            sts.append(st_ref[sq, h])
            os_.append(lax.dot_general((qs[c_] * jnp.exp(bs[c_])).astype(bf16), sts[c_].astype(bf16), NT,
                                       preferred_element_type=f32))
        for c_ in range(len(chains)):
            q, k, v, b, o = qs[c_], ks[c_], vs[c_], bs[c_], os_[c_]
            parts = [o[r0_:r0_ + SUBLANES] for r0_ in range(0, nb, SUBLANES)]
            for s in range(nb):
                for gi in range(s // SUBLANES, nb // SUBLANES):
                    rs = slice(gi * SUBLANES, (gi + 1) * SUBLANES)
                    w = q[rs] * k[s:s + 1] * jnp.exp(b[rs] - b[s:s + 1])
                    a = jnp.sum(w, axis=-1, keepdims=True)
                    if gi == s // SUBLANES:
                        a = jnp.where(t_col[rs] >= s, a, 0.0)
                    parts[gi] = parts[gi] + a * v[s:s + 1]
            os_[c_] = jnp.concatenate(parts, axis=0)
        for c_, (sq, h) in enumerate(chains):
            b_last = bs[c_][nb - 1:nb]
            kt = ks[c_] * jnp.exp(b_last - bs[c_])
            upd = lax.dot_general(vs[c_].astype(bf16), kt.astype(bf16), TN, preferred_element_type=f32)
            st_ref[sq, h] = jnp.exp(b_last) * sts[c_] + upd
            o_ref[sq, pl.ds(r0, nb), h * HG_DIM:(h + 1) * HG_DIM] = _hgrn_out(os_[c_], gn, sl(sq, h, 3)).astype(bf16)
        return 0

    lax.fori_loop(0, tc // nb, block, 0, unroll=2)

    @pl.when(c == pl.num_programs(1) - 1)
    def _():
        for sq in range(nseq):
            for h in range(HG_HEADS):
                s_out_ref[sq, h] = st_ref[sq, h].T


def _hgrn_prompt(h4, lb, gn, B, L, tc=256, nseq=4):
    nc = L // tc
    nseq = math.gcd(nseq, B)
    return pl.pallas_call(
        functools.partial(_hgrn_prompt_kernel, tc=tc),
        out_shape=(jax.ShapeDtypeStruct((B, L, HG_WIDTH), bf16),
                   jax.ShapeDtypeStruct((B, HG_HEADS, HG_DIM, HG_DIM), f32)),
        grid=(B // nseq, nc),
        in_specs=[pl.BlockSpec((nseq, tc, 4 * HG_WIDTH), lambda b, c: (b, c, 0)),
                  pl.BlockSpec(lb.shape, lambda b, c: (0, 0)),
                  pl.BlockSpec(gn.shape, lambda b, c: (0, 0))],
        out_specs=(pl.BlockSpec((nseq, tc, HG_WIDTH), lambda b, c: (b, c, 0)),
                   pl.BlockSpec((nseq, HG_HEADS, HG_DIM, HG_DIM), lambda b, c: (b, 0, 0, 0))),
        scratch_shapes=[pltpu.VMEM((nseq, HG_HEADS, HG_DIM, HG_DIM), f32)],
        compiler_params=_params(("arbitrary", "arbitrary")),
        name="hgrn_prompt",
    )(h4.reshape(B, L, 4 * HG_WIDTH), lb, gn)


def _ffn_kernel(*refs, decode, tiles_per_seq):
    if decode:
        (x_ref, on_ref, oh_ref, mg_ref, wn_ref, wh_ref, wo_ref, g2_ref, wg_ref, wu_ref, cw_ref, cb_ref, wd_ref,
         cs0_ref, cs1_ref, y_ref, gate_out_ref, x1_ref, xn2_ref, acc_ref) = refs
    else:
        (x_ref, on_ref, oh_ref, mg_ref, wn_ref, wh_ref, wo_ref, g2_ref, wg_ref, wu_ref, cw_ref, cb_ref, wd_ref,
         y_ref, gate_out_ref, x1_ref, xn2_ref, acc_ref, carry_ref) = refs
    i = pl.program_id(0)
    fi = pl.program_id(1)
    tm = x_ref.shape[0]

    @pl.when(fi == 0)
    def _():
        y_a = jnp.dot(on_ref[...], wn_ref[...], preferred_element_type=f32)
        y_b = jnp.dot(oh_ref[...], wh_ref[...], preferred_element_type=f32)
        mg = mg_ref[...]
        merged = mg[:, 0:D_MODEL] * y_a + mg[:, D_MODEL:2 * D_MODEL] * y_b
        x1 = x_ref[...] + jnp.dot(merged.astype(bf16), wo_ref[...], preferred_element_type=f32)
        x1_ref[...] = x1
        xn2_ref[...] = _rmsnorm_rows(x1, g2_ref[...]).astype(bf16)
        acc_ref[...] = jnp.zeros_like(acc_ref)

    xn2 = xn2_ref[...]
    tf = wg_ref.shape[1]
    if not decode:
        @pl.when(i % tiles_per_seq == 0)
        def _():
            carry_ref[fi] = jnp.zeros(carry_ref.shape[1:], f32)

    acc = acc_ref[...]
    for a in range(0, tf, FFN_SLICE):
        cs = slice(a, min(a + FFN_SLICE, tf))
        gate = jnp.dot(xn2, wg_ref[:, cs], preferred_element_type=f32)
        up = jnp.dot(xn2, wu_ref[:, cs], preferred_element_type=f32)
        cw = cw_ref[:, cs]
        if decode:
            prev2, prev1 = cs0_ref[:, cs], cs1_ref[:, cs]
            gate_out_ref[:, cs] = gate
        else:
            carry = carry_ref[fi, :, cs]
            rid = lax.broadcasted_iota(i32, gate.shape, 0)
            prev1 = jnp.where(rid == 0, carry[7:8], pltpu.roll(gate, 1, axis=0))
            prev2 = jnp.where(rid == 0, carry[6:7], jnp.where(rid == 1, carry[7:8], pltpu.roll(gate, 2, axis=0)))
            carry_ref[fi, 6:8, cs] = gate[tm - 2:tm]
            gate_out_ref[:, cs] = gate[tm - 2:tm]
        conv = cb_ref[:, cs] + cw[0:1] * prev2 + cw[1:2] * prev1 + cw[2:3] * gate
        hmid = (jax.nn.silu(conv) * up).astype(bf16)
        acc = acc + jnp.dot(hmid, wd_ref[cs, :], preferred_element_type=f32)
    acc_ref[...] = acc

    @pl.when(fi == pl.num_programs(1) - 1)
    def _():
        y_ref[...] = x1_ref[...] + acc_ref[...]


def _ffn(x2d, o_nsa, o_hg, mgs, wn, wh, wo, g2, wg, wu, cw, cb, wd, *, tm, tf, seq_len=None, conv_state=None):
    n = x2d.shape[0]
    decode = conv_state is not None
    nf = D_FF // tf
    row = lambda width: pl.BlockSpec((tm, width), lambda i, f: (i, 0))
    full = lambda a: pl.BlockSpec(a.shape, lambda i, f: (0,) * a.ndim)
    in_specs = [row(D_MODEL), row(NSA_WIDTH), row(HG_WIDTH), row(2 * D_MODEL), full(wn), full(wh), full(wo), full(g2),
                pl.BlockSpec((D_MODEL, tf), lambda i, f: (0, f)), pl.BlockSpec((D_MODEL, tf), lambda i, f: (0, f)),
                pl.BlockSpec((3, tf), lambda i, f: (0, f)), pl.BlockSpec((1, tf), lambda i, f: (0, f)),
                pl.BlockSpec((tf, D_MODEL), lambda i, f: (f, 0))]
    args = [x2d, o_nsa, o_hg, mgs, wn, wh, wo, g2, wg, wu, cw, cb, wd]
    scratch = [pltpu.VMEM((tm, D_MODEL), f32), pltpu.VMEM((tm, D_MODEL), bf16), pltpu.VMEM((tm, D_MODEL), f32)]
    if decode:
        in_specs += [pl.BlockSpec((tm, tf), lambda i, f: (i, f))] * 2
        args += [conv_state[:, 0], conv_state[:, 1]]
        gate_shape = jax.ShapeDtypeStruct((n, D_FF), f32)
        gate_spec = pl.BlockSpec((tm, tf), lambda i, f: (i, f))
        tiles_per_seq = 1
    else:
        tiles_per_seq = seq_len // tm
        gate_shape = jax.ShapeDtypeStruct((n // tm, 2, D_FF), f32)
        gate_spec = pl.BlockSpec((None, 2, tf), lambda i, f: (i, 0, f))
        scratch.append(pltpu.VMEM((nf, 8, tf), f32))
    return pl.pallas_call(
        functools.partial(_ffn_kernel, decode=decode, tiles_per_seq=tiles_per_seq),
        out_shape=(jax.ShapeDtypeStruct((n, D_MODEL), f32), gate_shape),
        grid=(n // tm, nf),
        in_specs=in_specs,
        out_specs=(row(D_MODEL), gate_spec),
        scratch_shapes=scratch,
        compiler_params=_params(("arbitrary", "arbitrary")),
        name="ffn_decode" if decode else "ffn_prompt",
    )(*args)


def _nsa_decode_kernel(*refs, n_pages, n_win, nseq):
    page_refs = refs[1:1 + nseq * n_pages]
    (q_ref, kvn_ref, winn_ref, win_all_ref, win_ref, w1_ref, w2_ref, gk0_ref, bias_ref,
     ow_ref, win_out_ref, cmp_ref, ssel_ref, snew_ref, vsel_ref, src_ref, ksel_ref, wcol_ref) = refs[1 + nseq * n_pages:]
    step = pl.program_id(0)
    past = n_pages * PAGE_SIZE
    n_cmp = past // CMP_BLOCK
    heads = NSA_HEADS
    samples = range(nseq)

    @pl.when(step == 0)
    def _():
        for t in range(2):
            wcol_ref[t * LANES:(t + 1) * LANES, :] = win_all_ref[:, t * LANES:(t + 1) * LANES].T

    for sq in samples:
        for k in range(n_pages):
            page = page_refs[sq * n_pages + k]
            cols = slice(k * PAGE_SIZE, (k + 1) * PAGE_SIZE)
            _stage_cmp_chunk(src_ref.at[sq], k, page[0:LANES, :], page[LANES:2 * LANES, :], n_cmp)
            ksel_ref[sq, :, cols] = page[2 * LANES:3 * LANES, :].astype(bf16)
            vsel_ref[sq, :, cols] = page[3 * LANES:4 * LANES, :].astype(bf16)

    hrow = lax.broadcasted_iota(i32, (heads, 1), 0)
    g0 = hrow < NSA_GROUP
    bias = bias_ref[...]
    b_s = bias[:, 0:past]
    b_w = bias[:, past:past + n_win]
    b_new = bias[:, past + n_win + LANES:past + n_win + LANES + 1]
    qs = [q_ref[sq] for sq in samples]
    attend = _decode_attend

    def scores(q, k_t, b_past, k_new):
        s_past = jnp.dot(q, k_t, preferred_element_type=f32) + b_past
        s_new = jnp.sum(q.astype(f32) * k_new, axis=-1, keepdims=True) + b_new
        return s_past, s_new

    lane_b = lax.broadcasted_iota(i32, wcol_ref.shape, 1)
    lane_w = lax.broadcasted_iota(i32, win_ref.shape[1:], 1)
    for sq in samples:
        winn, kvn = winn_ref[sq], kvn_ref[sq]
        s_past, s_new = scores(qs[sq], ksel_ref[sq], b_s, kvn[:, 2 * LANES:3 * LANES])
        ssel_ref[sq] = s_past
        snew_ref[sq] = jnp.broadcast_to(s_new, (heads, LANES))
        s_win = scores(qs[sq], win_ref[sq, 0:LANES, :].astype(bf16), b_w, winn[:, 0:LANES])
        ow_ref[sq] = attend(s_win, None, win_ref[sq, LANES:2 * LANES, :].astype(bf16), winn[:, LANES:2 * LANES],
                            g0)
        new_col = jnp.sum(jnp.where(lane_b == step * nseq + sq, wcol_ref[...], 0.0), axis=-1, keepdims=True)
        win_out_ref[sq] = jnp.where(lane_w == n_win - 1, new_col, pltpu.roll(win_ref[sq], n_win - 1, axis=1))

    cmp = _compress([src_ref.at[sq] for sq in samples], w1_ref, w2_ref, gk0_ref[...], n_cmp)
    for sq in samples:
        cmp_ref[sq] = cmp[sq * n_cmp:(sq + 1) * n_cmp].astype(bf16)


def _decode_attend(s, mask, v_t, v_new, g0):
    s_past, s_new = s
    sm = s_past if mask is None else jnp.where(mask, s_past, NEG)
    m = jnp.maximum(jnp.max(sm, axis=-1, keepdims=True), s_new)
    e = jnp.exp(sm - m)
    e_new = jnp.exp(s_new - m)
    den = jnp.sum(e, axis=-1, keepdims=True) + e_new
    o2 = (lax.dot_general(e.astype(bf16), v_t, NT, preferred_element_type=f32) + e_new * v_new) / den
    return jnp.where(g0, o2[:, 0:HEAD_DIM], o2[:, HEAD_DIM:2 * HEAD_DIM])


def _nsa_decode_tail_kernel(q_ref, cmp_ref, ssel_ref, snew_ref, vsel_ref, ow_ref, kvn_ref, gate_ref, bias_ref, ex_ref,
                            o_ref, *, n_sel):
    samples = range(q_ref.shape[0])
    heads = NSA_HEADS
    n_cmp = cmp_ref.shape[1]
    g0 = lax.broadcasted_iota(i32, (heads, 1), 0) < NSA_GROUP
    b_c = bias_ref[...]
    cur = n_sel
    jrow = lax.broadcasted_iota(i32, (NSA_KV_HEADS, n_sel), 1)
    forced = (jrow == 0) | (jrow >= cur - 1)
    n_keep = min(TOP_N, n_sel + 1) - 1
    ri = lax.broadcasted_iota(i32, (n_sel, n_sel), 0)
    ci = lax.broadcasted_iota(i32, (n_sel, n_sel), 1)

    s_c = [lax.dot_general(q_ref[sq], cmp_ref[sq, :, 0:LANES], NT, preferred_element_type=f32) + b_c for sq in samples]
    p_c = []
    for sq in samples:
        e_c, den_c = _softmax_parts(s_c[sq], jnp.full(s_c[sq].shape, True))
        p_c.append(e_c / jnp.where(den_c > 0, den_c, 1.0))
    o_c = [jnp.dot(p_c[sq].astype(bf16), cmp_ref[sq, :, LANES:2 * LANES], preferred_element_type=f32) for sq in samples]
    sels = []
    for sq in samples:
        ps = jnp.concatenate([jnp.sum(p_c[sq][0:NSA_GROUP], axis=0, keepdims=True),
                              jnp.sum(p_c[sq][NSA_GROUP:heads], axis=0, keepdims=True)], axis=0)
        imp = ps[:, 0:n_sel] + ps[:, n_sel:2 * n_sel]
        score_row = jnp.where(forced, FORCED_SCORE, imp)
        rows = []
        for g in range(NSA_KV_HEADS):
            rowb = jnp.broadcast_to(score_row[g:g + 1], (n_sel, n_sel))
            colb = jnp.sum(jnp.where(ri == ci, rowb, 0.0), axis=-1, keepdims=True)
            better = (colb > rowb) | ((colb == rowb) & (ri < ci))
            cnt = jnp.sum(jnp.where(better, 1.0, 0.0), axis=0, keepdims=True)
            rows.append(jnp.where(cnt < n_keep, 1.0, 0.0))
        sels.append(jnp.concatenate(rows, axis=0).astype(bf16))
    selx = [jnp.dot(sels[sq], ex_ref[...], preferred_element_type=f32) for sq in samples]
    for sq in samples:
        mask = jnp.where(g0, selx[sq][0:1], selx[sq][1:2]) > 0.5
        o_s = _decode_attend((ssel_ref[sq], snew_ref[sq][:, 0:1]), mask, vsel_ref[sq],
                             kvn_ref[sq][:, 3 * LANES:4 * LANES], g0)
        gates = gate_ref[sq]
        o_cg = jnp.where(g0, o_c[sq][:, 0:HEAD_DIM], o_c[sq][:, HEAD_DIM:2 * HEAD_DIM])
        o_ref[sq] = (gates[:, 0:1] * o_cg + gates[:, 1:2] * o_s + gates[:, 2:3] * ow_ref[sq]).astype(bf16)


def _nsa_decode(cache_t, page_table, q_bd, kv32, win32, gates, win_t, w1bd, w2bd, gk0, bias, expand,
                nseq=2, nseq_tail=8):
    Bs, n_pages = page_table.shape
    n_win = win_t.shape[2]
    past = n_pages * PAGE_SIZE
    n_cmp = past // CMP_BLOCK
    nseq = math.gcd(nseq, Bs)
    per_b = lambda shape: pl.BlockSpec((nseq,) + shape, lambda b, pt: (b,) + (0,) * len(shape))
    const = lambda a: pl.BlockSpec(a.shape, lambda b, pt: (0,) * a.ndim)
    page_spec = lambda sq, k: pl.BlockSpec((None, 4 * LANES, PAGE_SIZE), lambda b, pt: (pt[b * nseq + sq, k], 0, 0))
    q3 = q_bd.reshape(Bs, NSA_HEADS, LANES)
    kvn3 = kv32.reshape(Bs, 1, 4 * LANES)
    gates3 = gates[:, 0:3 * NSA_HEADS].reshape(Bs, NSA_HEADS, 3)
    grid_spec = pltpu.PrefetchScalarGridSpec(
        num_scalar_prefetch=1,
        grid=(Bs // nseq,),
        in_specs=[page_spec(sq, k) for sq in range(nseq) for k in range(n_pages)] + [
            per_b((NSA_HEADS, LANES)), per_b((1, 4 * LANES)), per_b((1, 2 * LANES)), const(win32),
            per_b((2 * LANES, n_win)), const(w1bd), const(w2bd), const(gk0), const(bias)],
        out_specs=(per_b((NSA_HEADS, HEAD_DIM)), per_b((2 * LANES, n_win)), per_b((n_cmp, 2 * LANES)),
                   per_b((NSA_HEADS, past)), per_b((NSA_HEADS, LANES)), per_b((LANES, past))),
        scratch_shapes=[pltpu.VMEM((nseq, 2, _cmp_staging_rows(n_cmp), LANES), f32),
                        pltpu.VMEM((nseq, LANES, past), bf16), pltpu.VMEM((2 * LANES, Bs), f32)],
    )
    o_w, win_out, cmp, s_sel, s_new, v_sel = pl.pallas_call(
        functools.partial(_nsa_decode_kernel, n_pages=n_pages, n_win=n_win, nseq=nseq),
        out_shape=(jax.ShapeDtypeStruct((Bs, NSA_HEADS, HEAD_DIM), f32),
                   jax.ShapeDtypeStruct((Bs, 2 * LANES, n_win), f32),
                   jax.ShapeDtypeStruct((Bs, n_cmp, 2 * LANES), bf16),
                   jax.ShapeDtypeStruct((Bs, NSA_HEADS, past), f32),
                   jax.ShapeDtypeStruct((Bs, NSA_HEADS, LANES), f32),
                   jax.ShapeDtypeStruct((Bs, LANES, past), bf16)),
        grid_spec=grid_spec,
        compiler_params=_params(("arbitrary",)),
        name="nsa_decode",
    )(page_table, *([cache_t] * (nseq * n_pages)), q3, kvn3, win32.reshape(Bs, 1, 2 * LANES), win32, win_t,
      w1bd, w2bd, gk0, bias)

    ns = math.gcd(nseq_tail, Bs)
    per_t = lambda shape: pl.BlockSpec((ns,) + shape, lambda b: (b,) + (0,) * len(shape))
    const_t = lambda a: pl.BlockSpec(a.shape, lambda b: (0,) * a.ndim)
    b_c = bias[:, past + n_win:past + n_win + n_cmp]
    o = pl.pallas_call(
        functools.partial(_nsa_decode_tail_kernel, n_sel=past // SEL_BLOCK),
        out_shape=jax.ShapeDtypeStruct((Bs, NSA_HEADS, HEAD_DIM), bf16),
        grid=(Bs // ns,),
        in_specs=[per_t((NSA_HEADS, LANES)), per_t((n_cmp, 2 * LANES)), per_t((NSA_HEADS, past)),
                  per_t((NSA_HEADS, LANES)), per_t((LANES, past)), per_t((NSA_HEADS, HEAD_DIM)),
                  per_t((1, 4 * LANES)), per_t((NSA_HEADS, 3)), const_t(b_c), const_t(expand)],
        out_specs=per_t((NSA_HEADS, HEAD_DIM)),
        compiler_params=_params(("arbitrary",)),
        name="nsa_decode_select",
    )(q3, cmp, s_sel, s_new, v_sel, o_w, kvn3, gates3, b_c, expand)
    return o, win_out


def _hgrn_decode_kernel(h4_ref, lb_ref, gn_ref, s_ref, o_ref, s_out_ref, ft_ref):
    b = pl.program_id(0)
    nbatch = h4_ref.shape[0]

    @pl.when(b == 0)
    def _():
        for h in range(HG_HEADS):
            f, _, _ = _hgrn_gates(h4_ref[:, HG_WIDTH + h * HG_DIM:HG_WIDTH + (h + 1) * HG_DIM],
                                  lb_ref[:, h * HG_DIM:(h + 1) * HG_DIM])
            ft_ref[h] = f.T

    lane = lax.broadcasted_iota(i32, (HG_DIM, nbatch), 1)
    gn = gn_ref[...]
    for sq in range(s_ref.shape[0]):
        sample = b * s_ref.shape[0] + sq
        row = h4_ref[pl.ds(sample, 1), :]
        for h in range(HG_HEADS):
            part = lambda k: row[:, k * HG_WIDTH + h * HG_DIM:k * HG_WIDTH + (h + 1) * HG_DIM]
            f_col = jnp.sum(jnp.where(lane == sample, ft_ref[h], 0.0), axis=-1, keepdims=True)
            s_new = f_col * s_ref[sq, h] + (1.0 - f_col) * part(2)
            s_out_ref[sq, h] = s_new
            q = jnp.broadcast_to(part(0) * HG_SCALE, (8, HG_DIM)).astype(bf16)
            o = jnp.dot(q, s_new.astype(bf16), preferred_element_type=f32)[0:1]
            o_ref[sq, :, h * HG_DIM:(h + 1) * HG_DIM] = _hgrn_out(o, gn, part(3)).astype(bf16)


def _hgrn_decode(h4, lb, gn, state, nseq=8):
    Bs = h4.shape[0]
    nseq = math.gcd(nseq, Bs)
    sspec = pl.BlockSpec((nseq, HG_HEADS, HG_DIM, HG_DIM), lambda b: (b, 0, 0, 0))
    return pl.pallas_call(
        _hgrn_decode_kernel,
        out_shape=(jax.ShapeDtypeStruct((Bs, 1, HG_WIDTH), bf16),
                   jax.ShapeDtypeStruct((Bs, HG_HEADS, HG_DIM, HG_DIM), f32)),
        grid=(Bs // nseq,),
        in_specs=[pl.BlockSpec(h4.shape, lambda b: (0, 0)), pl.BlockSpec(lb.shape, lambda b: (0, 0)),
                  pl.BlockSpec(gn.shape, lambda b: (0, 0)), sspec],
        out_specs=(pl.BlockSpec((nseq, 1, HG_WIDTH), lambda b: (b, 0, 0)), sspec),
        scratch_shapes=[pltpu.VMEM((HG_HEADS, HG_DIM, Bs), f32)],
        compiler_params=_params(("arbitrary",)),
        name="hgrn_decode",
    )(h4, lb, gn, state)


def _split_w_in(w_t):
    idx = np.cumsum(SPLITS)[:-1]
    return jnp.split(w_t, [int(v) for v in idx], axis=0)


def kernel(x_prompt, x_sample, cache_kv, page_table, state_kv_win, state_hgrn, state_conv, rel_table, hg_lb_logits,
           norm1_g, w_in, q_norm_g, k_norm_g, phi_k_w1, phi_k_w2, phi_v_w1, phi_v_w2, hg_norm_g, w_nsa_out, w_hg_out,
           w_o, norm2_g, w_gate, w_up, conv_w, conv_b, w_down):
    assert w_in.shape[0] == 1, "one layer"
    Bp, L, _ = x_prompt.shape
    Bs = x_sample.shape[0]
    n_pages = page_table.shape[1]
    past = n_pages * PAGE_SIZE
    n_win = state_kv_win.shape[2]
    assert x_sample.shape[1] == 1 and n_win == WINDOW and past % (2 * CMP_BLOCK) == 0 and L % LANES == 0

    wq, wkv, wgl, whq, whf, whi, whg, wmg = _split_w_in(jnp.swapaxes(w_in[0], 0, 1).astype(bf16))
    wgl = jnp.pad(wgl, ((0, LANES - wgl.shape[0]), (0, 0)))
    w_prompt = jnp.concatenate([wq, wkv, whq, whf, whi, whg, wmg, wgl], axis=0)
    wq_h = wq.reshape(NSA_HEADS, HEAD_DIM, D_MODEL)
    zq = jnp.zeros_like(wq_h)
    in_g0 = (np.arange(NSA_HEADS) < NSA_GROUP)[:, None, None]
    wq_bd = jnp.where(in_g0, jnp.concatenate([wq_h, zq], axis=1), jnp.concatenate([zq, wq_h], axis=1))
    w_sample = jnp.concatenate([wq_bd.reshape(NSA_HEADS * LANES, D_MODEL), wkv, whq, whf, whi, whg, wmg, wgl], axis=0)
    g1 = norm1_g[0][None, :]
    gq = q_norm_g[0]
    gq_prompt = jnp.tile(gq, NSA_HEADS)[None, :]
    gq_sample = jnp.tile(gq, 2 * NSA_HEADS)[None, :]
    gk = [jnp.tile(k_norm_g[0, s], NSA_KV_HEADS)[None, :] for s in range(3)]

    def block_diag(mats):
        n = len(mats)
        lead = [(0, 0)] * (mats[0].ndim - 2)
        rows = [jnp.pad(m.astype(bf16), lead + [(0, 0), (t * HEAD_DIM, (n - 1 - t) * HEAD_DIM)])
                for t, m in enumerate(mats)]
        return jnp.concatenate(rows, axis=-2)

    w1bd = block_diag([phi_k_w1[0], phi_k_w1[0], phi_v_w1[0], phi_v_w1[0]]).reshape(CMP_BLOCK * 2 * LANES, 2 * LANES)
    w2bd = block_diag([phi_k_w2[0], phi_k_w2[0], phi_v_w2[0], phi_v_w2[0]])
    lb = jnp.cumsum(jax.nn.softmax(hg_lb_logits.astype(f32), axis=0), axis=0)[0][None, :]
    gn = hg_norm_g[0][None, :]
    ffn_w = (w_nsa_out[0].astype(bf16), w_hg_out[0].astype(bf16), w_o[0].astype(bf16), norm2_g[0][None, :],
             w_gate[0].astype(bf16), w_up[0].astype(bf16), conv_w[0], conv_b[0][None, :], w_down[0].astype(bf16))

    idx_c, idx_t = _prompt_bucket_tables(L)
    tc = _bias_tables(rel_table, jnp.asarray(idx_c))
    tt = _bias_tables(rel_table, jnp.asarray(idx_t))
    bias_dec = _bias_tables(rel_table, jnp.asarray(_decode_bucket_table(past, n_win))[None]).reshape(NSA_HEADS, -1)
    ex_prompt = jnp.asarray(_expand_np(L // SEL_BLOCK, L), dtype=bf16)
    ex_sample = jnp.asarray(_expand_np(past // SEL_BLOCK, past), dtype=bf16)

    xp = x_prompt.reshape(Bp * L, D_MODEL)
    qn, kv_t, win32, katt, h4, mgs, gates = _in_proj(xp, g1, w_prompt, gq_prompt, gk[1], gk[2],
                                                     qw=NSA_WIDTH, q_seg=HEAD_DIM, tm=512, seq_len=L)
    cmp = _compress_prompt(kv_t, w1bd, w2bd, gk[0])
    o_nsa = _nsa_prompt(qn, gates, katt, cmp, tc, tt, ex_prompt, Bp, L)
    o_hg, s_prompt = _hgrn_prompt(h4, lb, gn, Bp, L)
    tm_ffn = 512
    yp, gate_tails = _ffn(xp, o_nsa, o_hg.reshape(Bp * L, HG_WIDTH), mgs, *ffn_w, tm=tm_ffn, tf=D_FF // 2, seq_len=L)
    conv_p = gate_tails.reshape(Bp, L // tm_ffn, 2, D_FF)[:, -1]
    n_keep_win = min(WINDOW, L)
    win_p = jnp.swapaxes(win32[:, :, L - n_keep_win:], 1, 2).reshape(Bp, n_keep_win, 2, NSA_KV_HEADS, HEAD_DIM)

    xs = x_sample.reshape(Bs, D_MODEL)
    q_bd, kv32_s, win32_s, _, h4_s, mgs_s, gates_s = _in_proj(xs, g1, w_sample, gq_sample, gk[1], gk[2],
                                                              qw=NSA_HEADS * LANES, q_seg=LANES, tm=Bs)
    cache_t = jnp.swapaxes(cache_kv[0].reshape(cache_kv.shape[1], PAGE_SIZE, 4 * LANES), 1, 2)
    win_t = jnp.swapaxes(state_kv_win[0].reshape(Bs, n_win, 2 * LANES), 1, 2)
    o_nsa_s, win_out_t = _nsa_decode(cache_t, page_table, q_bd, kv32_s, win32_s, gates_s, win_t, w1bd, w2bd, gk[0],
                                     bias_dec, ex_sample)
    win_s = jnp.swapaxes(win_out_t, 1, 2)
    kv32 = jnp.swapaxes(kv_t, 1, 2)
    o_hg_s, s_sample = _hgrn_decode(h4_s, lb, gn, state_hgrn[0])
    ys, gate_s = _ffn(xs, o_nsa_s.reshape(Bs, NSA_WIDTH), o_hg_s.reshape(Bs, HG_WIDTH), mgs_s, *ffn_w,
                      tm=Bs, tf=256, conv_state=state_conv[0])
    conv_s = jnp.stack([state_conv[0][:, 1], gate_s], axis=1)

    kvh = (NSA_KV_HEADS, HEAD_DIM)
    return (yp.reshape(Bp, L, D_MODEL), ys.reshape(Bs, 1, D_MODEL),
            kv32.reshape((1, Bp, L, 4) + kvh), kv32_s.reshape((1, Bs, 1, 4) + kvh),
            win_p[None], win_s.reshape((1, Bs, n_win, 2) + kvh),
            s_prompt[None], s_sample[None], conv_p[None], conv_s[None])
```

```python
import functools
import math

import numpy as np
import jax
import jax.numpy as jnp
from jax import lax
from jax.experimental import pallas as pl
from jax.experimental.pallas import tpu as pltpu

f32, bf16, i32 = jnp.float32, jnp.bfloat16, jnp.int32

D_MODEL = 1024
PAGE_SIZE = 128
NSA_HEADS, NSA_KV_HEADS, NSA_GROUP, HEAD_DIM = 8, 2, 4, 64
CMP_BLOCK, SEL_BLOCK, TOP_N, WINDOW = 32, 64, 16, 512
SCALE = HEAD_DIM ** -0.5
NSA_WIDTH = NSA_HEADS * HEAD_DIM
KV_WIDTH = NSA_KV_HEADS * HEAD_DIM
HG_HEADS, HG_DIM = 4, 128
HG_WIDTH = HG_HEADS * HG_DIM
HG_SCALE = HG_DIM ** -0.5
HG_BLOCK = 16
SEL_CLASSES = 8
SUBLANES = 8
FFN_SLICE = 768
REL_BUCKETS, REL_MAX_DIST = 32, 128
FAR_TILE = -(-(REL_MAX_DIST + SEL_BLOCK) // SEL_BLOCK) + 1
D_FF = 2816
EPS = 1e-6
SPLITS = (NSA_WIDTH, 6 * KV_WIDTH, 3 * NSA_HEADS, HG_WIDTH, HG_WIDTH, HG_WIDTH, HG_WIDTH, 2 * D_MODEL)
LANES = 128
NEG = -1e30
FORCED_SCORE, INVALID_SCORE = 8.0, -1.0
VMEM_LIMIT = 56 * 1024 * 1024

NT = (((1,), (1,)), ((), ()))
TN = (((0,), (0,)), ((), ()))


def _params(sem):
    return pltpu.CompilerParams(dimension_semantics=sem, vmem_limit_bytes=VMEM_LIMIT)


def _bucket_np(dist):
    n = np.maximum(dist, 0)
    exact = REL_BUCKETS // 2
    ratio = np.log(np.maximum(n, 1).astype(np.float32) / np.float32(exact)) / np.float32(math.log(REL_MAX_DIST / exact))
    big = exact + (ratio.astype(np.float32) * np.float32(REL_BUCKETS - exact)).astype(np.int32)
    return np.where(n < exact, n, np.minimum(big, REL_BUCKETS - 1)).astype(np.int32)


def _cmp_perm(n_cmp):
    half = n_cmp // 2
    c = np.arange(n_cmp)
    return np.where(c < half, 2 * c, 2 * (c - half) + 1)


def _prompt_bucket_tables(L):
    nqb = L // SEL_BLOCK
    n_cmp = L // CMP_BLOCK
    qi = np.arange(SEL_BLOCK)
    c_end = (_cmp_perm(n_cmp) + 1) * CMP_BLOCK - 1
    idx_c = np.stack([_bucket_np((i * SEL_BLOCK + qi)[:, None] - c_end[None, :]) for i in range(nqb)])
    kj = np.arange(SEL_BLOCK)
    def tile(delta):
        return _bucket_np(delta * SEL_BLOCK + qi[:, None] - kj[None, :])
    idx_t = np.stack([np.concatenate([tile(e), tile(e - 1)], axis=1) for e in range(FAR_TILE + 1)])
    return idx_c.astype(np.int32), idx_t.astype(np.int32)


def _decode_bucket_table(past_len, n_win):
    n_cmp = past_len // CMP_BLOCK
    c_end = (_cmp_perm(n_cmp) + 1) * CMP_BLOCK - 1
    d_c = np.zeros((LANES,), np.int64)
    d_c[:n_cmp] = past_len - c_end
    d_s = past_len - np.arange(past_len)
    d_w = n_win - np.arange(n_win)
    d_new = np.zeros((LANES,), np.int64)
    return _bucket_np(np.concatenate([d_s, d_w, d_c, d_new]))[None, :].astype(np.int32)


def _expand_np(n_blocks, n_keys):
    e = (np.arange(n_keys)[None, :] // SEL_BLOCK == np.arange(n_blocks)[:, None])
    return e.astype(np.float32)


def _seg_rmsnorm(x, gain, seg):
    outs = []
    for t in range(x.shape[1] // LANES):
        xt = x[:, t * LANES:(t + 1) * LANES]
        sq = xt * xt
        if seg == LANES:
            r = lax.rsqrt(jnp.sum(sq, axis=-1, keepdims=True) * (1.0 / HEAD_DIM) + EPS)
        else:
            lo = lax.broadcasted_iota(i32, xt.shape, 1) < HEAD_DIM
            s_lo = jnp.sum(jnp.where(lo, sq, 0.0), axis=-1, keepdims=True)
            s_hi = jnp.sum(jnp.where(lo, 0.0, sq), axis=-1, keepdims=True)
            r = jnp.where(lo, lax.rsqrt(s_lo * (1.0 / HEAD_DIM) + EPS), lax.rsqrt(s_hi * (1.0 / HEAD_DIM) + EPS))
        outs.append(xt * r)
    y = outs[0] if len(outs) == 1 else jnp.concatenate(outs, axis=1)
    return y * gain


def _rmsnorm_rows(x, gain):
    return x * lax.rsqrt(jnp.mean(x * x, axis=-1, keepdims=True) + EPS) * gain


def _softmax_parts(s, mask):
    s = jnp.where(mask, s, NEG)
    m = jnp.max(s, axis=-1, keepdims=True)
    e = jnp.where(mask, jnp.exp(s - m), 0.0)
    return e, jnp.sum(e, axis=-1, keepdims=True)


def _bias_table_kernel(tab_ref, idx_ref, out_ref):
    idx = idx_ref[...]
    rows = idx.shape[0]
    for h in range(NSA_HEADS):
        acc = jnp.zeros(idx.shape, f32)
        for b in range(REL_BUCKETS):
            acc = jnp.where(idx == b, tab_ref[b, h], acc)
        r = h % NSA_GROUP
        out_ref[h // NSA_GROUP, r * rows:(r + 1) * rows, :] = acc


def _bias_tables(rel_table, idx):
    n, rows, w = idx.shape
    return pl.pallas_call(
        _bias_table_kernel,
        out_shape=jax.ShapeDtypeStruct((n, NSA_KV_HEADS, NSA_GROUP * rows, w), f32),
        grid=(n,),
        in_specs=[pl.BlockSpec(memory_space=pltpu.SMEM),
                  pl.BlockSpec((None, rows, w), lambda i: (i, 0, 0))],
        out_specs=pl.BlockSpec((None, NSA_KV_HEADS, NSA_GROUP * rows, w), lambda i: (i, 0, 0, 0)),
        compiler_params=_params(("arbitrary",)),
        name="bias_tables",
    )(rel_table, idx)


def _in_proj_kernel(x_ref, g1_ref, w_ref, gq_ref, gk1_ref, gk2_ref,
                    q_ref, kv_ref, win_ref, katt_ref, h4_ref, mg_ref, gl_ref, *, qw, q_seg, kv_feature_major):
    x = x_ref[...]
    xn = _rmsnorm_rows(x, g1_ref[...]).astype(bf16)

    def proj(c0, width):
        return jnp.dot(xn, w_ref[:, c0:c0 + width], preferred_element_type=f32)

    q_ref[...] = (_seg_rmsnorm(proj(0, qw), gq_ref[...], q_seg) * SCALE).astype(bf16)
    c = qw
    kv = proj(c, 6 * KV_WIDTH)
    c += 6 * KV_WIDTH
    k_sel = _seg_rmsnorm(kv[:, 2 * KV_WIDTH:3 * KV_WIDTH], gk1_ref[...], HEAD_DIM)
    k_win = _seg_rmsnorm(kv[:, 4 * KV_WIDTH:5 * KV_WIDTH], gk2_ref[...], HEAD_DIM)
    v_sel = kv[:, 3 * KV_WIDTH:4 * KV_WIDTH]
    v_win = kv[:, 5 * KV_WIDTH:6 * KV_WIDTH]
    if kv_feature_major:
        kv_ref[0:2 * KV_WIDTH, :] = kv[:, 0:2 * KV_WIDTH].T
        kv_ref[2 * KV_WIDTH:3 * KV_WIDTH, :] = k_sel.T
        kv_ref[3 * KV_WIDTH:4 * KV_WIDTH, :] = v_sel.T
    else:
        kv_ref[:, 0:2 * KV_WIDTH] = kv[:, 0:2 * KV_WIDTH]
        kv_ref[:, 2 * KV_WIDTH:3 * KV_WIDTH] = k_sel
        kv_ref[:, 3 * KV_WIDTH:4 * KV_WIDTH] = v_sel
    if kv_feature_major:
        win_ref[0:KV_WIDTH, :] = k_win.T
        win_ref[KV_WIDTH:2 * KV_WIDTH, :] = v_win.T
    else:
        win_ref[:, 0:KV_WIDTH] = k_win
        win_ref[:, KV_WIDTH:2 * KV_WIDTH] = v_win
    katt_ref[:, 0:KV_WIDTH] = k_sel.astype(bf16)
    katt_ref[:, KV_WIDTH:2 * KV_WIDTH] = v_sel.astype(bf16)
    katt_ref[:, 2 * KV_WIDTH:3 * KV_WIDTH] = k_win.astype(bf16)
    katt_ref[:, 3 * KV_WIDTH:4 * KV_WIDTH] = v_win.astype(bf16)
    h4_ref[...] = proj(c, 4 * HG_WIDTH)
    c += 4 * HG_WIDTH
    mg_ref[...] = jax.nn.sigmoid(proj(c, 2 * D_MODEL))
    c += 2 * D_MODEL
    gl_ref[...] = jax.nn.sigmoid(proj(c, LANES))


def _in_proj(x2d, g1, w, gq, gk1, gk2, *, qw, q_seg, tm, seq_len=None):
    n = x2d.shape[0]
    row = lambda width: pl.BlockSpec((tm, width), lambda i: (i, 0))
    full = lambda a: pl.BlockSpec(a.shape, lambda i: (0,) * a.ndim)
    widths = (qw, 4 * KV_WIDTH, 2 * KV_WIDTH, 4 * KV_WIDTH, 4 * HG_WIDTH, 2 * D_MODEL, LANES)
    dtypes = (bf16, f32, f32, bf16, f32, f32, f32)
    out_shape = [jax.ShapeDtypeStruct((n, wd), dt) for wd, dt in zip(widths, dtypes)]
    out_specs = [row(wd) for wd in widths]
    if seq_len is not None:
        tiles = seq_len // tm
        for o_, wd in ((1, 4 * KV_WIDTH), (2, 2 * KV_WIDTH)):
            out_shape[o_] = jax.ShapeDtypeStruct((n // seq_len, wd, seq_len), f32)
            out_specs[o_] = pl.BlockSpec((None, wd, tm), lambda i: (i // tiles, 0, i % tiles))
    return pl.pallas_call(
        functools.partial(_in_proj_kernel, qw=qw, q_seg=q_seg, kv_feature_major=seq_len is not None),
        out_shape=tuple(out_shape),
        grid=(n // tm,),
        in_specs=[row(D_MODEL), full(g1),
                  pl.BlockSpec(w.shape, lambda i: (0, 0), pipeline_mode=pl.Buffered(1)),
                  full(gq), full(gk1), full(gk2)],
        out_specs=tuple(out_specs),
        compiler_params=_params(("arbitrary",)),
        name="in_proj",
    )(x2d, g1, w, gq, gk1, gk2)


CMP_SLOT = CMP_BLOCK + 1


def _cmp_staging_rows(n_cmp):
    return -(-n_cmp * CMP_SLOT // SUBLANES) * SUBLANES


def _stage_cmp_chunk(src_ref, chunk, kc_t, vc_t, n_cmp):
    per_chunk = LANES // CMP_BLOCK
    for s, t in enumerate((kc_t.astype(bf16).T.astype(f32), vc_t.astype(bf16).T.astype(f32))):
        for nl in range(per_chunk):
            n = chunk * per_chunk + nl
            slot = n // 2 + (n % 2) * (n_cmp // 2)
            src_ref[s, slot * CMP_SLOT:slot * CMP_SLOT + CMP_BLOCK, :] = t[nl * CMP_BLOCK:(nl + 1) * CMP_BLOCK]


def _compress(src_refs, w1_ref, w2_ref, gk0, n_cmp):
    acc = jnp.zeros((len(src_refs) * n_cmp, 2 * LANES), f32)
    per_dot = 8
    for j0 in range(0, CMP_BLOCK, per_dot):
        xs = jnp.concatenate(
            [jnp.concatenate([src[s, pl.ds(j, n_cmp, stride=CMP_SLOT), :].astype(bf16)
                              for j in range(j0, j0 + per_dot) for s in range(2)], axis=1)
             for src in src_refs], axis=0)
        w = w1_ref[j0 * 2 * LANES:(j0 + per_dot) * 2 * LANES, :]
        acc = acc + jnp.dot(xs, w, preferred_element_type=f32)
    hmid = jax.nn.gelu(acc).astype(bf16)
    y = jnp.dot(hmid, w2_ref[...], preferred_element_type=f32)
    ck = _seg_rmsnorm(y[:, 0:LANES], gk0, HEAD_DIM)
    return jnp.concatenate([ck, y[:, LANES:2 * LANES]], axis=1)


def _compress_prompt_kernel(kv_ref, w1_ref, w2_ref, gk0_ref, out_ref, src_ref, *, n_cmp):
    for c in range(kv_ref.shape[1] // LANES):
        cols = slice(c * LANES, (c + 1) * LANES)
        _stage_cmp_chunk(src_ref, c, kv_ref[0:LANES, cols], kv_ref[LANES:2 * LANES, cols], n_cmp)
    out_ref[...] = _compress([src_ref], w1_ref, w2_ref, gk0_ref[...], n_cmp).astype(bf16)


def _compress_prompt(kv_t, w1bd, w2bd, gk0):
    B, _, L = kv_t.shape
    n_cmp = L // CMP_BLOCK
    return pl.pallas_call(
        functools.partial(_compress_prompt_kernel, n_cmp=n_cmp),
        out_shape=jax.ShapeDtypeStruct((B, n_cmp, 2 * LANES), bf16),
        grid=(B,),
        in_specs=[pl.BlockSpec((None, 2 * LANES, L), lambda b: (b, 0, 0)),
                  pl.BlockSpec(w1bd.shape, lambda b: (0, 0)),
                  pl.BlockSpec(w2bd.shape, lambda b: (0, 0)),
                  pl.BlockSpec(gk0.shape, lambda b: (0, 0))],
        out_specs=pl.BlockSpec((None, n_cmp, 2 * LANES), lambda b: (b, 0, 0)),
        scratch_shapes=[pltpu.VMEM((2, _cmp_staging_rows(n_cmp), LANES), f32)],
        compiler_params=_params(("arbitrary",)),
        name="compress_prompt",
    )(kv_t, w1bd, w2bd, gk0)


def _select_blocks_t(score, valid, n_keep):
    nb = score.shape[0]
    jj = lax.broadcasted_iota(i32, score.shape, 0)
    cnt = jnp.zeros(score.shape, f32)
    for k in range(nb):
        sk = score[k:k + 1, :]
        better = (sk > score) | ((sk == score) & (jj > k))
        cnt = cnt + jnp.where(better, 1.0, 0.0)
    return (cnt < n_keep) & valid


def _sel_tile_classes(n_tiles):
    return sorted({-(-n_tiles * c // SEL_CLASSES) for c in range(1, SEL_CLASSES + 1)})


def _nsa_prompt_kernel(q_ref, gate_ref, katt_ref, cmp_ref, tc_ref, tt_ref, ex_ref, o_ref,
                       selx_ref, s_ref, os_ref, part_ref, gsel_ref, *, L):
    i = pl.program_id(1)
    n_cmp = L // CMP_BLOCK
    n_sel = L // SEL_BLOCK
    n_tiles = L // LANES
    qb = SEL_BLOCK
    rows = NSA_GROUP * qb
    units = [(sq, g) for sq in range(q_ref.shape[0]) for g in range(NSA_KV_HEADS)]
    nu = range(len(units))
    row64 = lax.broadcasted_iota(i32, (qb, LANES), 0)
    lane = lax.broadcasted_iota(i32, (qb, LANES), 1)
    qpos = i * qb + row64

    qs = [jnp.concatenate(
        [q_ref[sq, :, (NSA_GROUP * g + r) * HEAD_DIM:(NSA_GROUP * g + r + 1) * HEAD_DIM] for r in range(NSA_GROUP)],
        axis=0) for sq, g in units]

    def attend(k_off, v_off, tile0, n_t, mask_fn):
        chunk = 4

        def key_rows(t0, count, off, u):
            sq, g = units[u]
            r0 = pl.multiple_of((tile0 + t0) * LANES, LANES)
            return katt_ref[sq, pl.ds(r0, count * LANES), off + g * HEAD_DIM:off + (g + 1) * HEAD_DIM]

        def scores(t0, count, mruns):
            dots = [lax.dot_general(qs[u], key_rows(t0, count, k_off, u), NT, preferred_element_type=f32)
                    for u in nu]
            out = list(mruns)
            for j in range(count):
                ta = tile0 + t0 + j
                bias_idx = jnp.clip(i - 2 * ta, 0, FAR_TILE)
                kpos = ta * LANES + lane
                for u in nu:
                    s = dots[u][:, j * LANES:(j + 1) * LANES] + tt_ref[bias_idx, units[u][1]]
                    msk = mask_fn(u, ta, kpos)
                    s = jnp.where(msk[None], s.reshape(NSA_GROUP, qb, LANES), NEG).reshape(rows, LANES)
                    s_ref[u, t0 + j] = s
                    out[u] = jnp.maximum(out[u], s)
            return tuple(out)

        def values(t0, count, carry, ms):
            ls, accs = list(carry[0]), list(carry[1])
            ps = []
            for u in nu:
                pj = [jnp.exp(s_ref[u, t0 + j] - ms[u]) for j in range(count)]
                for p in pj:
                    ls[u] = ls[u] + p
                ps.append(jnp.concatenate([p.astype(bf16) for p in pj], axis=1))
            for u in nu:
                accs[u] = accs[u] + jnp.dot(ps[u], key_rows(t0, count, v_off, u), preferred_element_type=f32)
            return tuple(ls), tuple(accs)

        def over_chunks(fn, carry):
            for t0 in range(0, n_t, chunk):
                carry = fn(t0, min(chunk, n_t - t0), carry)
            return carry

        mruns = over_chunks(scores, tuple(jnp.full((rows, LANES), NEG, f32) for _ in nu))
        ms = [jnp.max(m, axis=-1, keepdims=True) for m in mruns]
        init = (tuple(jnp.zeros((rows, LANES), f32) for _ in nu),
                tuple(jnp.zeros((rows, HEAD_DIM), f32) for _ in nu))
        ls, accs = over_chunks(lambda t0, count, cr: values(t0, count, cr, ms), init)
        return [accs[u] / jnp.sum(ls[u], axis=-1, keepdims=True) for u in nu]

    n_wt = min(WINDOW // LANES + 1, n_tiles)
    w0 = jnp.clip((i - WINDOW // qb) // 2, 0, n_tiles - n_wt)
    o_w = attend(2 * KV_WIDTH, 3 * KV_WIDTH, w0, n_wt,
                 lambda u, ta, kpos: (kpos <= qpos) & (qpos - kpos <= WINDOW))

    o_c, imp_t = [], []
    col = lax.broadcasted_iota(i32, (rows, n_cmp), 1)
    rq = lax.broadcasted_iota(i32, (rows, n_cmp), 0) & (qb - 1)
    blk = jnp.where(col < n_cmp // 2, 2 * col, 2 * col - (n_cmp - 1))
    vis = (blk + 1) * CMP_BLOCK - 1 <= i * qb + rq
    for u, (sq, g) in enumerate(units):
        ck = cmp_ref[sq, :, g * HEAD_DIM:(g + 1) * HEAD_DIM]
        cv = cmp_ref[sq, :, LANES + g * HEAD_DIM:LANES + (g + 1) * HEAD_DIM]
        s = lax.dot_general(qs[u], ck, NT, preferred_element_type=f32) + tc_ref[g]
        e, den = _softmax_parts(s, vis)
        p = e / jnp.where(den > 0, den, 1.0)
        o_c.append(jnp.dot(p.astype(bf16), cv, preferred_element_type=f32))
        ps_t = (p[0:qb] + p[qb:2 * qb] + p[2 * qb:3 * qb] + p[3 * qb:4 * qb]).T
        imp_t.append(ps_t[0:n_sel] + ps_t[n_sel:2 * n_sel])

    imp = jnp.concatenate(imp_t, axis=1)
    jj = lax.broadcasted_iota(i32, imp.shape, 0)
    valid = jj <= i
    forced = valid & ((jj == 0) | (jj >= i - 1))
    score = jnp.where(forced, FORCED_SCORE, jnp.where(valid, imp, INVALID_SCORE))
    sel = _select_blocks_t(score, valid, min(TOP_N, n_sel))
    selx = lax.dot_general(jnp.where(sel, 1.0, 0.0).astype(bf16), ex_ref[...], TN, preferred_element_type=f32)
    for u in nu:
        for t in range(n_tiles):
            selx_ref[u, t] = selx[u * qb:(u + 1) * qb, t * LANES:(t + 1) * LANES]

    for u, (sq, g) in enumerate(units):
        gates = gate_ref[sq]
        for r in range(NSA_GROUP):
            h = NSA_GROUP * g + r
            hs = slice(h * HEAD_DIM, (h + 1) * HEAD_DIM)
            sl = slice(r * qb, (r + 1) * qb)
            part_ref[sq, :, hs] = gates[:, 3 * h:3 * h + 1] * o_c[u][sl] + gates[:, 3 * h + 2:3 * h + 3] * o_w[u][sl]
            gsel_ref[sq, :, hs] = jnp.broadcast_to(gates[:, 3 * h + 1:3 * h + 2], (qb, HEAD_DIM))

    need = i // 2 + 1
    prev = 0
    for n_t in _sel_tile_classes(n_tiles):
        @pl.when((need > prev) & (need <= n_t))
        def _(n_t=n_t):
            o_s = attend(0, KV_WIDTH, 0, n_t, lambda u, ta, kpos: (selx_ref[u, ta] > 0.5) & (kpos <= qpos))
            for u in nu:
                os_ref[u] = o_s[u]
        prev = n_t

    for u, (sq, g) in enumerate(units):
        o_s = os_ref[u]
        for r in range(NSA_GROUP):
            hs = slice((NSA_GROUP * g + r) * HEAD_DIM, (NSA_GROUP * g + r + 1) * HEAD_DIM)
            o = part_ref[sq, :, hs] + gsel_ref[sq, :, hs] * o_s[r * qb:(r + 1) * qb]
            o_ref[sq, :, hs] = o.astype(bf16)


def _nsa_prompt(qn, gates, katt, cmp, tc, tt, expand, B, L, nseq=2):
    nqb = L // SEL_BLOCK
    qb = SEL_BLOCK
    n_tiles = L // LANES
    rows = NSA_GROUP * qb
    nseq = math.gcd(nseq, B)
    n_units = nseq * NSA_KV_HEADS
    seq3 = lambda a: a.reshape(B, L, a.shape[-1])
    return pl.pallas_call(
        functools.partial(_nsa_prompt_kernel, L=L),
        out_shape=jax.ShapeDtypeStruct((B, L, NSA_WIDTH), bf16),
        grid=(B // nseq, nqb),
        in_specs=[pl.BlockSpec((nseq, qb, NSA_WIDTH), lambda b, i: (b, i, 0)),
                  pl.BlockSpec((nseq, qb, LANES), lambda b, i: (b, i, 0)),
                  pl.BlockSpec((nseq, L, 4 * KV_WIDTH), lambda b, i: (b, 0, 0)),
                  pl.BlockSpec((nseq, L // CMP_BLOCK, 2 * LANES), lambda b, i: (b, 0, 0)),
                  pl.BlockSpec((None,) + tc.shape[1:], lambda b, i: (i, 0, 0, 0)),
                  pl.BlockSpec(tt.shape, lambda b, i: (0, 0, 0, 0)),
                  pl.BlockSpec(expand.shape, lambda b, i: (0, 0))],
        out_specs=pl.BlockSpec((nseq, qb, NSA_WIDTH), lambda b, i: (b, i, 0)),
        scratch_shapes=[pltpu.VMEM((n_units, n_tiles, qb, LANES), f32),
                        pltpu.VMEM((n_units, n_tiles, rows, LANES), f32),
                        pltpu.VMEM((n_units, rows, HEAD_DIM), f32),
                        pltpu.VMEM((nseq, qb, NSA_WIDTH), f32), pltpu.VMEM((nseq, qb, NSA_WIDTH), f32)],
        compiler_params=_params(("arbitrary", "arbitrary")),
        name="nsa_prompt",
    )(seq3(qn), seq3(gates), seq3(katt), cmp, tc, tt, expand).reshape(B * L, NSA_WIDTH)


def _hgrn_gates(hf, lb):
    f = lb + (1.0 - lb) * jax.nn.sigmoid(hf)
    return f, 1.0 - f, jnp.log(f)


def _hgrn_out(o, gn, hgate):
    return _rmsnorm_rows(o, gn) * jax.nn.silu(hgate)


def _hgrn_prompt_kernel(h4_ref, lb_ref, gn_ref, o_ref, s_out_ref, st_ref, *, tc):
    c = pl.program_id(1)
    nb = HG_BLOCK
    nseq = h4_ref.shape[0]

    @pl.when(c == 0)
    def _():
        st_ref[...] = jnp.zeros_like(st_ref)

    t_row = lax.broadcasted_iota(i32, (nb, HG_DIM), 0)
    t_col = lax.broadcasted_iota(i32, (nb, 1), 0)
    gn = gn_ref[...]
    chains = [(sq, h) for sq in range(nseq) for h in range(HG_HEADS)]

    def cumsum_rows(x):
        shift = 1
        while shift < nb:
            x = x + jnp.where(t_row >= shift, pltpu.roll(x, shift, axis=0), 0.0)
            shift *= 2
        return x

    def block(bi, _):
        r0 = pl.multiple_of(bi * nb, nb)

        def sl(sq, h, part):
            return h4_ref[sq, pl.ds(r0, nb), part * HG_WIDTH + h * HG_DIM:part * HG_WIDTH + (h + 1) * HG_DIM]

        qs, ks, vs, bs, sts, os_ = [], [], [], [], [], []
        for sq, h in chains:
            _, k, glog = _hgrn_gates(sl(sq, h, 1), lb_ref[:, h * HG_DIM:(h + 1) * HG_DIM])
            qs.append(sl(sq, h, 0) * HG_SCALE)
            ks.append(k)
            vs.append(sl(sq, h, 2))
            bs.append(cumsum_rows(glog))
        for c_, (sq, h) in enumerate(chains):
            sts.append(st_ref[sq, h])
            os_.append(lax.dot_general((qs[c_] * jnp.exp(bs[c_])).astype(bf16), sts[c_].astype(bf16), NT,
                                       preferred_element_type=f32))
        for c_ in range(len(chains)):
            q, k, v, b, o = qs[c_], ks[c_], vs[c_], bs[c_], os_[c_]
            parts = [o[r0_:r0_ + SUBLANES] for r0_ in range(0, nb, SUBLANES)]
            for s in range(nb):
                for gi in range(s // SUBLANES, nb // SUBLANES):
                    rs = slice(gi * SUBLANES, (gi + 1) * SUBLANES)
                    w = q[rs] * k[s:s + 1] * jnp.exp(b[rs] - b[s:s + 1])
                    a = jnp.sum(w, axis=-1, keepdims=True)
                    if gi == s // SUBLANES:
                        a = jnp.where(t_col[rs] >= s, a, 0.0)
                    parts[gi] = parts[gi] + a * v[s:s + 1]
            os_[c_] = jnp.concatenate(parts, axis=0)
        for c_, (sq, h) in enumerate(chains):
            b_last = bs[c_][nb - 1:nb]
            kt = ks[c_] * jnp.exp(b_last - bs[c_])
            upd = lax.dot_general(vs[c_].astype(bf16), kt.astype(bf16), TN, preferred_element_type=f32)
            st_ref[sq, h] = jnp.exp(b_last) * sts[c_] + upd
            o_ref[sq, pl.ds(r0, nb), h * HG_DIM:(h + 1) * HG_DIM] = _hgrn_out(os_[c_], gn, sl(sq, h, 3)).astype(bf16)
        return 0

    lax.fori_loop(0, tc // nb, block, 0, unroll=2)

    @pl.when(c == pl.num_programs(1) - 1)
    def _():
        for sq in range(nseq):
            for h in range(HG_HEADS):
                s_out_ref[sq, h] = st_ref[sq, h].T


def _hgrn_prompt(h4, lb, gn, B, L, tc=256, nseq=4):
    nc = L // tc
    nseq = math.gcd(nseq, B)
    return pl.pallas_call(
        functools.partial(_hgrn_prompt_kernel, tc=tc),
        out_shape=(jax.ShapeDtypeStruct((B, L, HG_WIDTH), bf16),
                   jax.ShapeDtypeStruct((B, HG_HEADS, HG_DIM, HG_DIM), f32)),
        grid=(B // nseq, nc),
        in_specs=[pl.BlockSpec((nseq, tc, 4 * HG_WIDTH), lambda b, c: (b, c, 0)),
                  pl.BlockSpec(lb.shape, lambda b, c: (0, 0)),
                  pl.BlockSpec(gn.shape, lambda b, c: (0, 0))],
        out_specs=(pl.BlockSpec((nseq, tc, HG_WIDTH), lambda b, c: (b, c, 0)),
                   pl.BlockSpec((nseq, HG_HEADS, HG_DIM, HG_DIM), lambda b, c: (b, 0, 0, 0))),
        scratch_shapes=[pltpu.VMEM((nseq, HG_HEADS, HG_DIM, HG_DIM), f32)],
        compiler_params=_params(("arbitrary", "arbitrary")),
        name="hgrn_prompt",
    )(h4.reshape(B, L, 4 * HG_WIDTH), lb, gn)


def _ffn_kernel(*refs, decode, tiles_per_seq):
    if decode:
        (x_ref, on_ref, oh_ref, mg_ref, wn_ref, wh_ref, wo_ref, g2_ref, wg_ref, wu_ref, cw_ref, cb_ref, wd_ref,
         cs0_ref, cs1_ref, y_ref, gate_out_ref, x1_ref, xn2_ref, acc_ref) = refs
    else:
        (x_ref, on_ref, oh_ref, mg_ref, wn_ref, wh_ref, wo_ref, g2_ref, wg_ref, wu_ref, cw_ref, cb_ref, wd_ref,
         y_ref, gate_out_ref, x1_ref, xn2_ref, acc_ref, carry_ref) = refs
    i = pl.program_id(0)
    fi = pl.program_id(1)
    tm = x_ref.shape[0]

    @pl.when(fi == 0)
    def _():
        y_a = jnp.dot(on_ref[...], wn_ref[...], preferred_element_type=f32)
        y_b = jnp.dot(oh_ref[...], wh_ref[...], preferred_element_type=f32)
        mg = mg_ref[...]
        merged = mg[:, 0:D_MODEL] * y_a + mg[:, D_MODEL:2 * D_MODEL] * y_b
        x1 = x_ref[...] + jnp.dot(merged.astype(bf16), wo_ref[...], preferred_element_type=f32)
        x1_ref[...] = x1
        xn2_ref[...] = _rmsnorm_rows(x1, g2_ref[...]).astype(bf16)
        acc_ref[...] = jnp.zeros_like(acc_ref)

    xn2 = xn2_ref[...]
    tf = wg_ref.shape[1]
    if not decode:
        @pl.when(i % tiles_per_seq == 0)
        def _():
            carry_ref[fi] = jnp.zeros(carry_ref.shape[1:], f32)

    acc = acc_ref[...]
    for a in range(0, tf, FFN_SLICE):
        cs = slice(a, min(a + FFN_SLICE, tf))
        gate = jnp.dot(xn2, wg_ref[:, cs], preferred_element_type=f32)
        up = jnp.dot(xn2, wu_ref[:, cs], preferred_element_type=f32)
        cw = cw_ref[:, cs]
        if decode:
            prev2, prev1 = cs0_ref[:, cs], cs1_ref[:, cs]
            gate_out_ref[:, cs] = gate
        else:
            carry = carry_ref[fi, :, cs]
            rid = lax.broadcasted_iota(i32, gate.shape, 0)
            c1, c2 = carry[SUBLANES - 1:SUBLANES], carry[SUBLANES - 2:SUBLANES - 1]
            prev1 = jnp.where(rid == 0, c1, pltpu.roll(gate, 1, axis=0))
            prev2 = jnp.where(rid == 0, c2, jnp.where(rid == 1, c1, pltpu.roll(gate, 2, axis=0)))
            carry_ref[fi, SUBLANES - 2:SUBLANES, cs] = gate[tm - 2:tm]
            gate_out_ref[:, cs] = gate[tm - 2:tm]
        conv = cb_ref[:, cs] + cw[0:1] * prev2 + cw[1:2] * prev1 + cw[2:3] * gate
        hmid = (jax.nn.silu(conv) * up).astype(bf16)
        acc = acc + jnp.dot(hmid, wd_ref[cs, :], preferred_element_type=f32)
    acc_ref[...] = acc

    @pl.when(fi == pl.num_programs(1) - 1)
    def _():
        y_ref[...] = x1_ref[...] + acc_ref[...]


def _ffn(x2d, o_nsa, o_hg, mgs, wn, wh, wo, g2, wg, wu, cw, cb, wd, *, tm, tf, seq_len=None, conv_state=None):
    n = x2d.shape[0]
    decode = conv_state is not None
    nf = D_FF // tf
    row = lambda width: pl.BlockSpec((tm, width), lambda i, f: (i, 0))
    full = lambda a: pl.BlockSpec(a.shape, lambda i, f: (0,) * a.ndim)
    in_specs = [row(D_MODEL), row(NSA_WIDTH), row(HG_WIDTH), row(2 * D_MODEL), full(wn), full(wh), full(wo), full(g2),
                pl.BlockSpec((D_MODEL, tf), lambda i, f: (0, f)), pl.BlockSpec((D_MODEL, tf), lambda i, f: (0, f)),
                pl.BlockSpec((3, tf), lambda i, f: (0, f)), pl.BlockSpec((1, tf), lambda i, f: (0, f)),
                pl.BlockSpec((tf, D_MODEL), lambda i, f: (f, 0))]
    args = [x2d, o_nsa, o_hg, mgs, wn, wh, wo, g2, wg, wu, cw, cb, wd]
    scratch = [pltpu.VMEM((tm, D_MODEL), f32), pltpu.VMEM((tm, D_MODEL), bf16), pltpu.VMEM((tm, D_MODEL), f32)]
    if decode:
        in_specs += [pl.BlockSpec((tm, tf), lambda i, f: (i, f))] * 2
        args += [conv_state[:, 0], conv_state[:, 1]]
        gate_shape = jax.ShapeDtypeStruct((n, D_FF), f32)
        gate_spec = pl.BlockSpec((tm, tf), lambda i, f: (i, f))
        tiles_per_seq = 1
    else:
        tiles_per_seq = seq_len // tm
        gate_shape = jax.ShapeDtypeStruct((n // tm, 2, D_FF), f32)
        gate_spec = pl.BlockSpec((None, 2, tf), lambda i, f: (i, 0, f))
        scratch.append(pltpu.VMEM((nf, SUBLANES, tf), f32))
    return pl.pallas_call(
        functools.partial(_ffn_kernel, decode=decode, tiles_per_seq=tiles_per_seq),
        out_shape=(jax.ShapeDtypeStruct((n, D_MODEL), f32), gate_shape),
        grid=(n // tm, nf),
        in_specs=in_specs,
        out_specs=(row(D_MODEL), gate_spec),
        scratch_shapes=scratch,
        compiler_params=_params(("arbitrary", "arbitrary")),
        name="ffn_decode" if decode else "ffn_prompt",
    )(*args)


def _nsa_decode_kernel(*refs, n_pages, n_win, nseq):
    page_refs = refs[1:1 + nseq * n_pages]
    (q_ref, kvn_ref, winn_ref, win_all_ref, win_ref, w1_ref, w2_ref, gk0_ref, bias_ref,
     ow_ref, win_out_ref, cmp_ref, ssel_ref, snew_ref, vsel_ref, src_ref, ksel_ref, wcol_ref) = refs[1 + nseq * n_pages:]
    step = pl.program_id(0)
    past = n_pages * PAGE_SIZE
    n_cmp = past // CMP_BLOCK
    heads = NSA_HEADS
    samples = range(nseq)

    @pl.when(step == 0)
    def _():
        for t in range(2):
            wcol_ref[t * LANES:(t + 1) * LANES, :] = win_all_ref[:, t * LANES:(t + 1) * LANES].T

    for sq in samples:
        for k in range(n_pages):
            page = page_refs[sq * n_pages + k]
            cols = slice(k * PAGE_SIZE, (k + 1) * PAGE_SIZE)
            _stage_cmp_chunk(src_ref.at[sq], k, page[0:LANES, :], page[LANES:2 * LANES, :], n_cmp)
            ksel_ref[sq, :, cols] = page[2 * LANES:3 * LANES, :].astype(bf16)
            vsel_ref[sq, :, cols] = page[3 * LANES:4 * LANES, :].astype(bf16)

    hrow = lax.broadcasted_iota(i32, (heads, 1), 0)
    g0 = hrow < NSA_GROUP
    bias = bias_ref[...]
    b_s = bias[:, 0:past]
    b_w = bias[:, past:past + n_win]
    b_new = bias[:, past + n_win + LANES:past + n_win + LANES + 1]
    qs = [q_ref[sq] for sq in samples]
    attend = _decode_attend

    def scores(q, k_t, b_past, k_new):
        s_past = jnp.dot(q, k_t, preferred_element_type=f32) + b_past
        s_new = jnp.sum(q.astype(f32) * k_new, axis=-1, keepdims=True) + b_new
        return s_past, s_new

    lane_b = lax.broadcasted_iota(i32, wcol_ref.shape, 1)
    lane_w = lax.broadcasted_iota(i32, win_ref.shape[1:], 1)
    for sq in samples:
        winn, kvn = winn_ref[sq], kvn_ref[sq]
        s_past, s_new = scores(qs[sq], ksel_ref[sq], b_s, kvn[:, 2 * LANES:3 * LANES])
        ssel_ref[sq] = s_past
        snew_ref[sq] = jnp.broadcast_to(s_new, (heads, LANES))
        s_win = scores(qs[sq], win_ref[sq, 0:LANES, :].astype(bf16), b_w, winn[:, 0:LANES])
        ow_ref[sq] = attend(s_win, None, win_ref[sq, LANES:2 * LANES, :].astype(bf16), winn[:, LANES:2 * LANES],
                            g0)
        new_col = jnp.sum(jnp.where(lane_b == step * nseq + sq, wcol_ref[...], 0.0), axis=-1, keepdims=True)
        win_out_ref[sq] = jnp.where(lane_w == n_win - 1, new_col, pltpu.roll(win_ref[sq], n_win - 1, axis=1))

    cmp = _compress([src_ref.at[sq] for sq in samples], w1_ref, w2_ref, gk0_ref[...], n_cmp)
    for sq in samples:
        cmp_ref[sq] = cmp[sq * n_cmp:(sq + 1) * n_cmp].astype(bf16)


def _decode_attend(s, mask, v_t, v_new, g0):
    s_past, s_new = s
    sm = s_past if mask is None else jnp.where(mask, s_past, NEG)
    m = jnp.maximum(jnp.max(sm, axis=-1, keepdims=True), s_new)
    e = jnp.exp(sm - m)
    e_new = jnp.exp(s_new - m)
    den = jnp.sum(e, axis=-1, keepdims=True) + e_new
    o2 = (lax.dot_general(e.astype(bf16), v_t, NT, preferred_element_type=f32) + e_new * v_new) / den
    return jnp.where(g0, o2[:, 0:HEAD_DIM], o2[:, HEAD_DIM:2 * HEAD_DIM])


def _nsa_decode_tail_kernel(q_ref, cmp_ref, ssel_ref, snew_ref, vsel_ref, ow_ref, kvn_ref, gate_ref, bias_ref, ex_ref,
                            o_ref, *, n_sel):
    samples = range(q_ref.shape[0])
    heads = NSA_HEADS
    n_cmp = cmp_ref.shape[1]
    g0 = lax.broadcasted_iota(i32, (heads, 1), 0) < NSA_GROUP
    b_c = bias_ref[...]
    cur = n_sel
    jrow = lax.broadcasted_iota(i32, (NSA_KV_HEADS, n_sel), 1)
    forced = (jrow == 0) | (jrow >= cur - 1)
    n_keep = min(TOP_N, n_sel + 1) - 1
    ri = lax.broadcasted_iota(i32, (n_sel, n_sel), 0)
    ci = lax.broadcasted_iota(i32, (n_sel, n_sel), 1)

    s_c = [lax.dot_general(q_ref[sq], cmp_ref[sq, :, 0:LANES], NT, preferred_element_type=f32) + b_c for sq in samples]
    p_c = []
    for sq in samples:
        e_c, den_c = _softmax_parts(s_c[sq], jnp.full(s_c[sq].shape, True))
        p_c.append(e_c / jnp.where(den_c > 0, den_c, 1.0))
    o_c = [jnp.dot(p_c[sq].astype(bf16), cmp_ref[sq, :, LANES:2 * LANES], preferred_element_type=f32) for sq in samples]
    sels = []
    for sq in samples:
        ps = jnp.concatenate([jnp.sum(p_c[sq][0:NSA_GROUP], axis=0, keepdims=True),
                              jnp.sum(p_c[sq][NSA_GROUP:heads], axis=0, keepdims=True)], axis=0)
        imp = ps[:, 0:n_sel] + ps[:, n_sel:2 * n_sel]
        score_row = jnp.where(forced, FORCED_SCORE, imp)
        rows = []
        for g in range(NSA_KV_HEADS):
            rowb = jnp.broadcast_to(score_row[g:g + 1], (n_sel, n_sel))
            colb = jnp.sum(jnp.where(ri == ci, rowb, 0.0), axis=-1, keepdims=True)
            better = (colb > rowb) | ((colb == rowb) & (ri < ci))
            cnt = jnp.sum(jnp.where(better, 1.0, 0.0), axis=0, keepdims=True)
            rows.append(jnp.where(cnt < n_keep, 1.0, 0.0))
        sels.append(jnp.concatenate(rows, axis=0).astype(bf16))
    selx = [jnp.dot(sels[sq], ex_ref[...], preferred_element_type=f32) for sq in samples]
    for sq in samples:
        mask = jnp.where(g0, selx[sq][0:1], selx[sq][1:2]) > 0.5
        o_s = _decode_attend((ssel_ref[sq], snew_ref[sq][:, 0:1]), mask, vsel_ref[sq],
                             kvn_ref[sq][:, 3 * LANES:4 * LANES], g0)
        gates = gate_ref[sq]
        o_cg = jnp.where(g0, o_c[sq][:, 0:HEAD_DIM], o_c[sq][:, HEAD_DIM:2 * HEAD_DIM])
        o_ref[sq] = (gates[:, 0:1] * o_cg + gates[:, 1:2] * o_s + gates[:, 2:3] * ow_ref[sq]).astype(bf16)


def _nsa_decode(cache_t, page_table, q_bd, kv32, win32, gates, win_t, w1bd, w2bd, gk0, bias, expand,
                nseq=2, nseq_tail=8):
    Bs, n_pages = page_table.shape
    n_win = win_t.shape[2]
    past = n_pages * PAGE_SIZE
    n_cmp = past // CMP_BLOCK
    nseq = math.gcd(nseq, Bs)
    per_b = lambda shape: pl.BlockSpec((nseq,) + shape, lambda b, pt: (b,) + (0,) * len(shape))
    const = lambda a: pl.BlockSpec(a.shape, lambda b, pt: (0,) * a.ndim)
    page_spec = lambda sq, k: pl.BlockSpec((None, 4 * LANES, PAGE_SIZE), lambda b, pt: (pt[b * nseq + sq, k], 0, 0))
    q3 = q_bd.reshape(Bs, NSA_HEADS, LANES)
    kvn3 = kv32.reshape(Bs, 1, 4 * LANES)
    gates3 = gates[:, 0:3 * NSA_HEADS].reshape(Bs, NSA_HEADS, 3)
    grid_spec = pltpu.PrefetchScalarGridSpec(
        num_scalar_prefetch=1,
        grid=(Bs // nseq,),
        in_specs=[page_spec(sq, k) for sq in range(nseq) for k in range(n_pages)] + [
            per_b((NSA_HEADS, LANES)), per_b((1, 4 * LANES)), per_b((1, 2 * LANES)), const(win32),
            per_b((2 * LANES, n_win)), const(w1bd), const(w2bd), const(gk0), const(bias)],
        out_specs=(per_b((NSA_HEADS, HEAD_DIM)), per_b((2 * LANES, n_win)), per_b((n_cmp, 2 * LANES)),
                   per_b((NSA_HEADS, past)), per_b((NSA_HEADS, LANES)), per_b((LANES, past))),
        scratch_shapes=[pltpu.VMEM((nseq, 2, _cmp_staging_rows(n_cmp), LANES), f32),
                        pltpu.VMEM((nseq, LANES, past), bf16), pltpu.VMEM((2 * LANES, Bs), f32)],
    )
    o_w, win_out, cmp, s_sel, s_new, v_sel = pl.pallas_call(
        functools.partial(_nsa_decode_kernel, n_pages=n_pages, n_win=n_win, nseq=nseq),
        out_shape=(jax.ShapeDtypeStruct((Bs, NSA_HEADS, HEAD_DIM), f32),
                   jax.ShapeDtypeStruct((Bs, 2 * LANES, n_win), f32),
                   jax.ShapeDtypeStruct((Bs, n_cmp, 2 * LANES), bf16),
                   jax.ShapeDtypeStruct((Bs, NSA_HEADS, past), f32),
                   jax.ShapeDtypeStruct((Bs, NSA_HEADS, LANES), f32),
                   jax.ShapeDtypeStruct((Bs, LANES, past), bf16)),
        grid_spec=grid_spec,
        compiler_params=_params(("arbitrary",)),
        name="nsa_decode",
    )(page_table, *([cache_t] * (nseq * n_pages)), q3, kvn3, win32.reshape(Bs, 1, 2 * LANES), win32, win_t,
      w1bd, w2bd, gk0, bias)

    ns = math.gcd(nseq_tail, Bs)
    per_t = lambda shape: pl.BlockSpec((ns,) + shape, lambda b: (b,) + (0,) * len(shape))
    const_t = lambda a: pl.BlockSpec(a.shape, lambda b: (0,) * a.ndim)
    b_c = bias[:, past + n_win:past + n_win + n_cmp]
    o = pl.pallas_call(
        functools.partial(_nsa_decode_tail_kernel, n_sel=past // SEL_BLOCK),
        out_shape=jax.ShapeDtypeStruct((Bs, NSA_HEADS, HEAD_DIM), bf16),
        grid=(Bs // ns,),
        in_specs=[per_t((NSA_HEADS, LANES)), per_t((n_cmp, 2 * LANES)), per_t((NSA_HEADS, past)),
                  per_t((NSA_HEADS, LANES)), per_t((LANES, past)), per_t((NSA_HEADS, HEAD_DIM)),
                  per_t((1, 4 * LANES)), per_t((NSA_HEADS, 3)), const_t(b_c), const_t(expand)],
        out_specs=per_t((NSA_HEADS, HEAD_DIM)),
        compiler_params=_params(("arbitrary",)),
        name="nsa_decode_select",
    )(q3, cmp, s_sel, s_new, v_sel, o_w, kvn3, gates3, b_c, expand)
    return o, win_out


def _hgrn_decode_kernel(h4_ref, lb_ref, gn_ref, s_ref, o_ref, s_out_ref, ft_ref):
    b = pl.program_id(0)
    nbatch = h4_ref.shape[0]

    @pl.when(b == 0)
    def _():
        for h in range(HG_HEADS):
            f, _, _ = _hgrn_gates(h4_ref[:, HG_WIDTH + h * HG_DIM:HG_WIDTH + (h + 1) * HG_DIM],
                                  lb_ref[:, h * HG_DIM:(h + 1) * HG_DIM])
            ft_ref[h] = f.T

    lane = lax.broadcasted_iota(i32, (HG_DIM, nbatch), 1)
    gn = gn_ref[...]
    for sq in range(s_ref.shape[0]):
        sample = b * s_ref.shape[0] + sq
        row = h4_ref[pl.ds(sample, 1), :]
        for h in range(HG_HEADS):
            part = lambda k: row[:, k * HG_WIDTH + h * HG_DIM:k * HG_WIDTH + (h + 1) * HG_DIM]
            f_col = jnp.sum(jnp.where(lane == sample, ft_ref[h], 0.0), axis=-1, keepdims=True)
            s_new = f_col * s_ref[sq, h] + (1.0 - f_col) * part(2)
            s_out_ref[sq, h] = s_new
            q = jnp.broadcast_to(part(0) * HG_SCALE, (SUBLANES, HG_DIM)).astype(bf16)
            o = jnp.dot(q, s_new.astype(bf16), preferred_element_type=f32)[0:1]
            o_ref[sq, :, h * HG_DIM:(h + 1) * HG_DIM] = _hgrn_out(o, gn, part(3)).astype(bf16)


def _hgrn_decode(h4, lb, gn, state, nseq=8):
    Bs = h4.shape[0]
    nseq = math.gcd(nseq, Bs)
    sspec = pl.BlockSpec((nseq, HG_HEADS, HG_DIM, HG_DIM), lambda b: (b, 0, 0, 0))
    return pl.pallas_call(
        _hgrn_decode_kernel,
        out_shape=(jax.ShapeDtypeStruct((Bs, 1, HG_WIDTH), bf16),
                   jax.ShapeDtypeStruct((Bs, HG_HEADS, HG_DIM, HG_DIM), f32)),
        grid=(Bs // nseq,),
        in_specs=[pl.BlockSpec(h4.shape, lambda b: (0, 0)), pl.BlockSpec(lb.shape, lambda b: (0, 0)),
                  pl.BlockSpec(gn.shape, lambda b: (0, 0)), sspec],
        out_specs=(pl.BlockSpec((nseq, 1, HG_WIDTH), lambda b: (b, 0, 0)), sspec),
        scratch_shapes=[pltpu.VMEM((HG_HEADS, HG_DIM, Bs), f32)],
        compiler_params=_params(("arbitrary",)),
        name="hgrn_decode",
    )(h4, lb, gn, state)


def _split_w_in(w):
    idx = np.cumsum(SPLITS)[:-1]
    return jnp.split(w, [int(v) for v in idx], axis=1)


def kernel(x_prompt, x_sample, cache_kv, page_table, state_kv_win, state_hgrn, state_conv, rel_table, hg_lb_logits,
           norm1_g, w_in, q_norm_g, k_norm_g, phi_k_w1, phi_k_w2, phi_v_w1, phi_v_w2, hg_norm_g, w_nsa_out, w_hg_out,
           w_o, norm2_g, w_gate, w_up, conv_w, conv_b, w_down):
    assert w_in.shape[0] == 1, "one layer"
    Bp, L, _ = x_prompt.shape
    Bs = x_sample.shape[0]
    n_pages = page_table.shape[1]
    past = n_pages * PAGE_SIZE
    n_win = state_kv_win.shape[2]
    assert x_sample.shape[1] == 1 and n_win == WINDOW and past % (2 * CMP_BLOCK) == 0 and L % LANES == 0

    wq, wkv, wgl, whq, whf, whi, whg, wmg = _split_w_in(w_in[0])
    wgl = jnp.pad(wgl, ((0, 0), (0, LANES - wgl.shape[1])))
    w_prompt = jnp.concatenate([wq, wkv, whq, whf, whi, whg, wmg, wgl], axis=1).astype(bf16)
    wq_h = wq.reshape(D_MODEL, NSA_HEADS, HEAD_DIM)
    zq = jnp.zeros_like(wq_h)
    in_g0 = (np.arange(NSA_HEADS) < NSA_GROUP)[None, :, None]
    wq_bd = jnp.where(in_g0, jnp.concatenate([wq_h, zq], axis=-1), jnp.concatenate([zq, wq_h], axis=-1))
    w_sample = jnp.concatenate([wq_bd.reshape(D_MODEL, NSA_HEADS * LANES).astype(bf16), w_prompt[:, NSA_WIDTH:]],
                               axis=1)
    g1 = norm1_g[0][None, :]
    gq = q_norm_g[0]
    gq_prompt = jnp.tile(gq, NSA_HEADS)[None, :]
    gq_sample = jnp.tile(gq, 2 * NSA_HEADS)[None, :]
    gk = [jnp.tile(k_norm_g[0, s], NSA_KV_HEADS)[None, :] for s in range(3)]

    def block_diag(mats):
        n = len(mats)
        lead = [(0, 0)] * (mats[0].ndim - 2)
        rows = [jnp.pad(m.astype(bf16), lead + [(0, 0), (t * HEAD_DIM, (n - 1 - t) * HEAD_DIM)])
                for t, m in enumerate(mats)]
        return jnp.concatenate(rows, axis=-2)

    w1bd = block_diag([phi_k_w1[0], phi_k_w1[0], phi_v_w1[0], phi_v_w1[0]]).reshape(CMP_BLOCK * 2 * LANES, 2 * LANES)
    w2bd = block_diag([phi_k_w2[0], phi_k_w2[0], phi_v_w2[0], phi_v_w2[0]])
    lb = jnp.cumsum(jax.nn.softmax(hg_lb_logits.astype(f32), axis=0), axis=0)[0][None, :]
    gn = hg_norm_g[0][None, :]
    ffn_w = (w_nsa_out[0].astype(bf16), w_hg_out[0].astype(bf16), w_o[0].astype(bf16), norm2_g[0][None, :],
             w_gate[0].astype(bf16), w_up[0].astype(bf16), conv_w[0], conv_b[0][None, :], w_down[0].astype(bf16))

    idx_c, idx_t = _prompt_bucket_tables(L)
    tc = _bias_tables(rel_table, jnp.asarray(idx_c))
    tt = _bias_tables(rel_table, jnp.asarray(idx_t))
    bias_dec = _bias_tables(rel_table, jnp.asarray(_decode_bucket_table(past, n_win))[None]).reshape(NSA_HEADS, -1)
    ex_prompt = jnp.asarray(_expand_np(L // SEL_BLOCK, L), dtype=bf16)
    ex_sample = jnp.asarray(_expand_np(past // SEL_BLOCK, past), dtype=bf16)

    xp = x_prompt.reshape(Bp * L, D_MODEL)
    qn, kv_t, win32, katt, h4, mgs, gates = _in_proj(xp, g1, w_prompt, gq_prompt, gk[1], gk[2],
                                                     qw=NSA_WIDTH, q_seg=HEAD_DIM, tm=512, seq_len=L)
    cmp = _compress_prompt(kv_t, w1bd, w2bd, gk[0])
    o_nsa = _nsa_prompt(qn, gates, katt, cmp, tc, tt, ex_prompt, Bp, L)
    o_hg, s_prompt = _hgrn_prompt(h4, lb, gn, Bp, L)
    tm_ffn = 512
    yp, gate_tails = _ffn(xp, o_nsa, o_hg.reshape(Bp * L, HG_WIDTH), mgs, *ffn_w, tm=tm_ffn, tf=D_FF // 2, seq_len=L)
    conv_p = gate_tails.reshape(Bp, L // tm_ffn, 2, D_FF)[:, -1]
    n_keep_win = min(WINDOW, L)
    win_p = jnp.swapaxes(win32[:, :, L - n_keep_win:], 1, 2).reshape(Bp, n_keep_win, 2, NSA_KV_HEADS, HEAD_DIM)

    xs = x_sample.reshape(Bs, D_MODEL)
    q_bd, kv32_s, win32_s, _, h4_s, mgs_s, gates_s = _in_proj(xs, g1, w_sample, gq_sample, gk[1], gk[2],
                                                              qw=NSA_HEADS * LANES, q_seg=LANES, tm=Bs)
    cache_t = jnp.swapaxes(cache_kv[0].reshape(cache_kv.shape[1], PAGE_SIZE, 4 * LANES), 1, 2)
    win_t = jnp.swapaxes(state_kv_win[0].reshape(Bs, n_win, 2 * LANES), 1, 2)
    o_nsa_s, win_out_t = _nsa_decode(cache_t, page_table, q_bd, kv32_s, win32_s, gates_s, win_t, w1bd, w2bd, gk[0],
                                     bias_dec, ex_sample)
    win_s = jnp.swapaxes(win_out_t, 1, 2)
    kv32 = jnp.swapaxes(kv_t, 1, 2)
    o_hg_s, s_sample = _hgrn_decode(h4_s, lb, gn, state_hgrn[0])
    ys, gate_s = _ffn(xs, o_nsa_s.reshape(Bs, NSA_WIDTH), o_hg_s.reshape(Bs, HG_WIDTH), mgs_s, *ffn_w,
                      tm=Bs, tf=256, conv_state=state_conv[0])
    conv_s = jnp.stack([state_conv[0][:, 1], gate_s], axis=1)

    kvh = (NSA_KV_HEADS, HEAD_DIM)
    return (yp.reshape(Bp, L, D_MODEL), ys.reshape(Bs, 1, D_MODEL),
            kv32.reshape((1, Bp, L, 4) + kvh), kv32_s.reshape((1, Bs, 1, 4) + kvh),
            win_p[None], win_s.reshape((1, Bs, n_win, 2) + kvh),
            s_prompt[None], s_sample[None], conv_p[None], conv_s[None])
```

```python
import functools
import math

import numpy as np
import jax
import jax.numpy as jnp
from jax import lax
from jax.experimental import pallas as pl
from jax.experimental.pallas import tpu as pltpu

f32, bf16, i32 = jnp.float32, jnp.bfloat16, jnp.int32

D_MODEL = 1024
PAGE_SIZE = 128
NSA_HEADS, NSA_KV_HEADS, NSA_GROUP, HEAD_DIM = 8, 2, 4, 64
CMP_BLOCK, SEL_BLOCK, TOP_N, WINDOW = 32, 64, 16, 512
SCALE = HEAD_DIM ** -0.5
NSA_WIDTH = NSA_HEADS * HEAD_DIM
KV_WIDTH = NSA_KV_HEADS * HEAD_DIM
HG_HEADS, HG_DIM = 4, 128
HG_WIDTH = HG_HEADS * HG_DIM
HG_SCALE = HG_DIM ** -0.5
HG_BLOCK = 16
SEL_CLASSES = 8
SUBLANES = 8
FFN_SLICE = 768
REL_BUCKETS, REL_MAX_DIST = 32, 128
FAR_TILE = -(-(REL_MAX_DIST + SEL_BLOCK) // SEL_BLOCK) + 1
D_FF = 2816
EPS = 1e-6
SPLITS = (NSA_WIDTH, 6 * KV_WIDTH, 3 * NSA_HEADS, HG_WIDTH, HG_WIDTH, HG_WIDTH, HG_WIDTH, 2 * D_MODEL)
LANES = 128
NEG = -1e30
FORCED_SCORE, INVALID_SCORE = 8.0, -1.0
VMEM_LIMIT = 56 * 1024 * 1024

NT = (((1,), (1,)), ((), ()))
TN = (((0,), (0,)), ((), ()))


def _params(sem):
    return pltpu.CompilerParams(dimension_semantics=sem, vmem_limit_bytes=VMEM_LIMIT)


def _bucket_np(dist):
    n = np.maximum(dist, 0)
    exact = REL_BUCKETS // 2
    ratio = np.log(np.maximum(n, 1).astype(np.float32) / np.float32(exact)) / np.float32(math.log(REL_MAX_DIST / exact))
    big = exact + (ratio.astype(np.float32) * np.float32(REL_BUCKETS - exact)).astype(np.int32)
    return np.where(n < exact, n, np.minimum(big, REL_BUCKETS - 1)).astype(np.int32)


def _cmp_perm(n_cmp):
    half = n_cmp // 2
    c = np.arange(n_cmp)
    return np.where(c < half, 2 * c, 2 * (c - half) + 1)


def _prompt_bucket_tables(L):
    nqb = L // SEL_BLOCK
    n_cmp = L // CMP_BLOCK
    qi = np.arange(SEL_BLOCK)
    c_end = (_cmp_perm(n_cmp) + 1) * CMP_BLOCK - 1
    idx_c = np.stack([_bucket_np((i * SEL_BLOCK + qi)[:, None] - c_end[None, :]) for i in range(nqb)])
    kj = np.arange(SEL_BLOCK)
    def tile(delta):
        return _bucket_np(delta * SEL_BLOCK + qi[:, None] - kj[None, :])
    idx_t = np.stack([np.concatenate([tile(e), tile(e - 1)], axis=1) for e in range(FAR_TILE + 1)])
    return idx_c.astype(np.int32), idx_t.astype(np.int32)


def _decode_bucket_table(past_len, n_win):
    n_cmp = past_len // CMP_BLOCK
    c_end = (_cmp_perm(n_cmp) + 1) * CMP_BLOCK - 1
    d_c = np.zeros((LANES,), np.int64)
    d_c[:n_cmp] = past_len - c_end
    d_s = past_len - np.arange(past_len)
    d_w = n_win - np.arange(n_win)
    d_new = np.zeros((LANES,), np.int64)
    return _bucket_np(np.concatenate([d_s, d_w, d_c, d_new]))[None, :].astype(np.int32)


def _expand_np(n_blocks, n_keys):
    e = (np.arange(n_keys)[None, :] // SEL_BLOCK == np.arange(n_blocks)[:, None])
    return e.astype(np.float32)


def _seg_rmsnorm(x, gain, seg):
    outs = []
    for t in range(x.shape[1] // LANES):
        xt = x[:, t * LANES:(t + 1) * LANES]
        sq = xt * xt
        if seg == LANES:
            r = lax.rsqrt(jnp.sum(sq, axis=-1, keepdims=True) * (1.0 / HEAD_DIM) + EPS)
        else:
            lo = lax.broadcasted_iota(i32, xt.shape, 1) < HEAD_DIM
            s_lo = jnp.sum(jnp.where(lo, sq, 0.0), axis=-1, keepdims=True)
            s_hi = jnp.sum(jnp.where(lo, 0.0, sq), axis=-1, keepdims=True)
            r = jnp.where(lo, lax.rsqrt(s_lo * (1.0 / HEAD_DIM) + EPS), lax.rsqrt(s_hi * (1.0 / HEAD_DIM) + EPS))
        outs.append(xt * r)
    y = outs[0] if len(outs) == 1 else jnp.concatenate(outs, axis=1)
    return y * gain


def _rmsnorm_rows(x, gain):
    return x * lax.rsqrt(jnp.mean(x * x, axis=-1, keepdims=True) + EPS) * gain


def _softmax_parts(s, mask):
    s = jnp.where(mask, s, NEG)
    m = jnp.max(s, axis=-1, keepdims=True)
    e = jnp.where(mask, jnp.exp(s - m), 0.0)
    return e, jnp.sum(e, axis=-1, keepdims=True)


def _bias_table_kernel(tab_ref, idx_ref, out_ref):
    idx = idx_ref[...]
    rows = idx.shape[0]
    for h in range(NSA_HEADS):
        acc = jnp.zeros(idx.shape, f32)
        for b in range(REL_BUCKETS):
            acc = jnp.where(idx == b, tab_ref[b, h], acc)
        r = h % NSA_GROUP
        out_ref[h // NSA_GROUP, r * rows:(r + 1) * rows, :] = acc


def _bias_tables(rel_table, idx):
    n, rows, w = idx.shape
    return pl.pallas_call(
        _bias_table_kernel,
        out_shape=jax.ShapeDtypeStruct((n, NSA_KV_HEADS, NSA_GROUP * rows, w), f32),
        grid=(n,),
        in_specs=[pl.BlockSpec(memory_space=pltpu.SMEM),
                  pl.BlockSpec((None, rows, w), lambda i: (i, 0, 0))],
        out_specs=pl.BlockSpec((None, NSA_KV_HEADS, NSA_GROUP * rows, w), lambda i: (i, 0, 0, 0)),
        compiler_params=_params(("arbitrary",)),
        name="bias_tables",
    )(rel_table, idx)


def _in_proj_kernel(x_ref, g1_ref, w_ref, gq_ref, gk1_ref, gk2_ref,
                    q_ref, kv_ref, win_ref, katt_ref, h4_ref, mg_ref, gl_ref, *, qw, q_seg, kv_feature_major):
    x = x_ref[...]
    xn = _rmsnorm_rows(x, g1_ref[...]).astype(bf16)

    def proj(c0, width):
        return jnp.dot(xn, w_ref[:, c0:c0 + width], preferred_element_type=f32)

    q_ref[...] = (_seg_rmsnorm(proj(0, qw), gq_ref[...], q_seg) * SCALE).astype(bf16)
    c = qw
    kv = proj(c, 6 * KV_WIDTH)
    c += 6 * KV_WIDTH
    k_sel = _seg_rmsnorm(kv[:, 2 * KV_WIDTH:3 * KV_WIDTH], gk1_ref[...], HEAD_DIM)
    k_win = _seg_rmsnorm(kv[:, 4 * KV_WIDTH:5 * KV_WIDTH], gk2_ref[...], HEAD_DIM)
    v_sel = kv[:, 3 * KV_WIDTH:4 * KV_WIDTH]
    v_win = kv[:, 5 * KV_WIDTH:6 * KV_WIDTH]
    if kv_feature_major:
        kv_ref[0:2 * KV_WIDTH, :] = kv[:, 0:2 * KV_WIDTH].T
        kv_ref[2 * KV_WIDTH:3 * KV_WIDTH, :] = k_sel.T
        kv_ref[3 * KV_WIDTH:4 * KV_WIDTH, :] = v_sel.T
    else:
        kv_ref[:, 0:2 * KV_WIDTH] = kv[:, 0:2 * KV_WIDTH]
        kv_ref[:, 2 * KV_WIDTH:3 * KV_WIDTH] = k_sel
        kv_ref[:, 3 * KV_WIDTH:4 * KV_WIDTH] = v_sel
    if kv_feature_major:
        win_ref[0:KV_WIDTH, :] = k_win.T
        win_ref[KV_WIDTH:2 * KV_WIDTH, :] = v_win.T
    else:
        win_ref[:, 0:KV_WIDTH] = k_win
        win_ref[:, KV_WIDTH:2 * KV_WIDTH] = v_win
    katt_ref[:, 0:KV_WIDTH] = k_sel.astype(bf16)
    katt_ref[:, KV_WIDTH:2 * KV_WIDTH] = v_sel.astype(bf16)
    katt_ref[:, 2 * KV_WIDTH:3 * KV_WIDTH] = k_win.astype(bf16)
    katt_ref[:, 3 * KV_WIDTH:4 * KV_WIDTH] = v_win.astype(bf16)
    h4_ref[...] = proj(c, 4 * HG_WIDTH)
    c += 4 * HG_WIDTH
    mg_ref[...] = jax.nn.sigmoid(proj(c, 2 * D_MODEL))
    c += 2 * D_MODEL
    gl_ref[...] = jax.nn.sigmoid(proj(c, LANES))


def _in_proj(x2d, g1, w, gq, gk1, gk2, *, qw, q_seg, tm, seq_len=None):
    n = x2d.shape[0]
    row = lambda width: pl.BlockSpec((tm, width), lambda i: (i, 0))
    full = lambda a: pl.BlockSpec(a.shape, lambda i: (0,) * a.ndim)
    widths = (qw, 4 * KV_WIDTH, 2 * KV_WIDTH, 4 * KV_WIDTH, 4 * HG_WIDTH, 2 * D_MODEL, LANES)
    dtypes = (bf16, f32, f32, bf16, f32, f32, f32)
    out_shape = [jax.ShapeDtypeStruct((n, wd), dt) for wd, dt in zip(widths, dtypes)]
    out_specs = [row(wd) for wd in widths]
    if seq_len is not None:
        tiles = seq_len // tm
        for o_, wd in ((1, 4 * KV_WIDTH), (2, 2 * KV_WIDTH)):
            out_shape[o_] = jax.ShapeDtypeStruct((n // seq_len, wd, seq_len), f32)
            out_specs[o_] = pl.BlockSpec((None, wd, tm), lambda i: (i // tiles, 0, i % tiles))
    return pl.pallas_call(
        functools.partial(_in_proj_kernel, qw=qw, q_seg=q_seg, kv_feature_major=seq_len is not None),
        out_shape=tuple(out_shape),
        grid=(n // tm,),
        in_specs=[row(D_MODEL), full(g1),
                  pl.BlockSpec(w.shape, lambda i: (0, 0), pipeline_mode=pl.Buffered(1)),
                  full(gq), full(gk1), full(gk2)],
        out_specs=tuple(out_specs),
        compiler_params=_params(("arbitrary",)),
        name="in_proj",
    )(x2d, g1, w, gq, gk1, gk2)


CMP_SLOT = CMP_BLOCK + 1


def _cmp_staging_rows(n_cmp):
    return -(-n_cmp * CMP_SLOT // SUBLANES) * SUBLANES


def _stage_cmp_chunk(src_ref, chunk, kc_t, vc_t, n_cmp):
    per_chunk = LANES // CMP_BLOCK
    for s, t in enumerate((kc_t.astype(bf16).T.astype(f32), vc_t.astype(bf16).T.astype(f32))):
        for nl in range(per_chunk):
            n = chunk * per_chunk + nl
            slot = n // 2 + (n % 2) * (n_cmp // 2)
            src_ref[s, slot * CMP_SLOT:slot * CMP_SLOT + CMP_BLOCK, :] = t[nl * CMP_BLOCK:(nl + 1) * CMP_BLOCK]


def _compress(src_refs, w1_ref, w2_ref, gk0, n_cmp):
    acc = jnp.zeros((len(src_refs) * n_cmp, 2 * LANES), f32)
    per_dot = 8
    for j0 in range(0, CMP_BLOCK, per_dot):
        xs = jnp.concatenate(
            [jnp.concatenate([src[s, pl.ds(j, n_cmp, stride=CMP_SLOT), :].astype(bf16)
                              for j in range(j0, j0 + per_dot) for s in range(2)], axis=1)
             for src in src_refs], axis=0)
        w = w1_ref[j0 * 2 * LANES:(j0 + per_dot) * 2 * LANES, :]
        acc = acc + jnp.dot(xs, w, preferred_element_type=f32)
    hmid = jax.nn.gelu(acc).astype(bf16)
    y = jnp.dot(hmid, w2_ref[...], preferred_element_type=f32)
    ck = _seg_rmsnorm(y[:, 0:LANES], gk0, HEAD_DIM)
    return jnp.concatenate([ck, y[:, LANES:2 * LANES]], axis=1)


def _compress_prompt_kernel(kv_ref, w1_ref, w2_ref, gk0_ref, out_ref, src_ref, *, n_cmp):
    for c in range(kv_ref.shape[1] // LANES):
        cols = slice(c * LANES, (c + 1) * LANES)
        _stage_cmp_chunk(src_ref, c, kv_ref[0:LANES, cols], kv_ref[LANES:2 * LANES, cols], n_cmp)
    out_ref[...] = _compress([src_ref], w1_ref, w2_ref, gk0_ref[...], n_cmp).astype(bf16)


def _compress_prompt(kv_t, w1bd, w2bd, gk0):
    B, _, L = kv_t.shape
    n_cmp = L // CMP_BLOCK
    return pl.pallas_call(
        functools.partial(_compress_prompt_kernel, n_cmp=n_cmp),
        out_shape=jax.ShapeDtypeStruct((B, n_cmp, 2 * LANES), bf16),
        grid=(B,),
        in_specs=[pl.BlockSpec((None, 2 * LANES, L), lambda b: (b, 0, 0)),
                  pl.BlockSpec(w1bd.shape, lambda b: (0, 0)),
                  pl.BlockSpec(w2bd.shape, lambda b: (0, 0)),
                  pl.BlockSpec(gk0.shape, lambda b: (0, 0))],
        out_specs=pl.BlockSpec((None, n_cmp, 2 * LANES), lambda b: (b, 0, 0)),
        scratch_shapes=[pltpu.VMEM((2, _cmp_staging_rows(n_cmp), LANES), f32)],
        compiler_params=_params(("arbitrary",)),
        name="compress_prompt",
    )(kv_t, w1bd, w2bd, gk0)


def _select_blocks_t(score, valid, n_keep):
    nb = score.shape[0]
    jj = lax.broadcasted_iota(i32, score.shape, 0)
    cnt = jnp.zeros(score.shape, f32)
    for k in range(nb):
        sk = score[k:k + 1, :]
        better = (sk > score) | ((sk == score) & (jj > k))
        cnt = cnt + jnp.where(better, 1.0, 0.0)
    return (cnt < n_keep) & valid


def _sel_tile_classes(n_tiles):
    return sorted({-(-n_tiles * c // SEL_CLASSES) for c in range(1, SEL_CLASSES + 1)})


def _nsa_prompt_kernel(q_ref, gate_ref, katt_ref, cmp_ref, tc_ref, tt_ref, ex_ref, o_ref,
                       selx_ref, s_ref, os_ref, part_ref, gsel_ref, *, L):
    i = pl.program_id(1)
    n_cmp = L // CMP_BLOCK
    n_sel = L // SEL_BLOCK
    n_tiles = L // LANES
    qb = SEL_BLOCK
    rows = NSA_GROUP * qb
    units = [(sq, g) for sq in range(q_ref.shape[0]) for g in range(NSA_KV_HEADS)]
    nu = range(len(units))
    row64 = lax.broadcasted_iota(i32, (qb, LANES), 0)
    lane = lax.broadcasted_iota(i32, (qb, LANES), 1)
    qpos = i * qb + row64

    qs = [jnp.concatenate(
        [q_ref[sq, :, (NSA_GROUP * g + r) * HEAD_DIM:(NSA_GROUP * g + r + 1) * HEAD_DIM] for r in range(NSA_GROUP)],
        axis=0) for sq, g in units]

    def attend(k_off, v_off, tile0, n_t, mask_fn):
        chunk = 4

        def key_rows(t0, count, off, u):
            sq, g = units[u]
            r0 = pl.multiple_of((tile0 + t0) * LANES, LANES)
            return katt_ref[sq, pl.ds(r0, count * LANES), off + g * HEAD_DIM:off + (g + 1) * HEAD_DIM]

        def scores(t0, count, mruns):
            dots = [lax.dot_general(qs[u], key_rows(t0, count, k_off, u), NT, preferred_element_type=f32)
                    for u in nu]
            out = list(mruns)
            for j in range(count):
                ta = tile0 + t0 + j
                bias_idx = jnp.clip(i - 2 * ta, 0, FAR_TILE)
                kpos = ta * LANES + lane
                for u in nu:
                    s = dots[u][:, j * LANES:(j + 1) * LANES] + tt_ref[bias_idx, units[u][1]]
                    msk = mask_fn(u, ta, kpos)
                    s = jnp.where(msk[None], s.reshape(NSA_GROUP, qb, LANES), NEG).reshape(rows, LANES)
                    s_ref[u, t0 + j] = s
                    out[u] = jnp.maximum(out[u], s)
            return tuple(out)

        def values(t0, count, carry, ms):
            ls, accs = list(carry[0]), list(carry[1])
            ps = []
            for u in nu:
                pj = [jnp.exp(s_ref[u, t0 + j] - ms[u]) for j in range(count)]
                for p in pj:
                    ls[u] = ls[u] + p
                ps.append(jnp.concatenate([p.astype(bf16) for p in pj], axis=1))
            for u in nu:
                accs[u] = accs[u] + jnp.dot(ps[u], key_rows(t0, count, v_off, u), preferred_element_type=f32)
            return tuple(ls), tuple(accs)

        def over_chunks(fn, carry):
            for t0 in range(0, n_t, chunk):
                carry = fn(t0, min(chunk, n_t - t0), carry)
            return carry

        mruns = over_chunks(scores, tuple(jnp.full((rows, LANES), NEG, f32) for _ in nu))
        ms = [jnp.max(m, axis=-1, keepdims=True) for m in mruns]
        init = (tuple(jnp.zeros((rows, LANES), f32) for _ in nu),
                tuple(jnp.zeros((rows, HEAD_DIM), f32) for _ in nu))
        ls, accs = over_chunks(lambda t0, count, cr: values(t0, count, cr, ms), init)
        return [accs[u] / jnp.sum(ls[u], axis=-1, keepdims=True) for u in nu]

    n_wt = min(WINDOW // LANES + 1, n_tiles)
    w0 = jnp.clip((i - WINDOW // qb) // 2, 0, n_tiles - n_wt)
    o_w = attend(2 * KV_WIDTH, 3 * KV_WIDTH, w0, n_wt,
                 lambda u, ta, kpos: (kpos <= qpos) & (qpos - kpos <= WINDOW))

    o_c, imp_t = [], []
    col = lax.broadcasted_iota(i32, (rows, n_cmp), 1)
    rq = lax.broadcasted_iota(i32, (rows, n_cmp), 0) & (qb - 1)
    blk = jnp.where(col < n_cmp // 2, 2 * col, 2 * col - (n_cmp - 1))
    vis = (blk + 1) * CMP_BLOCK - 1 <= i * qb + rq
    for u, (sq, g) in enumerate(units):
        ck = cmp_ref[sq, :, g * HEAD_DIM:(g + 1) * HEAD_DIM]
        cv = cmp_ref[sq, :, LANES + g * HEAD_DIM:LANES + (g + 1) * HEAD_DIM]
        s = lax.dot_general(qs[u], ck, NT, preferred_element_type=f32) + tc_ref[g]
        e, den = _softmax_parts(s, vis)
        p = e / jnp.where(den > 0, den, 1.0)
        o_c.append(jnp.dot(p.astype(bf16), cv, preferred_element_type=f32))
        ps_t = (p[0:qb] + p[qb:2 * qb] + p[2 * qb:3 * qb] + p[3 * qb:4 * qb]).T
        imp_t.append(ps_t[0:n_sel] + ps_t[n_sel:2 * n_sel])

    imp = jnp.concatenate(imp_t, axis=1)
    jj = lax.broadcasted_iota(i32, imp.shape, 0)
    valid = jj <= i
    forced = valid & ((jj == 0) | (jj >= i - 1))
    score = jnp.where(forced, FORCED_SCORE, jnp.where(valid, imp, INVALID_SCORE))
    sel = jnp.where(_select_blocks_t(score, valid, min(TOP_N, n_sel)), 1.0, 0.0).astype(bf16)

    for u, (sq, g) in enumerate(units):
        gates = gate_ref[sq]
        for r in range(NSA_GROUP):
            h = NSA_GROUP * g + r
            hs = slice(h * HEAD_DIM, (h + 1) * HEAD_DIM)
            sl = slice(r * qb, (r + 1) * qb)
            part_ref[sq, :, hs] = gates[:, 3 * h:3 * h + 1] * o_c[u][sl] + gates[:, 3 * h + 2:3 * h + 3] * o_w[u][sl]
            gsel_ref[sq, :, hs] = jnp.broadcast_to(gates[:, 3 * h + 1:3 * h + 2], (qb, HEAD_DIM))

    need = i // 2 + 1
    prev = 0
    for n_t in _sel_tile_classes(n_tiles):
        @pl.when((need > prev) & (need <= n_t))
        def _(n_t=n_t):
            selx = lax.dot_general(sel, ex_ref[:, 0:n_t * LANES], TN, preferred_element_type=f32)
            for u in nu:
                for t in range(n_t):
                    selx_ref[u, t] = selx[u * qb:(u + 1) * qb, t * LANES:(t + 1) * LANES]
            o_s = attend(0, KV_WIDTH, 0, n_t, lambda u, ta, kpos: (selx_ref[u, ta] > 0.5) & (kpos <= qpos))
            for u in nu:
                os_ref[u] = o_s[u]
        prev = n_t

    for u, (sq, g) in enumerate(units):
        o_s = os_ref[u]
        for r in range(NSA_GROUP):
            hs = slice((NSA_GROUP * g + r) * HEAD_DIM, (NSA_GROUP * g + r + 1) * HEAD_DIM)
            o = part_ref[sq, :, hs] + gsel_ref[sq, :, hs] * o_s[r * qb:(r + 1) * qb]
            o_ref[sq, :, hs] = o.astype(bf16)


def _nsa_prompt(qn, gates, katt, cmp, tc, tt, expand, B, L, nseq=2):
    nqb = L // SEL_BLOCK
    qb = SEL_BLOCK
    n_tiles = L // LANES
    rows = NSA_GROUP * qb
    nseq = math.gcd(nseq, B)
    n_units = nseq * NSA_KV_HEADS
    seq3 = lambda a: a.reshape(B, L, a.shape[-1])
    return pl.pallas_call(
        functools.partial(_nsa_prompt_kernel, L=L),
        out_shape=jax.ShapeDtypeStruct((B, L, NSA_WIDTH), bf16),
        grid=(B // nseq, nqb),
        in_specs=[pl.BlockSpec((nseq, qb, NSA_WIDTH), lambda b, i: (b, i, 0)),
                  pl.BlockSpec((nseq, qb, LANES), lambda b, i: (b, i, 0)),
                  pl.BlockSpec((nseq, L, 4 * KV_WIDTH), lambda b, i: (b, 0, 0)),
                  pl.BlockSpec((nseq, L // CMP_BLOCK, 2 * LANES), lambda b, i: (b, 0, 0)),
                  pl.BlockSpec((None,) + tc.shape[1:], lambda b, i: (i, 0, 0, 0)),
                  pl.BlockSpec(tt.shape, lambda b, i: (0, 0, 0, 0)),
                  pl.BlockSpec(expand.shape, lambda b, i: (0, 0))],
        out_specs=pl.BlockSpec((nseq, qb, NSA_WIDTH), lambda b, i: (b, i, 0)),
        scratch_shapes=[pltpu.VMEM((n_units, n_tiles, qb, LANES), f32),
                        pltpu.VMEM((n_units, n_tiles, rows, LANES), f32),
                        pltpu.VMEM((n_units, rows, HEAD_DIM), f32),
                        pltpu.VMEM((nseq, qb, NSA_WIDTH), f32), pltpu.VMEM((nseq, qb, NSA_WIDTH), f32)],
        compiler_params=_params(("arbitrary", "arbitrary")),
        name="nsa_prompt",
    )(seq3(qn), seq3(gates), seq3(katt), cmp, tc, tt, expand).reshape(B * L, NSA_WIDTH)


def _hgrn_gates(hf, lb):
    f = lb + (1.0 - lb) * jax.nn.sigmoid(hf)
    return f, 1.0 - f, jnp.log(f)


def _hgrn_out(o, gn, hgate):
    return _rmsnorm_rows(o, gn) * jax.nn.silu(hgate)


def _hgrn_prompt_kernel(h4_ref, lb_ref, gn_ref, o_ref, s_out_ref, st_ref, *, tc):
    c = pl.program_id(1)
    nb = HG_BLOCK
    nseq = h4_ref.shape[0]

    @pl.when(c == 0)
    def _():
        st_ref[...] = jnp.zeros_like(st_ref)

    t_row = lax.broadcasted_iota(i32, (nb, HG_DIM), 0)
    t_col = lax.broadcasted_iota(i32, (nb, 1), 0)
    gn = gn_ref[...]
    chains = [(sq, h) for sq in range(nseq) for h in range(HG_HEADS)]

    def cumsum_rows(x):
        shift = 1
        while shift < nb:
            x = x + jnp.where(t_row >= shift, pltpu.roll(x, shift, axis=0), 0.0)
            shift *= 2
        return x

    def block(bi, _):
        r0 = pl.multiple_of(bi * nb, nb)

        def sl(sq, h, part):
            return h4_ref[sq, pl.ds(r0, nb), part * HG_WIDTH + h * HG_DIM:part * HG_WIDTH + (h + 1) * HG_DIM]

        qs, ks, vs, bs, sts, os_ = [], [], [], [], [], []
        for sq, h in chains:
            _, k, glog = _hgrn_gates(sl(sq, h, 1), lb_ref[:, h * HG_DIM:(h + 1) * HG_DIM])
            qs.append(sl(sq, h, 0) * HG_SCALE)
            ks.append(k)
            vs.append(sl(sq, h, 2))
            bs.append(cumsum_rows(glog))
        for c_, (sq, h) in enumerate(chains):
            sts.append(st_ref[sq, h])
            os_.append(lax.dot_general((qs[c_] * jnp.exp(bs[c_])).astype(bf16), sts[c_].astype(bf16), NT,
                                       preferred_element_type=f32))
        for c_ in range(len(chains)):
            q, k, v, b, o = qs[c_], ks[c_], vs[c_], bs[c_], os_[c_]
            parts = [o[r0_:r0_ + SUBLANES] for r0_ in range(0, nb, SUBLANES)]
            for s in range(nb):
                for gi in range(s // SUBLANES, nb // SUBLANES):
                    rs = slice(gi * SUBLANES, (gi + 1) * SUBLANES)
                    w = q[rs] * k[s:s + 1] * jnp.exp(b[rs] - b[s:s + 1])
                    a = jnp.sum(w, axis=-1, keepdims=True)
                    if gi == s // SUBLANES:
                        a = jnp.where(t_col[rs] >= s, a, 0.0)
                    parts[gi] = parts[gi] + a * v[s:s + 1]
            os_[c_] = jnp.concatenate(parts, axis=0)
        for c_, (sq, h) in enumerate(chains):
            b_last = bs[c_][nb - 1:nb]
            kt = ks[c_] * jnp.exp(b_last - bs[c_])
            upd = lax.dot_general(vs[c_].astype(bf16), kt.astype(bf16), TN, preferred_element_type=f32)
            st_ref[sq, h] = jnp.exp(b_last) * sts[c_] + upd
            o_ref[sq, pl.ds(r0, nb), h * HG_DIM:(h + 1) * HG_DIM] = _hgrn_out(os_[c_], gn, sl(sq, h, 3)).astype(bf16)
        return 0

    lax.fori_loop(0, tc // nb, block, 0, unroll=2)

    @pl.when(c == pl.num_programs(1) - 1)
    def _():
        for sq in range(nseq):
            for h in range(HG_HEADS):
                s_out_ref[sq, h] = st_ref[sq, h].T


def _hgrn_prompt(h4, lb, gn, B, L, tc=256, nseq=4):
    nc = L // tc
    nseq = math.gcd(nseq, B)
    return pl.pallas_call(
        functools.partial(_hgrn_prompt_kernel, tc=tc),
        out_shape=(jax.ShapeDtypeStruct((B, L, HG_WIDTH), bf16),
                   jax.ShapeDtypeStruct((B, HG_HEADS, HG_DIM, HG_DIM), f32)),
        grid=(B // nseq, nc),
        in_specs=[pl.BlockSpec((nseq, tc, 4 * HG_WIDTH), lambda b, c: (b, c, 0)),
                  pl.BlockSpec(lb.shape, lambda b, c: (0, 0)),
                  pl.BlockSpec(gn.shape, lambda b, c: (0, 0))],
        out_specs=(pl.BlockSpec((nseq, tc, HG_WIDTH), lambda b, c: (b, c, 0)),
                   pl.BlockSpec((nseq, HG_HEADS, HG_DIM, HG_DIM), lambda b, c: (b, 0, 0, 0))),
        scratch_shapes=[pltpu.VMEM((nseq, HG_HEADS, HG_DIM, HG_DIM), f32)],
        compiler_params=_params(("arbitrary", "arbitrary")),
        name="hgrn_prompt",
    )(h4.reshape(B, L, 4 * HG_WIDTH), lb, gn)


def _ffn_kernel(*refs, decode, tiles_per_seq):
    if decode:
        (x_ref, on_ref, oh_ref, mg_ref, wn_ref, wh_ref, wo_ref, g2_ref, wg_ref, wu_ref, cw_ref, cb_ref, wd_ref,
         cs0_ref, cs1_ref, y_ref, gate_out_ref, x1_ref, xn2_ref, acc_ref) = refs
    else:
        (x_ref, on_ref, oh_ref, mg_ref, wn_ref, wh_ref, wo_ref, g2_ref, wg_ref, wu_ref, cw_ref, cb_ref, wd_ref,
         y_ref, gate_out_ref, x1_ref, xn2_ref, acc_ref, carry_ref) = refs
    i = pl.program_id(0)
    fi = pl.program_id(1)
    tm = x_ref.shape[0]

    @pl.when(fi == 0)
    def _():
        y_a = jnp.dot(on_ref[...], wn_ref[...], preferred_element_type=f32)
        y_b = jnp.dot(oh_ref[...], wh_ref[...], preferred_element_type=f32)
        mg = mg_ref[...]
        merged = mg[:, 0:D_MODEL] * y_a + mg[:, D_MODEL:2 * D_MODEL] * y_b
        x1 = x_ref[...] + jnp.dot(merged.astype(bf16), wo_ref[...], preferred_element_type=f32)
        x1_ref[...] = x1
        xn2_ref[...] = _rmsnorm_rows(x1, g2_ref[...]).astype(bf16)
        acc_ref[...] = jnp.zeros_like(acc_ref)

    xn2 = xn2_ref[...]
    tf = wg_ref.shape[1]
    if not decode:
        @pl.when(i % tiles_per_seq == 0)
        def _():
            carry_ref[fi] = jnp.zeros(carry_ref.shape[1:], f32)

    acc = acc_ref[...]
    for a in range(0, tf, FFN_SLICE):
        cs = slice(a, min(a + FFN_SLICE, tf))
        gate = jnp.dot(xn2, wg_ref[:, cs], preferred_element_type=f32)
        up = jnp.dot(xn2, wu_ref[:, cs], preferred_element_type=f32)
        cw = cw_ref[:, cs]
        if decode:
            prev2, prev1 = cs0_ref[:, cs], cs1_ref[:, cs]
            gate_out_ref[:, cs] = gate
        else:
            carry = carry_ref[fi, :, cs]
            rid = lax.broadcasted_iota(i32, gate.shape, 0)
            c1, c2 = carry[SUBLANES - 1:SUBLANES], carry[SUBLANES - 2:SUBLANES - 1]
            prev1 = jnp.where(rid == 0, c1, pltpu.roll(gate, 1, axis=0))
            prev2 = jnp.where(rid == 0, c2, jnp.where(rid == 1, c1, pltpu.roll(gate, 2, axis=0)))
            carry_ref[fi, SUBLANES - 2:SUBLANES, cs] = gate[tm - 2:tm]
            gate_out_ref[:, cs] = gate[tm - 2:tm]
        conv = cb_ref[:, cs] + cw[0:1] * prev2 + cw[1:2] * prev1 + cw[2:3] * gate
        hmid = (jax.nn.silu(conv) * up).astype(bf16)
        acc = acc + jnp.dot(hmid, wd_ref[cs, :], preferred_element_type=f32)
    acc_ref[...] = acc

    @pl.when(fi == pl.num_programs(1) - 1)
    def _():
        y_ref[...] = x1_ref[...] + acc_ref[...]


def _ffn(x2d, o_nsa, o_hg, mgs, wn, wh, wo, g2, wg, wu, cw, cb, wd, *, tm, tf, seq_len=None, conv_state=None):
    n = x2d.shape[0]
    decode = conv_state is not None
    nf = D_FF // tf
    row = lambda width: pl.BlockSpec((tm, width), lambda i, f: (i, 0))
    full = lambda a: pl.BlockSpec(a.shape, lambda i, f: (0,) * a.ndim)
    in_specs = [row(D_MODEL), row(NSA_WIDTH), row(HG_WIDTH), row(2 * D_MODEL), full(wn), full(wh), full(wo), full(g2),
                pl.BlockSpec((D_MODEL, tf), lambda i, f: (0, f)), pl.BlockSpec((D_MODEL, tf), lambda i, f: (0, f)),
                pl.BlockSpec((3, tf), lambda i, f: (0, f)), pl.BlockSpec((1, tf), lambda i, f: (0, f)),
                pl.BlockSpec((tf, D_MODEL), lambda i, f: (f, 0))]
    args = [x2d, o_nsa, o_hg, mgs, wn, wh, wo, g2, wg, wu, cw, cb, wd]
    scratch = [pltpu.VMEM((tm, D_MODEL), f32), pltpu.VMEM((tm, D_MODEL), bf16), pltpu.VMEM((tm, D_MODEL), f32)]
    if decode:
        in_specs += [pl.BlockSpec((tm, tf), lambda i, f: (i, f))] * 2
        args += [conv_state[:, 0], conv_state[:, 1]]
        gate_shape = jax.ShapeDtypeStruct((n, D_FF), f32)
        gate_spec = pl.BlockSpec((tm, tf), lambda i, f: (i, f))
        tiles_per_seq = 1
    else:
        tiles_per_seq = seq_len // tm
        gate_shape = jax.ShapeDtypeStruct((n // tm, 2, D_FF), f32)
        gate_spec = pl.BlockSpec((None, 2, tf), lambda i, f: (i, 0, f))
        scratch.append(pltpu.VMEM((nf, SUBLANES, tf), f32))
    return pl.pallas_call(
        functools.partial(_ffn_kernel, decode=decode, tiles_per_seq=tiles_per_seq),
        out_shape=(jax.ShapeDtypeStruct((n, D_MODEL), f32), gate_shape),
        grid=(n // tm, nf),
        in_specs=in_specs,
        out_specs=(row(D_MODEL), gate_spec),
        scratch_shapes=scratch,
        compiler_params=_params(("arbitrary", "arbitrary")),
        name="ffn_decode" if decode else "ffn_prompt",
    )(*args)


def _nsa_decode_kernel(*refs, n_pages, n_win, nseq):
    page_refs = refs[1:1 + nseq * n_pages]
    (q_ref, kvn_ref, winn_ref, win_all_ref, win_ref, w1_ref, w2_ref, gk0_ref, bias_ref,
     ow_ref, win_out_ref, cmp_ref, ssel_ref, snew_ref, vsel_ref, src_ref, ksel_ref, wcol_ref) = refs[1 + nseq * n_pages:]
    step = pl.program_id(0)
    past = n_pages * PAGE_SIZE
    n_cmp = past // CMP_BLOCK
    heads = NSA_HEADS
    samples = range(nseq)

    @pl.when(step == 0)
    def _():
        for t in range(2):
            wcol_ref[t * LANES:(t + 1) * LANES, :] = win_all_ref[:, t * LANES:(t + 1) * LANES].T

    for sq in samples:
        for k in range(n_pages):
            page = page_refs[sq * n_pages + k]
            cols = slice(k * PAGE_SIZE, (k + 1) * PAGE_SIZE)
            _stage_cmp_chunk(src_ref.at[sq], k, page[0:LANES, :], page[LANES:2 * LANES, :], n_cmp)
            ksel_ref[sq, :, cols] = page[2 * LANES:3 * LANES, :].astype(bf16)
            vsel_ref[sq, :, cols] = page[3 * LANES:4 * LANES, :].astype(bf16)

    hrow = lax.broadcasted_iota(i32, (heads, 1), 0)
    g0 = hrow < NSA_GROUP
    bias = bias_ref[...]
    b_s = bias[:, 0:past]
    b_w = bias[:, past:past + n_win]
    b_new = bias[:, past + n_win + LANES:past + n_win + LANES + 1]
    qs = [q_ref[sq] for sq in samples]
    attend = _decode_attend

    def scores(q, k_t, b_past, k_new):
        s_past = jnp.dot(q, k_t, preferred_element_type=f32) + b_past
        s_new = jnp.sum(q.astype(f32) * k_new, axis=-1, keepdims=True) + b_new
        return s_past, s_new

    lane_b = lax.broadcasted_iota(i32, wcol_ref.shape, 1)
    lane_w = lax.broadcasted_iota(i32, win_ref.shape[1:], 1)
    for sq in samples:
        winn, kvn = winn_ref[sq], kvn_ref[sq]
        s_past, s_new = scores(qs[sq], ksel_ref[sq], b_s, kvn[:, 2 * LANES:3 * LANES])
        ssel_ref[sq] = s_past
        snew_ref[sq] = jnp.broadcast_to(s_new, (heads, LANES))
        s_win = scores(qs[sq], win_ref[sq, 0:LANES, :].astype(bf16), b_w, winn[:, 0:LANES])
        ow_ref[sq] = attend(s_win, None, win_ref[sq, LANES:2 * LANES, :].astype(bf16), winn[:, LANES:2 * LANES],
                            g0)
        new_col = jnp.sum(jnp.where(lane_b == step * nseq + sq, wcol_ref[...], 0.0), axis=-1, keepdims=True)
        win_out_ref[sq] = jnp.where(lane_w == n_win - 1, new_col, pltpu.roll(win_ref[sq], n_win - 1, axis=1))

    cmp = _compress([src_ref.at[sq] for sq in samples], w1_ref, w2_ref, gk0_ref[...], n_cmp)
    for sq in samples:
        cmp_ref[sq] = cmp[sq * n_cmp:(sq + 1) * n_cmp].astype(bf16)


def _decode_attend(s, mask, v_t, v_new, g0):
    s_past, s_new = s
    sm = s_past if mask is None else jnp.where(mask, s_past, NEG)
    m = jnp.maximum(jnp.max(sm, axis=-1, keepdims=True), s_new)
    e = jnp.exp(sm - m)
    e_new = jnp.exp(s_new - m)
    den = jnp.sum(e, axis=-1, keepdims=True) + e_new
    o2 = (lax.dot_general(e.astype(bf16), v_t, NT, preferred_element_type=f32) + e_new * v_new) / den
    return jnp.where(g0, o2[:, 0:HEAD_DIM], o2[:, HEAD_DIM:2 * HEAD_DIM])


def _nsa_decode_tail_kernel(q_ref, cmp_ref, ssel_ref, snew_ref, vsel_ref, ow_ref, kvn_ref, gate_ref, bias_ref, ex_ref,
                            o_ref, *, n_sel):
    samples = range(q_ref.shape[0])
    heads = NSA_HEADS
    n_cmp = cmp_ref.shape[1]
    g0 = lax.broadcasted_iota(i32, (heads, 1), 0) < NSA_GROUP
    b_c = bias_ref[...]
    cur = n_sel
    jrow = lax.broadcasted_iota(i32, (NSA_KV_HEADS, n_sel), 1)
    forced = (jrow == 0) | (jrow >= cur - 1)
    n_keep = min(TOP_N, n_sel + 1) - 1
    ri = lax.broadcasted_iota(i32, (n_sel, n_sel), 0)
    ci = lax.broadcasted_iota(i32, (n_sel, n_sel), 1)

    s_c = [lax.dot_general(q_ref[sq], cmp_ref[sq, :, 0:LANES], NT, preferred_element_type=f32) + b_c for sq in samples]
    p_c = []
    for sq in samples:
        e_c, den_c = _softmax_parts(s_c[sq], jnp.full(s_c[sq].shape, True))
        p_c.append(e_c / jnp.where(den_c > 0, den_c, 1.0))
    o_c = [jnp.dot(p_c[sq].astype(bf16), cmp_ref[sq, :, LANES:2 * LANES], preferred_element_type=f32) for sq in samples]
    sels = []
    for sq in samples:
        ps = jnp.concatenate([jnp.sum(p_c[sq][0:NSA_GROUP], axis=0, keepdims=True),
                              jnp.sum(p_c[sq][NSA_GROUP:heads], axis=0, keepdims=True)], axis=0)
        imp = ps[:, 0:n_sel] + ps[:, n_sel:2 * n_sel]
        score_row = jnp.where(forced, FORCED_SCORE, imp)
        rows = []
        for g in range(NSA_KV_HEADS):
            rowb = jnp.broadcast_to(score_row[g:g + 1], (n_sel, n_sel))
            colb = jnp.sum(jnp.where(ri == ci, rowb, 0.0), axis=-1, keepdims=True)
            better = (colb > rowb) | ((colb == rowb) & (ri < ci))
            cnt = jnp.sum(jnp.where(better, 1.0, 0.0), axis=0, keepdims=True)
            rows.append(jnp.where(cnt < n_keep, 1.0, 0.0))
        sels.append(jnp.concatenate(rows, axis=0).astype(bf16))
    selx = [jnp.dot(sels[sq], ex_ref[...], preferred_element_type=f32) for sq in samples]
    for sq in samples:
        mask = jnp.where(g0, selx[sq][0:1], selx[sq][1:2]) > 0.5
        o_s = _decode_attend((ssel_ref[sq], snew_ref[sq][:, 0:1]), mask, vsel_ref[sq],
                             kvn_ref[sq][:, 3 * LANES:4 * LANES], g0)
        gates = gate_ref[sq]
        o_cg = jnp.where(g0, o_c[sq][:, 0:HEAD_DIM], o_c[sq][:, HEAD_DIM:2 * HEAD_DIM])
        o_ref[sq] = (gates[:, 0:1] * o_cg + gates[:, 1:2] * o_s + gates[:, 2:3] * ow_ref[sq]).astype(bf16)


def _nsa_decode(cache_t, page_table, q_bd, kv32, win32, gates, win_t, w1bd, w2bd, gk0, bias, expand,
                nseq=2, nseq_tail=8):
    Bs, n_pages = page_table.shape
    n_win = win_t.shape[2]
    past = n_pages * PAGE_SIZE
    n_cmp = past // CMP_BLOCK
    nseq = math.gcd(nseq, Bs)
    per_b = lambda shape: pl.BlockSpec((nseq,) + shape, lambda b, pt: (b,) + (0,) * len(shape))
    const = lambda a: pl.BlockSpec(a.shape, lambda b, pt: (0,) * a.ndim)
    page_spec = lambda sq, k: pl.BlockSpec((None, 4 * LANES, PAGE_SIZE), lambda b, pt: (pt[b * nseq + sq, k], 0, 0))
    q3 = q_bd.reshape(Bs, NSA_HEADS, LANES)
    kvn3 = kv32.reshape(Bs, 1, 4 * LANES)
    gates3 = gates[:, 0:3 * NSA_HEADS].reshape(Bs, NSA_HEADS, 3)
    grid_spec = pltpu.PrefetchScalarGridSpec(
        num_scalar_prefetch=1,
        grid=(Bs // nseq,),
        in_specs=[page_spec(sq, k) for sq in range(nseq) for k in range(n_pages)] + [
            per_b((NSA_HEADS, LANES)), per_b((1, 4 * LANES)), per_b((1, 2 * LANES)), const(win32),
            per_b((2 * LANES, n_win)), const(w1bd), const(w2bd), const(gk0), const(bias)],
        out_specs=(per_b((NSA_HEADS, HEAD_DIM)), per_b((2 * LANES, n_win)), per_b((n_cmp, 2 * LANES)),
                   per_b((NSA_HEADS, past)), per_b((NSA_HEADS, LANES)), per_b((LANES, past))),
        scratch_shapes=[pltpu.VMEM((nseq, 2, _cmp_staging_rows(n_cmp), LANES), f32),
                        pltpu.VMEM((nseq, LANES, past), bf16), pltpu.VMEM((2 * LANES, Bs), f32)],
    )
    o_w, win_out, cmp, s_sel, s_new, v_sel = pl.pallas_call(
        functools.partial(_nsa_decode_kernel, n_pages=n_pages, n_win=n_win, nseq=nseq),
        out_shape=(jax.ShapeDtypeStruct((Bs, NSA_HEADS, HEAD_DIM), f32),
                   jax.ShapeDtypeStruct((Bs, 2 * LANES, n_win), f32),
                   jax.ShapeDtypeStruct((Bs, n_cmp, 2 * LANES), bf16),
                   jax.ShapeDtypeStruct((Bs, NSA_HEADS, past), f32),
                   jax.ShapeDtypeStruct((Bs, NSA_HEADS, LANES), f32),
                   jax.ShapeDtypeStruct((Bs, LANES, past), bf16)),
        grid_spec=grid_spec,
        compiler_params=_params(("arbitrary",)),
        name="nsa_decode",
    )(page_table, *([cache_t] * (nseq * n_pages)), q3, kvn3, win32.reshape(Bs, 1, 2 * LANES), win32, win_t,
      w1bd, w2bd, gk0, bias)

    ns = math.gcd(nseq_tail, Bs)
    per_t = lambda shape: pl.BlockSpec((ns,) + shape, lambda b: (b,) + (0,) * len(shape))
    const_t = lambda a: pl.BlockSpec(a.shape, lambda b: (0,) * a.ndim)
    b_c = bias[:, past + n_win:past + n_win + n_cmp]
    o = pl.pallas_call(
        functools.partial(_nsa_decode_tail_kernel, n_sel=past // SEL_BLOCK),
        out_shape=jax.ShapeDtypeStruct((Bs, NSA_HEADS, HEAD_DIM), bf16),
        grid=(Bs // ns,),
        in_specs=[per_t((NSA_HEADS, LANES)), per_t((n_cmp, 2 * LANES)), per_t((NSA_HEADS, past)),
                  per_t((NSA_HEADS, LANES)), per_t((LANES, past)), per_t((NSA_HEADS, HEAD_DIM)),
                  per_t((1, 4 * LANES)), per_t((NSA_HEADS, 3)), const_t(b_c), const_t(expand)],
        out_specs=per_t((NSA_HEADS, HEAD_DIM)),
        compiler_params=_params(("arbitrary",)),
        name="nsa_decode_select",
    )(q3, cmp, s_sel, s_new, v_sel, o_w, kvn3, gates3, b_c, expand)
    return o, win_out


def _hgrn_decode_kernel(h4_ref, lb_ref, gn_ref, s_ref, o_ref, s_out_ref, ft_ref):
    b = pl.program_id(0)
    nbatch = h4_ref.shape[0]

    @pl.when(b == 0)
    def _():
        for h in range(HG_HEADS):
            f, _, _ = _hgrn_gates(h4_ref[:, HG_WIDTH + h * HG_DIM:HG_WIDTH + (h + 1) * HG_DIM],
                                  lb_ref[:, h * HG_DIM:(h + 1) * HG_DIM])
            ft_ref[h] = f.T

    lane = lax.broadcasted_iota(i32, (HG_DIM, nbatch), 1)
    gn = gn_ref[...]
    for sq in range(s_ref.shape[0]):
        sample = b * s_ref.shape[0] + sq
        row = h4_ref[pl.ds(sample, 1), :]
        for h in range(HG_HEADS):
            part = lambda k: row[:, k * HG_WIDTH + h * HG_DIM:k * HG_WIDTH + (h + 1) * HG_DIM]
            f_col = jnp.sum(jnp.where(lane == sample, ft_ref[h], 0.0), axis=-1, keepdims=True)
            s_new = f_col * s_ref[sq, h] + (1.0 - f_col) * part(2)
            s_out_ref[sq, h] = s_new
            q = jnp.broadcast_to(part(0) * HG_SCALE, (SUBLANES, HG_DIM)).astype(bf16)
            o = jnp.dot(q, s_new.astype(bf16), preferred_element_type=f32)[0:1]
            o_ref[sq, :, h * HG_DIM:(h + 1) * HG_DIM] = _hgrn_out(o, gn, part(3)).astype(bf16)


def _hgrn_decode(h4, lb, gn, state, nseq=8):
    Bs = h4.shape[0]
    nseq = math.gcd(nseq, Bs)
    sspec = pl.BlockSpec((nseq, HG_HEADS, HG_DIM, HG_DIM), lambda b: (b, 0, 0, 0))
    return pl.pallas_call(
        _hgrn_decode_kernel,
        out_shape=(jax.ShapeDtypeStruct((Bs, 1, HG_WIDTH), bf16),
                   jax.ShapeDtypeStruct((Bs, HG_HEADS, HG_DIM, HG_DIM), f32)),
        grid=(Bs // nseq,),
        in_specs=[pl.BlockSpec(h4.shape, lambda b: (0, 0)), pl.BlockSpec(lb.shape, lambda b: (0, 0)),
                  pl.BlockSpec(gn.shape, lambda b: (0, 0)), sspec],
        out_specs=(pl.BlockSpec((nseq, 1, HG_WIDTH), lambda b: (b, 0, 0)), sspec),
        scratch_shapes=[pltpu.VMEM((HG_HEADS, HG_DIM, Bs), f32)],
        compiler_params=_params(("arbitrary",)),
        name="hgrn_decode",
    )(h4, lb, gn, state)


def _split_w_in(w):
    idx = np.cumsum(SPLITS)[:-1]
    return jnp.split(w, [int(v) for v in idx], axis=1)


def kernel(x_prompt, x_sample, cache_kv, page_table, state_kv_win, state_hgrn, state_conv, rel_table, hg_lb_logits,
           norm1_g, w_in, q_norm_g, k_norm_g, phi_k_w1, phi_k_w2, phi_v_w1, phi_v_w2, hg_norm_g, w_nsa_out, w_hg_out,
           w_o, norm2_g, w_gate, w_up, conv_w, conv_b, w_down):
    assert w_in.shape[0] == 1, "one layer"
    Bp, L, _ = x_prompt.shape
    Bs = x_sample.shape[0]
    n_pages = page_table.shape[1]
    past = n_pages * PAGE_SIZE
    n_win = state_kv_win.shape[2]
    assert x_sample.shape[1] == 1 and n_win == WINDOW and past % (2 * CMP_BLOCK) == 0 and L % LANES == 0

    wq, wkv, wgl, whq, whf, whi, whg, wmg = _split_w_in(w_in[0])
    wgl = jnp.pad(wgl, ((0, 0), (0, LANES - wgl.shape[1])))
    w_prompt = jnp.concatenate([wq, wkv, whq, whf, whi, whg, wmg, wgl], axis=1).astype(bf16)
    wq_h = wq.reshape(D_MODEL, NSA_HEADS, HEAD_DIM)
    zq = jnp.zeros_like(wq_h)
    in_g0 = (np.arange(NSA_HEADS) < NSA_GROUP)[None, :, None]
    wq_bd = jnp.where(in_g0, jnp.concatenate([wq_h, zq], axis=-1), jnp.concatenate([zq, wq_h], axis=-1))
    w_sample = jnp.concatenate([wq_bd.reshape(D_MODEL, NSA_HEADS * LANES).astype(bf16), w_prompt[:, NSA_WIDTH:]],
                               axis=1)
    g1 = norm1_g[0][None, :]
    gq = q_norm_g[0]
    gq_prompt = jnp.tile(gq, NSA_HEADS)[None, :]
    gq_sample = jnp.tile(gq, 2 * NSA_HEADS)[None, :]
    gk = [jnp.tile(k_norm_g[0, s], NSA_KV_HEADS)[None, :] for s in range(3)]

    def block_diag(mats):
        n = len(mats)
        lead = [(0, 0)] * (mats[0].ndim - 2)
        rows = [jnp.pad(m.astype(bf16), lead + [(0, 0), (t * HEAD_DIM, (n - 1 - t) * HEAD_DIM)])
                for t, m in enumerate(mats)]
        return jnp.concatenate(rows, axis=-2)

    w1bd = block_diag([phi_k_w1[0], phi_k_w1[0], phi_v_w1[0], phi_v_w1[0]]).reshape(CMP_BLOCK * 2 * LANES, 2 * LANES)
    w2bd = block_diag([phi_k_w2[0], phi_k_w2[0], phi_v_w2[0], phi_v_w2[0]])
    lb = jnp.cumsum(jax.nn.softmax(hg_lb_logits.astype(f32), axis=0), axis=0)[0][None, :]
    gn = hg_norm_g[0][None, :]
    ffn_w = (w_nsa_out[0].astype(bf16), w_hg_out[0].astype(bf16), w_o[0].astype(bf16), norm2_g[0][None, :],
             w_gate[0].astype(bf16), w_up[0].astype(bf16), conv_w[0], conv_b[0][None, :], w_down[0].astype(bf16))

    idx_c, idx_t = _prompt_bucket_tables(L)
    tc = _bias_tables(rel_table, jnp.asarray(idx_c))
    tt = _bias_tables(rel_table, jnp.asarray(idx_t))
    bias_dec = _bias_tables(rel_table, jnp.asarray(_decode_bucket_table(past, n_win))[None]).reshape(NSA_HEADS, -1)
    ex_prompt = jnp.asarray(_expand_np(L // SEL_BLOCK, L), dtype=bf16)
    ex_sample = jnp.asarray(_expand_np(past // SEL_BLOCK, past), dtype=bf16)

    xp = x_prompt.reshape(Bp * L, D_MODEL)
    qn, kv_t, win32, katt, h4, mgs, gates = _in_proj(xp, g1, w_prompt, gq_prompt, gk[1], gk[2],
                                                     qw=NSA_WIDTH, q_seg=HEAD_DIM, tm=512, seq_len=L)
    cmp = _compress_prompt(kv_t, w1bd, w2bd, gk[0])
    o_nsa = _nsa_prompt(qn, gates, katt, cmp, tc, tt, ex_prompt, Bp, L)
    o_hg, s_prompt = _hgrn_prompt(h4, lb, gn, Bp, L)
    tm_ffn = 512
    yp, gate_tails = _ffn(xp, o_nsa, o_hg.reshape(Bp * L, HG_WIDTH), mgs, *ffn_w, tm=tm_ffn, tf=D_FF // 2, seq_len=L)
    conv_p = gate_tails.reshape(Bp, L // tm_ffn, 2, D_FF)[:, -1]
    n_keep_win = min(WINDOW, L)
    win_p = jnp.swapaxes(win32[:, :, L - n_keep_win:], 1, 2).reshape(Bp, n_keep_win, 2, NSA_KV_HEADS, HEAD_DIM)

    xs = x_sample.reshape(Bs, D_MODEL)
    q_bd, kv32_s, win32_s, _, h4_s, mgs_s, gates_s = _in_proj(xs, g1, w_sample, gq_sample, gk[1], gk[2],
                                                              qw=NSA_HEADS * LANES, q_seg=LANES, tm=Bs)
    cache_t = jnp.swapaxes(cache_kv[0].reshape(cache_kv.shape[1], PAGE_SIZE, 4 * LANES), 1, 2)
    win_t = jnp.swapaxes(state_kv_win[0].reshape(Bs, n_win, 2 * LANES), 1, 2)
    o_nsa_s, win_out_t = _nsa_decode(cache_t, page_table, q_bd, kv32_s, win32_s, gates_s, win_t, w1bd, w2bd, gk[0],
                                     bias_dec, ex_sample)
    win_s = jnp.swapaxes(win_out_t, 1, 2)
    kv32 = jnp.swapaxes(kv_t, 1, 2)
    o_hg_s, s_sample = _hgrn_decode(h4_s, lb, gn, state_hgrn[0])
    ys, gate_s = _ffn(xs, o_nsa_s.reshape(Bs, NSA_WIDTH), o_hg_s.reshape(Bs, HG_WIDTH), mgs_s, *ffn_w,
                      tm=Bs, tf=256, conv_state=state_conv[0])
    conv_s = jnp.stack([state_conv[0][:, 1], gate_s], axis=1)

    kvh = (NSA_KV_HEADS, HEAD_DIM)
    return (yp.reshape(Bp, L, D_MODEL), ys.reshape(Bs, 1, D_MODEL),
            kv32.reshape((1, Bp, L, 4) + kvh), kv32_s.reshape((1, Bs, 1, 4) + kvh),
            win_p[None], win_s.reshape((1, Bs, n_win, 2) + kvh),
            s_prompt[None], s_sample[None], conv_p[None], conv_s[None])
```

```python
import functools
import math

import numpy as np
import jax
import jax.numpy as jnp
from jax import lax
from jax.experimental import pallas as pl
from jax.experimental.pallas import tpu as pltpu

f32, bf16, i32 = jnp.float32, jnp.bfloat16, jnp.int32

D_MODEL = 1024
PAGE_SIZE = 128
NSA_HEADS, NSA_KV_HEADS, NSA_GROUP, HEAD_DIM = 8, 2, 4, 64
CMP_BLOCK, SEL_BLOCK, TOP_N, WINDOW = 32, 64, 16, 512
SCALE = HEAD_DIM ** -0.5
NSA_WIDTH = NSA_HEADS * HEAD_DIM
KV_WIDTH = NSA_KV_HEADS * HEAD_DIM
HG_HEADS, HG_DIM = 4, 128
HG_WIDTH = HG_HEADS * HG_DIM
HG_SCALE = HG_DIM ** -0.5
HG_BLOCK = 16
SEL_CLASSES = 16
SUBLANES = 8
FFN_SLICE = 768
REL_BUCKETS, REL_MAX_DIST = 32, 128
FAR_TILE = -(-(REL_MAX_DIST + SEL_BLOCK) // SEL_BLOCK) + 1
D_FF = 2816
EPS = 1e-6
SPLITS = (NSA_WIDTH, 6 * KV_WIDTH, 3 * NSA_HEADS, HG_WIDTH, HG_WIDTH, HG_WIDTH, HG_WIDTH, 2 * D_MODEL)
LANES = 128
NEG = -1e30
FORCED_SCORE, INVALID_SCORE = 8.0, -1.0
VMEM_LIMIT = 56 * 1024 * 1024

NT = (((1,), (1,)), ((), ()))
TN = (((0,), (0,)), ((), ()))


def _params(sem):
    return pltpu.CompilerParams(dimension_semantics=sem, vmem_limit_bytes=VMEM_LIMIT)


def _bucket_np(dist):
    n = np.maximum(dist, 0)
    exact = REL_BUCKETS // 2
    ratio = np.log(np.maximum(n, 1).astype(np.float32) / np.float32(exact)) / np.float32(math.log(REL_MAX_DIST / exact))
    big = exact + (ratio.astype(np.float32) * np.float32(REL_BUCKETS - exact)).astype(np.int32)
    return np.where(n < exact, n, np.minimum(big, REL_BUCKETS - 1)).astype(np.int32)


def _cmp_perm(n_cmp):
    half = n_cmp // 2
    c = np.arange(n_cmp)
    return np.where(c < half, 2 * c, 2 * (c - half) + 1)


def _prompt_bucket_tables(L):
    nqb = L // SEL_BLOCK
    n_cmp = L // CMP_BLOCK
    qi = np.arange(SEL_BLOCK)
    c_end = (_cmp_perm(n_cmp) + 1) * CMP_BLOCK - 1
    idx_c = np.stack([_bucket_np((i * SEL_BLOCK + qi)[:, None] - c_end[None, :]) for i in range(nqb)])
    kj = np.arange(SEL_BLOCK)
    def tile(delta):
        return _bucket_np(delta * SEL_BLOCK + qi[:, None] - kj[None, :])
    idx_t = np.stack([np.concatenate([tile(e), tile(e - 1)], axis=1) for e in range(FAR_TILE + 1)])
    return idx_c.astype(np.int32), idx_t.astype(np.int32)


def _decode_bucket_table(past_len, n_win):
    n_cmp = past_len // CMP_BLOCK
    c_end = (_cmp_perm(n_cmp) + 1) * CMP_BLOCK - 1
    d_c = np.zeros((LANES,), np.int64)
    d_c[:n_cmp] = past_len - c_end
    d_s = past_len - np.arange(past_len)
    d_w = n_win - np.arange(n_win)
    d_new = np.zeros((LANES,), np.int64)
    return _bucket_np(np.concatenate([d_s, d_w, d_c, d_new]))[None, :].astype(np.int32)


def _expand_np(n_blocks, n_keys):
    e = (np.arange(n_keys)[None, :] // SEL_BLOCK == np.arange(n_blocks)[:, None])
    return e.astype(np.float32)


def _seg_rmsnorm(x, gain, seg):
    outs = []
    for t in range(x.shape[1] // LANES):
        xt = x[:, t * LANES:(t + 1) * LANES]
        sq = xt * xt
        if seg == LANES:
            r = lax.rsqrt(jnp.sum(sq, axis=-1, keepdims=True) * (1.0 / HEAD_DIM) + EPS)
        else:
            lo = lax.broadcasted_iota(i32, xt.shape, 1) < HEAD_DIM
            s_lo = jnp.sum(jnp.where(lo, sq, 0.0), axis=-1, keepdims=True)
            s_hi = jnp.sum(jnp.where(lo, 0.0, sq), axis=-1, keepdims=True)
            r = jnp.where(lo, lax.rsqrt(s_lo * (1.0 / HEAD_DIM) + EPS), lax.rsqrt(s_hi * (1.0 / HEAD_DIM) + EPS))
        outs.append(xt * r)
    y = outs[0] if len(outs) == 1 else jnp.concatenate(outs, axis=1)
    return y * gain


def _rmsnorm_rows(x, gain):
    return x * lax.rsqrt(jnp.mean(x * x, axis=-1, keepdims=True) + EPS) * gain


def _softmax_parts(s, mask):
    s = jnp.where(mask, s, NEG)
    m = jnp.max(s, axis=-1, keepdims=True)
    e = jnp.where(mask, jnp.exp(s - m), 0.0)
    return e, jnp.sum(e, axis=-1, keepdims=True)


def _bias_table_kernel(tab_ref, idx_ref, out_ref):
    idx = idx_ref[...]
    rows = idx.shape[0]
    for h in range(NSA_HEADS):
        acc = jnp.zeros(idx.shape, f32)
        for b in range(REL_BUCKETS):
            acc = jnp.where(idx == b, tab_ref[b, h], acc)
        r = h % NSA_GROUP
        out_ref[h // NSA_GROUP, r * rows:(r + 1) * rows, :] = acc


def _bias_tables(rel_table, idx):
    n, rows, w = idx.shape
    return pl.pallas_call(
        _bias_table_kernel,
        out_shape=jax.ShapeDtypeStruct((n, NSA_KV_HEADS, NSA_GROUP * rows, w), f32),
        grid=(n,),
        in_specs=[pl.BlockSpec(memory_space=pltpu.SMEM),
                  pl.BlockSpec((None, rows, w), lambda i: (i, 0, 0))],
        out_specs=pl.BlockSpec((None, NSA_KV_HEADS, NSA_GROUP * rows, w), lambda i: (i, 0, 0, 0)),
        compiler_params=_params(("arbitrary",)),
        name="bias_tables",
    )(rel_table, idx)


def _in_proj_kernel(x_ref, g1_ref, w_ref, gq_ref, gk1_ref, gk2_ref,
                    q_ref, kv_ref, win_ref, katt_ref, h4_ref, mg_ref, gl_ref, *, qw, q_seg, kv_feature_major):
    x = x_ref[...]
    xn = _rmsnorm_rows(x, g1_ref[...]).astype(bf16)

    def proj(c0, width):
        return jnp.dot(xn, w_ref[:, c0:c0 + width], preferred_element_type=f32)

    q_ref[...] = (_seg_rmsnorm(proj(0, qw), gq_ref[...], q_seg) * SCALE).astype(bf16)
    c = qw
    kv = proj(c, 6 * KV_WIDTH)
    c += 6 * KV_WIDTH
    k_sel = _seg_rmsnorm(kv[:, 2 * KV_WIDTH:3 * KV_WIDTH], gk1_ref[...], HEAD_DIM)
    k_win = _seg_rmsnorm(kv[:, 4 * KV_WIDTH:5 * KV_WIDTH], gk2_ref[...], HEAD_DIM)
    v_sel = kv[:, 3 * KV_WIDTH:4 * KV_WIDTH]
    v_win = kv[:, 5 * KV_WIDTH:6 * KV_WIDTH]
    if kv_feature_major:
        kv_ref[0:2 * KV_WIDTH, :] = kv[:, 0:2 * KV_WIDTH].T
        kv_ref[2 * KV_WIDTH:3 * KV_WIDTH, :] = k_sel.T
        kv_ref[3 * KV_WIDTH:4 * KV_WIDTH, :] = v_sel.T
    else:
        kv_ref[:, 0:2 * KV_WIDTH] = kv[:, 0:2 * KV_WIDTH]
        kv_ref[:, 2 * KV_WIDTH:3 * KV_WIDTH] = k_sel
        kv_ref[:, 3 * KV_WIDTH:4 * KV_WIDTH] = v_sel
    if kv_feature_major:
        win_ref[0:KV_WIDTH, :] = k_win.T
        win_ref[KV_WIDTH:2 * KV_WIDTH, :] = v_win.T
    else:
        win_ref[:, 0:KV_WIDTH] = k_win
        win_ref[:, KV_WIDTH:2 * KV_WIDTH] = v_win
    katt_ref[:, 0:KV_WIDTH] = k_sel.astype(bf16)
    katt_ref[:, KV_WIDTH:2 * KV_WIDTH] = v_sel.astype(bf16)
    katt_ref[:, 2 * KV_WIDTH:3 * KV_WIDTH] = k_win.astype(bf16)
    katt_ref[:, 3 * KV_WIDTH:4 * KV_WIDTH] = v_win.astype(bf16)
    h4_ref[...] = proj(c, 4 * HG_WIDTH)
    c += 4 * HG_WIDTH
    mg_ref[...] = jax.nn.sigmoid(proj(c, 2 * D_MODEL))
    c += 2 * D_MODEL
    gl_ref[...] = jax.nn.sigmoid(proj(c, LANES))


def _in_proj(x2d, g1, w, gq, gk1, gk2, *, qw, q_seg, tm, seq_len=None):
    n = x2d.shape[0]
    row = lambda width: pl.BlockSpec((tm, width), lambda i: (i, 0))
    full = lambda a: pl.BlockSpec(a.shape, lambda i: (0,) * a.ndim)
    widths = (qw, 4 * KV_WIDTH, 2 * KV_WIDTH, 4 * KV_WIDTH, 4 * HG_WIDTH, 2 * D_MODEL, LANES)
    dtypes = (bf16, f32, f32, bf16, f32, f32, f32)
    out_shape = [jax.ShapeDtypeStruct((n, wd), dt) for wd, dt in zip(widths, dtypes)]
    out_specs = [row(wd) for wd in widths]
    if seq_len is not None:
        tiles = seq_len // tm
        for o_, wd in ((1, 4 * KV_WIDTH), (2, 2 * KV_WIDTH)):
            out_shape[o_] = jax.ShapeDtypeStruct((n // seq_len, wd, seq_len), f32)
            out_specs[o_] = pl.BlockSpec((None, wd, tm), lambda i: (i // tiles, 0, i % tiles))
    return pl.pallas_call(
        functools.partial(_in_proj_kernel, qw=qw, q_seg=q_seg, kv_feature_major=seq_len is not None),
        out_shape=tuple(out_shape),
        grid=(n // tm,),
        in_specs=[row(D_MODEL), full(g1),
                  pl.BlockSpec(w.shape, lambda i: (0, 0), pipeline_mode=pl.Buffered(1)),
                  full(gq), full(gk1), full(gk2)],
        out_specs=tuple(out_specs),
        compiler_params=_params(("arbitrary",)),
        name="in_proj",
    )(x2d, g1, w, gq, gk1, gk2)


CMP_SLOT = CMP_BLOCK + 1


def _cmp_staging_rows(n_cmp):
    return -(-n_cmp * CMP_SLOT // SUBLANES) * SUBLANES


def _stage_cmp_chunk(src_ref, chunk, kc_t, vc_t, n_cmp):
    per_chunk = LANES // CMP_BLOCK
    for s, t in enumerate((kc_t.astype(bf16).T.astype(f32), vc_t.astype(bf16).T.astype(f32))):
        for nl in range(per_chunk):
            n = chunk * per_chunk + nl
            slot = n // 2 + (n % 2) * (n_cmp // 2)
            src_ref[s, slot * CMP_SLOT:slot * CMP_SLOT + CMP_BLOCK, :] = t[nl * CMP_BLOCK:(nl + 1) * CMP_BLOCK]


def _compress(src_refs, w1_ref, w2_ref, gk0, n_cmp):
    acc = jnp.zeros((len(src_refs) * n_cmp, 2 * LANES), f32)
    per_dot = 8
    for j0 in range(0, CMP_BLOCK, per_dot):
        xs = jnp.concatenate(
            [jnp.concatenate([src[s, pl.ds(j, n_cmp, stride=CMP_SLOT), :].astype(bf16)
                              for j in range(j0, j0 + per_dot) for s in range(2)], axis=1)
             for src in src_refs], axis=0)
        w = w1_ref[j0 * 2 * LANES:(j0 + per_dot) * 2 * LANES, :]
        acc = acc + jnp.dot(xs, w, preferred_element_type=f32)
    hmid = jax.nn.gelu(acc).astype(bf16)
    y = jnp.dot(hmid, w2_ref[...], preferred_element_type=f32)
    ck = _seg_rmsnorm(y[:, 0:LANES], gk0, HEAD_DIM)
    return jnp.concatenate([ck, y[:, LANES:2 * LANES]], axis=1)


def _compress_prompt_kernel(kv_ref, w1_ref, w2_ref, gk0_ref, out_ref, src_ref, *, n_cmp):
    for c in range(kv_ref.shape[1] // LANES):
        cols = slice(c * LANES, (c + 1) * LANES)
        _stage_cmp_chunk(src_ref, c, kv_ref[0:LANES, cols], kv_ref[LANES:2 * LANES, cols], n_cmp)
    out_ref[...] = _compress([src_ref], w1_ref, w2_ref, gk0_ref[...], n_cmp).astype(bf16)


def _compress_prompt(kv_t, w1bd, w2bd, gk0):
    B, _, L = kv_t.shape
    n_cmp = L // CMP_BLOCK
    return pl.pallas_call(
        functools.partial(_compress_prompt_kernel, n_cmp=n_cmp),
        out_shape=jax.ShapeDtypeStruct((B, n_cmp, 2 * LANES), bf16),
        grid=(B,),
        in_specs=[pl.BlockSpec((None, 2 * LANES, L), lambda b: (b, 0, 0)),
                  pl.BlockSpec(w1bd.shape, lambda b: (0, 0)),
                  pl.BlockSpec(w2bd.shape, lambda b: (0, 0)),
                  pl.BlockSpec(gk0.shape, lambda b: (0, 0))],
        out_specs=pl.BlockSpec((None, n_cmp, 2 * LANES), lambda b: (b, 0, 0)),
        scratch_shapes=[pltpu.VMEM((2, _cmp_staging_rows(n_cmp), LANES), f32)],
        compiler_params=_params(("arbitrary",)),
        name="compress_prompt",
    )(kv_t, w1bd, w2bd, gk0)


def _select_blocks_t(score, valid, n_keep):
    nb = score.shape[0]
    jj = lax.broadcasted_iota(i32, score.shape, 0)
    cnt = jnp.zeros(score.shape, f32)
    for k in range(nb):
        sk = score[k:k + 1, :]
        better = (sk > score) | ((sk == score) & (jj > k))
        cnt = cnt + jnp.where(better, 1.0, 0.0)
    return (cnt < n_keep) & valid


def _sel_tile_classes(n_tiles):
    return sorted({-(-n_tiles * c // SEL_CLASSES) for c in range(1, SEL_CLASSES + 1)})


def _nsa_prompt_kernel(q_ref, gate_ref, katt_ref, cmp_ref, tc_ref, tt_ref, ex_ref, o_ref,
                       selx_ref, s_ref, os_ref, part_ref, gsel_ref, *, L):
    i = pl.program_id(1)
    n_cmp = L // CMP_BLOCK
    n_sel = L // SEL_BLOCK
    n_tiles = L // LANES
    qb = SEL_BLOCK
    rows = NSA_GROUP * qb
    units = [(sq, g) for sq in range(q_ref.shape[0]) for g in range(NSA_KV_HEADS)]
    nu = range(len(units))
    row64 = lax.broadcasted_iota(i32, (qb, LANES), 0)
    lane = lax.broadcasted_iota(i32, (qb, LANES), 1)
    qpos = i * qb + row64

    qs = [jnp.concatenate(
        [q_ref[sq, :, (NSA_GROUP * g + r) * HEAD_DIM:(NSA_GROUP * g + r + 1) * HEAD_DIM] for r in range(NSA_GROUP)],
        axis=0) for sq, g in units]

    def attend(k_off, v_off, tile0, n_t, mask_fn):
        chunk = 4

        def key_rows(t0, count, off, u):
            sq, g = units[u]
            r0 = pl.multiple_of((tile0 + t0) * LANES, LANES)
            return katt_ref[sq, pl.ds(r0, count * LANES), off + g * HEAD_DIM:off + (g + 1) * HEAD_DIM]

        def scores(t0, count, mruns):
            dots = [lax.dot_general(qs[u], key_rows(t0, count, k_off, u), NT, preferred_element_type=f32)
                    for u in nu]
            out = list(mruns)
            for j in range(count):
                ta = tile0 + t0 + j
                bias_idx = jnp.clip(i - 2 * ta, 0, FAR_TILE)
                kpos = ta * LANES + lane
                for u in nu:
                    s = dots[u][:, j * LANES:(j + 1) * LANES] + tt_ref[bias_idx, units[u][1]]
                    msk = mask_fn(u, ta, kpos)
                    s = jnp.where(msk[None], s.reshape(NSA_GROUP, qb, LANES), NEG).reshape(rows, LANES)
                    s_ref[u, t0 + j] = s
                    out[u] = jnp.maximum(out[u], s)
            return tuple(out)

        def values(t0, count, carry, ms):
            ls, accs = list(carry[0]), list(carry[1])
            ps = []
            for u in nu:
                pj = [jnp.exp(s_ref[u, t0 + j] - ms[u]) for j in range(count)]
                for p in pj:
                    ls[u] = ls[u] + p
                ps.append(jnp.concatenate([p.astype(bf16) for p in pj], axis=1))
            for u in nu:
                accs[u] = accs[u] + jnp.dot(ps[u], key_rows(t0, count, v_off, u), preferred_element_type=f32)
            return tuple(ls), tuple(accs)

        def over_chunks(fn, carry):
            for t0 in range(0, n_t, chunk):
                carry = fn(t0, min(chunk, n_t - t0), carry)
            return carry

        mruns = over_chunks(scores, tuple(jnp.full((rows, LANES), NEG, f32) for _ in nu))
        ms = [jnp.max(m, axis=-1, keepdims=True) for m in mruns]
        init = (tuple(jnp.zeros((rows, LANES), f32) for _ in nu),
                tuple(jnp.zeros((rows, HEAD_DIM), f32) for _ in nu))
        ls, accs = over_chunks(lambda t0, count, cr: values(t0, count, cr, ms), init)
        return [accs[u] / jnp.sum(ls[u], axis=-1, keepdims=True) for u in nu]

    n_wt = min(WINDOW // LANES + 1, n_tiles)
    w0 = jnp.clip((i - WINDOW // qb) // 2, 0, n_tiles - n_wt)
    o_w = attend(2 * KV_WIDTH, 3 * KV_WIDTH, w0, n_wt,
                 lambda u, ta, kpos: (kpos <= qpos) & (qpos - kpos <= WINDOW))

    o_c, imp_t = [], []
    col = lax.broadcasted_iota(i32, (rows, n_cmp), 1)
    rq = lax.broadcasted_iota(i32, (rows, n_cmp), 0) & (qb - 1)
    blk = jnp.where(col < n_cmp // 2, 2 * col, 2 * col - (n_cmp - 1))
    vis = (blk + 1) * CMP_BLOCK - 1 <= i * qb + rq
    for u, (sq, g) in enumerate(units):
        ck = cmp_ref[sq, :, g * HEAD_DIM:(g + 1) * HEAD_DIM]
        cv = cmp_ref[sq, :, LANES + g * HEAD_DIM:LANES + (g + 1) * HEAD_DIM]
        s = lax.dot_general(qs[u], ck, NT, preferred_element_type=f32) + tc_ref[g]
        e, den = _softmax_parts(s, vis)
        p = e / jnp.where(den > 0, den, 1.0)
        o_c.append(jnp.dot(p.astype(bf16), cv, preferred_element_type=f32))
        ps_t = (p[0:qb] + p[qb:2 * qb] + p[2 * qb:3 * qb] + p[3 * qb:4 * qb]).T
        imp_t.append(ps_t[0:n_sel] + ps_t[n_sel:2 * n_sel])

    imp = jnp.concatenate(imp_t, axis=1)
    jj = lax.broadcasted_iota(i32, imp.shape, 0)
    valid = jj <= i
    forced = valid & ((jj == 0) | (jj >= i - 1))
    score = jnp.where(forced, FORCED_SCORE, jnp.where(valid, imp, INVALID_SCORE))
    sel = jnp.where(_select_blocks_t(score, valid, min(TOP_N, n_sel)), 1.0, 0.0).astype(bf16)

    for u, (sq, g) in enumerate(units):
        gates = gate_ref[sq]
        for r in range(NSA_GROUP):
            h = NSA_GROUP * g + r
            hs = slice(h * HEAD_DIM, (h + 1) * HEAD_DIM)
            sl = slice(r * qb, (r + 1) * qb)
            part_ref[sq, :, hs] = gates[:, 3 * h:3 * h + 1] * o_c[u][sl] + gates[:, 3 * h + 2:3 * h + 3] * o_w[u][sl]
            gsel_ref[sq, :, hs] = jnp.broadcast_to(gates[:, 3 * h + 1:3 * h + 2], (qb, HEAD_DIM))

    need = i // 2 + 1
    prev = 0
    for n_t in _sel_tile_classes(n_tiles):
        @pl.when((need > prev) & (need <= n_t))
        def _(n_t=n_t):
            selx = lax.dot_general(sel, ex_ref[:, 0:n_t * LANES], TN, preferred_element_type=f32)
            for u in nu:
                for t in range(n_t):
                    selx_ref[u, t] = selx[u * qb:(u + 1) * qb, t * LANES:(t + 1) * LANES]
            o_s = attend(0, KV_WIDTH, 0, n_t, lambda u, ta, kpos: (selx_ref[u, ta] > 0.5) & (kpos <= qpos))
            for u in nu:
                os_ref[u] = o_s[u]
        prev = n_t

    for u, (sq, g) in enumerate(units):
        o_s = os_ref[u]
        for r in range(NSA_GROUP):
            hs = slice((NSA_GROUP * g + r) * HEAD_DIM, (NSA_GROUP * g + r + 1) * HEAD_DIM)
            o = part_ref[sq, :, hs] + gsel_ref[sq, :, hs] * o_s[r * qb:(r + 1) * qb]
            o_ref[sq, :, hs] = o.astype(bf16)


def _nsa_prompt(qn, gates, katt, cmp, tc, tt, expand, B, L, nseq=2):
    nqb = L // SEL_BLOCK
    qb = SEL_BLOCK
    n_tiles = L // LANES
    rows = NSA_GROUP * qb
    nseq = math.gcd(nseq, B)
    n_units = nseq * NSA_KV_HEADS
    seq3 = lambda a: a.reshape(B, L, a.shape[-1])
    return pl.pallas_call(
        functools.partial(_nsa_prompt_kernel, L=L),
        out_shape=jax.ShapeDtypeStruct((B, L, NSA_WIDTH), bf16),
        grid=(B // nseq, nqb),
        in_specs=[pl.BlockSpec((nseq, qb, NSA_WIDTH), lambda b, i: (b, i, 0)),
                  pl.BlockSpec((nseq, qb, LANES), lambda b, i: (b, i, 0)),
                  pl.BlockSpec((nseq, L, 4 * KV_WIDTH), lambda b, i: (b, 0, 0)),
                  pl.BlockSpec((nseq, L // CMP_BLOCK, 2 * LANES), lambda b, i: (b, 0, 0)),
                  pl.BlockSpec((None,) + tc.shape[1:], lambda b, i: (i, 0, 0, 0)),
                  pl.BlockSpec(tt.shape, lambda b, i: (0, 0, 0, 0)),
                  pl.BlockSpec(expand.shape, lambda b, i: (0, 0))],
        out_specs=pl.BlockSpec((nseq, qb, NSA_WIDTH), lambda b, i: (b, i, 0)),
        scratch_shapes=[pltpu.VMEM((n_units, n_tiles, qb, LANES), f32),
                        pltpu.VMEM((n_units, n_tiles, rows, LANES), f32),
                        pltpu.VMEM((n_units, rows, HEAD_DIM), f32),
                        pltpu.VMEM((nseq, qb, NSA_WIDTH), f32), pltpu.VMEM((nseq, qb, NSA_WIDTH), f32)],
        compiler_params=_params(("arbitrary", "arbitrary")),
        name="nsa_prompt",
    )(seq3(qn), seq3(gates), seq3(katt), cmp, tc, tt, expand).reshape(B * L, NSA_WIDTH)


def _hgrn_gates(hf, lb):
    f = lb + (1.0 - lb) * jax.nn.sigmoid(hf)
    return f, 1.0 - f, jnp.log(f)


def _hgrn_out(o, gn, hgate):
    return _rmsnorm_rows(o, gn) * jax.nn.silu(hgate)


def _hgrn_prompt_kernel(h4_ref, lb_ref, gn_ref, o_ref, s_out_ref, st_ref, *, tc):
    c = pl.program_id(1)
    nb = HG_BLOCK
    nseq = h4_ref.shape[0]

    @pl.when(c == 0)
    def _():
        st_ref[...] = jnp.zeros_like(st_ref)

    t_row = lax.broadcasted_iota(i32, (nb, HG_DIM), 0)
    t_col = lax.broadcasted_iota(i32, (nb, 1), 0)
    gn = gn_ref[...]
    chains = [(sq, h) for sq in range(nseq) for h in range(HG_HEADS)]

    def cumsum_rows(x):
        shift = 1
        while shift < nb:
            x = x + jnp.where(t_row >= shift, pltpu.roll(x, shift, axis=0), 0.0)
            shift *= 2
        return x

    def block(bi, _):
        r0 = pl.multiple_of(bi * nb, nb)

        def sl(sq, h, part):
            return h4_ref[sq, pl.ds(r0, nb), part * HG_WIDTH + h * HG_DIM:part * HG_WIDTH + (h + 1) * HG_DIM]

        qs, ks, vs, bs, sts, os_ = [], [], [], [], [], []
        for sq, h in chains:
            _, k, glog = _hgrn_gates(sl(sq, h, 1), lb_ref[:, h * HG_DIM:(h + 1) * HG_DIM])
            qs.append(sl(sq, h, 0) * HG_SCALE)
            ks.append(k)
            vs.append(sl(sq, h, 2))
            bs.append(cumsum_rows(glog))
        for c_, (sq, h) in enumerate(chains):
            sts.append(st_ref[sq, h])
            os_.append(lax.dot_general((qs[c_] * jnp.exp(bs[c_])).astype(bf16), sts[c_].astype(bf16), NT,
                                       preferred_element_type=f32))
        for c_ in range(len(chains)):
            q, k, v, b, o = qs[c_], ks[c_], vs[c_], bs[c_], os_[c_]
            parts = [o[r0_:r0_ + SUBLANES] for r0_ in range(0, nb, SUBLANES)]
            for s in range(nb):
                for gi in range(s // SUBLANES, nb // SUBLANES):
                    rs = slice(gi * SUBLANES, (gi + 1) * SUBLANES)
                    w = q[rs] * k[s:s + 1] * jnp.exp(b[rs] - b[s:s + 1])
                    a = jnp.sum(w, axis=-1, keepdims=True)
                    if gi == s // SUBLANES:
                        a = jnp.where(t_col[rs] >= s, a, 0.0)
                    parts[gi] = parts[gi] + a * v[s:s + 1]
            os_[c_] = jnp.concatenate(parts, axis=0)
        for c_, (sq, h) in enumerate(chains):
            b_last = bs[c_][nb - 1:nb]
            kt = ks[c_] * jnp.exp(b_last - bs[c_])
            upd = lax.dot_general(vs[c_].astype(bf16), kt.astype(bf16), TN, preferred_element_type=f32)
            st_ref[sq, h] = jnp.exp(b_last) * sts[c_] + upd
            o_ref[sq, pl.ds(r0, nb), h * HG_DIM:(h + 1) * HG_DIM] = _hgrn_out(os_[c_], gn, sl(sq, h, 3)).astype(bf16)
        return 0

    lax.fori_loop(0, tc // nb, block, 0, unroll=2)

    @pl.when(c == pl.num_programs(1) - 1)
    def _():
        for sq in range(nseq):
            for h in range(HG_HEADS):
                s_out_ref[sq, h] = st_ref[sq, h].T


def _hgrn_prompt(h4, lb, gn, B, L, tc=256, nseq=4):
    nc = L // tc
    nseq = math.gcd(nseq, B)
    return pl.pallas_call(
        functools.partial(_hgrn_prompt_kernel, tc=tc),
        out_shape=(jax.ShapeDtypeStruct((B, L, HG_WIDTH), bf16),
                   jax.ShapeDtypeStruct((B, HG_HEADS, HG_DIM, HG_DIM), f32)),
        grid=(B // nseq, nc),
        in_specs=[pl.BlockSpec((nseq, tc, 4 * HG_WIDTH), lambda b, c: (b, c, 0)),
                  pl.BlockSpec(lb.shape, lambda b, c: (0, 0)),
                  pl.BlockSpec(gn.shape, lambda b, c: (0, 0))],
        out_specs=(pl.BlockSpec((nseq, tc, HG_WIDTH), lambda b, c: (b, c, 0)),
                   pl.BlockSpec((nseq, HG_HEADS, HG_DIM, HG_DIM), lambda b, c: (b, 0, 0, 0))),
        scratch_shapes=[pltpu.VMEM((nseq, HG_HEADS, HG_DIM, HG_DIM), f32)],
        compiler_params=_params(("arbitrary", "arbitrary")),
        name="hgrn_prompt",
    )(h4.reshape(B, L, 4 * HG_WIDTH), lb, gn)


def _ffn_kernel(*refs, decode, tiles_per_seq):
    if decode:
        (x_ref, on_ref, oh_ref, mg_ref, wn_ref, wh_ref, wo_ref, g2_ref, wg_ref, wu_ref, cw_ref, cb_ref, wd_ref,
         cs0_ref, cs1_ref, y_ref, gate_out_ref, x1_ref, xn2_ref, acc_ref) = refs
    else:
        (x_ref, on_ref, oh_ref, mg_ref, wn_ref, wh_ref, wo_ref, g2_ref, wg_ref, wu_ref, cw_ref, cb_ref, wd_ref,
         y_ref, gate_out_ref, x1_ref, xn2_ref, acc_ref, carry_ref) = refs
    i = pl.program_id(0)
    fi = pl.program_id(1)
    tm = x_ref.shape[0]

    @pl.when(fi == 0)
    def _():
        y_a = jnp.dot(on_ref[...], wn_ref[...], preferred_element_type=f32)
        y_b = jnp.dot(oh_ref[...], wh_ref[...], preferred_element_type=f32)
        mg = mg_ref[...]
        merged = mg[:, 0:D_MODEL] * y_a + mg[:, D_MODEL:2 * D_MODEL] * y_b
        x1 = x_ref[...] + jnp.dot(merged.astype(bf16), wo_ref[...], preferred_element_type=f32)
        x1_ref[...] = x1
        xn2_ref[...] = _rmsnorm_rows(x1, g2_ref[...]).astype(bf16)
        acc_ref[...] = jnp.zeros_like(acc_ref)

    xn2 = xn2_ref[...]
    tf = wg_ref.shape[1]
    if not decode:
        @pl.when(i % tiles_per_seq == 0)
        def _():
            carry_ref[fi] = jnp.zeros(carry_ref.shape[1:], f32)

    acc = acc_ref[...]
    for a in range(0, tf, FFN_SLICE):
        cs = slice(a, min(a + FFN_SLICE, tf))
        gate = jnp.dot(xn2, wg_ref[:, cs], preferred_element_type=f32)
        up = jnp.dot(xn2, wu_ref[:, cs], preferred_element_type=f32)
        cw = cw_ref[:, cs]
        if decode:
            prev2, prev1 = cs0_ref[:, cs], cs1_ref[:, cs]
            gate_out_ref[:, cs] = gate
        else:
            carry = carry_ref[fi, :, cs]
            rid = lax.broadcasted_iota(i32, gate.shape, 0)
            c1, c2 = carry[SUBLANES - 1:SUBLANES], carry[SUBLANES - 2:SUBLANES - 1]
            prev1 = jnp.where(rid == 0, c1, pltpu.roll(gate, 1, axis=0))
            prev2 = jnp.where(rid == 0, c2, jnp.where(rid == 1, c1, pltpu.roll(gate, 2, axis=0)))
            carry_ref[fi, SUBLANES - 2:SUBLANES, cs] = gate[tm - 2:tm]
            gate_out_ref[:, cs] = gate[tm - 2:tm]
        conv = cb_ref[:, cs] + cw[0:1] * prev2 + cw[1:2] * prev1 + cw[2:3] * gate
        hmid = (jax.nn.silu(conv) * up).astype(bf16)
        acc = acc + jnp.dot(hmid, wd_ref[cs, :], preferred_element_type=f32)
    acc_ref[...] = acc

    @pl.when(fi == pl.num_programs(1) - 1)
    def _():
        y_ref[...] = x1_ref[...] + acc_ref[...]


def _ffn(x2d, o_nsa, o_hg, mgs, wn, wh, wo, g2, wg, wu, cw, cb, wd, *, tm, tf, seq_len=None, conv_state=None):
    n = x2d.shape[0]
    decode = conv_state is not None
    nf = D_FF // tf
    row = lambda width: pl.BlockSpec((tm, width), lambda i, f: (i, 0))
    full = lambda a: pl.BlockSpec(a.shape, lambda i, f: (0,) * a.ndim)
    in_specs = [row(D_MODEL), row(NSA_WIDTH), row(HG_WIDTH), row(2 * D_MODEL), full(wn), full(wh), full(wo), full(g2),
                pl.BlockSpec((D_MODEL, tf), lambda i, f: (0, f)), pl.BlockSpec((D_MODEL, tf), lambda i, f: (0, f)),
                pl.BlockSpec((3, tf), lambda i, f: (0, f)), pl.BlockSpec((1, tf), lambda i, f: (0, f)),
                pl.BlockSpec((tf, D_MODEL), lambda i, f: (f, 0))]
    args = [x2d, o_nsa, o_hg, mgs, wn, wh, wo, g2, wg, wu, cw, cb, wd]
    scratch = [pltpu.VMEM((tm, D_MODEL), f32), pltpu.VMEM((tm, D_MODEL), bf16), pltpu.VMEM((tm, D_MODEL), f32)]
    if decode:
        in_specs += [pl.BlockSpec((tm, tf), lambda i, f: (i, f))] * 2
        args += [conv_state[:, 0], conv_state[:, 1]]
        gate_shape = jax.ShapeDtypeStruct((n, D_FF), f32)
        gate_spec = pl.BlockSpec((tm, tf), lambda i, f: (i, f))
        tiles_per_seq = 1
    else:
        tiles_per_seq = seq_len // tm
        gate_shape = jax.ShapeDtypeStruct((n // tm, 2, D_FF), f32)
        gate_spec = pl.BlockSpec((None, 2, tf), lambda i, f: (i, 0, f))
        scratch.append(pltpu.VMEM((nf, SUBLANES, tf), f32))
    return pl.pallas_call(
        functools.partial(_ffn_kernel, decode=decode, tiles_per_seq=tiles_per_seq),
        out_shape=(jax.ShapeDtypeStruct((n, D_MODEL), f32), gate_shape),
        grid=(n // tm, nf),
        in_specs=in_specs,
        out_specs=(row(D_MODEL), gate_spec),
        scratch_shapes=scratch,
        compiler_params=_params(("arbitrary", "arbitrary")),
        name="ffn_decode" if decode else "ffn_prompt",
    )(*args)


def _nsa_decode_kernel(*refs, n_pages, n_win, nseq):
    page_refs = refs[1:1 + nseq * n_pages]
    (q_ref, kvn_ref, winn_ref, win_all_ref, win_ref, w1_ref, w2_ref, gk0_ref, bias_ref,
     ow_ref, win_out_ref, cmp_ref, ssel_ref, snew_ref, vsel_ref, src_ref, ksel_ref, wcol_ref) = refs[1 + nseq * n_pages:]
    step = pl.program_id(0)
    past = n_pages * PAGE_SIZE
    n_cmp = past // CMP_BLOCK
    heads = NSA_HEADS
    samples = range(nseq)

    @pl.when(step == 0)
    def _():
        for t in range(2):
            wcol_ref[t * LANES:(t + 1) * LANES, :] = win_all_ref[:, t * LANES:(t + 1) * LANES].T

    for sq in samples:
        for k in range(n_pages):
            page = page_refs[sq * n_pages + k]
            cols = slice(k * PAGE_SIZE, (k + 1) * PAGE_SIZE)
            _stage_cmp_chunk(src_ref.at[sq], k, page[0:LANES, :], page[LANES:2 * LANES, :], n_cmp)
            ksel_ref[sq, :, cols] = page[2 * LANES:3 * LANES, :].astype(bf16)
            vsel_ref[sq, :, cols] = page[3 * LANES:4 * LANES, :].astype(bf16)

    hrow = lax.broadcasted_iota(i32, (heads, 1), 0)
    g0 = hrow < NSA_GROUP
    bias = bias_ref[...]
    b_s = bias[:, 0:past]
    b_w = bias[:, past:past + n_win]
    b_new = bias[:, past + n_win + LANES:past + n_win + LANES + 1]
    qs = [q_ref[sq] for sq in samples]
    attend = _decode_attend

    def scores(q, k_t, b_past, k_new):
        s_past = jnp.dot(q, k_t, preferred_element_type=f32) + b_past
        s_new = jnp.sum(q.astype(f32) * k_new, axis=-1, keepdims=True) + b_new
        return s_past, s_new

    lane_b = lax.broadcasted_iota(i32, wcol_ref.shape, 1)
    lane_w = lax.broadcasted_iota(i32, win_ref.shape[1:], 1)
    for sq in samples:
        winn, kvn = winn_ref[sq], kvn_ref[sq]
        s_past, s_new = scores(qs[sq], ksel_ref[sq], b_s, kvn[:, 2 * LANES:3 * LANES])
        ssel_ref[sq] = s_past
        snew_ref[sq] = jnp.broadcast_to(s_new, (heads, LANES))
        s_win = scores(qs[sq], win_ref[sq, 0:LANES, :].astype(bf16), b_w, winn[:, 0:LANES])
        ow_ref[sq] = attend(s_win, None, win_ref[sq, LANES:2 * LANES, :].astype(bf16), winn[:, LANES:2 * LANES],
                            g0)
        new_col = jnp.sum(jnp.where(lane_b == step * nseq + sq, wcol_ref[...], 0.0), axis=-1, keepdims=True)
        win_out_ref[sq] = jnp.where(lane_w == n_win - 1, new_col, pltpu.roll(win_ref[sq], n_win - 1, axis=1))

    cmp = _compress([src_ref.at[sq] for sq in samples], w1_ref, w2_ref, gk0_ref[...], n_cmp)
    for sq in samples:
        cmp_ref[sq] = cmp[sq * n_cmp:(sq + 1) * n_cmp].astype(bf16)


def _decode_attend(s, mask, v_t, v_new, g0):
    s_past, s_new = s
    sm = s_past if mask is None else jnp.where(mask, s_past, NEG)
    m = jnp.maximum(jnp.max(sm, axis=-1, keepdims=True), s_new)
    e = jnp.exp(sm - m)
    e_new = jnp.exp(s_new - m)
    den = jnp.sum(e, axis=-1, keepdims=True) + e_new
    o2 = (lax.dot_general(e.astype(bf16), v_t, NT, preferred_element_type=f32) + e_new * v_new) / den
    return jnp.where(g0, o2[:, 0:HEAD_DIM], o2[:, HEAD_DIM:2 * HEAD_DIM])


def _nsa_decode_tail_kernel(q_ref, cmp_ref, ssel_ref, snew_ref, vsel_ref, ow_ref, kvn_ref, gate_ref, bias_ref, ex_ref,
                            o_ref, *, n_sel):
    samples = range(q_ref.shape[0])
    heads = NSA_HEADS
    n_cmp = cmp_ref.shape[1]
    g0 = lax.broadcasted_iota(i32, (heads, 1), 0) < NSA_GROUP
    b_c = bias_ref[...]
    cur = n_sel
    jrow = lax.broadcasted_iota(i32, (NSA_KV_HEADS, n_sel), 1)
    forced = (jrow == 0) | (jrow >= cur - 1)
    n_keep = min(TOP_N, n_sel + 1) - 1
    ri = lax.broadcasted_iota(i32, (n_sel, n_sel), 0)
    ci = lax.broadcasted_iota(i32, (n_sel, n_sel), 1)

    s_c = [lax.dot_general(q_ref[sq], cmp_ref[sq, :, 0:LANES], NT, preferred_element_type=f32) + b_c for sq in samples]
    p_c = []
    for sq in samples:
        e_c, den_c = _softmax_parts(s_c[sq], jnp.full(s_c[sq].shape, True))
        p_c.append(e_c / jnp.where(den_c > 0, den_c, 1.0))
    o_c = [jnp.dot(p_c[sq].astype(bf16), cmp_ref[sq, :, LANES:2 * LANES], preferred_element_type=f32) for sq in samples]
    sels = []
    for sq in samples:
        ps = jnp.concatenate([jnp.sum(p_c[sq][0:NSA_GROUP], axis=0, keepdims=True),
                              jnp.sum(p_c[sq][NSA_GROUP:heads], axis=0, keepdims=True)], axis=0)
        imp = ps[:, 0:n_sel] + ps[:, n_sel:2 * n_sel]
        score_row = jnp.where(forced, FORCED_SCORE, imp)
        rows = []
        for g in range(NSA_KV_HEADS):
            rowb = jnp.broadcast_to(score_row[g:g + 1], (n_sel, n_sel))
            colb = jnp.sum(jnp.where(ri == ci, rowb, 0.0), axis=-1, keepdims=True)
            better = (colb > rowb) | ((colb == rowb) & (ri < ci))
            cnt = jnp.sum(jnp.where(better, 1.0, 0.0), axis=0, keepdims=True)
            rows.append(jnp.where(cnt < n_keep, 1.0, 0.0))
        sels.append(jnp.concatenate(rows, axis=0).astype(bf16))
    selx = [jnp.dot(sels[sq], ex_ref[...], preferred_element_type=f32) for sq in samples]
    for sq in samples:
        mask = jnp.where(g0, selx[sq][0:1], selx[sq][1:2]) > 0.5
        o_s = _decode_attend((ssel_ref[sq], snew_ref[sq][:, 0:1]), mask, vsel_ref[sq],
                             kvn_ref[sq][:, 3 * LANES:4 * LANES], g0)
        gates = gate_ref[sq]
        o_cg = jnp.where(g0, o_c[sq][:, 0:HEAD_DIM], o_c[sq][:, HEAD_DIM:2 * HEAD_DIM])
        o_ref[sq] = (gates[:, 0:1] * o_cg + gates[:, 1:2] * o_s + gates[:, 2:3] * ow_ref[sq]).astype(bf16)


def _nsa_decode(cache_t, page_table, q_bd, kv32, win32, gates, win_t, w1bd, w2bd, gk0, bias, expand,
                nseq=2, nseq_tail=8):
    Bs, n_pages = page_table.shape
    n_win = win_t.shape[2]
    past = n_pages * PAGE_SIZE
    n_cmp = past // CMP_BLOCK
    nseq = math.gcd(nseq, Bs)
    per_b = lambda shape: pl.BlockSpec((nseq,) + shape, lambda b, pt: (b,) + (0,) * len(shape))
    const = lambda a: pl.BlockSpec(a.shape, lambda b, pt: (0,) * a.ndim)
    page_spec = lambda sq, k: pl.BlockSpec((None, 4 * LANES, PAGE_SIZE), lambda b, pt: (pt[b * nseq + sq, k], 0, 0))
    q3 = q_bd.reshape(Bs, NSA_HEADS, LANES)
    kvn3 = kv32.reshape(Bs, 1, 4 * LANES)
    gates3 = gates[:, 0:3 * NSA_HEADS].reshape(Bs, NSA_HEADS, 3)
    grid_spec = pltpu.PrefetchScalarGridSpec(
        num_scalar_prefetch=1,
        grid=(Bs // nseq,),
        in_specs=[page_spec(sq, k) for sq in range(nseq) for k in range(n_pages)] + [
            per_b((NSA_HEADS, LANES)), per_b((1, 4 * LANES)), per_b((1, 2 * LANES)), const(win32),
            per_b((2 * LANES, n_win)), const(w1bd), const(w2bd), const(gk0), const(bias)],
        out_specs=(per_b((NSA_HEADS, HEAD_DIM)), per_b((2 * LANES, n_win)), per_b((n_cmp, 2 * LANES)),
                   per_b((NSA_HEADS, past)), per_b((NSA_HEADS, LANES)), per_b((LANES, past))),
        scratch_shapes=[pltpu.VMEM((nseq, 2, _cmp_staging_rows(n_cmp), LANES), f32),
                        pltpu.VMEM((nseq, LANES, past), bf16), pltpu.VMEM((2 * LANES, Bs), f32)],
    )
    o_w, win_out, cmp, s_sel, s_new, v_sel = pl.pallas_call(
        functools.partial(_nsa_decode_kernel, n_pages=n_pages, n_win=n_win, nseq=nseq),
        out_shape=(jax.ShapeDtypeStruct((Bs, NSA_HEADS, HEAD_DIM), f32),
                   jax.ShapeDtypeStruct((Bs, 2 * LANES, n_win), f32),
                   jax.ShapeDtypeStruct((Bs, n_cmp, 2 * LANES), bf16),
                   jax.ShapeDtypeStruct((Bs, NSA_HEADS, past), f32),
                   jax.ShapeDtypeStruct((Bs, NSA_HEADS, LANES), f32),
                   jax.ShapeDtypeStruct((Bs, LANES, past), bf16)),
        grid_spec=grid_spec,
        compiler_params=_params(("arbitrary",)),
        name="nsa_decode",
    )(page_table, *([cache_t] * (nseq * n_pages)), q3, kvn3, win32.reshape(Bs, 1, 2 * LANES), win32, win_t,
      w1bd, w2bd, gk0, bias)

    ns = math.gcd(nseq_tail, Bs)
    per_t = lambda shape: pl.BlockSpec((ns,) + shape, lambda b: (b,) + (0,) * len(shape))
    const_t = lambda a: pl.BlockSpec(a.shape, lambda b: (0,) * a.ndim)
    b_c = bias[:, past + n_win:past + n_win + n_cmp]
    o = pl.pallas_call(
        functools.partial(_nsa_decode_tail_kernel, n_sel=past // SEL_BLOCK),
        out_shape=jax.ShapeDtypeStruct((Bs, NSA_HEADS, HEAD_DIM), bf16),
        grid=(Bs // ns,),
        in_specs=[per_t((NSA_HEADS, LANES)), per_t((n_cmp, 2 * LANES)), per_t((NSA_HEADS, past)),
                  per_t((NSA_HEADS, LANES)), per_t((LANES, past)), per_t((NSA_HEADS, HEAD_DIM)),
                  per_t((1, 4 * LANES)), per_t((NSA_HEADS, 3)), const_t(b_c), const_t(expand)],
        out_specs=per_t((NSA_HEADS, HEAD_DIM)),
        compiler_params=_params(("arbitrary",)),
        name="nsa_decode_select",
    )(q3, cmp, s_sel, s_new, v_sel, o_w, kvn3, gates3, b_c, expand)
    return o, win_out


def _hgrn_decode_kernel(h4_ref, lb_ref, gn_ref, s_ref, o_ref, s_out_ref, ft_ref):
    b = pl.program_id(0)
    nbatch = h4_ref.shape[0]

    @pl.when(b == 0)
    def _():
        for h in range(HG_HEADS):
            f, _, _ = _hgrn_gates(h4_ref[:, HG_WIDTH + h * HG_DIM:HG_WIDTH + (h + 1) * HG_DIM],
                                  lb_ref[:, h * HG_DIM:(h + 1) * HG_DIM])
            ft_ref[h] = f.T

    lane = lax.broadcasted_iota(i32, (HG_DIM, nbatch), 1)
    gn = gn_ref[...]
    for sq in range(s_ref.shape[0]):
        sample = b * s_ref.shape[0] + sq
        row = h4_ref[pl.ds(sample, 1), :]
        for h in range(HG_HEADS):
            part = lambda k: row[:, k * HG_WIDTH + h * HG_DIM:k * HG_WIDTH + (h + 1) * HG_DIM]
            f_col = jnp.sum(jnp.where(lane == sample, ft_ref[h], 0.0), axis=-1, keepdims=True)
            s_new = f_col * s_ref[sq, h] + (1.0 - f_col) * part(2)
            s_out_ref[sq, h] = s_new
            q = jnp.broadcast_to(part(0) * HG_SCALE, (SUBLANES, HG_DIM)).astype(bf16)
            o = jnp.dot(q, s_new.astype(bf16), preferred_element_type=f32)[0:1]
            o_ref[sq, :, h * HG_DIM:(h + 1) * HG_DIM] = _hgrn_out(o, gn, part(3)).astype(bf16)


def _hgrn_decode(h4, lb, gn, state, nseq=8):
    Bs = h4.shape[0]
    nseq = math.gcd(nseq, Bs)
    sspec = pl.BlockSpec((nseq, HG_HEADS, HG_DIM, HG_DIM), lambda b: (b, 0, 0, 0))
    return pl.pallas_call(
        _hgrn_decode_kernel,
        out_shape=(jax.ShapeDtypeStruct((Bs, 1, HG_WIDTH), bf16),
                   jax.ShapeDtypeStruct((Bs, HG_HEADS, HG_DIM, HG_DIM), f32)),
        grid=(Bs // nseq,),
        in_specs=[pl.BlockSpec(h4.shape, lambda b: (0, 0)), pl.BlockSpec(lb.shape, lambda b: (0, 0)),
                  pl.BlockSpec(gn.shape, lambda b: (0, 0)), sspec],
        out_specs=(pl.BlockSpec((nseq, 1, HG_WIDTH), lambda b: (b, 0, 0)), sspec),
        scratch_shapes=[pltpu.VMEM((HG_HEADS, HG_DIM, Bs), f32)],
        compiler_params=_params(("arbitrary",)),
        name="hgrn_decode",
    )(h4, lb, gn, state)


def _split_w_in(w):
    idx = np.cumsum(SPLITS)[:-1]
    return jnp.split(w, [int(v) for v in idx], axis=1)


def kernel(x_prompt, x_sample, cache_kv, page_table, state_kv_win, state_hgrn, state_conv, rel_table, hg_lb_logits,
           norm1_g, w_in, q_norm_g, k_norm_g, phi_k_w1, phi_k_w2, phi_v_w1, phi_v_w2, hg_norm_g, w_nsa_out, w_hg_out,
           w_o, norm2_g, w_gate, w_up, conv_w, conv_b, w_down):
    assert w_in.shape[0] == 1, "one layer"
    Bp, L, _ = x_prompt.shape
    Bs = x_sample.shape[0]
    n_pages = page_table.shape[1]
    past = n_pages * PAGE_SIZE
    n_win = state_kv_win.shape[2]
    assert x_sample.shape[1] == 1 and n_win == WINDOW and past % (2 * CMP_BLOCK) == 0 and L % LANES == 0

    wq, wkv, wgl, whq, whf, whi, whg, wmg = _split_w_in(w_in[0])
    wgl = jnp.pad(wgl, ((0, 0), (0, LANES - wgl.shape[1])))
    w_prompt = jnp.concatenate([wq, wkv, whq, whf, whi, whg, wmg, wgl], axis=1).astype(bf16)
    wq_h = wq.reshape(D_MODEL, NSA_HEADS, HEAD_DIM)
    zq = jnp.zeros_like(wq_h)
    in_g0 = (np.arange(NSA_HEADS) < NSA_GROUP)[None, :, None]
    wq_bd = jnp.where(in_g0, jnp.concatenate([wq_h, zq], axis=-1), jnp.concatenate([zq, wq_h], axis=-1))
    w_sample = jnp.concatenate([wq_bd.reshape(D_MODEL, NSA_HEADS * LANES).astype(bf16), w_prompt[:, NSA_WIDTH:]],
                               axis=1)
    g1 = norm1_g[0][None, :]
    gq = q_norm_g[0]
    gq_prompt = jnp.tile(gq, NSA_HEADS)[None, :]
    gq_sample = jnp.tile(gq, 2 * NSA_HEADS)[None, :]
    gk = [jnp.tile(k_norm_g[0, s], NSA_KV_HEADS)[None, :] for s in range(3)]

    def block_diag(mats):
        n = len(mats)
        lead = [(0, 0)] * (mats[0].ndim - 2)
        rows = [jnp.pad(m.astype(bf16), lead + [(0, 0), (t * HEAD_DIM, (n - 1 - t) * HEAD_DIM)])
                for t, m in enumerate(mats)]
        return jnp.concatenate(rows, axis=-2)

    w1bd = block_diag([phi_k_w1[0], phi_k_w1[0], phi_v_w1[0], phi_v_w1[0]]).reshape(CMP_BLOCK * 2 * LANES, 2 * LANES)
    w2bd = block_diag([phi_k_w2[0], phi_k_w2[0], phi_v_w2[0], phi_v_w2[0]])
    lb = jnp.cumsum(jax.nn.softmax(hg_lb_logits.astype(f32), axis=0), axis=0)[0][None, :]
    gn = hg_norm_g[0][None, :]
    ffn_w = (w_nsa_out[0].astype(bf16), w_hg_out[0].astype(bf16), w_o[0].astype(bf16), norm2_g[0][None, :],
             w_gate[0].astype(bf16), w_up[0].astype(bf16), conv_w[0], conv_b[0][None, :], w_down[0].astype(bf16))

    idx_c, idx_t = _prompt_bucket_tables(L)
    tc = _bias_tables(rel_table, jnp.asarray(idx_c))
    tt = _bias_tables(rel_table, jnp.asarray(idx_t))
    bias_dec = _bias_tables(rel_table, jnp.asarray(_decode_bucket_table(past, n_win))[None]).reshape(NSA_HEADS, -1)
    ex_prompt = jnp.asarray(_expand_np(L // SEL_BLOCK, L), dtype=bf16)
    ex_sample = jnp.asarray(_expand_np(past // SEL_BLOCK, past), dtype=bf16)

    xp = x_prompt.reshape(Bp * L, D_MODEL)
    qn, kv_t, win32, katt, h4, mgs, gates = _in_proj(xp, g1, w_prompt, gq_prompt, gk[1], gk[2],
                                                     qw=NSA_WIDTH, q_seg=HEAD_DIM, tm=512, seq_len=L)
    cmp = _compress_prompt(kv_t, w1bd, w2bd, gk[0])
    o_nsa = _nsa_prompt(qn, gates, katt, cmp, tc, tt, ex_prompt, Bp, L)
    o_hg, s_prompt = _hgrn_prompt(h4, lb, gn, Bp, L)
    tm_ffn = 512
    yp, gate_tails = _ffn(xp, o_nsa, o_hg.reshape(Bp * L, HG_WIDTH), mgs, *ffn_w, tm=tm_ffn, tf=D_FF // 2, seq_len=L)
    conv_p = gate_tails.reshape(Bp, L // tm_ffn, 2, D_FF)[:, -1]
    n_keep_win = min(WINDOW, L)
    win_p = jnp.swapaxes(win32[:, :, L - n_keep_win:], 1, 2).reshape(Bp, n_keep_win, 2, NSA_KV_HEADS, HEAD_DIM)

    xs = x_sample.reshape(Bs, D_MODEL)
    q_bd, kv32_s, win32_s, _, h4_s, mgs_s, gates_s = _in_proj(xs, g1, w_sample, gq_sample, gk[1], gk[2],
                                                              qw=NSA_HEADS * LANES, q_seg=LANES, tm=Bs)
    cache_t = jnp.swapaxes(cache_kv[0].reshape(cache_kv.shape[1], PAGE_SIZE, 4 * LANES), 1, 2)
    win_t = jnp.swapaxes(state_kv_win[0].reshape(Bs, n_win, 2 * LANES), 1, 2)
    o_nsa_s, win_out_t = _nsa_decode(cache_t, page_table, q_bd, kv32_s, win32_s, gates_s, win_t, w1bd, w2bd, gk[0],
                                     bias_dec, ex_sample)
    win_s = jnp.swapaxes(win_out_t, 1, 2)
    kv32 = jnp.swapaxes(kv_t, 1, 2)
    o_hg_s, s_sample = _hgrn_decode(h4_s, lb, gn, state_hgrn[0])
    ys, gate_s = _ffn(xs, o_nsa_s.reshape(Bs, NSA_WIDTH), o_hg_s.reshape(Bs, HG_WIDTH), mgs_s, *ffn_w,
                      tm=Bs, tf=256, conv_state=state_conv[0])
    conv_s = jnp.stack([state_conv[0][:, 1], gate_s], axis=1)

    kvh = (NSA_KV_HEADS, HEAD_DIM)
    return (yp.reshape(Bp, L, D_MODEL), ys.reshape(Bs, 1, D_MODEL),
            kv32.reshape((1, Bp, L, 4) + kvh), kv32_s.reshape((1, Bs, 1, 4) + kvh),
            win_p[None], win_s.reshape((1, Bs, n_win, 2) + kvh),
            s_prompt[None], s_sample[None], conv_p[None], conv_s[None])
```
